```python
import math
import jax, jax.numpy as jnp
from jax import lax
import numpy as np

D_MODEL = 1024
BATCH = 32
SEQ = 256
DEPTH = 1
DEC_BATCH = 8
DEC_SEQ = 1024
PAST_LEN = 256

GRID_W = 64
A_WIDTH = D_MODEL // 2
B_WIDTH = D_MODEL - A_WIDTH
H_A = 4
DV_A = A_WIDTH // H_A
DK_A = DV_A // 2
H_B = 4
DV_B = B_WIDTH // H_B
DK_B = DV_B // 2
GATE_RANK = 16
GATE_NORM = 16.0
GLA_CHUNK = 64
D_FF = 4 * D_MODEL
N_MOD = 6
Q_BLOCK = 128
ROPE_BASE = 10000.0
EPS = 1e-6

kernel_name = "hybrid_diffattn_gla_prefix_dit_step"


def rmsnorm(x, g):
    xf = x.astype(jnp.float32)
    y = xf * lax.rsqrt(jnp.mean(xf * xf, axis=-1, keepdims=True) + EPS)
    return (y * g.astype(jnp.float32)).astype(x.dtype)


def rope_1d(x, pos):
    half = x.shape[-1] // 2
    inv = ROPE_BASE ** (-jnp.arange(half, dtype=jnp.float32) / half)
    ang = pos.astype(jnp.float32)[:, None] * inv[None, :]
    cos, sin = jnp.cos(ang).astype(x.dtype), jnp.sin(ang).astype(x.dtype)
    x1, x2 = x[..., :half], x[..., half:]
    return jnp.concatenate([x1 * cos - x2 * sin, x1 * sin + x2 * cos], axis=-1)


def rope2d(x, row_pos, col_pos):
    d = x.shape[-1] // 2
    return jnp.concatenate([rope_1d(x[..., :d], row_pos), rope_1d(x[..., d:], col_pos)], axis=-1)


def diff_attention(q1, q2, k1, k2, v, lam):
    B, H, Lq, d = q1.shape
    nb = Lq // Q_BLOCK
    scale = d ** -0.5

    def to_blocks(q):
        return jnp.moveaxis(q.reshape(B, H, nb, Q_BLOCK, d), 2, 0)

    def one_block(qs):
        a1, a2 = qs
        s1 = jnp.einsum('bhqd,bhkd->bhqk', a1, k1).astype(jnp.float32) * scale
        s2 = jnp.einsum('bhqd,bhkd->bhqk', a2, k2).astype(jnp.float32) * scale
        p = jax.nn.softmax(s1, axis=-1) - lam * jax.nn.softmax(s2, axis=-1)
        return jnp.einsum('bhqk,bhkv->bhqv', p.astype(v.dtype), v)

    o = lax.map(one_block, (to_blocks(q1), to_blocks(q2)))
    return jnp.moveaxis(o, 0, 2).reshape(B, H, Lq, v.shape[-1])


def gla_scan(q, k, v, g, s0):
    B, H, L, dk = q.shape
    dv = v.shape[-1]
    n = L // GLA_CHUNK

    def chunks(t):
        return jnp.moveaxis(t.astype(jnp.float32).reshape(B, H, n, GLA_CHUNK, t.shape[-1]), 2, 0)

    causal = jnp.tril(jnp.ones((GLA_CHUNK, GLA_CHUNK), dtype=bool))[:, :, None]

    def step(S, inp):
        qc, kc, vc, gc = inp
        b = jnp.cumsum(gc, axis=-2)
        o_inter = jnp.einsum('bhtk,bhkv->bhtv', qc * jnp.exp(b), S)
        diff = b[..., :, None, :] - b[..., None, :, :]
        decay = jnp.where(causal, jnp.exp(jnp.where(causal, diff, 0.0)), 0.0)
        att = jnp.einsum('bhtk,bhsk,bhtsk->bhts', qc, kc, decay)
        o = o_inter + jnp.einsum('bhts,bhsv->bhtv', att, vc)
        b_last = b[..., -1:, :]
        S_new = jnp.exp(b_last)[..., 0, :, None] * S + jnp.einsum('bhsk,bhsv->bhkv', kc * jnp.exp(b_last - b), vc)
        return S_new, o

    S_fin, o = lax.scan(step, s0.astype(jnp.float32), (chunks(q), chunks(k), chunks(v), chunks(g)))
    return jnp.moveaxis(o, 0, 2).reshape(B, H, L, dv), S_fin


def adaln(cvec, lw):
    m = jax.nn.silu(cvec) @ lw['w_ada'] + lw['b_ada']
    return [t[:, None, :] for t in jnp.split(m, N_MOD, axis=-1)]


def token_mixers(h, lw, lam_init, pos, ctx_k, ctx_v, s0_f, s0_b):
    B, L, _ = h.shape
    sizes = [H_A * 2 * DK_A, H_A * 2 * DK_A, H_A * DV_A, H_B * DK_B, H_B * DK_B,
             H_B * DV_B, H_B * DV_B, GATE_RANK, GATE_RANK]
    cuts = [int(s) for s in np.cumsum(sizes)[:-1]]
    qa, ka, va, qb, kb, vb, rb, glf, glb = jnp.split(h @ lw['w_in'], cuts, axis=-1)

    def heads(t, nh):
        return t.reshape(B, L, nh, -1).transpose(0, 2, 1, 3)

    qa, ka, va = heads(qa, H_A), heads(ka, H_A), heads(va, H_A)
    q1, q2, k1, k2 = qa[..., :DK_A], qa[..., DK_A:], ka[..., :DK_A], ka[..., DK_A:]
    if pos is not None:
        row_pos, col_pos = pos
        q1, q2, k1, k2 = (rope2d(t, row_pos, col_pos) for t in (q1, q2, k1, k2))
    own_k = jnp.concatenate([k1, k2], axis=-1)
    if ctx_k is not None:
        keys = jnp.concatenate([ctx_k, own_k], axis=2)
        values = jnp.concatenate([ctx_v, va], axis=2)
    else:
        keys, values = own_k, va
    f32 = jnp.float32
    lam = (jnp.exp(jnp.sum(lw['lam_q1'].astype(f32) * lw['lam_k1'].astype(f32)))
           - jnp.exp(jnp.sum(lw['lam_q2'].astype(f32) * lw['lam_k2'].astype(f32))) + lam_init)
    o_a = diff_attention(q1, q2, keys[..., :DK_A], keys[..., DK_A:], values, lam)
    o_a = rmsnorm(o_a, lw['diff_norm']) * (1.0 - lam_init)
    o_a = o_a.transpose(0, 2, 1, 3).reshape(B, L, H_A * DV_A)

    qb = heads(qb, H_B) * (DK_B ** -0.5)
    kb, vb = heads(kb, H_B), heads(vb, H_B)
    g_f = heads(jax.nn.log_sigmoid((glf @ lw['w_gate_fwd'] + lw['b_gate_fwd']).astype(f32)) / GATE_NORM, H_B)
    g_b = heads(jax.nn.log_sigmoid((glb @ lw['w_gate_bwd'] + lw['b_gate_bwd']).astype(f32)) / GATE_NORM, H_B)
    if s0_f is None:
        s0_f = jnp.zeros((B, H_B, DK_B, DV_B), f32)
        s0_b = jnp.zeros((B, H_B, DK_B, DV_B), f32)
    o_f, s_f = gla_scan(qb, kb, vb, g_f, s0_f)
    flip = lambda t: jnp.flip(t, axis=2)
    o_b, s_b = gla_scan(flip(qb), flip(kb), flip(vb), flip(g_b), s0_b)
    o_g = rmsnorm(o_f + flip(o_b), lw['gla_norm'])
    o_g = (o_g.transpose(0, 2, 1, 3).reshape(B, L, H_B * DV_B) * jax.nn.silu(rb.astype(f32))).astype(h.dtype)

    out = jnp.concatenate([o_a, o_g], axis=-1) @ lw['w_out']
    return out, (own_k, va, s_f, s_b)


def layer(x, cvec, lw, lam_init, pos, ctx_k, ctx_v, s0_f, s0_b):
    shift1, scale1, gate1, shift2, scale2, gate2 = adaln(cvec, lw)
    h = rmsnorm(x, lw['norm_attn_pre']) * (1.0 + scale1) + shift1
    mix, ctx_tensors = token_mixers(h, lw, lam_init, pos, ctx_k, ctx_v, s0_f, s0_b)
    x = x + gate1 * rmsnorm(mix, lw['norm_attn_post'])
    h2 = rmsnorm(x, lw['norm_mlp_pre']) * (1.0 + scale2) + shift2
    f = jnp.square(jax.nn.relu(h2 @ lw['w_mlp1'])) @ lw['w_mlp2']
    x = x + gate2 * rmsnorm(f, lw['norm_mlp_post'])
    return x, ctx_tensors


def setup_inputs(seed: int = 0) -> dict:
    key = jax.random.key(seed)
    ks = jax.random.split(key, 32)
    nrm = lambda k, shape, s=1.0: jax.random.normal(k, shape, jnp.float32) * s
    gain = lambda k: 1.0 + nrm(k, (DEPTH, D_MODEL), 0.05)
    in_cols = 2 * H_A * 2 * DK_A + H_A * DV_A + 2 * H_B * DK_B + 2 * H_B * DV_B + 2 * GATE_RANK
    return {
        "x_prompt": nrm(ks[0], (BATCH, SEQ, D_MODEL)),
        "x_sample": nrm(ks[1], (DEC_BATCH, DEC_SEQ, D_MODEL)),
        "c": nrm(ks[2], (DEC_BATCH, D_MODEL)),
        "cache_k": nrm(ks[3], (DEC_BATCH, DEPTH, H_A, PAST_LEN, 2 * DK_A)),
        "cache_v": nrm(ks[4], (DEC_BATCH, DEPTH, H_A, PAST_LEN, DV_A)),
        "state_fwd": nrm(ks[5], (DEC_BATCH, DEPTH, H_B, DK_B, DV_B), 0.5),
        "state_bwd": nrm(ks[6], (DEC_BATCH, DEPTH, H_B, DK_B, DV_B), 0.5),
        "c_ctx": nrm(ks[7], (D_MODEL,)),
        "w_ada": nrm(ks[8], (DEPTH, D_MODEL, N_MOD * D_MODEL), 0.5 * D_MODEL ** -0.5),
        "b_ada": nrm(ks[9], (DEPTH, N_MOD * D_MODEL), 0.02),
        "norm_attn_pre": gain(ks[10]),
        "norm_attn_post": gain(ks[11]),
        "norm_mlp_pre": gain(ks[12]),
        "norm_mlp_post": gain(ks[13]),
        "w_in": nrm(ks[14], (DEPTH, D_MODEL, in_cols), D_MODEL ** -0.5),
        "w_gate_fwd": nrm(ks[15], (DEPTH, GATE_RANK, H_B * DK_B), GATE_RANK ** -0.5),
        "b_gate_fwd": nrm(ks[16], (DEPTH, H_B * DK_B), 0.1),
        "w_gate_bwd": nrm(ks[17], (DEPTH, GATE_RANK, H_B * DK_B), GATE_RANK ** -0.5),
        "b_gate_bwd": nrm(ks[18], (DEPTH, H_B * DK_B), 0.1),
        "lam_q1": nrm(ks[19], (DEPTH, DK_A), 0.1),
        "lam_k1": nrm(ks[20], (DEPTH, DK_A), 0.1),
        "lam_q2": nrm(ks[21], (DEPTH, DK_A), 0.1),
        "lam_k2": nrm(ks[22], (DEPTH, DK_A), 0.1),
        "diff_norm": 1.0 + nrm(ks[23], (DEPTH, DV_A), 0.05),
        "gla_norm": 1.0 + nrm(ks[24], (DEPTH, DV_B), 0.05),
        "w_out": nrm(ks[25], (DEPTH, D_MODEL, D_MODEL), D_MODEL ** -0.5),
        "w_mlp1": nrm(ks[26], (DEPTH, D_MODEL, D_FF), D_MODEL ** -0.5),
        "w_mlp2": nrm(ks[27], (DEPTH, D_FF, D_MODEL), D_FF ** -0.5),
    }


def reference(x_prompt, x_sample, c, cache_k, cache_v, state_fwd, state_bwd, c_ctx,
              w_ada, b_ada, norm_attn_pre, norm_attn_post, norm_mlp_pre, norm_mlp_post,
              w_in, w_gate_fwd, b_gate_fwd, w_gate_bwd, b_gate_bwd,
              lam_q1, lam_k1, lam_q2, lam_k2, diff_norm, gla_norm, w_out, w_mlp1, w_mlp2):
    n_lat = x_sample.shape[1]
    ROWS = n_lat // GRID_W
    row_pos = jnp.repeat(jnp.arange(ROWS, dtype=jnp.int32), GRID_W)
    col_pos = jnp.arange(ROWS * GRID_W, dtype=jnp.int32) % GRID_W
    pos = (row_pos, col_pos)

    y_prompt, y_sample = x_prompt, x_sample
    new_k, new_v, new_sf, new_sb = [], [], [], []
    for l in range(DEPTH):
        lw = {
            'w_ada': w_ada[l], 'b_ada': b_ada[l],
            'norm_attn_pre': norm_attn_pre[l], 'norm_attn_post': norm_attn_post[l],
            'norm_mlp_pre': norm_mlp_pre[l], 'norm_mlp_post': norm_mlp_post[l],
            'w_in': w_in[l], 'w_gate_fwd': w_gate_fwd[l], 'b_gate_fwd': b_gate_fwd[l],
            'w_gate_bwd': w_gate_bwd[l], 'b_gate_bwd': b_gate_bwd[l],
            'lam_q1': lam_q1[l], 'lam_k1': lam_k1[l], 'lam_q2': lam_q2[l], 'lam_k2': lam_k2[l],
            'diff_norm': diff_norm[l], 'gla_norm': gla_norm[l], 'w_out': w_out[l],
            'w_mlp1': w_mlp1[l], 'w_mlp2': w_mlp2[l],
        }
        lam_init = 0.8 - 0.6 * math.exp(-0.3 * l)
        y_prompt, (k_c, v_c, s_f, s_b) = layer(y_prompt, c_ctx[None, :], lw, lam_init,
                                                None, None, None, None, None)
        new_k.append(k_c)
        new_v.append(v_c)
        new_sf.append(s_f)
        new_sb.append(s_b)
        y_sample, _ = layer(y_sample, c, lw, lam_init, pos,
                            cache_k[:, l], cache_v[:, l], state_fwd[:, l], state_bwd[:, l])
    new_cache_k = jnp.stack(new_k, axis=1)
    new_cache_v = jnp.stack(new_v, axis=1)
    new_state_fwd = jnp.stack(new_sf, axis=1)
    new_state_bwd = jnp.stack(new_sb, axis=1)
    return (y_prompt, y_sample, new_cache_k, new_cache_v, new_state_fwd, new_state_bwd)
```

```python
import functools
import math

import numpy as np
import jax
import jax.numpy as jnp
from jax import lax
from jax.experimental import pallas as pl
from jax.experimental.pallas import tpu as pltpu

F32 = jnp.float32
BF16 = jnp.bfloat16

D_MODEL = 1024
GRID_W = 64
H_A = 4
DV_A = 128
DK_A = 64
H_B = 4
DV_B = 128
DK_B = 64
GATE_RANK = 16
GATE_NORM = 16.0
GLA_CHUNK = 64
D_FF = 4 * D_MODEL
N_MOD = 6
ROPE_BASE = 10000.0
EPS = 1e-6
LAM_INIT = 0.8 - 0.6 * math.exp(-0.3 * 0)

C_QA, C_KA, C_VA, C_QB, C_KB, C_VB, C_RB, C_GL = 0, 512, 1024, 1536, 1792, 2048, 2560, 3072
IN_COLS = 3104
IN_COLS_PAD = 3200

VMEM_LIMIT = 56 * 1024 * 1024


def _rms(x, g):
    return x * lax.rsqrt(jnp.mean(x * x, axis=-1, keepdims=True) + EPS) * g


def _dot(a, b):
    return jnp.dot(a, b, preferred_element_type=F32)


def _dot_nt(a, b):
    return lax.dot_general(a, b, (((1,), (1,)), ((), ())), preferred_element_type=F32)


def _dot_tn(a, b):
    return lax.dot_general(a, b, (((0,), (0,)), ((), ())), preferred_element_type=F32)


def _adaln_kernel(c_ref, w_ref, b_ref, o_ref):
    c = c_ref[...]
    s = c * jax.nn.sigmoid(c)
    o_ref[...] = _dot(s.astype(BF16), w_ref[...].astype(BF16)) + b_ref[...]


def _adaln(cvec, w_ada, b_ada):
    rows = cvec.shape[0]
    n = w_ada.shape[1]
    tn = 1536
    return pl.pallas_call(
        _adaln_kernel,
        out_shape=jax.ShapeDtypeStruct((rows, n), F32),
        grid=(n // tn,),
        in_specs=[
            pl.BlockSpec((rows, D_MODEL), lambda j: (0, 0)),
            pl.BlockSpec((D_MODEL, tn), lambda j: (0, j)),
            pl.BlockSpec((1, tn), lambda j: (0, j)),
        ],
        out_specs=pl.BlockSpec((rows, tn), lambda j: (0, j)),
        compiler_params=pltpu.CompilerParams(
            dimension_semantics=("arbitrary",), vmem_limit_bytes=VMEM_LIMIT),
        name="adaln",
    )(cvec, w_ada, b_ada)


def _inproj_kernel(*refs, rope, tm):
    if rope:
        (x_ref, mod_ref, g_ref, w_ref, wg_ref, bg_ref, cos_ref, sa_ref, sb_ref,
         bf_ref, f32_ref) = refs
    else:
        (x_ref, mod_ref, g_ref, w_ref, wg_ref, bg_ref,
         bf_ref, f32_ref, nk_ref, nv_ref) = refs
    x = x_ref[...]
    h = _rms(x, g_ref[...]) * (1.0 + mod_ref[1:2, :]) + mod_ref[0:1, :]
    p = _dot(h.astype(BF16), w_ref[...])

    for blk in range(8):
        t = p[:, blk * 128:(blk + 1) * 128]
        if rope:
            t = (t * cos_ref[...] + pltpu.roll(t, 112, 1) * sa_ref[...]
                 + pltpu.roll(t, 16, 1) * sb_ref[...])
        elif blk >= 4:
            nk_ref[blk - 4] = t
        bf_ref[:, blk * 128:(blk + 1) * 128] = t.astype(BF16)
    va = p[:, C_VA:C_VA + 512]
    bf_ref[:, C_VA:C_VA + 512] = va.astype(BF16)
    if not rope:
        for hh in range(H_A):
            nv_ref[hh] = va[:, hh * 128:(hh + 1) * 128]
    bf_ref[:, 1536:2048] = p[:, C_VB:C_VB + 512].astype(BF16)
    f32_ref[:, 0:256] = p[:, C_QB:C_QB + 256] * (DK_B ** -0.5)
    f32_ref[:, 256:512] = p[:, C_KB:C_KB + 256]
    z = _dot(p[:, C_GL:C_GL + 128].astype(BF16), wg_ref[...]) + bg_ref[...]
    ls = jnp.minimum(z, 0.0) - jnp.log1p(jnp.exp(-jnp.abs(z)))
    f32_ref[:, 512:1024] = ls * (1.0 / GATE_NORM)
    f32_ref[:, 1024:1536] = p[:, C_RB:C_RB + 512]


def _inproj(x2d, mods, g_pre, w_in_b, wg, bg, rope_tabs, seq_len, batch, ctx_out):
    T = x2d.shape[0]
    tm = 256
    per_b = seq_len // tm
    shared_mod = mods.shape[0] == 1
    mod_idx = (lambda i: (0, 0, 0)) if shared_mod else (lambda i: (i // per_b, 0, 0))
    in_specs = [
        pl.BlockSpec((tm, D_MODEL), lambda i: (i, 0)),
        pl.BlockSpec((None, N_MOD, D_MODEL), mod_idx),
        pl.BlockSpec((1, D_MODEL), lambda i: (0, 0)),
        pl.BlockSpec((D_MODEL, IN_COLS_PAD), lambda i: (0, 0), pipeline_mode=pl.Buffered(1)),
        pl.BlockSpec((128, 512), lambda i: (0, 0)),
        pl.BlockSpec((1, 512), lambda i: (0, 0)),
    ]
    args = [x2d, mods, g_pre, w_in_b, wg, bg]
    out_shape = [jax.ShapeDtypeStruct((T, 2048), BF16), jax.ShapeDtypeStruct((T, 1536), F32)]
    out_specs = [pl.BlockSpec((tm, 2048), lambda i: (i, 0)), pl.BlockSpec((tm, 1536), lambda i: (i, 0))]
    rope = rope_tabs is not None
    if rope:
        for t in rope_tabs:
            in_specs.append(pl.BlockSpec((tm, 128), lambda i: (i % per_b, 0)))
            args.append(t)
    if ctx_out:
        assert tm == seq_len
        for _ in range(2):
            out_shape.append(jax.ShapeDtypeStruct((batch, 1, H_A, seq_len, 128), F32))
            out_specs.append(pl.BlockSpec((None, None, H_A, seq_len, 128), lambda i: (i, 0, 0, 0, 0)))
    return pl.pallas_call(
        functools.partial(_inproj_kernel, rope=rope, tm=tm),
        out_shape=out_shape,
        grid=(T // tm,),
        in_specs=in_specs,
        out_specs=out_specs,
        compiler_params=pltpu.CompilerParams(
            dimension_semantics=("arbitrary",), vmem_limit_bytes=VMEM_LIMIT),
        name="inproj_rope" if rope else "inproj_ctx",
    )(*args)


def _softmax(s):
    m = jnp.max(s, axis=-1, keepdims=True)
    e = jnp.exp(s - m)
    return e * (1.0 / jnp.sum(e, axis=-1, keepdims=True))


def _attn_kernel(*refs, cached):
    if cached:
        q_ref, k_ref, v_ref, ck_ref, cv_ref, lam_ref, dn_ref, o_ref = refs
    else:
        q_ref, k_ref, v_ref, lam_ref, dn_ref, o_ref = refs
    lp = lam_ref[...]
    lam = (jnp.exp(jnp.sum(lp[0:1] * lp[1:2], axis=-1, keepdims=True))
           - jnp.exp(jnp.sum(lp[2:3] * lp[3:4], axis=-1, keepdims=True)) + LAM_INIT)
    q = q_ref[...]
    lane = lax.broadcasted_iota(jnp.int32, (1, 128), 1)
    scale = DK_A ** -0.5
    q1 = q * jnp.where(lane < DK_A, scale, 0.0).astype(BF16)
    q2 = q * jnp.where(lane >= DK_A, scale, 0.0).astype(BF16)
    k = k_ref[...]
    v = v_ref[...]
    if cached:
        k = jnp.concatenate([ck_ref[...].astype(BF16), k], axis=0)
        v = jnp.concatenate([cv_ref[...].astype(BF16), v], axis=0)
    p = _softmax(_dot_nt(q1, k)) - lam * _softmax(_dot_nt(q2, k))
    o = _dot(p.astype(BF16), v)
    o_ref[...] = (_rms(o, dn_ref[...]) * (1.0 - LAM_INIT)).astype(BF16)


def _attention(bf, cache_k, cache_v, lam_p, diff_norm, batch, seq_len):
    tq = 256
    nq = seq_len // tq
    cached = cache_k is not None
    in_specs = [
        pl.BlockSpec((tq, 128), lambda b, h, qi: (b * nq + qi, h)),
        pl.BlockSpec((seq_len, 128), lambda b, h, qi: (b, 4 + h)),
        pl.BlockSpec((seq_len, 128), lambda b, h, qi: (b, 8 + h)),
    ]
    args = [bf, bf, bf]
    if cached:
        past = cache_k.shape[3]
        in_specs += [
            pl.BlockSpec((None, None, None, past, 128), lambda b, h, qi: (b, 0, h, 0, 0)),
            pl.BlockSpec((None, None, None, past, 128), lambda b, h, qi: (b, 0, h, 0, 0)),
        ]
        args += [cache_k, cache_v]
    in_specs += [
        pl.BlockSpec((4, DK_A), lambda b, h, qi: (0, 0)),
        pl.BlockSpec((1, DV_A), lambda b, h, qi: (0, 0)),
    ]
    args += [lam_p, diff_norm]
    return pl.pallas_call(
        functools.partial(_attn_kernel, cached=cached),
        out_shape=jax.ShapeDtypeStruct((batch * seq_len, H_A * DV_A), BF16),
        grid=(batch, H_A, nq),
        in_specs=in_specs,
        out_specs=pl.BlockSpec((tq, 128), lambda b, h, qi: (b * nq + qi, h)),
        compiler_params=pltpu.CompilerParams(
            dimension_semantics=("arbitrary", "arbitrary", "arbitrary"), vmem_limit_bytes=VMEM_LIMIT),
        name="attn_cached" if cached else "attn_ctx",
    )(*args)


def _split3(g):
    hi = g.astype(BF16)
    r1 = g - hi.astype(F32)
    mid = r1.astype(BF16)
    lo = (r1 - mid.astype(F32)).astype(BF16)
    return hi, mid, lo


def _gla_direction(q_ref, k_ref, g_ref, v_ref, o_ref, st_ref, n_chunks, reverse, first):
    C = GLA_CHUNK
    ri = lax.broadcasted_iota(jnp.int32, (C, C), 0)
    ci = lax.broadcasted_iota(jnp.int32, (C, C), 1)
    if reverse:
        keep = ci >= ri
        last, ref = 0, C // 2
    else:
        keep = ci <= ri
        last, ref = C - 1, C // 2 - 1
    tri = jnp.where(keep, 1.0, 0.0).astype(BF16)
    lane = lax.broadcasted_iota(jnp.int32, (1, 2 * DK_B), 1)
    h0 = jnp.where(lane < DK_B, 1.0, 0.0).astype(BF16)
    h1 = jnp.where(lane >= DK_B, 1.0, 0.0).astype(BF16)
    vlane = lax.broadcasted_iota(jnp.int32, (1, 2 * DV_B), 1)
    v0 = jnp.where(vlane < DV_B, 1.0, 0.0).astype(BF16)
    v1 = jnp.where(vlane >= DV_B, 1.0, 0.0).astype(BF16)
    br = lax.broadcasted_iota(jnp.int32, (2 * DV_B, 2 * DK_B), 0)
    bc = lax.broadcasted_iota(jnp.int32, (2 * DV_B, 2 * DK_B), 1)
    bdiag = (br < DV_B) == (bc < DK_B)

    def body(i, carry):
        c = (n_chunks - 1 - i) if reverse else i
        rows = pl.ds(pl.multiple_of(c * C, C), C)
        g = g_ref[rows, :]
        ghi, gmid, glo = _split3(g)
        b = _dot(tri, ghi) + _dot(tri, gmid) + _dot(tri, glo)
        bl = b[last:last + 1, :]
        bm = b[ref:ref + 1, :]
        q = q_ref[rows, :]
        k = k_ref[rows, :]
        v = v_ref[rows, :]
        qt = (q * jnp.exp(b - bm)).astype(BF16)
        kt = (k * jnp.exp(bm - b)).astype(BF16)
        att0 = jnp.where(keep, _dot_nt(qt * h0, kt), 0.0).astype(BF16)
        att1 = jnp.where(keep, _dot_nt(qt * h1, kt), 0.0).astype(BF16)
        o = _dot(att0, v * v0) + _dot(att1, v * v1)
        st = st_ref[...]
        qe = (q * jnp.exp(b)).astype(BF16)
        o = o + _dot_nt(qe, st.astype(BF16))
        if first:
            o_ref[rows, :] = o
        else:
            o_ref[rows, :] = o_ref[rows, :] + o
        kd = (k * jnp.exp(bl - b)).astype(BF16)
        ut = _dot_tn(v, kd)
        st_ref[...] = st * jnp.exp(bl) + jnp.where(bdiag, ut, 0.0)
        return carry

    lax.fori_loop(0, n_chunks, body, 0)


def _state_to_t(s_pair):
    z = jnp.zeros((DK_B, DV_B), F32)
    a = jnp.concatenate([s_pair[0], z], axis=0).T
    b = jnp.concatenate([z, s_pair[1]], axis=0).T
    return jnp.concatenate([a, b], axis=0)


def _gla_kernel(*refs, n_chunks, has_state):
    if has_state:
        q_ref, k_ref, gf_ref, gb_ref, rb_ref, v_ref, gn_ref, sf_ref, sb_ref, og_ref, o_acc, st = refs
    else:
        q_ref, k_ref, gf_ref, gb_ref, rb_ref, v_ref, gn_ref, og_ref, nsf_ref, nsb_ref, o_acc, st = refs

    def emit_state(dst_ref):
        s = st[...]
        dst_ref[0] = s[0:DV_B, :].T[0:DK_B, :]
        dst_ref[1] = s[DV_B:2 * DV_B, :].T[DK_B:2 * DK_B, :]

    st[...] = _state_to_t(sf_ref[...]) if has_state else jnp.zeros_like(st)
    _gla_direction(q_ref, k_ref, gf_ref, v_ref, o_acc, st, n_chunks, reverse=False, first=True)
    if not has_state:
        emit_state(nsf_ref)
    st[...] = _state_to_t(sb_ref[...]) if has_state else jnp.zeros_like(st)
    _gla_direction(q_ref, k_ref, gb_ref, v_ref, o_acc, st, n_chunks, reverse=True, first=False)
    if not has_state:
        emit_state(nsb_ref)

    L = n_chunks * GLA_CHUNK
    tr = 256

    def fin(i, carry):
        rows = pl.ds(pl.multiple_of(i * tr, tr), tr)
        o = o_acc[rows, :]
        r = rb_ref[rows, :]
        for j in range(2):
            oj = _rms(o[:, j * DV_B:(j + 1) * DV_B], gn_ref[...])
            rj = r[:, j * DV_B:(j + 1) * DV_B]
            og_ref[rows, j * DV_B:(j + 1) * DV_B] = (oj * (rj * jax.nn.sigmoid(rj))).astype(BF16)
        return carry

    lax.fori_loop(0, L // tr, fin, 0)


def _gla(f32a, bf, gla_norm, state_f, state_b, batch, seq_len):
    has_state = state_f is not None
    n_chunks = seq_len // GLA_CHUNK
    L = seq_len
    in_specs = [
        pl.BlockSpec((L, 128), lambda b, p: (b, p)),
        pl.BlockSpec((L, 128), lambda b, p: (b, 2 + p)),
        pl.BlockSpec((L, 128), lambda b, p: (b, 4 + p)),
        pl.BlockSpec((L, 128), lambda b, p: (b, 6 + p)),
        pl.BlockSpec((L, 256), lambda b, p: (b, 4 + p)),
        pl.BlockSpec((L, 256), lambda b, p: (b, 6 + p)),
        pl.BlockSpec((1, DV_B), lambda b, p: (0, 0)),
    ]
    args = [f32a, f32a, f32a, f32a, f32a, bf, gla_norm]
    out_shape = [jax.ShapeDtypeStruct((batch * L, H_B * DV_B), BF16)]
    out_specs = [pl.BlockSpec((L, 256), lambda b, p: (b, p))]
    st_spec = pl.BlockSpec((None, None, 2, DK_B, DV_B), lambda b, p: (b, 0, p, 0, 0))
    if has_state:
        in_specs += [st_spec, st_spec]
        args += [state_f, state_b]
    else:
        for _ in range(2):
            out_shape.append(jax.ShapeDtypeStruct((batch, 1, H_B, DK_B, DV_B), F32))
            out_specs.append(st_spec)
    return pl.pallas_call(
        functools.partial(_gla_kernel, n_chunks=n_chunks, has_state=has_state),
        out_shape=out_shape,
        grid=(batch, 2),
        in_specs=in_specs,
        out_specs=out_specs,
        scratch_shapes=[pltpu.VMEM((L, 2 * DV_B), F32), pltpu.VMEM((2 * DV_B, 2 * DK_B), F32)],
        compiler_params=pltpu.CompilerParams(
            dimension_semantics=("arbitrary", "arbitrary"), vmem_limit_bytes=VMEM_LIMIT),
        name="gla_state" if has_state else "gla_ctx",
    )(*args)


def _out_mlp_kernel(x_ref, oa_ref, og_ref, mod_ref, wo_ref, gpost_ref, gmpre_ref, gmpost_ref,
                    w1_ref, w2_ref, y_ref):
    x = x_ref[...]
    mix = _dot(oa_ref[...], wo_ref[0:512, :]) + _dot(og_ref[...], wo_ref[512:1024, :])
    x1 = x + mod_ref[2:3, :] * _rms(mix, gpost_ref[...])
    h2 = (_rms(x1, gmpre_ref[...]) * (1.0 + mod_ref[4:5, :]) + mod_ref[3:4, :]).astype(BF16)
    acc = None
    tf = 1024
    for j in range(D_FF // tf):
        u = jnp.maximum(_dot(h2, w1_ref[:, j * tf:(j + 1) * tf]), 0.0)
        part = _dot((u * u).astype(BF16), w2_ref[j * tf:(j + 1) * tf, :])
        acc = part if acc is None else acc + part
    y_ref[...] = x1 + mod_ref[5:6, :] * _rms(acc, gmpost_ref[...])


def _out_mlp(x2d, oa, og, mods, w_out_b, g_post, g_mpre, g_mpost, w1_b, w2_b, seq_len):
    T = x2d.shape[0]
    tm = 512
    per_b = max(seq_len // tm, 1)
    shared_mod = mods.shape[0] == 1
    mod_idx = (lambda i: (0, 0, 0)) if shared_mod else (lambda i: (i // per_b, 0, 0))
    const = lambda i: (0, 0)
    return pl.pallas_call(
        _out_mlp_kernel,
        out_shape=jax.ShapeDtypeStruct((T, D_MODEL), F32),
        grid=(T // tm,),
        in_specs=[
            pl.BlockSpec((tm, D_MODEL), lambda i: (i, 0)),
            pl.BlockSpec((tm, 512), lambda i: (i, 0)),
            pl.BlockSpec((tm, 512), lambda i: (i, 0)),
            pl.BlockSpec((None, N_MOD, D_MODEL), mod_idx),
            pl.BlockSpec((D_MODEL, D_MODEL), const, pipeline_mode=pl.Buffered(1)),
            pl.BlockSpec((1, D_MODEL), const),
            pl.BlockSpec((1, D_MODEL), const),
            pl.BlockSpec((1, D_MODEL), const),
            pl.BlockSpec((D_MODEL, D_FF), const, pipeline_mode=pl.Buffered(1)),
            pl.BlockSpec((D_FF, D_MODEL), const, pipeline_mode=pl.Buffered(1)),
        ],
        out_specs=pl.BlockSpec((tm, D_MODEL), lambda i: (i, 0)),
        compiler_params=pltpu.CompilerParams(
            dimension_semantics=("arbitrary",), vmem_limit_bytes=VMEM_LIMIT),
        name="out_mlp",
    )(x2d, oa, og, mods, w_out_b, g_post, g_mpre, g_mpost, w1_b, w2_b)


def _rope_tables(n_lat):
    pos = np.arange(n_lat)
    row_pos = (pos // GRID_W).astype(np.float64)
    col_pos = (pos % GRID_W).astype(np.float64)
    half = DK_A // 4
    inv = ROPE_BASE ** (-np.arange(half, dtype=np.float64) / half)
    lane = np.arange(128)
    in64 = lane % 64
    in32 = in64 % 32
    p = np.where((in64 < 32)[None, :], row_pos[:, None], col_pos[:, None])
    ang = p * inv[in32 % half][None, :]
    cos, sin = np.cos(ang), np.sin(ang)
    first = (in32 < half)[None, :]
    sa = np.where(first, -sin, 0.0)
    sb = np.where(first, 0.0, sin)
    return tuple(jnp.asarray(t, dtype=F32) for t in (cos, sa, sb))


def _layer(x, mods, w, rope_tabs, cache_k, cache_v, state_f, state_b):
    batch, seq_len, _ = x.shape
    x2d = x.reshape(batch * seq_len, D_MODEL)
    ctx = cache_k is None
    outs = _inproj(x2d, mods, w["g_pre"], w["w_in"], w["wg"], w["bg"], rope_tabs, seq_len, batch, ctx)
    bf, f32a = outs[0], outs[1]
    oa = _attention(bf, cache_k, cache_v, w["lam"], w["diff_norm"], batch, seq_len)
    gouts = _gla(f32a, bf, w["gla_norm"], state_f, state_b, batch, seq_len)
    og = gouts[0]
    y = _out_mlp(x2d, oa, og, mods, w["w_out"], w["g_post"], w["g_mpre"], w["g_mpost"],
                 w["w1"], w["w2"], seq_len)
    y = y.reshape(batch, seq_len, D_MODEL)
    if ctx:
        return y, outs[2], outs[3], gouts[1], gouts[2]
    return y


def kernel(x_prompt, x_sample, c, cache_k, cache_v, state_fwd, state_bwd, c_ctx, w_ada, b_ada,
           norm_attn_pre, norm_attn_post, norm_mlp_pre, norm_mlp_post, w_in, w_gate_fwd, b_gate_fwd,
           w_gate_bwd, b_gate_bwd, lam_q1, lam_k1, lam_q2, lam_k2, diff_norm, gla_norm, w_out,
           w_mlp1, w_mlp2):
    dec_batch = c.shape[0]
    rows = 16
    cvec = jnp.zeros((rows, D_MODEL), F32).at[0].set(c_ctx).at[1:1 + dec_batch].set(c)
    m = _adaln(cvec, w_ada[0], b_ada[0][None, :])
    mods_ctx = m[0:1].reshape(1, N_MOD, D_MODEL)
    mods_lat = m[1:1 + dec_batch].reshape(dec_batch, N_MOD, D_MODEL)

    wg = jnp.zeros((128, 512), F32)
    wg = wg.at[0:GATE_RANK, 0:256].set(w_gate_fwd[0]).at[GATE_RANK:2 * GATE_RANK, 256:512].set(w_gate_bwd[0])
    w = {
        "g_pre": norm_attn_pre[0][None, :],
        "g_post": norm_attn_post[0][None, :],
        "g_mpre": norm_mlp_pre[0][None, :],
        "g_mpost": norm_mlp_post[0][None, :],
        "w_in": jnp.pad(w_in[0], ((0, 0), (0, IN_COLS_PAD - IN_COLS))).astype(BF16),
        "wg": wg.astype(BF16),
        "bg": jnp.concatenate([b_gate_fwd[0], b_gate_bwd[0]])[None, :],
        "lam": jnp.stack([lam_q1[0], lam_k1[0], lam_q2[0], lam_k2[0]]),
        "diff_norm": diff_norm[0][None, :],
        "gla_norm": gla_norm[0][None, :],
        "w_out": w_out[0].astype(BF16),
        "w1": w_mlp1[0].astype(BF16),
        "w2": w_mlp2[0].astype(BF16),
    }
    y_prompt, new_k, new_v, new_sf, new_sb = _layer(x_prompt, mods_ctx, w, None, None, None, None, None)
    y_sample = _layer(x_sample, mods_lat, w, _rope_tables(x_sample.shape[1]),
                      cache_k, cache_v, state_fwd, state_bwd)
    return (y_prompt, y_sample, new_k, new_v, new_sf, new_sb)
```

```python
import functools
import math

import numpy as np
import jax
import jax.numpy as jnp
from jax import lax
from jax.experimental import pallas as pl
from jax.experimental.pallas import tpu as pltpu

F32 = jnp.float32
BF16 = jnp.bfloat16

D_MODEL = 1024
GRID_W = 64
H_A = 4
DV_A = 128
DK_A = 64
H_B = 4
DV_B = 128
DK_B = 64
GATE_RANK = 16
GATE_NORM = 16.0
GLA_CHUNK = 64
D_FF = 4 * D_MODEL
N_MOD = 6
ROPE_BASE = 10000.0
EPS = 1e-6
LAM_INIT = 0.8 - 0.6 * math.exp(-0.3 * 0)

C_QA, C_KA, C_VA, C_QB, C_KB, C_VB, C_RB, C_GL = 0, 512, 1024, 1536, 1792, 2048, 2560, 3072
IN_COLS = 3104

VMEM_LIMIT = 56 * 1024 * 1024


def _rms(x, g):
    return x * lax.rsqrt(jnp.mean(x * x, axis=-1, keepdims=True) + EPS) * g


def _dot(a, b):
    return jnp.dot(a, b, preferred_element_type=F32)


def _dot_nt(a, b):
    return lax.dot_general(a, b, (((1,), (1,)), ((), ())), preferred_element_type=F32)


def _dot_tn(a, b):
    return lax.dot_general(a, b, (((0,), (0,)), ((), ())), preferred_element_type=F32)


def _adaln_kernel(c_ref, w_ref, b_ref, o_ref):
    c = c_ref[...]
    s = c * jax.nn.sigmoid(c)
    o_ref[...] = _dot(s.astype(BF16), w_ref[...].astype(BF16)) + b_ref[...]


def _adaln(cvec, w_ada, b_ada):
    rows = cvec.shape[0]
    n = w_ada.shape[1]
    tn = 1536
    return pl.pallas_call(
        _adaln_kernel,
        out_shape=jax.ShapeDtypeStruct((rows, n), F32),
        grid=(n // tn,),
        in_specs=[
            pl.BlockSpec((rows, D_MODEL), lambda j: (0, 0)),
            pl.BlockSpec((D_MODEL, tn), lambda j: (0, j)),
            pl.BlockSpec((1, tn), lambda j: (0, j)),
        ],
        out_specs=pl.BlockSpec((rows, tn), lambda j: (0, j)),
        compiler_params=pltpu.CompilerParams(
            dimension_semantics=("arbitrary",), vmem_limit_bytes=VMEM_LIMIT),
        name="adaln",
    )(cvec, w_ada, b_ada)


def _inproj_kernel(*refs, rope, tm):
    if rope:
        (x_ref, mod_ref, g_ref, w_ref, wg_ref, bg_ref, cos_ref, sa_ref, sb_ref,
         bf_ref, f32_ref) = refs
    else:
        (x_ref, mod_ref, g_ref, w_ref, wg_ref, bg_ref,
         bf_ref, f32_ref, nk_ref, nv_ref) = refs
    x = x_ref[...]
    h = _rms(x, g_ref[...]) * (1.0 + mod_ref[1:2, :]) + mod_ref[0:1, :]
    p = _dot(h.astype(BF16), w_ref[...])

    for blk in range(8):
        t = p[:, blk * 128:(blk + 1) * 128]
        if rope:
            t = (t * cos_ref[...] + pltpu.roll(t, 112, 1) * sa_ref[...]
                 + pltpu.roll(t, 16, 1) * sb_ref[...])
        elif blk >= 4:
            nk_ref[blk - 4] = t
        bf_ref[:, blk * 128:(blk + 1) * 128] = t.astype(BF16)
    va = p[:, C_VA:C_VA + 512]
    bf_ref[:, C_VA:C_VA + 512] = va.astype(BF16)
    if not rope:
        for hh in range(H_A):
            nv_ref[hh] = va[:, hh * 128:(hh + 1) * 128]
    bf_ref[:, 1536:2048] = p[:, C_VB:C_VB + 512].astype(BF16)
    f32_ref[:, 0:256] = p[:, C_QB:C_QB + 256] * (DK_B ** -0.5)
    f32_ref[:, 256:512] = p[:, C_KB:C_KB + 256]
    z = _dot(p[:, C_GL:C_GL + 2 * GATE_RANK].astype(BF16), wg_ref[...]) + bg_ref[...]
    ls = jnp.minimum(z, 0.0) - jnp.log1p(jnp.exp(-jnp.abs(z)))
    f32_ref[:, 512:1024] = ls * (1.0 / GATE_NORM)
    f32_ref[:, 1024:1536] = p[:, C_RB:C_RB + 512]


def _inproj(x2d, mods, g_pre, w_in_b, wg, bg, rope_tabs, seq_len, batch, ctx_out):
    T = x2d.shape[0]
    tm = 256
    per_b = seq_len // tm
    shared_mod = mods.shape[0] == 1
    mod_idx = (lambda i: (0, 0, 0)) if shared_mod else (lambda i: (i // per_b, 0, 0))
    in_specs = [
        pl.BlockSpec((tm, D_MODEL), lambda i: (i, 0)),
        pl.BlockSpec((None, N_MOD, D_MODEL), mod_idx),
        pl.BlockSpec((1, D_MODEL), lambda i: (0, 0)),
        pl.BlockSpec((D_MODEL, IN_COLS), lambda i: (0, 0), pipeline_mode=pl.Buffered(1)),
        pl.BlockSpec((2 * GATE_RANK, 512), lambda i: (0, 0)),
        pl.BlockSpec((1, 512), lambda i: (0, 0)),
    ]
    args = [x2d, mods, g_pre, w_in_b, wg, bg]
    out_shape = [jax.ShapeDtypeStruct((T, 2048), BF16), jax.ShapeDtypeStruct((T, 1536), F32)]
    out_specs = [pl.BlockSpec((tm, 2048), lambda i: (i, 0)), pl.BlockSpec((tm, 1536), lambda i: (i, 0))]
    rope = rope_tabs is not None
    if rope:
        for t in rope_tabs:
            in_specs.append(pl.BlockSpec((tm, 128), lambda i: (i % per_b, 0)))
            args.append(t)
    if ctx_out:
        assert tm == seq_len
        for _ in range(2):
            out_shape.append(jax.ShapeDtypeStruct((batch, 1, H_A, seq_len, 128), F32))
            out_specs.append(pl.BlockSpec((None, None, H_A, seq_len, 128), lambda i: (i, 0, 0, 0, 0)))
    return pl.pallas_call(
        functools.partial(_inproj_kernel, rope=rope, tm=tm),
        out_shape=out_shape,
        grid=(T // tm,),
        in_specs=in_specs,
        out_specs=out_specs,
        compiler_params=pltpu.CompilerParams(
            dimension_semantics=("arbitrary",), vmem_limit_bytes=VMEM_LIMIT),
        name="inproj_rope" if rope else "inproj_ctx",
    )(*args)


def _attn_kernel(*refs, cached):
    if cached:
        q_ref, k_ref, v_ref, ck_ref, cv_ref, lam_ref, dn_ref, o_ref = refs
    else:
        q_ref, k_ref, v_ref, lam_ref, dn_ref, o_ref = refs
    lp = lam_ref[...]
    lam = (jnp.exp(jnp.sum(lp[0:1] * lp[1:2], axis=-1, keepdims=True))
           - jnp.exp(jnp.sum(lp[2:3] * lp[3:4], axis=-1, keepdims=True)) + LAM_INIT)
    lane = lax.broadcasted_iota(jnp.int32, (1, 128), 1)
    scale = DK_A ** -0.5
    m1 = jnp.where(lane < DK_A, scale, 0.0).astype(BF16)
    m2 = jnp.where(lane >= DK_A, scale, 0.0).astype(BF16)
    n_keys = k_ref.shape[0] + (ck_ref.shape[1] if cached else 0)
    ones = jnp.ones((n_keys, 128), BF16)

    def exp_scores(qm, k):
        s = _dot_nt(qm, k)
        return jnp.exp(s - jnp.max(s, axis=-1, keepdims=True)).astype(BF16)

    for h in range(H_A):
        cols = slice(h * 128, (h + 1) * 128)
        q = q_ref[:, cols]
        k = k_ref[:, cols]
        v = v_ref[:, cols]
        if cached:
            k = jnp.concatenate([ck_ref[h].astype(BF16), k], axis=0)
            v = jnp.concatenate([cv_ref[h].astype(BF16), v], axis=0)
        v1 = jnp.concatenate([v, ones], axis=1)
        r1 = _dot(exp_scores(q * m1, k), v1)
        r2 = _dot(exp_scores(q * m2, k), v1)
        o = r1[:, 0:128] / r1[:, 128:256] - lam * (r2[:, 0:128] / r2[:, 128:256])
        o_ref[:, cols] = (_rms(o, dn_ref[...]) * (1.0 - LAM_INIT)).astype(BF16)


def _attention(bf, cache_k, cache_v, lam_p, diff_norm, batch, seq_len):
    tq = 256
    nq = seq_len // tq
    cached = cache_k is not None
    in_specs = [
        pl.BlockSpec((tq, 512), lambda b, qi: (b * nq + qi, 0)),
        pl.BlockSpec((seq_len, 512), lambda b, qi: (b, 1)),
        pl.BlockSpec((seq_len, 512), lambda b, qi: (b, 2)),
    ]
    args = [bf, bf, bf]
    if cached:
        past = cache_k.shape[3]
        cspec = pl.BlockSpec((None, None, H_A, past, 128), lambda b, qi: (b, 0, 0, 0, 0))
        in_specs += [cspec, cspec]
        args += [cache_k, cache_v]
    in_specs += [
        pl.BlockSpec((4, DK_A), lambda b, qi: (0, 0)),
        pl.BlockSpec((1, DV_A), lambda b, qi: (0, 0)),
    ]
    args += [lam_p, diff_norm]
    return pl.pallas_call(
        functools.partial(_attn_kernel, cached=cached),
        out_shape=jax.ShapeDtypeStruct((batch * seq_len, H_A * DV_A), BF16),
        grid=(batch, nq),
        in_specs=in_specs,
        out_specs=pl.BlockSpec((tq, 512), lambda b, qi: (b * nq + qi, 0)),
        compiler_params=pltpu.CompilerParams(
            dimension_semantics=("arbitrary", "arbitrary"), vmem_limit_bytes=VMEM_LIMIT),
        name="attn_cached" if cached else "attn_ctx",
    )(*args)


def _split3(g):
    hi = g.astype(BF16)
    r1 = g - hi.astype(F32)
    mid = r1.astype(BF16)
    lo = (r1 - mid.astype(F32)).astype(BF16)
    return hi, mid, lo


def _gla_block(g, q, k, v, reverse):
    C = GLA_CHUNK
    R = g.shape[0]
    ri = lax.broadcasted_iota(jnp.int32, (R, R), 0)
    ci = lax.broadcasted_iota(jnp.int32, (R, R), 1)
    same_chunk = jnp.right_shift(ri, 6) == jnp.right_shift(ci, 6)
    if reverse:
        keep = same_chunk & (ci >= ri)
        last, ref = 0, C // 2
    else:
        keep = same_chunk & (ci <= ri)
        last, ref = C - 1, C // 2 - 1
    tri = jnp.where(keep, 1.0, 0.0).astype(BF16)
    lane = lax.broadcasted_iota(jnp.int32, (1, 2 * DK_B), 1)
    h0 = jnp.where(lane < DK_B, 1.0, 0.0).astype(BF16)
    h1 = jnp.where(lane >= DK_B, 1.0, 0.0).astype(BF16)

    hi, mid, lo = _split3(g)
    b3 = _dot(tri, jnp.concatenate([hi, mid, lo], axis=1))
    b = b3[:, 0:128] + b3[:, 128:256] + b3[:, 256:384]
    qt, kt, qe, kd, dec = [], [], [], [], []
    zeros = jnp.zeros((C, 2 * DK_B), BF16)
    for c in range(R // C):
        s = slice(c * C, (c + 1) * C)
        bc = b[s]
        bm = bc[ref:ref + 1]
        bl = bc[last:last + 1]
        qt.append((q[s] * jnp.exp(bc - bm)).astype(BF16))
        kt.append((k[s] * jnp.exp(bm - bc)).astype(BF16))
        qe.append((q[s] * jnp.exp(bc)).astype(BF16))
        kdc = (k[s] * jnp.exp(bl - bc)).astype(BF16)
        kd.append(jnp.concatenate([kdc if j == c else zeros for j in range(R // C)], axis=1))
        dec.append(jnp.exp(bl))
    qt = jnp.concatenate(qt, axis=0)
    kt = jnp.concatenate(kt, axis=0)
    a0 = jnp.where(keep, _dot_nt(qt * h0, kt), 0.0).astype(BF16)
    a1 = jnp.where(keep, _dot_nt(qt * h1, kt), 0.0).astype(BF16)
    o = jnp.concatenate([_dot(a0, v[:, 0:DV_B]), _dot(a1, v[:, DV_B:2 * DV_B])], axis=1)
    ut = _dot_tn(v, jnp.concatenate(kd, axis=0))
    br = lax.broadcasted_iota(jnp.int32, ut.shape, 0)
    bcol = lax.broadcasted_iota(jnp.int32, ut.shape, 1)
    ut = jnp.where((br < DV_B) == ((bcol & (2 * DK_B - 1)) < DK_B), ut, 0.0)
    return o, jnp.concatenate(qe, axis=0), ut, jnp.concatenate(dec, axis=0)


def _state_to_t(s_pair):
    z = jnp.zeros((DK_B, DV_B), F32)
    a = jnp.concatenate([s_pair[0], z], axis=0).T
    b = jnp.concatenate([z, s_pair[1]], axis=0).T
    return jnp.concatenate([a, b], axis=0)


def _gla_kernel(*refs, n_chunks, has_state):
    if has_state:
        (q_ref, k_ref, gf_ref, gb_ref, rb_ref, v_ref, gn_ref, sf_ref, sb_ref, og_ref, *scratch) = refs
    else:
        (q_ref, k_ref, gf_ref, gb_ref, rb_ref, v_ref, gn_ref, og_ref, nsf_ref, nsb_ref, *scratch) = refs
    o_acc, oi_f, oi_b, st_f, st_b, qe_f, qe_b, ut_f, ut_b, dec_f, dec_b = scratch
    C = GLA_CHUNK
    R = 256
    L = n_chunks * C
    cpb = R // C

    for blk in range(L // R):
        rows = slice(blk * R, (blk + 1) * R)
        q, k, v = q_ref[rows, :], k_ref[rows, :], v_ref[rows, :]
        o_sum = None
        for g_ref, qe_s, ut_s, dec_s, rev in ((gf_ref, qe_f, ut_f, dec_f, False), (gb_ref, qe_b, ut_b, dec_b, True)):
            o, qe, ut, dec = _gla_block(g_ref[rows, :], q, k, v, rev)
            qe_s[rows, :] = qe
            for c in range(cpb):
                ut_s[blk * cpb + c] = ut[:, c * 2 * DK_B:(c + 1) * 2 * DK_B]
                dec_s[blk * cpb + c:blk * cpb + c + 1, :] = dec[c:c + 1]
            o_sum = o if o_sum is None else o_sum + o
        o_acc[rows, :] = o_sum

    st_f[...] = _state_to_t(sf_ref[...]) if has_state else jnp.zeros_like(st_f)
    st_b[...] = _state_to_t(sb_ref[...]) if has_state else jnp.zeros_like(st_b)

    def scan(i, carry):
        for st, qe_s, ut_s, dec_s, oi, c in ((st_f, qe_f, ut_f, dec_f, oi_f, i),
                                             (st_b, qe_b, ut_b, dec_b, oi_b, n_chunks - 1 - i)):
            rows = pl.ds(pl.multiple_of(c * C, C), C)
            s = st[...]
            oi[rows, :] = _dot_nt(qe_s[rows, :], s.astype(BF16))
            st[...] = s * dec_s[pl.ds(c, 1), :] + ut_s[c]
        return carry

    lax.fori_loop(0, n_chunks, scan, 0, unroll=min(n_chunks, 4))

    if not has_state:
        for st, dst_ref in ((st_f, nsf_ref), (st_b, nsb_ref)):
            s = st[...]
            dst_ref[0] = s[0:DV_B, :].T[0:DK_B, :]
            dst_ref[1] = s[DV_B:2 * DV_B, :].T[DK_B:2 * DK_B, :]

    tr = 256

    def fin(i, carry):
        rows = pl.ds(pl.multiple_of(i * tr, tr), tr)
        o = o_acc[rows, :] + oi_f[rows, :] + oi_b[rows, :]
        r = rb_ref[rows, :]
        for j in range(2):
            oj = _rms(o[:, j * DV_B:(j + 1) * DV_B], gn_ref[...])
            rj = r[:, j * DV_B:(j + 1) * DV_B]
            og_ref[rows, j * DV_B:(j + 1) * DV_B] = (oj * (rj * jax.nn.sigmoid(rj))).astype(BF16)
        return carry

    lax.fori_loop(0, L // tr, fin, 0)


def _gla(f32a, bf, gla_norm, state_f, state_b, batch, seq_len):
    has_state = state_f is not None
    n_chunks = seq_len // GLA_CHUNK
    L = seq_len
    in_specs = [
        pl.BlockSpec((L, 128), lambda b, p: (b, p)),
        pl.BlockSpec((L, 128), lambda b, p: (b, 2 + p)),
        pl.BlockSpec((L, 128), lambda b, p: (b, 4 + p)),
        pl.BlockSpec((L, 128), lambda b, p: (b, 6 + p)),
        pl.BlockSpec((L, 256), lambda b, p: (b, 4 + p)),
        pl.BlockSpec((L, 256), lambda b, p: (b, 6 + p)),
        pl.BlockSpec((1, DV_B), lambda b, p: (0, 0)),
    ]
    args = [f32a, f32a, f32a, f32a, f32a, bf, gla_norm]
    out_shape = [jax.ShapeDtypeStruct((batch * L, H_B * DV_B), BF16)]
    out_specs = [pl.BlockSpec((L, 256), lambda b, p: (b, p))]
    st_spec = pl.BlockSpec((None, None, 2, DK_B, DV_B), lambda b, p: (b, 0, p, 0, 0))
    if has_state:
        in_specs += [st_spec, st_spec]
        args += [state_f, state_b]
    else:
        for _ in range(2):
            out_shape.append(jax.ShapeDtypeStruct((batch, 1, H_B, DK_B, DV_B), F32))
            out_specs.append(st_spec)
    return pl.pallas_call(
        functools.partial(_gla_kernel, n_chunks=n_chunks, has_state=has_state),
        out_shape=out_shape,
        grid=(batch, 2),
        in_specs=in_specs,
        out_specs=out_specs,
        scratch_shapes=(
            [pltpu.VMEM((L, 2 * DV_B), F32)] * 3
            + [pltpu.VMEM((2 * DV_B, 2 * DK_B), F32)] * 2
            + [pltpu.VMEM((L, 2 * DK_B), BF16)] * 2
            + [pltpu.VMEM((n_chunks, 2 * DV_B, 2 * DK_B), F32)] * 2
            + [pltpu.VMEM((max(n_chunks, 8), 2 * DK_B), F32)] * 2),
        compiler_params=pltpu.CompilerParams(
            dimension_semantics=("arbitrary", "arbitrary"), vmem_limit_bytes=VMEM_LIMIT),
        name="gla_state" if has_state else "gla_ctx",
    )(*args)


def _out_mlp_kernel(x_ref, oa_ref, og_ref, mod_ref, wo_ref, gpost_ref, gmpre_ref, gmpost_ref,
                    w1_ref, w2_ref, y_ref):
    x = x_ref[...]
    mix = _dot(oa_ref[...], wo_ref[0:512, :]) + _dot(og_ref[...], wo_ref[512:1024, :])
    x1 = x + mod_ref[2:3, :] * _rms(mix, gpost_ref[...])
    h2 = (_rms(x1, gmpre_ref[...]) * (1.0 + mod_ref[4:5, :]) + mod_ref[3:4, :]).astype(BF16)
    acc = None
    tf = 1024
    for j in range(D_FF // tf):
        u = jnp.maximum(_dot(h2, w1_ref[:, j * tf:(j + 1) * tf]), 0.0)
        part = _dot((u * u).astype(BF16), w2_ref[j * tf:(j + 1) * tf, :])
        acc = part if acc is None else acc + part
    y_ref[...] = x1 + mod_ref[5:6, :] * _rms(acc, gmpost_ref[...])


def _out_mlp(x2d, oa, og, mods, w_out_b, g_post, g_mpre, g_mpost, w1_b, w2_b, seq_len):
    T = x2d.shape[0]
    tm = 512
    per_b = max(seq_len // tm, 1)
    shared_mod = mods.shape[0] == 1
    mod_idx = (lambda i: (0, 0, 0)) if shared_mod else (lambda i: (i // per_b, 0, 0))
    const = lambda i: (0, 0)
    return pl.pallas_call(
        _out_mlp_kernel,
        out_shape=jax.ShapeDtypeStruct((T, D_MODEL), F32),
        grid=(T // tm,),
        in_specs=[
            pl.BlockSpec((tm, D_MODEL), lambda i: (i, 0)),
            pl.BlockSpec((tm, 512), lambda i: (i, 0)),
            pl.BlockSpec((tm, 512), lambda i: (i, 0)),
            pl.BlockSpec((None, N_MOD, D_MODEL), mod_idx),
            pl.BlockSpec((D_MODEL, D_MODEL), const, pipeline_mode=pl.Buffered(1)),
            pl.BlockSpec((1, D_MODEL), const),
            pl.BlockSpec((1, D_MODEL), const),
            pl.BlockSpec((1, D_MODEL), const),
            pl.BlockSpec((D_MODEL, D_FF), const, pipeline_mode=pl.Buffered(1)),
            pl.BlockSpec((D_FF, D_MODEL), const, pipeline_mode=pl.Buffered(1)),
        ],
        out_specs=pl.BlockSpec((tm, D_MODEL), lambda i: (i, 0)),
        compiler_params=pltpu.CompilerParams(
            dimension_semantics=("arbitrary",), vmem_limit_bytes=VMEM_LIMIT),
        name="out_mlp",
    )(x2d, oa, og, mods, w_out_b, g_post, g_mpre, g_mpost, w1_b, w2_b)


def _rope_tables(n_lat):
    pos = np.arange(n_lat)
    row_pos = (pos // GRID_W).astype(np.float64)
    col_pos = (pos % GRID_W).astype(np.float64)
    half = DK_A // 4
    inv = ROPE_BASE ** (-np.arange(half, dtype=np.float64) / half)
    lane = np.arange(128)
    in64 = lane % 64
    in32 = in64 % 32
    p = np.where((in64 < 32)[None, :], row_pos[:, None], col_pos[:, None])
    ang = p * inv[in32 % half][None, :]
    cos, sin = np.cos(ang), np.sin(ang)
    first = (in32 < half)[None, :]
    sa = np.where(first, -sin, 0.0)
    sb = np.where(first, 0.0, sin)
    return tuple(jnp.asarray(t, dtype=F32) for t in (cos, sa, sb))


def _layer(x, mods, w, rope_tabs, cache_k, cache_v, state_f, state_b):
    batch, seq_len, _ = x.shape
    x2d = x.reshape(batch * seq_len, D_MODEL)
    ctx = cache_k is None
    outs = _inproj(x2d, mods, w["g_pre"], w["w_in"], w["wg"], w["bg"], rope_tabs, seq_len, batch, ctx)
    bf, f32a = outs[0], outs[1]
    oa = _attention(bf, cache_k, cache_v, w["lam"], w["diff_norm"], batch, seq_len)
    gouts = _gla(f32a, bf, w["gla_norm"], state_f, state_b, batch, seq_len)
    og = gouts[0]
    y = _out_mlp(x2d, oa, og, mods, w["w_out"], w["g_post"], w["g_mpre"], w["g_mpost"],
                 w["w1"], w["w2"], seq_len)
    y = y.reshape(batch, seq_len, D_MODEL)
    if ctx:
        return y, outs[2], outs[3], gouts[1], gouts[2]
    return y


def kernel(x_prompt, x_sample, c, cache_k, cache_v, state_fwd, state_bwd, c_ctx, w_ada, b_ada,
           norm_attn_pre, norm_attn_post, norm_mlp_pre, norm_mlp_post, w_in, w_gate_fwd, b_gate_fwd,
           w_gate_bwd, b_gate_bwd, lam_q1, lam_k1, lam_q2, lam_k2, diff_norm, gla_norm, w_out,
           w_mlp1, w_mlp2):
    dec_batch = c.shape[0]
    rows = 16
    cvec = jnp.zeros((rows, D_MODEL), F32).at[0].set(c_ctx).at[1:1 + dec_batch].set(c)
    m = _adaln(cvec, w_ada[0], b_ada[0][None, :])
    mods_ctx = m[0:1].reshape(1, N_MOD, D_MODEL)
    mods_lat = m[1:1 + dec_batch].reshape(dec_batch, N_MOD, D_MODEL)

    wg = jnp.zeros((2 * GATE_RANK, 512), F32)
    wg = wg.at[0:GATE_RANK, 0:256].set(w_gate_fwd[0]).at[GATE_RANK:2 * GATE_RANK, 256:512].set(w_gate_bwd[0])
    w = {
        "g_pre": norm_attn_pre[0][None, :],
        "g_post": norm_attn_post[0][None, :],
        "g_mpre": norm_mlp_pre[0][None, :],
        "g_mpost": norm_mlp_post[0][None, :],
        "w_in": w_in[0].astype(BF16),
        "wg": wg.astype(BF16),
        "bg": jnp.concatenate([b_gate_fwd[0], b_gate_bwd[0]])[None, :],
        "lam": jnp.stack([lam_q1[0], lam_k1[0], lam_q2[0], lam_k2[0]]),
        "diff_norm": diff_norm[0][None, :],
        "gla_norm": gla_norm[0][None, :],
        "w_out": w_out[0].astype(BF16),
        "w1": w_mlp1[0].astype(BF16),
        "w2": w_mlp2[0].astype(BF16),
    }
    y_prompt, new_k, new_v, new_sf, new_sb = _layer(x_prompt, mods_ctx, w, None, None, None, None, None)
    y_sample = _layer(x_sample, mods_lat, w, _rope_tables(x_sample.shape[1]),
                      cache_k, cache_v, state_fwd, state_bwd)
    return (y_prompt, y_sample, new_k, new_v, new_sf, new_sb)
```

```python
import functools
import math

import numpy as np
import jax
import jax.numpy as jnp
from jax import lax
from jax.experimental import pallas as pl
from jax.experimental.pallas import tpu as pltpu

F32 = jnp.float32
BF16 = jnp.bfloat16

D_MODEL = 1024
GRID_W = 64
H_A = 4
DV_A = 128
DK_A = 64
H_B = 4
DV_B = 128
DK_B = 64
GATE_RANK = 16
GATE_NORM = 16.0
GLA_CHUNK = 64
D_FF = 4 * D_MODEL
N_MOD = 6
ROPE_BASE = 10000.0
EPS = 1e-6
LAM_INIT = 0.8 - 0.6 * math.exp(-0.3 * 0)

C_QA, C_KA, C_VA, C_QB, C_KB, C_VB, C_RB, C_GL = 0, 512, 1024, 1536, 1792, 2048, 2560, 3072
IN_COLS = 3104

VMEM_LIMIT = 56 * 1024 * 1024
ROW_CHUNK = 256


def _rms(x, g):
    return x * lax.rsqrt(jnp.mean(x * x, axis=-1, keepdims=True) + EPS) * g


def _dot(a, b):
    return jnp.dot(a, b, preferred_element_type=F32)


def _dot_nt(a, b):
    return lax.dot_general(a, b, (((1,), (1,)), ((), ())), preferred_element_type=F32)


def _run_skewed(gens):
    pending, running = list(gens), []
    while pending or running:
        if pending:
            running.append(pending.pop(0))
        for g in list(running):
            try:
                next(g)
            except StopIteration:
                running.remove(g)


def _adaln_kernel(c_ref, w_ref, b_ref, o_ref):
    c = c_ref[...]
    s = c * jax.nn.sigmoid(c)
    o_ref[...] = _dot(s.astype(BF16), w_ref[...].astype(BF16)) + b_ref[...]


def _adaln(cvec, w_ada, b_ada):
    rows = cvec.shape[0]
    n = w_ada.shape[1]
    tn = 1536
    return pl.pallas_call(
        _adaln_kernel,
        out_shape=jax.ShapeDtypeStruct((rows, n), F32),
        grid=(n // tn,),
        in_specs=[
            pl.BlockSpec((rows, D_MODEL), lambda j: (0, 0)),
            pl.BlockSpec((D_MODEL, tn), lambda j: (0, j)),
            pl.BlockSpec((1, tn), lambda j: (0, j)),
        ],
        out_specs=pl.BlockSpec((rows, tn), lambda j: (0, j)),
        compiler_params=pltpu.CompilerParams(
            dimension_semantics=("arbitrary",), vmem_limit_bytes=VMEM_LIMIT),
        name="adaln",
    )(cvec, w_ada, b_ada)


def _inproj_kernel(*refs, rope, seq):
    if rope:
        (x_ref, mod_ref, g_ref, w_ref, wg_ref, bg_ref, cos_ref, sa_ref, sb_ref,
         bf_ref, f32_ref) = refs
    else:
        (x_ref, mod_ref, g_ref, w_ref, wg_ref, bg_ref,
         bf_ref, f32_ref, nk_ref, nv_ref) = refs
    rc = ROW_CHUNK
    if not rope:
        assert rc == seq

    def chunk(r0):
        rows = slice(r0, r0 + rc)
        x = x_ref[rows, :]
        h = (_rms(x, g_ref[...]) * (1.0 + mod_ref[1:2, :]) + mod_ref[0:1, :]).astype(BF16)
        yield
        p = _dot(h, w_ref[...])
        yield
        for blk in range(8):
            t = p[:, blk * 128:(blk + 1) * 128]
            if rope:
                t = (t * cos_ref[rows, :] + pltpu.roll(t, 112, 1) * sa_ref[rows, :]
                     + pltpu.roll(t, 16, 1) * sb_ref[rows, :])
            elif blk >= 4:
                nk_ref[r0 // seq, blk - 4] = t
            bf_ref[rows, blk * 128:(blk + 1) * 128] = t.astype(BF16)
        va = p[:, C_VA:C_VA + 512]
        bf_ref[rows, C_VA:C_VA + 512] = va.astype(BF16)
        if not rope:
            for hh in range(H_A):
                nv_ref[r0 // seq, hh] = va[:, hh * 128:(hh + 1) * 128]
        bf_ref[rows, 1536:2048] = p[:, C_VB:C_VB + 512].astype(BF16)
        f32_ref[rows, 0:256] = p[:, C_QB:C_QB + 256] * (DK_B ** -0.5)
        f32_ref[rows, 256:512] = p[:, C_KB:C_KB + 256]
        z = _dot(p[:, C_GL:C_GL + 2 * GATE_RANK].astype(BF16), wg_ref[...]) + bg_ref[...]
        ls = jnp.minimum(z, 0.0) - jnp.log1p(jnp.exp(-jnp.abs(z)))
        f32_ref[rows, 512:1024] = ls * (1.0 / GATE_NORM)
        f32_ref[rows, 1024:1536] = p[:, C_RB:C_RB + 512]

    _run_skewed([chunk(r0) for r0 in range(0, x_ref.shape[0], rc)])


def _inproj(x2d, mods, g_pre, w_in_b, wg, bg, rope_tabs, seq_len, batch, ctx_out):
    T = x2d.shape[0]
    tm = 1024
    shared_mod = mods.shape[0] == 1
    assert (tm % seq_len == 0) if shared_mod else (seq_len % tm == 0)
    per_b = max(seq_len // tm, 1)
    mod_idx = (lambda i: (0, 0, 0)) if shared_mod else (lambda i: (i // per_b, 0, 0))
    in_specs = [
        pl.BlockSpec((tm, D_MODEL), lambda i: (i, 0)),
        pl.BlockSpec((None, N_MOD, D_MODEL), mod_idx),
        pl.BlockSpec((1, D_MODEL), lambda i: (0, 0)),
        pl.BlockSpec((D_MODEL, IN_COLS), lambda i: (0, 0), pipeline_mode=pl.Buffered(1)),
        pl.BlockSpec((2 * GATE_RANK, 512), lambda i: (0, 0)),
        pl.BlockSpec((1, 512), lambda i: (0, 0)),
    ]
    args = [x2d, mods, g_pre, w_in_b, wg, bg]
    out_shape = [jax.ShapeDtypeStruct((T, 2048), BF16), jax.ShapeDtypeStruct((T, 1536), F32)]
    out_specs = [pl.BlockSpec((tm, 2048), lambda i: (i, 0)), pl.BlockSpec((tm, 1536), lambda i: (i, 0))]
    rope = rope_tabs is not None
    if rope:
        for t in rope_tabs:
            in_specs.append(pl.BlockSpec((tm, 128), lambda i: (i % per_b, 0)))
            args.append(t)
    if ctx_out:
        nb = tm // seq_len
        for _ in range(2):
            out_shape.append(jax.ShapeDtypeStruct((batch, 1, H_A, seq_len, 128), F32))
            out_specs.append(pl.BlockSpec((nb, None, H_A, seq_len, 128), lambda i: (i, 0, 0, 0, 0)))
    return pl.pallas_call(
        functools.partial(_inproj_kernel, rope=rope, seq=seq_len),
        out_shape=out_shape,
        grid=(T // tm,),
        in_specs=in_specs,
        out_specs=out_specs,
        compiler_params=pltpu.CompilerParams(
            dimension_semantics=("arbitrary",), vmem_limit_bytes=VMEM_LIMIT),
        name="inproj_rope" if rope else "inproj_ctx",
    )(*args)


def _attn_kernel(*refs, cached):
    if cached:
        q_ref, k_ref, v_ref, ck_ref, cv_ref, lam_ref, dn_ref, o_ref = refs
    else:
        q_ref, k_ref, v_ref, lam_ref, dn_ref, o_ref = refs
    rc = ROW_CHUNK
    lp = lam_ref[...]
    lam = (jnp.exp(jnp.sum(lp[0:1] * lp[1:2], axis=-1, keepdims=True))
           - jnp.exp(jnp.sum(lp[2:3] * lp[3:4], axis=-1, keepdims=True)) + LAM_INIT)
    lane = lax.broadcasted_iota(jnp.int32, (1, 128), 1)
    scale = DK_A ** -0.5
    m1 = jnp.where(lane < DK_A, scale, 0.0).astype(BF16)
    m2 = jnp.where(lane >= DK_A, scale, 0.0).astype(BF16)
    n_keys = (k_ref.shape[0] + ck_ref.shape[1]) if cached else rc
    ones = jnp.ones((n_keys, 128), BF16)

    kv = {}
    res = {}

    def chain(r0, h, branch):
        rows = slice(r0, r0 + rc)
        cols = slice(h * 128, (h + 1) * 128)
        key = h if cached else (h, r0)
        if key not in kv:
            if cached:
                k = jnp.concatenate([ck_ref[h].astype(BF16), k_ref[:, cols]], axis=0)
                v = jnp.concatenate([cv_ref[h].astype(BF16), v_ref[:, cols]], axis=0)
            else:
                k, v = k_ref[rows, cols], v_ref[rows, cols]
            kv[key] = (k, jnp.concatenate([v, ones], axis=1))
        k, v1 = kv[key]
        s = _dot_nt(q_ref[rows, cols] * (m2 if branch else m1), k)
        yield
        e = jnp.exp(s - jnp.max(s, axis=-1, keepdims=True)).astype(BF16)
        yield
        res[r0, h, branch] = _dot(e, v1)
        yield
        if branch:
            r1, r2 = res[r0, h, 0], res[r0, h, 1]
            o = r1[:, 0:128] / r1[:, 128:256] - lam * (r2[:, 0:128] / r2[:, 128:256])
            o_ref[rows, cols] = (_rms(o, dn_ref[...]) * (1.0 - LAM_INIT)).astype(BF16)

    _run_skewed([chain(r0, h, br) for r0 in range(0, q_ref.shape[0], rc) for h in range(H_A) for br in range(2)])


def _attention(bf, cache_k, cache_v, lam_p, diff_norm, batch, seq_len):
    cached = cache_k is not None
    tq = 512
    n_tiles = batch * seq_len // tq
    if cached:
        per_b = seq_len // tq
        kv_idx = lambda c: (lambda i: (i // per_b, c))
        kv_rows = seq_len
    else:
        assert seq_len == ROW_CHUNK
        kv_idx = lambda c: (lambda i: (i, c))
        kv_rows = tq
    in_specs = [
        pl.BlockSpec((tq, 512), lambda i: (i, 0)),
        pl.BlockSpec((kv_rows, 512), kv_idx(1)),
        pl.BlockSpec((kv_rows, 512), kv_idx(2)),
    ]
    args = [bf, bf, bf]
    if cached:
        past = cache_k.shape[3]
        cspec = pl.BlockSpec((None, None, H_A, past, 128), lambda i: (i // per_b, 0, 0, 0, 0))
        in_specs += [cspec, cspec]
        args += [cache_k, cache_v]
    in_specs += [
        pl.BlockSpec((4, DK_A), lambda i: (0, 0)),
        pl.BlockSpec((1, DV_A), lambda i: (0, 0)),
    ]
    args += [lam_p, diff_norm]
    return pl.pallas_call(
        functools.partial(_attn_kernel, cached=cached),
        out_shape=jax.ShapeDtypeStruct((batch * seq_len, H_A * DV_A), BF16),
        grid=(n_tiles,),
        in_specs=in_specs,
        out_specs=pl.BlockSpec((tq, 512), lambda i: (i, 0)),
        compiler_params=pltpu.CompilerParams(
            dimension_semantics=("arbitrary",), vmem_limit_bytes=VMEM_LIMIT),
        name="attn_cached" if cached else "attn_ctx",
    )(*args)


def _split3(g):
    hi = g.astype(BF16)
    r1 = g - hi.astype(F32)
    mid = r1.astype(BF16)
    lo = (r1 - mid.astype(F32)).astype(BF16)
    return hi, mid, lo


def _gla_block_stages(q_ref, k_ref, g_ref, v_ref, vt_cache, rows, blk, reverse, oa_s, qe_s, ut_s, dec_s):
    C = GLA_CHUNK
    R = rows.stop - rows.start
    cpb = R // C
    ri = lax.broadcasted_iota(jnp.int32, (R, R), 0)
    ci = lax.broadcasted_iota(jnp.int32, (R, R), 1)
    same_chunk = jnp.right_shift(ri, 6) == jnp.right_shift(ci, 6)
    if reverse:
        keep = same_chunk & (ci >= ri)
        last, ref = 0, C // 2
    else:
        keep = same_chunk & (ci <= ri)
        last, ref = C - 1, C // 2 - 1
    tri = jnp.where(keep, 1.0, 0.0).astype(BF16)
    lane = lax.broadcasted_iota(jnp.int32, (1, 2 * DK_B), 1)
    h0 = jnp.where(lane < DK_B, 1.0, 0.0).astype(BF16)
    h1 = jnp.where(lane >= DK_B, 1.0, 0.0).astype(BF16)

    hi, mid, lo = _split3(g_ref[rows, :])
    b3 = _dot(tri, jnp.concatenate([hi, mid, lo], axis=1))
    yield
    b = b3[:, 0:128] + b3[:, 128:256] + b3[:, 256:384]
    q = q_ref[rows, :]
    k = k_ref[rows, :]
    qt, kt, qe, kd = [], [], [], []
    zeros = jnp.zeros((C, 2 * DK_B), BF16)
    for c in range(cpb):
        s = slice(c * C, (c + 1) * C)
        bc = b[s]
        bm = bc[ref:ref + 1]
        bl = bc[last:last + 1]
        qt.append((q[s] * jnp.exp(bc - bm)).astype(BF16))
        kt.append((k[s] * jnp.exp(bm - bc)).astype(BF16))
        qe.append((q[s] * jnp.exp(bc)).astype(BF16))
        kdc = (k[s] * jnp.exp(bl - bc)).astype(BF16)
        kd.append(jnp.concatenate([kdc if j == c else zeros for j in range(cpb)], axis=1))
        dec_s[blk * cpb + c:blk * cpb + c + 1, :] = jnp.exp(bl)
    qe_s[rows, :] = jnp.concatenate(qe, axis=0)
    qt = jnp.concatenate(qt, axis=0)
    kt = jnp.concatenate(kt, axis=0)
    yield
    s0 = _dot_nt(qt * h0, kt)
    s1 = _dot_nt(qt * h1, kt)
    v = v_ref[rows, :]
    if blk not in vt_cache:
        vt_cache[blk] = v.T
    ut = _dot(vt_cache[blk], jnp.concatenate(kd, axis=0))
    yield
    a0 = jnp.where(keep, s0, 0.0).astype(BF16)
    a1 = jnp.where(keep, s1, 0.0).astype(BF16)
    br = lax.broadcasted_iota(jnp.int32, ut.shape, 0)
    bcol = lax.broadcasted_iota(jnp.int32, ut.shape, 1)
    ut = jnp.where((br < DV_B) == ((bcol & (2 * DK_B - 1)) < DK_B), ut, 0.0)
    for c in range(cpb):
        ut_s[blk * cpb + c] = ut[:, c * 2 * DK_B:(c + 1) * 2 * DK_B]
    yield
    oa_s[rows, :] = jnp.concatenate([_dot(a0, v[:, 0:DV_B]), _dot(a1, v[:, DV_B:2 * DV_B])], axis=1)


def _state_to_t(s_pair):
    z = jnp.zeros((DK_B, DV_B), F32)
    a = jnp.concatenate([s_pair[0], z], axis=0).T
    b = jnp.concatenate([z, s_pair[1]], axis=0).T
    return jnp.concatenate([a, b], axis=0)


def _gla_kernel(*refs, n_chunks, has_state):
    if has_state:
        (q_ref, k_ref, gf_ref, gb_ref, rb_ref, v_ref, gn_ref, sf_ref, sb_ref, og_ref, *scratch) = refs
    else:
        (q_ref, k_ref, gf_ref, gb_ref, rb_ref, v_ref, gn_ref, og_ref, nsf_ref, nsb_ref, *scratch) = refs
    oa_f, oa_b, oi_f, oi_b, st_f, st_b, qe_f, qe_b, ut_f, ut_b, dec_f, dec_b = scratch
    C = GLA_CHUNK
    R = 256
    L = n_chunks * C

    vt_cache = {}
    gens = []
    for blk in range(L // R):
        rows = slice(blk * R, (blk + 1) * R)
        gens.append(_gla_block_stages(q_ref, k_ref, gf_ref, v_ref, vt_cache, rows, blk, False, oa_f, qe_f, ut_f, dec_f))
        gens.append(_gla_block_stages(q_ref, k_ref, gb_ref, v_ref, vt_cache, rows, blk, True, oa_b, qe_b, ut_b, dec_b))
    _run_skewed(gens)

    st_f[...] = _state_to_t(sf_ref[...]) if has_state else jnp.zeros_like(st_f)
    st_b[...] = _state_to_t(sb_ref[...]) if has_state else jnp.zeros_like(st_b)

    def scan(i, carry):
        for st, qe_s, ut_s, dec_s, oi, c in ((st_f, qe_f, ut_f, dec_f, oi_f, i),
                                             (st_b, qe_b, ut_b, dec_b, oi_b, n_chunks - 1 - i)):
            rows = pl.ds(pl.multiple_of(c * C, C), C)
            s = st[...]
            oi[rows, :] = _dot_nt(qe_s[rows, :], s.astype(BF16))
            st[...] = s * dec_s[pl.ds(c, 1), :] + ut_s[c]
        return carry

    lax.fori_loop(0, n_chunks, scan, 0, unroll=min(n_chunks, 4))

    if not has_state:
        for st, dst_ref in ((st_f, nsf_ref), (st_b, nsb_ref)):
            s = st[...]
            dst_ref[0] = s[0:DV_B, :].T[0:DK_B, :]
            dst_ref[1] = s[DV_B:2 * DV_B, :].T[DK_B:2 * DK_B, :]

    tr = 256

    def fin(i, carry):
        rows = pl.ds(pl.multiple_of(i * tr, tr), tr)
        o = (oa_f[rows, :] + oi_f[rows, :]) + (oa_b[rows, :] + oi_b[rows, :])
        r = rb_ref[rows, :]
        for j in range(2):
            oj = _rms(o[:, j * DV_B:(j + 1) * DV_B], gn_ref[...])
            rj = r[:, j * DV_B:(j + 1) * DV_B]
            og_ref[rows, j * DV_B:(j + 1) * DV_B] = (oj * (rj * jax.nn.sigmoid(rj))).astype(BF16)
        return carry

    lax.fori_loop(0, L // tr, fin, 0)


def _gla(f32a, bf, gla_norm, state_f, state_b, batch, seq_len):
    has_state = state_f is not None
    n_chunks = seq_len // GLA_CHUNK
    L = seq_len
    in_specs = [
        pl.BlockSpec((L, 128), lambda b, p: (b, p)),
        pl.BlockSpec((L, 128), lambda b, p: (b, 2 + p)),
        pl.BlockSpec((L, 128), lambda b, p: (b, 4 + p)),
        pl.BlockSpec((L, 128), lambda b, p: (b, 6 + p)),
        pl.BlockSpec((L, 256), lambda b, p: (b, 4 + p)),
        pl.BlockSpec((L, 256), lambda b, p: (b, 6 + p)),
        pl.BlockSpec((1, DV_B), lambda b, p: (0, 0)),
    ]
    args = [f32a, f32a, f32a, f32a, f32a, bf, gla_norm]
    out_shape = [jax.ShapeDtypeStruct((batch * L, H_B * DV_B), BF16)]
    out_specs = [pl.BlockSpec((L, 256), lambda b, p: (b, p))]
    st_spec = pl.BlockSpec((None, None, 2, DK_B, DV_B), lambda b, p: (b, 0, p, 0, 0))
    if has_state:
        in_specs += [st_spec, st_spec]
        args += [state_f, state_b]
    else:
        for _ in range(2):
            out_shape.append(jax.ShapeDtypeStruct((batch, 1, H_B, DK_B, DV_B), F32))
            out_specs.append(st_spec)
    return pl.pallas_call(
        functools.partial(_gla_kernel, n_chunks=n_chunks, has_state=has_state),
        out_shape=out_shape,
        grid=(batch, 2),
        in_specs=in_specs,
        out_specs=out_specs,
        scratch_shapes=(
            [pltpu.VMEM((L, 2 * DV_B), F32)] * 4
            + [pltpu.VMEM((2 * DV_B, 2 * DK_B), F32)] * 2
            + [pltpu.VMEM((L, 2 * DK_B), BF16)] * 2
            + [pltpu.VMEM((n_chunks, 2 * DV_B, 2 * DK_B), F32)] * 2
            + [pltpu.VMEM((max(n_chunks, 8), 2 * DK_B), F32)] * 2),
        compiler_params=pltpu.CompilerParams(
            dimension_semantics=("arbitrary", "arbitrary"), vmem_limit_bytes=VMEM_LIMIT),
        name="gla_state" if has_state else "gla_ctx",
    )(*args)


def _out_mlp_kernel(x_ref, oa_ref, og_ref, mod_ref, wo_ref, gpost_ref, gmpre_ref, gmpost_ref,
                    w1_ref, w2_ref, y_ref):
    rc = ROW_CHUNK
    tf = 1024

    def chunk(r0):
        rows = slice(r0, r0 + rc)
        mix = _dot(oa_ref[rows, :], wo_ref[0:512, :]) + _dot(og_ref[rows, :], wo_ref[512:1024, :])
        yield
        x1 = x_ref[rows, :] + mod_ref[2:3, :] * _rms(mix, gpost_ref[...])
        h2 = (_rms(x1, gmpre_ref[...]) * (1.0 + mod_ref[4:5, :]) + mod_ref[3:4, :]).astype(BF16)
        yield
        acc = None
        for j in range(D_FF // tf):
            u = jnp.maximum(_dot(h2, w1_ref[:, j * tf:(j + 1) * tf]), 0.0)
            part = _dot((u * u).astype(BF16), w2_ref[j * tf:(j + 1) * tf, :])
            acc = part if acc is None else acc + part
            yield
        y_ref[rows, :] = x1 + mod_ref[5:6, :] * _rms(acc, gmpost_ref[...])

    _run_skewed([chunk(r0) for r0 in range(0, x_ref.shape[0], rc)])


def _out_mlp(x2d, oa, og, mods, w_out_b, g_post, g_mpre, g_mpost, w1_b, w2_b, seq_len):
    T = x2d.shape[0]
    tm = 512
    per_b = max(seq_len // tm, 1)
    shared_mod = mods.shape[0] == 1
    mod_idx = (lambda i: (0, 0, 0)) if shared_mod else (lambda i: (i // per_b, 0, 0))
    const = lambda i: (0, 0)
    return pl.pallas_call(
        _out_mlp_kernel,
        out_shape=jax.ShapeDtypeStruct((T, D_MODEL), F32),
        grid=(T // tm,),
        in_specs=[
            pl.BlockSpec((tm, D_MODEL), lambda i: (i, 0)),
            pl.BlockSpec((tm, 512), lambda i: (i, 0)),
            pl.BlockSpec((tm, 512), lambda i: (i, 0)),
            pl.BlockSpec((None, N_MOD, D_MODEL), mod_idx),
            pl.BlockSpec((D_MODEL, D_MODEL), const, pipeline_mode=pl.Buffered(1)),
            pl.BlockSpec((1, D_MODEL), const),
            pl.BlockSpec((1, D_MODEL), const),
            pl.BlockSpec((1, D_MODEL), const),
            pl.BlockSpec((D_MODEL, D_FF), const, pipeline_mode=pl.Buffered(1)),
            pl.BlockSpec((D_FF, D_MODEL), const, pipeline_mode=pl.Buffered(1)),
        ],
        out_specs=pl.BlockSpec((tm, D_MODEL), lambda i: (i, 0)),
        compiler_params=pltpu.CompilerParams(
            dimension_semantics=("arbitrary",), vmem_limit_bytes=VMEM_LIMIT),
        name="out_mlp",
    )(x2d, oa, og, mods, w_out_b, g_post, g_mpre, g_mpost, w1_b, w2_b)


def _rope_tables(n_lat):
    pos = np.arange(n_lat)
    row_pos = (pos // GRID_W).astype(np.float64)
    col_pos = (pos % GRID_W).astype(np.float64)
    half = DK_A // 4
    inv = ROPE_BASE ** (-np.arange(half, dtype=np.float64) / half)
    lane = np.arange(128)
    in64 = lane % 64
    in32 = in64 % 32
    p = np.where((in64 < 32)[None, :], row_pos[:, None], col_pos[:, None])
    ang = p * inv[in32 % half][None, :]
    cos, sin = np.cos(ang), np.sin(ang)
    first = (in32 < half)[None, :]
    sa = np.where(first, -sin, 0.0)
    sb = np.where(first, 0.0, sin)
    return tuple(jnp.asarray(t, dtype=F32) for t in (cos, sa, sb))


def _layer(x, mods, w, rope_tabs, cache_k, cache_v, state_f, state_b):
    batch, seq_len, _ = x.shape
    x2d = x.reshape(batch * seq_len, D_MODEL)
    ctx = cache_k is None
    outs = _inproj(x2d, mods, w["g_pre"], w["w_in"], w["wg"], w["bg"], rope_tabs, seq_len, batch, ctx)
    bf, f32a = outs[0], outs[1]
    oa = _attention(bf, cache_k, cache_v, w["lam"], w["diff_norm"], batch, seq_len)
    gouts = _gla(f32a, bf, w["gla_norm"], state_f, state_b, batch, seq_len)
    og = gouts[0]
    y = _out_mlp(x2d, oa, og, mods, w["w_out"], w["g_post"], w["g_mpre"], w["g_mpost"],
                 w["w1"], w["w2"], seq_len)
    y = y.reshape(batch, seq_len, D_MODEL)
    if ctx:
        return y, outs[2], outs[3], gouts[1], gouts[2]
    return y


def kernel(x_prompt, x_sample, c, cache_k, cache_v, state_fwd, state_bwd, c_ctx, w_ada, b_ada,
           norm_attn_pre, norm_attn_post, norm_mlp_pre, norm_mlp_post, w_in, w_gate_fwd, b_gate_fwd,
           w_gate_bwd, b_gate_bwd, lam_q1, lam_k1, lam_q2, lam_k2, diff_norm, gla_norm, w_out,
           w_mlp1, w_mlp2):
    dec_batch = c.shape[0]
    rows = 16
    cvec = jnp.zeros((rows, D_MODEL), F32).at[0].set(c_ctx).at[1:1 + dec_batch].set(c)
    m = _adaln(cvec, w_ada[0], b_ada[0][None, :])
    mods_ctx = m[0:1].reshape(1, N_MOD, D_MODEL)
    mods_lat = m[1:1 + dec_batch].reshape(dec_batch, N_MOD, D_MODEL)

    wg = jnp.zeros((2 * GATE_RANK, 512), F32)
    wg = wg.at[0:GATE_RANK, 0:256].set(w_gate_fwd[0]).at[GATE_RANK:2 * GATE_RANK, 256:512].set(w_gate_bwd[0])
    w = {
        "g_pre": norm_attn_pre[0][None, :],
        "g_post": norm_attn_post[0][None, :],
        "g_mpre": norm_mlp_pre[0][None, :],
        "g_mpost": norm_mlp_post[0][None, :],
        "w_in": w_in[0].astype(BF16),
        "wg": wg.astype(BF16),
        "bg": jnp.concatenate([b_gate_fwd[0], b_gate_bwd[0]])[None, :],
        "lam": jnp.stack([lam_q1[0], lam_k1[0], lam_q2[0], lam_k2[0]]),
        "diff_norm": diff_norm[0][None, :],
        "gla_norm": gla_norm[0][None, :],
        "w_out": w_out[0].astype(BF16),
        "w1": w_mlp1[0].astype(BF16),
        "w2": w_mlp2[0].astype(BF16),
    }
    y_prompt, new_k, new_v, new_sf, new_sb = _layer(x_prompt, mods_ctx, w, None, None, None, None, None)
    y_sample = _layer(x_sample, mods_lat, w, _rope_tables(x_sample.shape[1]),
                      cache_k, cache_v, state_fwd, state_bwd)
    return (y_prompt, y_sample, new_k, new_v, new_sf, new_sb)
```

```python
import functools
import math

import numpy as np
import jax
import jax.numpy as jnp
from jax import lax
from jax.experimental import pallas as pl
from jax.experimental.pallas import tpu as pltpu

F32 = jnp.float32
BF16 = jnp.bfloat16

D_MODEL = 1024
GRID_W = 64
H_A = 4
DV_A = 128
DK_A = 64
H_B = 4
DV_B = 128
DK_B = 64
GATE_RANK = 16
GATE_NORM = 16.0
GLA_CHUNK = 64
D_FF = 4 * D_MODEL
N_MOD = 6
ROPE_BASE = 10000.0
EPS = 1e-6
LAM_INIT = 0.8 - 0.6 * math.exp(-0.3 * 0)

C_QA, C_KA, C_VA, C_QB, C_KB, C_VB, C_RB, C_GL = 0, 512, 1024, 1536, 1792, 2048, 2560, 3072
IN_COLS = 3104

V7X_VMEM_BYTES = 64 * 1024 * 1024
VMEM_LIMIT = V7X_VMEM_BYTES - 4 * 1024 * 1024
ROW_CHUNK = 256


def _rms(x, g):
    return x * lax.rsqrt(jnp.mean(x * x, axis=-1, keepdims=True) + EPS) * g


def _dot(a, b):
    return jnp.dot(a, b, preferred_element_type=F32)


def _dot_nt(a, b):
    return lax.dot_general(a, b, (((1,), (1,)), ((), ())), preferred_element_type=F32)


def _run_skewed(gens):
    pending, running = list(gens), []
    while pending or running:
        if pending:
            running.append(pending.pop(0))
        for g in list(running):
            try:
                next(g)
            except StopIteration:
                running.remove(g)


def _adaln_kernel(c_ref, w_ref, b_ref, o_ref):
    c = c_ref[...]
    s = c * jax.nn.sigmoid(c)
    o_ref[...] = _dot(s.astype(BF16), w_ref[...].astype(BF16)) + b_ref[...]


def _adaln(cvec, w_ada, b_ada):
    rows = cvec.shape[0]
    n = w_ada.shape[1]
    tn = 1536
    return pl.pallas_call(
        _adaln_kernel,
        out_shape=jax.ShapeDtypeStruct((rows, n), F32),
        grid=(n // tn,),
        in_specs=[
            pl.BlockSpec((rows, D_MODEL), lambda j: (0, 0)),
            pl.BlockSpec((D_MODEL, tn), lambda j: (0, j)),
            pl.BlockSpec((1, tn), lambda j: (0, j)),
        ],
        out_specs=pl.BlockSpec((rows, tn), lambda j: (0, j)),
        compiler_params=pltpu.CompilerParams(
            dimension_semantics=("arbitrary",), vmem_limit_bytes=VMEM_LIMIT),
        name="adaln",
    )(cvec, w_ada, b_ada)


def _inproj_kernel(*refs, rope, seq):
    if rope:
        (x_ref, mod_ref, g_ref, w_ref, wg_ref, bg_ref, cos_ref, sa_ref, sb_ref,
         bf_ref, f32_ref) = refs
    else:
        (x_ref, mod_ref, g_ref, w_ref, wg_ref, bg_ref,
         bf_ref, f32_ref, nk_ref, nv_ref) = refs
    rc = ROW_CHUNK
    if not rope:
        assert rc == seq

    def chunk(r0):
        rows = slice(r0, r0 + rc)
        x = x_ref[rows, :]
        h = (_rms(x, g_ref[...]) * (1.0 + mod_ref[1:2, :]) + mod_ref[0:1, :]).astype(BF16)
        yield
        p = _dot(h, w_ref[...])
        yield
        for blk in range(8):
            t = p[:, blk * 128:(blk + 1) * 128]
            if rope:
                t = (t * cos_ref[rows, :] + pltpu.roll(t, 112, 1) * sa_ref[rows, :]
                     + pltpu.roll(t, 16, 1) * sb_ref[rows, :])
            elif blk >= 4:
                nk_ref[r0 // seq, blk - 4] = t
            bf_ref[rows, blk * 128:(blk + 1) * 128] = t.astype(BF16)
        va = p[:, C_VA:C_VA + 512]
        bf_ref[rows, C_VA:C_VA + 512] = va.astype(BF16)
        if not rope:
            for hh in range(H_A):
                nv_ref[r0 // seq, hh] = va[:, hh * 128:(hh + 1) * 128]
        bf_ref[rows, 1536:2048] = p[:, C_VB:C_VB + 512].astype(BF16)
        f32_ref[rows, 0:256] = p[:, C_QB:C_QB + 256] * (DK_B ** -0.5)
        f32_ref[rows, 256:512] = p[:, C_KB:C_KB + 256]
        z = _dot(p[:, C_GL:C_GL + 2 * GATE_RANK].astype(BF16), wg_ref[...]) + bg_ref[...]
        ls = jnp.minimum(z, 0.0) - jnp.log1p(jnp.exp(-jnp.abs(z)))
        f32_ref[rows, 512:1024] = ls * (1.0 / GATE_NORM)
        f32_ref[rows, 1024:1536] = p[:, C_RB:C_RB + 512]

    _run_skewed([chunk(r0) for r0 in range(0, x_ref.shape[0], rc)])


def _inproj(x2d, mods, g_pre, w_in_b, wg, bg, rope_tabs, seq_len, batch, ctx_out):
    T = x2d.shape[0]
    tm = 1024
    shared_mod = mods.shape[0] == 1
    assert (tm % seq_len == 0) if shared_mod else (seq_len % tm == 0)
    per_b = max(seq_len // tm, 1)
    mod_idx = (lambda i: (0, 0, 0)) if shared_mod else (lambda i: (i // per_b, 0, 0))
    in_specs = [
        pl.BlockSpec((tm, D_MODEL), lambda i: (i, 0)),
        pl.BlockSpec((None, N_MOD, D_MODEL), mod_idx),
        pl.BlockSpec((1, D_MODEL), lambda i: (0, 0)),
        pl.BlockSpec((D_MODEL, IN_COLS), lambda i: (0, 0), pipeline_mode=pl.Buffered(1)),
        pl.BlockSpec((2 * GATE_RANK, 512), lambda i: (0, 0)),
        pl.BlockSpec((1, 512), lambda i: (0, 0)),
    ]
    args = [x2d, mods, g_pre, w_in_b, wg, bg]
    out_shape = [jax.ShapeDtypeStruct((T, 2048), BF16), jax.ShapeDtypeStruct((T, 1536), F32)]
    out_specs = [pl.BlockSpec((tm, 2048), lambda i: (i, 0)), pl.BlockSpec((tm, 1536), lambda i: (i, 0))]
    rope = rope_tabs is not None
    if rope:
        for t in rope_tabs:
            in_specs.append(pl.BlockSpec((tm, 128), lambda i: (i % per_b, 0)))
            args.append(t)
    if ctx_out:
        nb = tm // seq_len
        for _ in range(2):
            out_shape.append(jax.ShapeDtypeStruct((batch, 1, H_A, seq_len, 128), F32))
            out_specs.append(pl.BlockSpec((nb, None, H_A, seq_len, 128), lambda i: (i, 0, 0, 0, 0)))
    return pl.pallas_call(
        functools.partial(_inproj_kernel, rope=rope, seq=seq_len),
        out_shape=out_shape,
        grid=(T // tm,),
        in_specs=in_specs,
        out_specs=out_specs,
        compiler_params=pltpu.CompilerParams(
            dimension_semantics=("arbitrary",), vmem_limit_bytes=VMEM_LIMIT),
        name="inproj_rope" if rope else "inproj_ctx",
    )(*args)


def _attn_kernel(*refs, cached):
    if cached:
        q_ref, k_ref, v_ref, ck_ref, cv_ref, lam_ref, dn_ref, o_ref = refs
    else:
        q_ref, k_ref, v_ref, lam_ref, dn_ref, o_ref = refs
    rc = ROW_CHUNK
    lp = lam_ref[...]
    lam = (jnp.exp(jnp.sum(lp[0:1] * lp[1:2], axis=-1, keepdims=True))
           - jnp.exp(jnp.sum(lp[2:3] * lp[3:4], axis=-1, keepdims=True)) + LAM_INIT)
    lane = lax.broadcasted_iota(jnp.int32, (1, 128), 1)
    scale = DK_A ** -0.5
    m1 = jnp.where(lane < DK_A, scale, 0.0).astype(BF16)
    m2 = jnp.where(lane >= DK_A, scale, 0.0).astype(BF16)
    n_keys = (k_ref.shape[0] + ck_ref.shape[1]) if cached else rc
    ones = jnp.ones((n_keys, 128), BF16)

    kv = {}
    res = {}

    def chain(r0, h, branch):
        rows = slice(r0, r0 + rc)
        cols = slice(h * 128, (h + 1) * 128)
        key = h if cached else (h, r0)
        if key not in kv:
            if cached:
                k = jnp.concatenate([ck_ref[h].astype(BF16), k_ref[:, cols]], axis=0)
                v = jnp.concatenate([cv_ref[h].astype(BF16), v_ref[:, cols]], axis=0)
            else:
                k, v = k_ref[rows, cols], v_ref[rows, cols]
            kv[key] = (k, jnp.concatenate([v, ones], axis=1))
        k, v1 = kv[key]
        s = _dot_nt(q_ref[rows, cols] * (m2 if branch else m1), k)
        yield
        e = jnp.exp(s - jnp.max(s, axis=-1, keepdims=True)).astype(BF16)
        yield
        res[r0, h, branch] = _dot(e, v1)
        yield
        if branch:
            r1, r2 = res[r0, h, 0], res[r0, h, 1]
            o = r1[:, 0:128] / r1[:, 128:256] - lam * (r2[:, 0:128] / r2[:, 128:256])
            o_ref[rows, cols] = (_rms(o, dn_ref[...]) * (1.0 - LAM_INIT)).astype(BF16)

    _run_skewed([chain(r0, h, br) for r0 in range(0, q_ref.shape[0], rc) for h in range(H_A) for br in range(2)])


def _attention(bf, cache_k, cache_v, lam_p, diff_norm, batch, seq_len):
    cached = cache_k is not None
    tq = 512
    n_tiles = batch * seq_len // tq
    if cached:
        per_b = seq_len // tq
        kv_idx = lambda c: (lambda i: (i // per_b, c))
        kv_rows = seq_len
    else:
        assert seq_len == ROW_CHUNK
        kv_idx = lambda c: (lambda i: (i, c))
        kv_rows = tq
    in_specs = [
        pl.BlockSpec((tq, 512), lambda i: (i, 0)),
        pl.BlockSpec((kv_rows, 512), kv_idx(1)),
        pl.BlockSpec((kv_rows, 512), kv_idx(2)),
    ]
    args = [bf, bf, bf]
    if cached:
        past = cache_k.shape[3]
        cspec = pl.BlockSpec((None, None, H_A, past, 128), lambda i: (i // per_b, 0, 0, 0, 0))
        in_specs += [cspec, cspec]
        args += [cache_k, cache_v]
    in_specs += [
        pl.BlockSpec((4, DK_A), lambda i: (0, 0)),
        pl.BlockSpec((1, DV_A), lambda i: (0, 0)),
    ]
    args += [lam_p, diff_norm]
    return pl.pallas_call(
        functools.partial(_attn_kernel, cached=cached),
        out_shape=jax.ShapeDtypeStruct((batch * seq_len, H_A * DV_A), BF16),
        grid=(n_tiles,),
        in_specs=in_specs,
        out_specs=pl.BlockSpec((tq, 512), lambda i: (i, 0)),
        compiler_params=pltpu.CompilerParams(
            dimension_semantics=("arbitrary",), vmem_limit_bytes=VMEM_LIMIT),
        name="attn_cached" if cached else "attn_ctx",
    )(*args)


def _split3(g):
    hi = g.astype(BF16)
    r1 = g - hi.astype(F32)
    mid = r1.astype(BF16)
    lo = (r1 - mid.astype(F32)).astype(BF16)
    return hi, mid, lo


def _gla_block_stages(q_ref, k_ref, g_ref, v_ref, qc, vc, vt_cache, rows, blk, reverse, oa_s, qe_s, ut_s, dec_s):
    C = GLA_CHUNK
    R = rows.stop - rows.start
    cpb = R // C
    ri = lax.broadcasted_iota(jnp.int32, (R, R), 0)
    ci = lax.broadcasted_iota(jnp.int32, (R, R), 1)
    same_chunk = jnp.right_shift(ri, 6) == jnp.right_shift(ci, 6)
    if reverse:
        keep = same_chunk & (ci >= ri)
        last, ref = 0, C // 2
    else:
        keep = same_chunk & (ci <= ri)
        last, ref = C - 1, C // 2 - 1
    tri = jnp.where(keep, 1.0, 0.0).astype(BF16)
    lane = lax.broadcasted_iota(jnp.int32, (1, 2 * DK_B), 1)
    h0 = jnp.where(lane < DK_B, 1.0, 0.0).astype(BF16)
    h1 = jnp.where(lane >= DK_B, 1.0, 0.0).astype(BF16)

    hi, mid, lo = _split3(g_ref[rows, qc])
    b3 = _dot(tri, jnp.concatenate([hi, mid, lo], axis=1))
    yield
    b = b3[:, 0:128] + b3[:, 128:256] + b3[:, 256:384]
    q = q_ref[rows, qc]
    k = k_ref[rows, qc]
    qt, kt, qe, kd = [], [], [], []
    zeros = jnp.zeros((C, 2 * DK_B), BF16)
    for c in range(cpb):
        s = slice(c * C, (c + 1) * C)
        bc = b[s]
        bm = bc[ref:ref + 1]
        bl = bc[last:last + 1]
        qt.append((q[s] * jnp.exp(bc - bm)).astype(BF16))
        kt.append((k[s] * jnp.exp(bm - bc)).astype(BF16))
        qe.append((q[s] * jnp.exp(bc)).astype(BF16))
        kdc = (k[s] * jnp.exp(bl - bc)).astype(BF16)
        kd.append(jnp.concatenate([kdc if j == c else zeros for j in range(cpb)], axis=1))
        dec_s[blk * cpb + c:blk * cpb + c + 1, :] = jnp.exp(bl)
    qe_s[rows, :] = jnp.concatenate(qe, axis=0)
    qt = jnp.concatenate(qt, axis=0)
    kt = jnp.concatenate(kt, axis=0)
    yield
    s0 = _dot_nt(qt * h0, kt)
    s1 = _dot_nt(qt * h1, kt)
    v = v_ref[rows, vc]
    if (blk, vc.start) not in vt_cache:
        vt_cache[blk, vc.start] = v.T
    ut = _dot(vt_cache[blk, vc.start], jnp.concatenate(kd, axis=0))
    yield
    a0 = jnp.where(keep, s0, 0.0).astype(BF16)
    a1 = jnp.where(keep, s1, 0.0).astype(BF16)
    br = lax.broadcasted_iota(jnp.int32, ut.shape, 0)
    bcol = lax.broadcasted_iota(jnp.int32, ut.shape, 1)
    ut = jnp.where((br < DV_B) == ((bcol & (2 * DK_B - 1)) < DK_B), ut, 0.0)
    for c in range(cpb):
        ut_s[blk * cpb + c] = ut[:, c * 2 * DK_B:(c + 1) * 2 * DK_B]
    yield
    oa_s[rows, :] = jnp.concatenate([_dot(a0, v[:, 0:DV_B]), _dot(a1, v[:, DV_B:2 * DV_B])], axis=1)


def _state_to_t(s_pair):
    z = jnp.zeros((DK_B, DV_B), F32)
    a = jnp.concatenate([s_pair[0], z], axis=0).T
    b = jnp.concatenate([z, s_pair[1]], axis=0).T
    return jnp.concatenate([a, b], axis=0)


def _gla_kernel(*refs, n_chunks, has_state):
    if has_state:
        (q_ref, k_ref, gf_ref, gb_ref, rb_ref, v_ref, gn_ref, sf_ref, sb_ref, og_ref, *scratch) = refs
    else:
        (q_ref, k_ref, gf_ref, gb_ref, rb_ref, v_ref, gn_ref, og_ref, nsf_ref, nsb_ref, *scratch) = refs
    oa, oi, st, qe, ut, dec = scratch
    C = GLA_CHUNK
    R = 256
    L = n_chunks * C
    n_pairs = H_B // 2
    chains = [(p, d) for p in range(n_pairs) for d in range(2)]

    vt_cache = {}
    gens = []
    for blk in range(L // R):
        rows = slice(blk * R, (blk + 1) * R)
        for p, d in chains:
            ch = 2 * p + d
            gens.append(_gla_block_stages(
                q_ref, k_ref, gb_ref if d else gf_ref, v_ref,
                slice(p * 2 * DK_B, (p + 1) * 2 * DK_B), slice(p * 2 * DV_B, (p + 1) * 2 * DV_B),
                vt_cache, rows, blk, bool(d), oa.at[ch], qe.at[ch], ut.at[ch], dec.at[ch]))
    _run_skewed(gens)

    for p, d in chains:
        if has_state:
            st[2 * p + d] = _state_to_t((sb_ref if d else sf_ref)[2 * p:2 * p + 2])
        else:
            st[2 * p + d] = jnp.zeros(st.shape[1:], F32)

    def scan(i, carry):
        for p, d in chains:
            ch = 2 * p + d
            c = (n_chunks - 1 - i) if d else i
            rows = pl.ds(pl.multiple_of(c * C, C), C)
            s = st[ch]
            oi[ch, rows, :] = _dot_nt(qe[ch, rows, :], s.astype(BF16))
            st[ch] = s * dec[ch, pl.ds(c, 1), :] + ut[ch, c]
        return carry

    lax.fori_loop(0, n_chunks, scan, 0, unroll=min(n_chunks, 4))

    if not has_state:
        for p, d in chains:
            s = st[2 * p + d]
            dst_ref = nsb_ref if d else nsf_ref
            dst_ref[2 * p] = s[0:DV_B, :].T[0:DK_B, :]
            dst_ref[2 * p + 1] = s[DV_B:2 * DV_B, :].T[DK_B:2 * DK_B, :]

    tr = 256

    def fin(i, carry):
        rows = pl.ds(pl.multiple_of(i * tr, tr), tr)
        for p in range(n_pairs):
            o = (oa[2 * p, rows, :] + oi[2 * p, rows, :]) + (oa[2 * p + 1, rows, :] + oi[2 * p + 1, rows, :])
            for j in range(2):
                cols = slice((2 * p + j) * DV_B, (2 * p + j + 1) * DV_B)
                oj = _rms(o[:, j * DV_B:(j + 1) * DV_B], gn_ref[...])
                rj = rb_ref[rows, cols]
                og_ref[rows, cols] = (oj * (rj * jax.nn.sigmoid(rj))).astype(BF16)
        return carry

    lax.fori_loop(0, L // tr, fin, 0)


def _gla(f32a, bf, gla_norm, state_f, state_b, batch, seq_len):
    has_state = state_f is not None
    n_chunks = seq_len // GLA_CHUNK
    L = seq_len
    n_ch = H_B
    wk, wv = H_B * DK_B, H_B * DV_B
    in_specs = [
        pl.BlockSpec((L, wk), lambda b: (b, 0)),
        pl.BlockSpec((L, wk), lambda b: (b, 1)),
        pl.BlockSpec((L, wk), lambda b: (b, 2)),
        pl.BlockSpec((L, wk), lambda b: (b, 3)),
        pl.BlockSpec((L, wv), lambda b: (b, 2)),
        pl.BlockSpec((L, wv), lambda b: (b, 3)),
        pl.BlockSpec((1, DV_B), lambda b: (0, 0)),
    ]
    args = [f32a, f32a, f32a, f32a, f32a, bf, gla_norm]
    out_shape = [jax.ShapeDtypeStruct((batch * L, wv), BF16)]
    out_specs = [pl.BlockSpec((L, wv), lambda b: (b, 0))]
    st_spec = pl.BlockSpec((None, None, H_B, DK_B, DV_B), lambda b: (b, 0, 0, 0, 0))
    if has_state:
        in_specs += [st_spec, st_spec]
        args += [state_f, state_b]
    else:
        for _ in range(2):
            out_shape.append(jax.ShapeDtypeStruct((batch, 1, H_B, DK_B, DV_B), F32))
            out_specs.append(st_spec)
    return pl.pallas_call(
        functools.partial(_gla_kernel, n_chunks=n_chunks, has_state=has_state),
        out_shape=out_shape,
        grid=(batch,),
        in_specs=in_specs,
        out_specs=out_specs,
        scratch_shapes=[
            pltpu.VMEM((n_ch, L, 2 * DV_B), F32),
            pltpu.VMEM((n_ch, L, 2 * DV_B), F32),
            pltpu.VMEM((n_ch, 2 * DV_B, 2 * DK_B), F32),
            pltpu.VMEM((n_ch, L, 2 * DK_B), BF16),
            pltpu.VMEM((n_ch, n_chunks, 2 * DV_B, 2 * DK_B), F32),
            pltpu.VMEM((n_ch, max(n_chunks, 8), 2 * DK_B), F32),
        ],
        compiler_params=pltpu.CompilerParams(
            dimension_semantics=("arbitrary",), vmem_limit_bytes=VMEM_LIMIT),
        name="gla_state" if has_state else "gla_ctx",
    )(*args)


def _out_mlp_kernel(x_ref, oa_ref, og_ref, mod_ref, wo_ref, gpost_ref, gmpre_ref, gmpost_ref,
                    w1_ref, w2_ref, y_ref):
    rc = ROW_CHUNK
    tf = 1024

    def chunk(r0):
        rows = slice(r0, r0 + rc)
        mix = _dot(oa_ref[rows, :], wo_ref[0:512, :]) + _dot(og_ref[rows, :], wo_ref[512:1024, :])
        yield
        x1 = x_ref[rows, :] + mod_ref[2:3, :] * _rms(mix, gpost_ref[...])
        h2 = (_rms(x1, gmpre_ref[...]) * (1.0 + mod_ref[4:5, :]) + mod_ref[3:4, :]).astype(BF16)
        yield
        acc = None
        for j in range(D_FF // tf):
            u = jnp.maximum(_dot(h2, w1_ref[:, j * tf:(j + 1) * tf]), 0.0)
            part = _dot((u * u).astype(BF16), w2_ref[j * tf:(j + 1) * tf, :])
            acc = part if acc is None else acc + part
            yield
        y_ref[rows, :] = x1 + mod_ref[5:6, :] * _rms(acc, gmpost_ref[...])

    _run_skewed([chunk(r0) for r0 in range(0, x_ref.shape[0], rc)])


def _out_mlp(x2d, oa, og, mods, w_out_b, g_post, g_mpre, g_mpost, w1_b, w2_b, seq_len):
    T = x2d.shape[0]
    tm = 1024
    per_b = max(seq_len // tm, 1)
    shared_mod = mods.shape[0] == 1
    mod_idx = (lambda i: (0, 0, 0)) if shared_mod else (lambda i: (i // per_b, 0, 0))
    const = lambda i: (0, 0)
    return pl.pallas_call(
        _out_mlp_kernel,
        out_shape=jax.ShapeDtypeStruct((T, D_MODEL), F32),
        grid=(T // tm,),
        in_specs=[
            pl.BlockSpec((tm, D_MODEL), lambda i: (i, 0)),
            pl.BlockSpec((tm, 512), lambda i: (i, 0)),
            pl.BlockSpec((tm, 512), lambda i: (i, 0)),
            pl.BlockSpec((None, N_MOD, D_MODEL), mod_idx),
            pl.BlockSpec((D_MODEL, D_MODEL), const, pipeline_mode=pl.Buffered(1)),
            pl.BlockSpec((1, D_MODEL), const),
            pl.BlockSpec((1, D_MODEL), const),
            pl.BlockSpec((1, D_MODEL), const),
            pl.BlockSpec((D_MODEL, D_FF), const, pipeline_mode=pl.Buffered(1)),
            pl.BlockSpec((D_FF, D_MODEL), const, pipeline_mode=pl.Buffered(1)),
        ],
        out_specs=pl.BlockSpec((tm, D_MODEL), lambda i: (i, 0)),
        compiler_params=pltpu.CompilerParams(
            dimension_semantics=("arbitrary",), vmem_limit_bytes=VMEM_LIMIT),
        name="out_mlp",
    )(x2d, oa, og, mods, w_out_b, g_post, g_mpre, g_mpost, w1_b, w2_b)


def _rope_tables(n_lat):
    pos = np.arange(n_lat)
    row_pos = (pos // GRID_W).astype(np.float64)
    col_pos = (pos % GRID_W).astype(np.float64)
    half = DK_A // 4
    inv = ROPE_BASE ** (-np.arange(half, dtype=np.float64) / half)
    lane = np.arange(128)
    in64 = lane % 64
    in32 = in64 % 32
    p = np.where((in64 < 32)[None, :], row_pos[:, None], col_pos[:, None])
    ang = p * inv[in32 % half][None, :]
    cos, sin = np.cos(ang), np.sin(ang)
    first = (in32 < half)[None, :]
    sa = np.where(first, -sin, 0.0)
    sb = np.where(first, 0.0, sin)
    return tuple(jnp.asarray(t, dtype=F32) for t in (cos, sa, sb))


def _layer(x, mods, w, rope_tabs, cache_k, cache_v, state_f, state_b):
    batch, seq_len, _ = x.shape
    x2d = x.reshape(batch * seq_len, D_MODEL)
    ctx = cache_k is None
    outs = _inproj(x2d, mods, w["g_pre"], w["w_in"], w["wg"], w["bg"], rope_tabs, seq_len, batch, ctx)
    bf, f32a = outs[0], outs[1]
    oa = _attention(bf, cache_k, cache_v, w["lam"], w["diff_norm"], batch, seq_len)
    gouts = _gla(f32a, bf, w["gla_norm"], state_f, state_b, batch, seq_len)
    og = gouts[0]
    y = _out_mlp(x2d, oa, og, mods, w["w_out"], w["g_post"], w["g_mpre"], w["g_mpost"],
                 w["w1"], w["w2"], seq_len)
    y = y.reshape(batch, seq_len, D_MODEL)
    if ctx:
        return y, outs[2], outs[3], gouts[1], gouts[2]
    return y


def kernel(x_prompt, x_sample, c, cache_k, cache_v, state_fwd, state_bwd, c_ctx, w_ada, b_ada,
           norm_attn_pre, norm_attn_post, norm_mlp_pre, norm_mlp_post, w_in, w_gate_fwd, b_gate_fwd,
           w_gate_bwd, b_gate_bwd, lam_q1, lam_k1, lam_q2, lam_k2, diff_norm, gla_norm, w_out,
           w_mlp1, w_mlp2):
    dec_batch = c.shape[0]
    rows = 16
    cvec = jnp.zeros((rows, D_MODEL), F32).at[0].set(c_ctx).at[1:1 + dec_batch].set(c)
    m = _adaln(cvec, w_ada[0], b_ada[0][None, :])
    mods_ctx = m[0:1].reshape(1, N_MOD, D_MODEL)
    mods_lat = m[1:1 + dec_batch].reshape(dec_batch, N_MOD, D_MODEL)

    wg = jnp.zeros((2 * GATE_RANK, 512), F32)
    wg = wg.at[0:GATE_RANK, 0:256].set(w_gate_fwd[0]).at[GATE_RANK:2 * GATE_RANK, 256:512].set(w_gate_bwd[0])
    w = {
        "g_pre": norm_attn_pre[0][None, :],
        "g_post": norm_attn_post[0][None, :],
        "g_mpre": norm_mlp_pre[0][None, :],
        "g_mpost": norm_mlp_post[0][None, :],
        "w_in": w_in[0].astype(BF16),
        "wg": wg.astype(BF16),
        "bg": jnp.concatenate([b_gate_fwd[0], b_gate_bwd[0]])[None, :],
        "lam": jnp.stack([lam_q1[0], lam_k1[0], lam_q2[0], lam_k2[0]]),
        "diff_norm": diff_norm[0][None, :],
        "gla_norm": gla_norm[0][None, :],
        "w_out": w_out[0].astype(BF16),
        "w1": w_mlp1[0].astype(BF16),
        "w2": w_mlp2[0].astype(BF16),
    }
    y_prompt, new_k, new_v, new_sf, new_sb = _layer(x_prompt, mods_ctx, w, None, None, None, None, None)
    y_sample = _layer(x_sample, mods_lat, w, _rope_tables(x_sample.shape[1]),
                      cache_k, cache_v, state_fwd, state_bwd)
    return (y_prompt, y_sample, new_k, new_v, new_sf, new_sb)
```

```python
import functools
import math

import numpy as np
import jax
import jax.numpy as jnp
from jax import lax
from jax.experimental import pallas as pl
from jax.experimental.pallas import tpu as pltpu

F32 = jnp.float32
BF16 = jnp.bfloat16

D_MODEL = 1024
GRID_W = 64
H_A = 4
DV_A = 128
DK_A = 64
H_B = 4
DV_B = 128
DK_B = 64
GATE_RANK = 16
GATE_NORM = 16.0
GLA_CHUNK = 64
D_FF = 4 * D_MODEL
N_MOD = 6
ROPE_BASE = 10000.0
EPS = 1e-6
LAM_INIT = 0.8 - 0.6 * math.exp(-0.3 * 0)

C_QA, C_KA, C_VA, C_QB, C_KB, C_VB, C_RB, C_GL = 0, 512, 1024, 1536, 1792, 2048, 2560, 3072
IN_COLS = 3104

V7X_VMEM_BYTES = 64 * 1024 * 1024
VMEM_LIMIT = V7X_VMEM_BYTES - 4 * 1024 * 1024
ROW_CHUNK = 256


def _rms(x, g):
    return x * lax.rsqrt(jnp.mean(x * x, axis=-1, keepdims=True) + EPS) * g


def _dot(a, b):
    return jnp.dot(a, b, preferred_element_type=F32)


def _dot_nt(a, b):
    return lax.dot_general(a, b, (((1,), (1,)), ((), ())), preferred_element_type=F32)


def _run_skewed(gens):
    pending, running = list(gens), []
    while pending or running:
        if pending:
            running.append(pending.pop(0))
        for g in list(running):
            try:
                next(g)
            except StopIteration:
                running.remove(g)


def _adaln_kernel(c_ref, w_ref, b_ref, o_ref):
    c = c_ref[...]
    s = c * jax.nn.sigmoid(c)
    o_ref[...] = _dot(s.astype(BF16), w_ref[...].astype(BF16)) + b_ref[...]


def _adaln(cvec, w_ada, b_ada):
    rows = cvec.shape[0]
    n = w_ada.shape[1]
    tn = 1536
    return pl.pallas_call(
        _adaln_kernel,
        out_shape=jax.ShapeDtypeStruct((rows, n), F32),
        grid=(n // tn,),
        in_specs=[
            pl.BlockSpec((rows, D_MODEL), lambda j: (0, 0)),
            pl.BlockSpec((D_MODEL, tn), lambda j: (0, j)),
            pl.BlockSpec((1, tn), lambda j: (0, j)),
        ],
        out_specs=pl.BlockSpec((rows, tn), lambda j: (0, j)),
        compiler_params=pltpu.CompilerParams(
            dimension_semantics=("arbitrary",), vmem_limit_bytes=VMEM_LIMIT),
        name="adaln",
    )(cvec, w_ada, b_ada)


def _inproj_kernel(*refs, rope, seq):
    if rope:
        (x_ref, mod_ref, g_ref, w_ref, wg_ref, bg_ref, cos_ref, sa_ref, sb_ref,
         bf_ref, f32_ref) = refs
    else:
        (x_ref, mod_ref, g_ref, w_ref, wg_ref, bg_ref,
         bf_ref, f32_ref, nk_ref, nv_ref) = refs
    rc = ROW_CHUNK
    if not rope:
        assert rc == seq

    def chunk(r0):
        rows = slice(r0, r0 + rc)
        x = x_ref[rows, :]
        h = (_rms(x, g_ref[...]) * (1.0 + mod_ref[1:2, :]) + mod_ref[0:1, :]).astype(BF16)
        yield
        pg = _dot(h, w_ref[:, C_RB:IN_COLS])
        pa = _dot(h, w_ref[:, C_QA:C_VA])
        z = _dot(pg[:, C_GL - C_RB:IN_COLS - C_RB].astype(BF16), wg_ref[...]) + bg_ref[...]
        ls = jnp.minimum(z, 0.0) - jnp.log1p(jnp.exp(-jnp.abs(z)))
        f32_ref[rows, 512:1024] = ls * (1.0 / GATE_NORM)
        f32_ref[rows, 1024:1536] = pg[:, 0:512]
        pb = _dot(h, w_ref[:, C_VA:C_RB])
        for blk in range(8):
            t = pa[:, blk * 128:(blk + 1) * 128]
            if rope:
                t = (t * cos_ref[rows, :] + pltpu.roll(t, 112, 1) * sa_ref[rows, :]
                     + pltpu.roll(t, 16, 1) * sb_ref[rows, :])
            elif blk >= 4:
                nk_ref[r0 // seq, blk - 4] = t
            bf_ref[rows, blk * 128:(blk + 1) * 128] = t.astype(BF16)
        va = pb[:, 0:512]
        bf_ref[rows, C_VA:C_VA + 512] = va.astype(BF16)
        if not rope:
            for hh in range(H_A):
                nv_ref[r0 // seq, hh] = va[:, hh * 128:(hh + 1) * 128]
        bf_ref[rows, 1536:2048] = pb[:, C_VB - C_VA:C_VB - C_VA + 512].astype(BF16)
        f32_ref[rows, 0:256] = pb[:, C_QB - C_VA:C_QB - C_VA + 256] * (DK_B ** -0.5)
        f32_ref[rows, 256:512] = pb[:, C_KB - C_VA:C_KB - C_VA + 256]

    _run_skewed([chunk(r0) for r0 in range(0, x_ref.shape[0], rc)])


def _inproj(x2d, mods, g_pre, w_in_b, wg, bg, rope_tabs, seq_len, batch, ctx_out):
    T = x2d.shape[0]
    tm = 1024
    shared_mod = mods.shape[0] == 1
    assert (tm % seq_len == 0) if shared_mod else (seq_len % tm == 0)
    per_b = max(seq_len // tm, 1)
    mod_idx = (lambda i: (0, 0, 0)) if shared_mod else (lambda i: (i // per_b, 0, 0))
    in_specs = [
        pl.BlockSpec((tm, D_MODEL), lambda i: (i, 0)),
        pl.BlockSpec((None, N_MOD, D_MODEL), mod_idx),
        pl.BlockSpec((1, D_MODEL), lambda i: (0, 0)),
        pl.BlockSpec((D_MODEL, IN_COLS), lambda i: (0, 0), pipeline_mode=pl.Buffered(1)),
        pl.BlockSpec((2 * GATE_RANK, 512), lambda i: (0, 0)),
        pl.BlockSpec((1, 512), lambda i: (0, 0)),
    ]
    args = [x2d, mods, g_pre, w_in_b, wg, bg]
    out_shape = [jax.ShapeDtypeStruct((T, 2048), BF16), jax.ShapeDtypeStruct((T, 1536), F32)]
    out_specs = [pl.BlockSpec((tm, 2048), lambda i: (i, 0)), pl.BlockSpec((tm, 1536), lambda i: (i, 0))]
    rope = rope_tabs is not None
    if rope:
        for t in rope_tabs:
            in_specs.append(pl.BlockSpec((tm, 128), lambda i: (i % per_b, 0)))
            args.append(t)
    if ctx_out:
        nb = tm // seq_len
        for _ in range(2):
            out_shape.append(jax.ShapeDtypeStruct((batch, 1, H_A, seq_len, 128), F32))
            out_specs.append(pl.BlockSpec((nb, None, H_A, seq_len, 128), lambda i: (i, 0, 0, 0, 0)))
    return pl.pallas_call(
        functools.partial(_inproj_kernel, rope=rope, seq=seq_len),
        out_shape=out_shape,
        grid=(T // tm,),
        in_specs=in_specs,
        out_specs=out_specs,
        compiler_params=pltpu.CompilerParams(
            dimension_semantics=("arbitrary",), vmem_limit_bytes=VMEM_LIMIT),
        name="inproj_rope" if rope else "inproj_ctx",
    )(*args)


def _attn_kernel(*refs, cached):
    if cached:
        q_ref, k_ref, v_ref, ck_ref, cv_ref, lam_ref, dn_ref, o_ref = refs
    else:
        q_ref, k_ref, v_ref, lam_ref, dn_ref, o_ref = refs
    rc = ROW_CHUNK
    lp = lam_ref[...]
    lam = (jnp.exp(jnp.sum(lp[0:1] * lp[1:2], axis=-1, keepdims=True))
           - jnp.exp(jnp.sum(lp[2:3] * lp[3:4], axis=-1, keepdims=True)) + LAM_INIT)
    lane = lax.broadcasted_iota(jnp.int32, (1, 128), 1)
    scale = DK_A ** -0.5
    m1 = jnp.where(lane < DK_A, scale, 0.0).astype(BF16)
    m2 = jnp.where(lane >= DK_A, scale, 0.0).astype(BF16)
    n_keys = (k_ref.shape[0] + ck_ref.shape[1]) if cached else rc
    ones = jnp.ones((n_keys, 128), BF16)

    kv = {}
    res = {}

    def chain(r0, h, branch):
        rows = slice(r0, r0 + rc)
        cols = slice(h * 128, (h + 1) * 128)
        key = h if cached else (h, r0)
        if key not in kv:
            if cached:
                k = jnp.concatenate([ck_ref[h].astype(BF16), k_ref[:, cols]], axis=0)
                v = jnp.concatenate([cv_ref[h].astype(BF16), v_ref[:, cols]], axis=0)
            else:
                k, v = k_ref[rows, cols], v_ref[rows, cols]
            kv[key] = (k, jnp.concatenate([v, ones], axis=1))
        k, v1 = kv[key]
        s = _dot_nt(q_ref[rows, cols] * (m2 if branch else m1), k)
        yield
        e = jnp.exp(s - jnp.max(s, axis=-1, keepdims=True)).astype(BF16)
        yield
        res[r0, h, branch] = _dot(e, v1)
        yield
        if branch:
            r1, r2 = res[r0, h, 0], res[r0, h, 1]
            o = r1[:, 0:128] / r1[:, 128:256] - lam * (r2[:, 0:128] / r2[:, 128:256])
            o_ref[rows, cols] = (_rms(o, dn_ref[...]) * (1.0 - LAM_INIT)).astype(BF16)

    _run_skewed([chain(r0, h, br) for r0 in range(0, q_ref.shape[0], rc) for h in range(H_A) for br in range(2)])


def _attention(bf, cache_k, cache_v, lam_p, diff_norm, batch, seq_len):
    cached = cache_k is not None
    tq = 1024
    n_tiles = batch * seq_len // tq
    if cached:
        per_b = seq_len // tq
        kv_idx = lambda c: (lambda i: (i // per_b, c))
        kv_rows = seq_len
    else:
        assert seq_len == ROW_CHUNK
        kv_idx = lambda c: (lambda i: (i, c))
        kv_rows = tq
    in_specs = [
        pl.BlockSpec((tq, 512), lambda i: (i, 0)),
        pl.BlockSpec((kv_rows, 512), kv_idx(1)),
        pl.BlockSpec((kv_rows, 512), kv_idx(2)),
    ]
    args = [bf, bf, bf]
    if cached:
        past = cache_k.shape[3]
        cspec = pl.BlockSpec((None, None, H_A, past, 128), lambda i: (i // per_b, 0, 0, 0, 0))
        in_specs += [cspec, cspec]
        args += [cache_k, cache_v]
    in_specs += [
        pl.BlockSpec((4, DK_A), lambda i: (0, 0)),
        pl.BlockSpec((1, DV_A), lambda i: (0, 0)),
    ]
    args += [lam_p, diff_norm]
    return pl.pallas_call(
        functools.partial(_attn_kernel, cached=cached),
        out_shape=jax.ShapeDtypeStruct((batch * seq_len, H_A * DV_A), BF16),
        grid=(n_tiles,),
        in_specs=in_specs,
        out_specs=pl.BlockSpec((tq, 512), lambda i: (i, 0)),
        compiler_params=pltpu.CompilerParams(
            dimension_semantics=("arbitrary",), vmem_limit_bytes=VMEM_LIMIT),
        name="attn_cached" if cached else "attn_ctx",
    )(*args)


def _split3(g):
    hi = g.astype(BF16)
    r1 = g - hi.astype(F32)
    mid = r1.astype(BF16)
    lo = (r1 - mid.astype(F32)).astype(BF16)
    return hi, mid, lo


def _gla_block_stages(q_ref, k_ref, g_ref, v_ref, qc, vc, vt_cache, rows, blk, reverse, oa_s, qe_s, ut_s, dec_s):
    C = GLA_CHUNK
    R = rows.stop - rows.start
    cpb = R // C
    ri = lax.broadcasted_iota(jnp.int32, (R, R), 0)
    ci = lax.broadcasted_iota(jnp.int32, (R, R), 1)
    same_chunk = jnp.right_shift(ri, 6) == jnp.right_shift(ci, 6)
    if reverse:
        keep = same_chunk & (ci >= ri)
        last, ref = 0, C // 2
    else:
        keep = same_chunk & (ci <= ri)
        last, ref = C - 1, C // 2 - 1
    tri = jnp.where(keep, 1.0, 0.0).astype(BF16)
    lane = lax.broadcasted_iota(jnp.int32, (1, 2 * DK_B), 1)
    h0 = jnp.where(lane < DK_B, 1.0, 0.0).astype(BF16)
    h1 = jnp.where(lane >= DK_B, 1.0, 0.0).astype(BF16)

    hi, mid, lo = _split3(g_ref[rows, qc])
    b3 = _dot(tri, jnp.concatenate([hi, mid, lo], axis=1))
    yield
    b = b3[:, 0:128] + b3[:, 128:256] + b3[:, 256:384]
    q = q_ref[rows, qc]
    k = k_ref[rows, qc]
    qt, kt, qe, kd = [], [], [], []
    zeros = jnp.zeros((C, 2 * DK_B), BF16)
    for c in range(cpb):
        s = slice(c * C, (c + 1) * C)
        bc = b[s]
        bm = bc[ref:ref + 1]
        bl = bc[last:last + 1]
        qt.append((q[s] * jnp.exp(bc - bm)).astype(BF16))
        kt.append((k[s] * jnp.exp(bm - bc)).astype(BF16))
        qe.append((q[s] * jnp.exp(bc)).astype(BF16))
        kdc = (k[s] * jnp.exp(bl - bc)).astype(BF16)
        kd.append(jnp.concatenate([kdc if j == c else zeros for j in range(cpb)], axis=1))
        dec_s[blk * cpb + c:blk * cpb + c + 1, :] = jnp.exp(bl)
    qe_s[rows, :] = jnp.concatenate(qe, axis=0)
    qt = jnp.concatenate(qt, axis=0)
    kt = jnp.concatenate(kt, axis=0)
    yield
    s0 = _dot_nt(qt * h0, kt)
    s1 = _dot_nt(qt * h1, kt)
    v = v_ref[rows, vc]
    if (blk, vc.start) not in vt_cache:
        vt_cache[blk, vc.start] = v.T
    ut = _dot(vt_cache[blk, vc.start], jnp.concatenate(kd, axis=0))
    yield
    a0 = jnp.where(keep, s0, 0.0).astype(BF16)
    a1 = jnp.where(keep, s1, 0.0).astype(BF16)
    br = lax.broadcasted_iota(jnp.int32, ut.shape, 0)
    bcol = lax.broadcasted_iota(jnp.int32, ut.shape, 1)
    ut = jnp.where((br < DV_B) == ((bcol & (2 * DK_B - 1)) < DK_B), ut, 0.0)
    for c in range(cpb):
        ut_s[blk * cpb + c] = ut[:, c * 2 * DK_B:(c + 1) * 2 * DK_B]
    yield
    oa_s[rows, :] = jnp.concatenate([_dot(a0, v[:, 0:DV_B]), _dot(a1, v[:, DV_B:2 * DV_B])], axis=1)


def _state_to_t(s_pair):
    z = jnp.zeros((DK_B, DV_B), F32)
    a = jnp.concatenate([s_pair[0], z], axis=0).T
    b = jnp.concatenate([z, s_pair[1]], axis=0).T
    return jnp.concatenate([a, b], axis=0)


def _gla_kernel(*refs, n_chunks, has_state):
    if has_state:
        (q_ref, k_ref, gf_ref, gb_ref, rb_ref, v_ref, gn_ref, sf_ref, sb_ref, og_ref, *scratch) = refs
    else:
        (q_ref, k_ref, gf_ref, gb_ref, rb_ref, v_ref, gn_ref, og_ref, nsf_ref, nsb_ref, *scratch) = refs
    oa, oi, st, qe, ut, dec = scratch
    C = GLA_CHUNK
    R = 256
    L = n_chunks * C
    n_pairs = H_B // 2
    chains = [(p, d) for p in range(n_pairs) for d in range(2)]

    vt_cache = {}
    gens = []
    for blk in range(L // R):
        rows = slice(blk * R, (blk + 1) * R)
        for p, d in chains:
            ch = 2 * p + d
            gens.append(_gla_block_stages(
                q_ref, k_ref, gb_ref if d else gf_ref, v_ref,
                slice(p * 2 * DK_B, (p + 1) * 2 * DK_B), slice(p * 2 * DV_B, (p + 1) * 2 * DV_B),
                vt_cache, rows, blk, bool(d), oa.at[ch], qe.at[ch], ut.at[ch], dec.at[ch]))
    _run_skewed(gens)

    for p, d in chains:
        if has_state:
            st[2 * p + d] = _state_to_t((sb_ref if d else sf_ref)[2 * p:2 * p + 2])
        else:
            st[2 * p + d] = jnp.zeros(st.shape[1:], F32)

    def scan(i, carry):
        for p, d in chains:
            ch = 2 * p + d
            c = (n_chunks - 1 - i) if d else i
            rows = pl.ds(pl.multiple_of(c * C, C), C)
            s = st[ch]
            oi[ch, rows, :] = _dot_nt(qe[ch, rows, :], s.astype(BF16))
            st[ch] = s * dec[ch, pl.ds(c, 1), :] + ut[ch, c]
        return carry

    lax.fori_loop(0, n_chunks, scan, 0, unroll=min(n_chunks, 4))

    if not has_state:
        for p, d in chains:
            s = st[2 * p + d]
            dst_ref = nsb_ref if d else nsf_ref
            dst_ref[2 * p] = s[0:DV_B, :].T[0:DK_B, :]
            dst_ref[2 * p + 1] = s[DV_B:2 * DV_B, :].T[DK_B:2 * DK_B, :]

    tr = 256

    def fin(i, carry):
        rows = pl.ds(pl.multiple_of(i * tr, tr), tr)
        for p in range(n_pairs):
            o = (oa[2 * p, rows, :] + oi[2 * p, rows, :]) + (oa[2 * p + 1, rows, :] + oi[2 * p + 1, rows, :])
            for j in range(2):
                cols = slice((2 * p + j) * DV_B, (2 * p + j + 1) * DV_B)
                oj = _rms(o[:, j * DV_B:(j + 1) * DV_B], gn_ref[...])
                rj = rb_ref[rows, cols]
                og_ref[rows, cols] = (oj * (rj * jax.nn.sigmoid(rj))).astype(BF16)
        return carry

    lax.fori_loop(0, L // tr, fin, 0)


def _gla(f32a, bf, gla_norm, state_f, state_b, batch, seq_len):
    has_state = state_f is not None
    n_chunks = seq_len // GLA_CHUNK
    L = seq_len
    n_ch = H_B
    wk, wv = H_B * DK_B, H_B * DV_B
    in_specs = [
        pl.BlockSpec((L, wk), lambda b: (b, 0)),
        pl.BlockSpec((L, wk), lambda b: (b, 1)),
        pl.BlockSpec((L, wk), lambda b: (b, 2)),
        pl.BlockSpec((L, wk), lambda b: (b, 3)),
        pl.BlockSpec((L, wv), lambda b: (b, 2)),
        pl.BlockSpec((L, wv), lambda b: (b, 3)),
        pl.BlockSpec((1, DV_B), lambda b: (0, 0)),
    ]
    args = [f32a, f32a, f32a, f32a, f32a, bf, gla_norm]
    out_shape = [jax.ShapeDtypeStruct((batch * L, wv), BF16)]
    out_specs = [pl.BlockSpec((L, wv), lambda b: (b, 0))]
    st_spec = pl.BlockSpec((None, None, H_B, DK_B, DV_B), lambda b: (b, 0, 0, 0, 0))
    if has_state:
        in_specs += [st_spec, st_spec]
        args += [state_f, state_b]
    else:
        for _ in range(2):
            out_shape.append(jax.ShapeDtypeStruct((batch, 1, H_B, DK_B, DV_B), F32))
            out_specs.append(st_spec)
    return pl.pallas_call(
        functools.partial(_gla_kernel, n_chunks=n_chunks, has_state=has_state),
        out_shape=out_shape,
        grid=(batch,),
        in_specs=in_specs,
        out_specs=out_specs,
        scratch_shapes=[
            pltpu.VMEM((n_ch, L, 2 * DV_B), F32),
            pltpu.VMEM((n_ch, L, 2 * DV_B), F32),
            pltpu.VMEM((n_ch, 2 * DV_B, 2 * DK_B), F32),
            pltpu.VMEM((n_ch, L, 2 * DK_B), BF16),
            pltpu.VMEM((n_ch, n_chunks, 2 * DV_B, 2 * DK_B), F32),
            pltpu.VMEM((n_ch, max(n_chunks, 8), 2 * DK_B), F32),
        ],
        compiler_params=pltpu.CompilerParams(
            dimension_semantics=("arbitrary",), vmem_limit_bytes=VMEM_LIMIT),
        name="gla_state" if has_state else "gla_ctx",
    )(*args)


def _out_mlp_kernel(xc_ref, oac_ref, ogc_ref, xl_ref, oal_ref, ogl_ref, mod_ref, wo_ref, gpost_ref, gmpre_ref,
                    gmpost_ref, w1_ref, w2_ref, yc_ref, yl_ref, *, n_ctx_tiles):
    rc = ROW_CHUNK
    tf = 1024

    def run(x_ref, oa_ref, og_ref, y_ref):
        def chunk(r0):
            rows = slice(r0, r0 + rc)
            mix = _dot(oa_ref[rows, :], wo_ref[0:512, :]) + _dot(og_ref[rows, :], wo_ref[512:1024, :])
            yield
            x1 = x_ref[rows, :] + mod_ref[2:3, :] * _rms(mix, gpost_ref[...])
            h2 = (_rms(x1, gmpre_ref[...]) * (1.0 + mod_ref[4:5, :]) + mod_ref[3:4, :]).astype(BF16)
            yield
            acc = None
            for j in range(D_FF // tf):
                u = jnp.maximum(_dot(h2, w1_ref[:, j * tf:(j + 1) * tf]), 0.0)
                part = _dot((u * u).astype(BF16), w2_ref[j * tf:(j + 1) * tf, :])
                acc = part if acc is None else acc + part
                yield
            y_ref[rows, :] = x1 + mod_ref[5:6, :] * _rms(acc, gmpost_ref[...])

        _run_skewed([chunk(r0) for r0 in range(0, x_ref.shape[0], rc)])

    is_ctx = pl.program_id(0) < n_ctx_tiles

    @pl.when(is_ctx)
    def _():
        run(xc_ref, oac_ref, ogc_ref, yc_ref)

    @pl.when(jnp.logical_not(is_ctx))
    def _():
        run(xl_ref, oal_ref, ogl_ref, yl_ref)


def _out_mlp(xc2d, oac, ogc, xl2d, oal, ogl, mods, lat_seq, w_out_b, g_post, g_mpre, g_mpost, w1_b, w2_b):
    tm = 512
    n_c, n_l = xc2d.shape[0] // tm, xl2d.shape[0] // tm
    per_b = lat_seq // tm
    ctx_row = lambda i: (jnp.minimum(i, n_c - 1), 0)
    lat_row = lambda i: (jnp.maximum(i - n_c, 0), 0)
    mod_idx = lambda i: (jnp.where(i < n_c, 0, 1 + jnp.maximum(i - n_c, 0) // per_b), 0, 0)
    const = lambda i: (0, 0)
    return pl.pallas_call(
        functools.partial(_out_mlp_kernel, n_ctx_tiles=n_c),
        out_shape=[jax.ShapeDtypeStruct(xc2d.shape, F32), jax.ShapeDtypeStruct(xl2d.shape, F32)],
        grid=(n_c + n_l,),
        in_specs=[
            pl.BlockSpec((tm, D_MODEL), ctx_row),
            pl.BlockSpec((tm, 512), ctx_row),
            pl.BlockSpec((tm, 512), ctx_row),
            pl.BlockSpec((tm, D_MODEL), lat_row),
            pl.BlockSpec((tm, 512), lat_row),
            pl.BlockSpec((tm, 512), lat_row),
            pl.BlockSpec((None, N_MOD, D_MODEL), mod_idx),
            pl.BlockSpec((D_MODEL, D_MODEL), const, pipeline_mode=pl.Buffered(1)),
            pl.BlockSpec((1, D_MODEL), const),
            pl.BlockSpec((1, D_MODEL), const),
            pl.BlockSpec((1, D_MODEL), const),
            pl.BlockSpec((D_MODEL, D_FF), const, pipeline_mode=pl.Buffered(1)),
            pl.BlockSpec((D_FF, D_MODEL), const, pipeline_mode=pl.Buffered(1)),
        ],
        out_specs=[pl.BlockSpec((tm, D_MODEL), ctx_row), pl.BlockSpec((tm, D_MODEL), lat_row)],
        compiler_params=pltpu.CompilerParams(
            dimension_semantics=("arbitrary",), vmem_limit_bytes=VMEM_LIMIT),
        name="out_mlp",
    )(xc2d, oac, ogc, xl2d, oal, ogl, mods, w_out_b, g_post, g_mpre, g_mpost, w1_b, w2_b)


def _rope_tables(n_lat):
    pos = np.arange(n_lat)
    row_pos = (pos // GRID_W).astype(np.float64)
    col_pos = (pos % GRID_W).astype(np.float64)
    half = DK_A // 4
    inv = ROPE_BASE ** (-np.arange(half, dtype=np.float64) / half)
    lane = np.arange(128)
    in64 = lane % 64
    in32 = in64 % 32
    p = np.where((in64 < 32)[None, :], row_pos[:, None], col_pos[:, None])
    ang = p * inv[in32 % half][None, :]
    cos, sin = np.cos(ang), np.sin(ang)
    first = (in32 < half)[None, :]
    sa = np.where(first, -sin, 0.0)
    sb = np.where(first, 0.0, sin)
    return tuple(jnp.asarray(t, dtype=F32) for t in (cos, sa, sb))


def _mixers(x, mods, w, rope_tabs, cache_k, cache_v, state_f, state_b):
    batch, seq_len, _ = x.shape
    x2d = x.reshape(batch * seq_len, D_MODEL)
    ctx = cache_k is None
    outs = _inproj(x2d, mods, w["g_pre"], w["w_in"], w["wg"], w["bg"], rope_tabs, seq_len, batch, ctx)
    bf, f32a = outs[0], outs[1]
    oa = _attention(bf, cache_k, cache_v, w["lam"], w["diff_norm"], batch, seq_len)
    gouts = _gla(f32a, bf, w["gla_norm"], state_f, state_b, batch, seq_len)
    return x2d, oa, gouts[0], tuple(outs[2:]) + tuple(gouts[1:])


def kernel(x_prompt, x_sample, c, cache_k, cache_v, state_fwd, state_bwd, c_ctx, w_ada, b_ada,
           norm_attn_pre, norm_attn_post, norm_mlp_pre, norm_mlp_post, w_in, w_gate_fwd, b_gate_fwd,
           w_gate_bwd, b_gate_bwd, lam_q1, lam_k1, lam_q2, lam_k2, diff_norm, gla_norm, w_out,
           w_mlp1, w_mlp2):
    dec_batch = c.shape[0]
    rows = 16
    cvec = jnp.zeros((rows, D_MODEL), F32).at[0].set(c_ctx).at[1:1 + dec_batch].set(c)
    m = _adaln(cvec, w_ada[0], b_ada[0][None, :])
    mods_ctx = m[0:1].reshape(1, N_MOD, D_MODEL)
    mods_lat = m[1:1 + dec_batch].reshape(dec_batch, N_MOD, D_MODEL)

    wg = jnp.zeros((2 * GATE_RANK, 512), F32)
    wg = wg.at[0:GATE_RANK, 0:256].set(w_gate_fwd[0]).at[GATE_RANK:2 * GATE_RANK, 256:512].set(w_gate_bwd[0])
    w = {
        "g_pre": norm_attn_pre[0][None, :],
        "g_post": norm_attn_post[0][None, :],
        "g_mpre": norm_mlp_pre[0][None, :],
        "g_mpost": norm_mlp_post[0][None, :],
        "w_in": w_in[0].astype(BF16),
        "wg": wg.astype(BF16),
        "bg": jnp.concatenate([b_gate_fwd[0], b_gate_bwd[0]])[None, :],
        "lam": jnp.stack([lam_q1[0], lam_k1[0], lam_q2[0], lam_k2[0]]),
        "diff_norm": diff_norm[0][None, :],
        "gla_norm": gla_norm[0][None, :],
        "w_out": w_out[0].astype(BF16),
        "w1": w_mlp1[0].astype(BF16),
        "w2": w_mlp2[0].astype(BF16),
    }
    xc2d, oac, ogc, (new_k, new_v, new_sf, new_sb) = _mixers(x_prompt, mods_ctx, w, None, None, None, None, None)
    xl2d, oal, ogl, _ = _mixers(x_sample, mods_lat, w, _rope_tables(x_sample.shape[1]),
                                cache_k, cache_v, state_fwd, state_bwd)
    mods_all = m[0:1 + dec_batch].reshape(1 + dec_batch, N_MOD, D_MODEL)
    y_prompt, y_sample = _out_mlp(xc2d, oac, ogc, xl2d, oal, ogl, mods_all, x_sample.shape[1],
                                  w["w_out"], w["g_post"], w["g_mpre"], w["g_mpost"], w["w1"], w["w2"])
    return (y_prompt.reshape(x_prompt.shape), y_sample.reshape(x_sample.shape), new_k, new_v, new_sf, new_sb)
```

```python
import functools
import math

import numpy as np
import jax
import jax.numpy as jnp
from jax import lax
from jax.experimental import pallas as pl
from jax.experimental.pallas import tpu as pltpu

F32 = jnp.float32
BF16 = jnp.bfloat16

D_MODEL = 1024
GRID_W = 64
H_A = 4
DV_A = 128
DK_A = 64
H_B = 4
DV_B = 128
DK_B = 64
GATE_RANK = 16
GATE_NORM = 16.0
GLA_CHUNK = 64
D_FF = 4 * D_MODEL
N_MOD = 6
ROPE_BASE = 10000.0
EPS = 1e-6
LAM_INIT = 0.8 - 0.6 * math.exp(-0.3 * 0)

C_QA, C_KA, C_VA, C_QB, C_KB, C_VB, C_RB, C_GL = 0, 512, 1024, 1536, 1792, 2048, 2560, 3072
IN_COLS = 3104

V7X_VMEM_BYTES = 64 * 1024 * 1024
VMEM_LIMIT = V7X_VMEM_BYTES - 4 * 1024 * 1024
ROW_CHUNK = 256


def _rms(x, g):
    return x * lax.rsqrt(jnp.mean(x * x, axis=-1, keepdims=True) + EPS) * g


def _dot(a, b):
    return jnp.dot(a, b, preferred_element_type=F32)


def _dot_nt(a, b):
    return lax.dot_general(a, b, (((1,), (1,)), ((), ())), preferred_element_type=F32)


def _run_skewed(gens, secondary=(), deps=None):
    deps = deps or {}
    queues, running, done = [list(gens), list(secondary)], [], set()
    while any(queues) or running:
        for queue in queues:
            for g in queue:
                if all(id(d) in done for d in deps.get(id(g), ())):
                    queue.remove(g)
                    running.append(g)
                    break
        assert running, "dependency cycle"
        for g in list(running):
            try:
                next(g)
            except StopIteration:
                running.remove(g)
                done.add(id(g))


def _adaln_kernel(c_ref, w_ref, b_ref, o_ref):
    c = c_ref[...]
    s = c * jax.nn.sigmoid(c)
    o_ref[...] = _dot(s.astype(BF16), w_ref[...].astype(BF16)) + b_ref[...]


def _adaln(cvec, w_ada, b_ada):
    rows = cvec.shape[0]
    n = w_ada.shape[1]
    tn = 1536
    return pl.pallas_call(
        _adaln_kernel,
        out_shape=jax.ShapeDtypeStruct((rows, n), F32),
        grid=(n // tn,),
        in_specs=[
            pl.BlockSpec((rows, D_MODEL), lambda j: (0, 0)),
            pl.BlockSpec((D_MODEL, tn), lambda j: (0, j)),
            pl.BlockSpec((1, tn), lambda j: (0, j)),
        ],
        out_specs=pl.BlockSpec((rows, tn), lambda j: (0, j)),
        compiler_params=pltpu.CompilerParams(
            dimension_semantics=("arbitrary",), vmem_limit_bytes=VMEM_LIMIT),
        name="adaln",
    )(cvec, w_ada, b_ada)


def _inproj_kernel(*refs, rope, seq):
    if rope:
        (x_ref, mod_ref, g_ref, w_ref, wg_ref, bg_ref, cos_ref, sa_ref, sb_ref,
         bf_ref, f32_ref) = refs
    else:
        (x_ref, mod_ref, g_ref, w_ref, wg_ref, bg_ref,
         bf_ref, f32_ref, nk_ref, nv_ref) = refs
    rc = ROW_CHUNK
    if not rope:
        assert rc == seq

    def chunk(r0):
        rows = slice(r0, r0 + rc)
        x = x_ref[rows, :]
        h = (_rms(x, g_ref[...]) * (1.0 + mod_ref[1:2, :]) + mod_ref[0:1, :]).astype(BF16)
        yield
        pg = _dot(h, w_ref[:, C_RB:IN_COLS])
        pa = _dot(h, w_ref[:, C_QA:C_VA])
        z = _dot(pg[:, C_GL - C_RB:IN_COLS - C_RB].astype(BF16), wg_ref[...]) + bg_ref[...]
        ls = jnp.minimum(z, 0.0) - jnp.log1p(jnp.exp(-jnp.abs(z)))
        f32_ref[rows, 512:1024] = ls * (1.0 / GATE_NORM)
        f32_ref[rows, 1024:1536] = pg[:, 0:512]
        pb = _dot(h, w_ref[:, C_VA:C_RB])
        for blk in range(8):
            t = pa[:, blk * 128:(blk + 1) * 128]
            if rope:
                t = (t * cos_ref[rows, :] + pltpu.roll(t, 112, 1) * sa_ref[rows, :]
                     + pltpu.roll(t, 16, 1) * sb_ref[rows, :])
            elif blk >= 4:
                nk_ref[r0 // seq, blk - 4] = t
            bf_ref[rows, blk * 128:(blk + 1) * 128] = t.astype(BF16)
        va = pb[:, 0:512]
        bf_ref[rows, C_VA:C_VA + 512] = va.astype(BF16)
        if not rope:
            for hh in range(H_A):
                nv_ref[r0 // seq, hh] = va[:, hh * 128:(hh + 1) * 128]
        bf_ref[rows, 1536:2048] = pb[:, C_VB - C_VA:C_VB - C_VA + 512].astype(BF16)
        f32_ref[rows, 0:256] = pb[:, C_QB - C_VA:C_QB - C_VA + 256] * (DK_B ** -0.5)
        f32_ref[rows, 256:512] = pb[:, C_KB - C_VA:C_KB - C_VA + 256]

    _run_skewed([chunk(r0) for r0 in range(0, x_ref.shape[0], rc)])


def _inproj(x2d, mods, g_pre, w_in_b, wg, bg, rope_tabs, seq_len, batch, ctx_out):
    T = x2d.shape[0]
    tm = 1024
    shared_mod = mods.shape[0] == 1
    assert (tm % seq_len == 0) if shared_mod else (seq_len % tm == 0)
    per_b = max(seq_len // tm, 1)
    mod_idx = (lambda i: (0, 0, 0)) if shared_mod else (lambda i: (i // per_b, 0, 0))
    in_specs = [
        pl.BlockSpec((tm, D_MODEL), lambda i: (i, 0)),
        pl.BlockSpec((None, N_MOD, D_MODEL), mod_idx),
        pl.BlockSpec((1, D_MODEL), lambda i: (0, 0)),
        pl.BlockSpec((D_MODEL, IN_COLS), lambda i: (0, 0), pipeline_mode=pl.Buffered(1)),
        pl.BlockSpec((2 * GATE_RANK, 512), lambda i: (0, 0)),
        pl.BlockSpec((1, 512), lambda i: (0, 0)),
    ]
    args = [x2d, mods, g_pre, w_in_b, wg, bg]
    out_shape = [jax.ShapeDtypeStruct((T, 2048), BF16), jax.ShapeDtypeStruct((T, 1536), F32)]
    out_specs = [pl.BlockSpec((tm, 2048), lambda i: (i, 0)), pl.BlockSpec((tm, 1536), lambda i: (i, 0))]
    rope = rope_tabs is not None
    if rope:
        for t in rope_tabs:
            in_specs.append(pl.BlockSpec((tm, 128), lambda i: (i % per_b, 0)))
            args.append(t)
    if ctx_out:
        nb = tm // seq_len
        for _ in range(2):
            out_shape.append(jax.ShapeDtypeStruct((batch, 1, H_A, seq_len, 128), F32))
            out_specs.append(pl.BlockSpec((nb, None, H_A, seq_len, 128), lambda i: (i, 0, 0, 0, 0)))
    return pl.pallas_call(
        functools.partial(_inproj_kernel, rope=rope, seq=seq_len),
        out_shape=out_shape,
        grid=(T // tm,),
        in_specs=in_specs,
        out_specs=out_specs,
        compiler_params=pltpu.CompilerParams(
            dimension_semantics=("arbitrary",), vmem_limit_bytes=VMEM_LIMIT),
        name="inproj_rope" if rope else "inproj_ctx",
    )(*args)


def _attn_kernel(*refs, cached):
    if cached:
        q_ref, k_ref, v_ref, ck_ref, cv_ref, lam_ref, dn_ref, o_ref = refs
    else:
        q_ref, k_ref, v_ref, lam_ref, dn_ref, o_ref = refs
    rc = ROW_CHUNK
    lp = lam_ref[...]
    lam = (jnp.exp(jnp.sum(lp[0:1] * lp[1:2], axis=-1, keepdims=True))
           - jnp.exp(jnp.sum(lp[2:3] * lp[3:4], axis=-1, keepdims=True)) + LAM_INIT)
    lane = lax.broadcasted_iota(jnp.int32, (1, 128), 1)
    scale = DK_A ** -0.5
    m1 = jnp.where(lane < DK_A, scale, 0.0).astype(BF16)
    m2 = jnp.where(lane >= DK_A, scale, 0.0).astype(BF16)
    n_keys = (k_ref.shape[0] + ck_ref.shape[1]) if cached else rc
    ones = jnp.ones((n_keys, 128), BF16)

    kv = {}
    res = {}

    def chain(r0, h, branch):
        rows = slice(r0, r0 + rc)
        cols = slice(h * 128, (h + 1) * 128)
        key = h if cached else (h, r0)
        if key not in kv:
            if cached:
                k = jnp.concatenate([ck_ref[h].astype(BF16), k_ref[:, cols]], axis=0)
                v = jnp.concatenate([cv_ref[h].astype(BF16), v_ref[:, cols]], axis=0)
            else:
                k, v = k_ref[rows, cols], v_ref[rows, cols]
            kv[key] = (k, jnp.concatenate([v, ones], axis=1))
        k, v1 = kv[key]
        s = _dot_nt(q_ref[rows, cols] * (m2 if branch else m1), k)
        yield
        e = jnp.exp(s - jnp.max(s, axis=-1, keepdims=True)).astype(BF16)
        yield
        res[r0, h, branch] = _dot(e, v1)
        yield
        if branch:
            r1, r2 = res[r0, h, 0], res[r0, h, 1]
            o = r1[:, 0:128] / r1[:, 128:256] - lam * (r2[:, 0:128] / r2[:, 128:256])
            o_ref[rows, cols] = (_rms(o, dn_ref[...]) * (1.0 - LAM_INIT)).astype(BF16)

    _run_skewed([chain(r0, h, br) for r0 in range(0, q_ref.shape[0], rc) for h in range(H_A) for br in range(2)])


def _attention(bf, cache_k, cache_v, lam_p, diff_norm, batch, seq_len):
    cached = cache_k is not None
    tq = 1024
    n_tiles = batch * seq_len // tq
    if cached:
        per_b = seq_len // tq
        kv_idx = lambda c: (lambda i: (i // per_b, c))
        kv_rows = seq_len
    else:
        assert seq_len == ROW_CHUNK
        kv_idx = lambda c: (lambda i: (i, c))
        kv_rows = tq
    in_specs = [
        pl.BlockSpec((tq, 512), lambda i: (i, 0)),
        pl.BlockSpec((kv_rows, 512), kv_idx(1)),
        pl.BlockSpec((kv_rows, 512), kv_idx(2)),
    ]
    args = [bf, bf, bf]
    if cached:
        past = cache_k.shape[3]
        cspec = pl.BlockSpec((None, None, H_A, past, 128), lambda i: (i // per_b, 0, 0, 0, 0))
        in_specs += [cspec, cspec]
        args += [cache_k, cache_v]
    in_specs += [
        pl.BlockSpec((4, DK_A), lambda i: (0, 0)),
        pl.BlockSpec((1, DV_A), lambda i: (0, 0)),
    ]
    args += [lam_p, diff_norm]
    return pl.pallas_call(
        functools.partial(_attn_kernel, cached=cached),
        out_shape=jax.ShapeDtypeStruct((batch * seq_len, H_A * DV_A), BF16),
        grid=(n_tiles,),
        in_specs=in_specs,
        out_specs=pl.BlockSpec((tq, 512), lambda i: (i, 0)),
        compiler_params=pltpu.CompilerParams(
            dimension_semantics=("arbitrary",), vmem_limit_bytes=VMEM_LIMIT),
        name="attn_cached" if cached else "attn_ctx",
    )(*args)


def _split3(g):
    hi = g.astype(BF16)
    r1 = g - hi.astype(F32)
    mid = r1.astype(BF16)
    lo = (r1 - mid.astype(F32)).astype(BF16)
    return hi, mid, lo


def _gla_block_stages(q_ref, k_ref, g_ref, v_ref, in_rows, qc, vc, vt_cache, blk, reverse, oa_s, qe_s, ut_s, dec_s):
    C = GLA_CHUNK
    R = in_rows.stop - in_rows.start
    cpb = R // C
    rows = slice(blk * R, (blk + 1) * R)
    ri = lax.broadcasted_iota(jnp.int32, (R, R), 0)
    ci = lax.broadcasted_iota(jnp.int32, (R, R), 1)
    same_chunk = jnp.right_shift(ri, 6) == jnp.right_shift(ci, 6)
    if reverse:
        keep = same_chunk & (ci >= ri)
        last, ref = 0, C // 2
    else:
        keep = same_chunk & (ci <= ri)
        last, ref = C - 1, C // 2 - 1
    tri = jnp.where(keep, 1.0, 0.0).astype(BF16)
    lane = lax.broadcasted_iota(jnp.int32, (1, 2 * DK_B), 1)
    h0 = jnp.where(lane < DK_B, 1.0, 0.0).astype(BF16)
    h1 = jnp.where(lane >= DK_B, 1.0, 0.0).astype(BF16)

    hi, mid, lo = _split3(g_ref[in_rows, qc])
    b3 = _dot(tri, jnp.concatenate([hi, mid, lo], axis=1))
    yield
    b = b3[:, 0:128] + b3[:, 128:256] + b3[:, 256:384]
    q = q_ref[in_rows, qc]
    k = k_ref[in_rows, qc]
    qt, kt, qe, kd = [], [], [], []
    zeros = jnp.zeros((C, 2 * DK_B), BF16)
    for c in range(cpb):
        s = slice(c * C, (c + 1) * C)
        bc = b[s]
        bm = bc[ref:ref + 1]
        bl = bc[last:last + 1]
        qt.append((q[s] * jnp.exp(bc - bm)).astype(BF16))
        kt.append((k[s] * jnp.exp(bm - bc)).astype(BF16))
        qe.append((q[s] * jnp.exp(bc)).astype(BF16))
        kdc = (k[s] * jnp.exp(bl - bc)).astype(BF16)
        kd.append(jnp.concatenate([kdc if j == c else zeros for j in range(cpb)], axis=1))
        dec_s[blk * cpb + c:blk * cpb + c + 1, :] = jnp.exp(bl)
    qe_s[rows, :] = jnp.concatenate(qe, axis=0)
    qt = jnp.concatenate(qt, axis=0)
    kt = jnp.concatenate(kt, axis=0)
    yield
    s0 = _dot_nt(qt * h0, kt)
    s1 = _dot_nt(qt * h1, kt)
    v = v_ref[in_rows, vc]
    if (in_rows.start, vc.start) not in vt_cache:
        vt_cache[in_rows.start, vc.start] = v.T
    ut = _dot(vt_cache[in_rows.start, vc.start], jnp.concatenate(kd, axis=0))
    yield
    a0 = jnp.where(keep, s0, 0.0).astype(BF16)
    a1 = jnp.where(keep, s1, 0.0).astype(BF16)
    br = lax.broadcasted_iota(jnp.int32, ut.shape, 0)
    bcol = lax.broadcasted_iota(jnp.int32, ut.shape, 1)
    ut = jnp.where((br < DV_B) == ((bcol & (2 * DK_B - 1)) < DK_B), ut, 0.0)
    for c in range(cpb):
        ut_s[blk * cpb + c] = ut[:, c * 2 * DK_B:(c + 1) * 2 * DK_B]
    yield
    oa_s[rows, :] = jnp.concatenate([_dot(a0, v[:, 0:DV_B]), _dot(a1, v[:, DV_B:2 * DV_B])], axis=1)


def _state_to_t(s_pair):
    z = jnp.zeros((DK_B, DV_B), F32)
    a = jnp.concatenate([s_pair[0], z], axis=0).T
    b = jnp.concatenate([z, s_pair[1]], axis=0).T
    return jnp.concatenate([a, b], axis=0)


def _gla_kernel(*refs, n_chunks, n_elems, has_state):
    if has_state:
        (q_ref, k_ref, gf_ref, gb_ref, rb_ref, v_ref, gn_ref, sf_ref, sb_ref, og_ref, *scratch) = refs
    else:
        (q_ref, k_ref, gf_ref, gb_ref, rb_ref, v_ref, gn_ref, og_ref, nsf_ref, nsb_ref, *scratch) = refs
    oa, oi, st, qe, ut, dec = scratch
    C = GLA_CHUNK
    R = 256
    cpb = R // C
    L = n_chunks * C
    n_blk = L // R
    n_pairs = H_B // 2

    def scan_stage(e, p, d, blk):
        ch = 2 * p + d
        if blk == (n_blk - 1 if d else 0):
            if has_state:
                assert n_elems == 1
                st[e, ch] = _state_to_t((sb_ref if d else sf_ref)[2 * p:2 * p + 2])
            else:
                st[e, ch] = jnp.zeros(st.shape[2:], F32)
        for i in range(cpb):
            c = blk * cpb + (cpb - 1 - i if d else i)
            rows = slice(c * C, (c + 1) * C)
            s = st[e, ch]
            oi[e, ch, rows, :] = _dot_nt(qe[e, ch, rows, :], s.astype(BF16))
            st[e, ch] = s * dec[e, ch, c:c + 1, :] + ut[e, ch, c]
            yield
        if not has_state and blk == (0 if d else n_blk - 1):
            s = st[e, ch]
            dst_ref = nsb_ref if d else nsf_ref
            dst_ref[e, 2 * p] = s[0:DV_B, :].T[0:DK_B, :]
            dst_ref[e, 2 * p + 1] = s[DV_B:2 * DV_B, :].T[DK_B:2 * DK_B, :]

    def fin_stage(e, p, blk):
        rows = slice(blk * R, (blk + 1) * R)
        out_rows = slice(e * L + blk * R, e * L + (blk + 1) * R)
        o = ((oa[e, 2 * p, rows, :] + oi[e, 2 * p, rows, :])
             + (oa[e, 2 * p + 1, rows, :] + oi[e, 2 * p + 1, rows, :]))
        for j in range(2):
            cols = slice((2 * p + j) * DV_B, (2 * p + j + 1) * DV_B)
            oj = _rms(o[:, j * DV_B:(j + 1) * DV_B], gn_ref[...])
            rj = rb_ref[out_rows, cols]
            og_ref[out_rows, cols] = (oj * (rj * jax.nn.sigmoid(rj))).astype(BF16)
        yield

    vt_cache = {}
    blocks, scans, deps = [], {}, {}
    for e in range(n_elems):
        for t in range(n_blk):
            for p in range(n_pairs):
                for d in range(2):
                    blk = n_blk - 1 - t if d else t
                    ch = 2 * p + d
                    in_rows = slice(e * L + blk * R, e * L + (blk + 1) * R)
                    g = _gla_block_stages(
                        q_ref, k_ref, gb_ref if d else gf_ref, v_ref, in_rows,
                        slice(p * 2 * DK_B, (p + 1) * 2 * DK_B), slice(p * 2 * DV_B, (p + 1) * 2 * DV_B),
                        vt_cache, blk, bool(d), oa.at[e, ch], qe.at[e, ch], ut.at[e, ch], dec.at[e, ch])
                    blocks.append(g)
                    s = scan_stage(e, p, d, blk)
                    prev = scans.get((e, p, d, blk + 1 if d else blk - 1))
                    deps[id(s)] = [g] + ([prev] if prev is not None else [])
                    scans[e, p, d, blk] = s
    aux = []
    for e in range(n_elems):
        for t in range(n_blk):
            for p in range(n_pairs):
                aux.append(scans[e, p, 0, t])
                aux.append(scans[e, p, 1, n_blk - 1 - t])
        for blk in range(n_blk):
            for p in range(n_pairs):
                f = fin_stage(e, p, blk)
                deps[id(f)] = [scans[e, p, 0, blk], scans[e, p, 1, blk]]
                aux.append(f)
    _run_skewed(blocks, aux, deps)


def _gla(f32a, bf, gla_norm, state_f, state_b, batch, seq_len):
    has_state = state_f is not None
    n_chunks = seq_len // GLA_CHUNK
    L = seq_len
    n_elems = 1 if has_state else 4
    n_ch = H_B
    wk, wv = H_B * DK_B, H_B * DV_B
    rows = n_elems * L
    in_specs = [
        pl.BlockSpec((rows, wk), lambda b: (b, 0)),
        pl.BlockSpec((rows, wk), lambda b: (b, 1)),
        pl.BlockSpec((rows, wk), lambda b: (b, 2)),
        pl.BlockSpec((rows, wk), lambda b: (b, 3)),
        pl.BlockSpec((rows, wv), lambda b: (b, 2)),
        pl.BlockSpec((rows, wv), lambda b: (b, 3)),
        pl.BlockSpec((1, DV_B), lambda b: (0, 0)),
    ]
    args = [f32a, f32a, f32a, f32a, f32a, bf, gla_norm]
    out_shape = [jax.ShapeDtypeStruct((batch * L, wv), BF16)]
    out_specs = [pl.BlockSpec((rows, wv), lambda b: (b, 0))]
    if has_state:
        st_spec = pl.BlockSpec((None, None, H_B, DK_B, DV_B), lambda b: (b, 0, 0, 0, 0))
        in_specs += [st_spec, st_spec]
        args += [state_f, state_b]
    else:
        for _ in range(2):
            out_shape.append(jax.ShapeDtypeStruct((batch, 1, H_B, DK_B, DV_B), F32))
            out_specs.append(pl.BlockSpec((n_elems, None, H_B, DK_B, DV_B), lambda b: (b, 0, 0, 0, 0)))
    return pl.pallas_call(
        functools.partial(_gla_kernel, n_chunks=n_chunks, n_elems=n_elems, has_state=has_state),
        out_shape=out_shape,
        grid=(batch // n_elems,),
        in_specs=in_specs,
        out_specs=out_specs,
        scratch_shapes=[
            pltpu.VMEM((n_elems, n_ch, L, 2 * DV_B), F32),
            pltpu.VMEM((n_elems, n_ch, L, 2 * DV_B), F32),
            pltpu.VMEM((n_elems, n_ch, 2 * DV_B, 2 * DK_B), F32),
            pltpu.VMEM((n_elems, n_ch, L, 2 * DK_B), BF16),
            pltpu.VMEM((n_elems, n_ch, n_chunks, 2 * DV_B, 2 * DK_B), F32),
            pltpu.VMEM((n_elems, n_ch, max(n_chunks, 8), 2 * DK_B), F32),
        ],
        compiler_params=pltpu.CompilerParams(
            dimension_semantics=("arbitrary",), vmem_limit_bytes=VMEM_LIMIT),
        name="gla_state" if has_state else "gla_ctx",
    )(*args)


def _out_mlp_kernel(xc_ref, oac_ref, ogc_ref, xl_ref, oal_ref, ogl_ref, mod_ref, wo_ref, gpost_ref, gmpre_ref,
                    gmpost_ref, w1_ref, w2_ref, yc_ref, yl_ref, *, n_ctx_tiles):
    rc = ROW_CHUNK
    tf = 1024

    def run(x_ref, oa_ref, og_ref, y_ref):
        def chunk(r0):
            rows = slice(r0, r0 + rc)
            mix = _dot(oa_ref[rows, :], wo_ref[0:512, :]) + _dot(og_ref[rows, :], wo_ref[512:1024, :])
            yield
            x1 = x_ref[rows, :] + mod_ref[2:3, :] * _rms(mix, gpost_ref[...])
            h2 = (_rms(x1, gmpre_ref[...]) * (1.0 + mod_ref[4:5, :]) + mod_ref[3:4, :]).astype(BF16)
            yield
            acc = None
            for j in range(D_FF // tf):
                u = jnp.maximum(_dot(h2, w1_ref[:, j * tf:(j + 1) * tf]), 0.0)
                part = _dot((u * u).astype(BF16), w2_ref[j * tf:(j + 1) * tf, :])
                acc = part if acc is None else acc + part
                yield
            y_ref[rows, :] = x1 + mod_ref[5:6, :] * _rms(acc, gmpost_ref[...])

        _run_skewed([chunk(r0) for r0 in range(0, x_ref.shape[0], rc)])

    is_ctx = pl.program_id(0) < n_ctx_tiles

    @pl.when(is_ctx)
    def _():
        run(xc_ref, oac_ref, ogc_ref, yc_ref)

    @pl.when(jnp.logical_not(is_ctx))
    def _():
        run(xl_ref, oal_ref, ogl_ref, yl_ref)


def _out_mlp(xc2d, oac, ogc, xl2d, oal, ogl, mods, lat_seq, w_out_b, g_post, g_mpre, g_mpost, w1_b, w2_b):
    tm = 512
    n_c, n_l = xc2d.shape[0] // tm, xl2d.shape[0] // tm
    per_b = lat_seq // tm
    ctx_row = lambda i: (jnp.minimum(i, n_c - 1), 0)
    lat_row = lambda i: (jnp.maximum(i - n_c, 0), 0)
    mod_idx = lambda i: (jnp.where(i < n_c, 0, 1 + jnp.maximum(i - n_c, 0) // per_b), 0, 0)
    const = lambda i: (0, 0)
    return pl.pallas_call(
        functools.partial(_out_mlp_kernel, n_ctx_tiles=n_c),
        out_shape=[jax.ShapeDtypeStruct(xc2d.shape, F32), jax.ShapeDtypeStruct(xl2d.shape, F32)],
        grid=(n_c + n_l,),
        in_specs=[
            pl.BlockSpec((tm, D_MODEL), ctx_row),
            pl.BlockSpec((tm, 512), ctx_row),
            pl.BlockSpec((tm, 512), ctx_row),
            pl.BlockSpec((tm, D_MODEL), lat_row),
            pl.BlockSpec((tm, 512), lat_row),
            pl.BlockSpec((tm, 512), lat_row),
            pl.BlockSpec((None, N_MOD, D_MODEL), mod_idx),
            pl.BlockSpec((D_MODEL, D_MODEL), const, pipeline_mode=pl.Buffered(1)),
            pl.BlockSpec((1, D_MODEL), const),
            pl.BlockSpec((1, D_MODEL), const),
            pl.BlockSpec((1, D_MODEL), const),
            pl.BlockSpec((D_MODEL, D_FF), const, pipeline_mode=pl.Buffered(1)),
            pl.BlockSpec((D_FF, D_MODEL), const, pipeline_mode=pl.Buffered(1)),
        ],
        out_specs=[pl.BlockSpec((tm, D_MODEL), ctx_row), pl.BlockSpec((tm, D_MODEL), lat_row)],
        compiler_params=pltpu.CompilerParams(
            dimension_semantics=("arbitrary",), vmem_limit_bytes=VMEM_LIMIT),
        name="out_mlp",
    )(xc2d, oac, ogc, xl2d, oal, ogl, mods, w_out_b, g_post, g_mpre, g_mpost, w1_b, w2_b)


def _rope_tables(n_lat):
    pos = np.arange(n_lat)
    row_pos = (pos // GRID_W).astype(np.float64)
    col_pos = (pos % GRID_W).astype(np.float64)
    half = DK_A // 4
    inv = ROPE_BASE ** (-np.arange(half, dtype=np.float64) / half)
    lane = np.arange(128)
    in64 = lane % 64
    in32 = in64 % 32
    p = np.where((in64 < 32)[None, :], row_pos[:, None], col_pos[:, None])
    ang = p * inv[in32 % half][None, :]
    cos, sin = np.cos(ang), np.sin(ang)
    first = (in32 < half)[None, :]
    sa = np.where(first, -sin, 0.0)
    sb = np.where(first, 0.0, sin)
    return tuple(jnp.asarray(t, dtype=F32) for t in (cos, sa, sb))


def _mixers(x, mods, w, rope_tabs, cache_k, cache_v, state_f, state_b):
    batch, seq_len, _ = x.shape
    x2d = x.reshape(batch * seq_len, D_MODEL)
    ctx = cache_k is None
    outs = _inproj(x2d, mods, w["g_pre"], w["w_in"], w["wg"], w["bg"], rope_tabs, seq_len, batch, ctx)
    bf, f32a = outs[0], outs[1]
    oa = _attention(bf, cache_k, cache_v, w["lam"], w["diff_norm"], batch, seq_len)
    gouts = _gla(f32a, bf, w["gla_norm"], state_f, state_b, batch, seq_len)
    return x2d, oa, gouts[0], tuple(outs[2:]) + tuple(gouts[1:])


def kernel(x_prompt, x_sample, c, cache_k, cache_v, state_fwd, state_bwd, c_ctx, w_ada, b_ada,
           norm_attn_pre, norm_attn_post, norm_mlp_pre, norm_mlp_post, w_in, w_gate_fwd, b_gate_fwd,
           w_gate_bwd, b_gate_bwd, lam_q1, lam_k1, lam_q2, lam_k2, diff_norm, gla_norm, w_out,
           w_mlp1, w_mlp2):
    dec_batch = c.shape[0]
    rows = 16
    cvec = jnp.zeros((rows, D_MODEL), F32).at[0].set(c_ctx).at[1:1 + dec_batch].set(c)
    m = _adaln(cvec, w_ada[0], b_ada[0][None, :])
    mods_ctx = m[0:1].reshape(1, N_MOD, D_MODEL)
    mods_lat = m[1:1 + dec_batch].reshape(dec_batch, N_MOD, D_MODEL)

    wg = jnp.zeros((2 * GATE_RANK, 512), F32)
    wg = wg.at[0:GATE_RANK, 0:256].set(w_gate_fwd[0]).at[GATE_RANK:2 * GATE_RANK, 256:512].set(w_gate_bwd[0])
    w = {
        "g_pre": norm_attn_pre[0][None, :],
        "g_post": norm_attn_post[0][None, :],
        "g_mpre": norm_mlp_pre[0][None, :],
        "g_mpost": norm_mlp_post[0][None, :],
        "w_in": w_in[0].astype(BF16),
        "wg": wg.astype(BF16),
        "bg": jnp.concatenate([b_gate_fwd[0], b_gate_bwd[0]])[None, :],
        "lam": jnp.stack([lam_q1[0], lam_k1[0], lam_q2[0], lam_k2[0]]),
        "diff_norm": diff_norm[0][None, :],
        "gla_norm": gla_norm[0][None, :],
        "w_out": w_out[0].astype(BF16),
        "w1": w_mlp1[0].astype(BF16),
        "w2": w_mlp2[0].astype(BF16),
    }
    xc2d, oac, ogc, (new_k, new_v, new_sf, new_sb) = _mixers(x_prompt, mods_ctx, w, None, None, None, None, None)
    xl2d, oal, ogl, _ = _mixers(x_sample, mods_lat, w, _rope_tables(x_sample.shape[1]),
                                cache_k, cache_v, state_fwd, state_bwd)
    mods_all = m[0:1 + dec_batch].reshape(1 + dec_batch, N_MOD, D_MODEL)
    y_prompt, y_sample = _out_mlp(xc2d, oac, ogc, xl2d, oal, ogl, mods_all, x_sample.shape[1],
                                  w["w_out"], w["g_post"], w["g_mpre"], w["g_mpost"], w["w1"], w["w2"])
    return (y_prompt.reshape(x_prompt.shape), y_sample.reshape(x_sample.shape), new_k, new_v, new_sf, new_sb)
```

```python
import functools
import math

import numpy as np
import jax
import jax.numpy as jnp
from jax import lax
from jax.experimental import pallas as pl
from jax.experimental.pallas import tpu as pltpu

F32 = jnp.float32
BF16 = jnp.bfloat16

D_MODEL = 1024
GRID_W = 64
H_A = 4
DV_A = 128
DK_A = 64
H_B = 4
DV_B = 128
DK_B = 64
GATE_RANK = 16
GATE_NORM = 16.0
GLA_CHUNK = 64
D_FF = 4 * D_MODEL
N_MOD = 6
ROPE_BASE = 10000.0
EPS = 1e-6
LAM_INIT = 0.8 - 0.6 * math.exp(-0.3 * 0)

C_QA, C_KA, C_VA, C_QB, C_KB, C_VB, C_RB, C_GL = 0, 512, 1024, 1536, 1792, 2048, 2560, 3072
IN_COLS = 3104
B_QB, B_KB, B_VB, B_RB, B_GL, B_COLS = (c - C_QB for c in (C_QB, C_KB, C_VB, C_RB, C_GL, IN_COLS))

V7X_VMEM_BYTES = 64 * 1024 * 1024
VMEM_LIMIT = V7X_VMEM_BYTES - 4 * 1024 * 1024
ROW_CHUNK = 256
MIX_TILE = 1024


def _rms(x, g):
    return x * lax.rsqrt(jnp.mean(x * x, axis=-1, keepdims=True) + EPS) * g


def _dot(a, b):
    return jnp.dot(a, b, preferred_element_type=F32)


def _dot_nt(a, b):
    return lax.dot_general(a, b, (((1,), (1,)), ((), ())), preferred_element_type=F32)


def _run_skewed(gens, secondary=(), deps=None):
    deps = deps or {}
    queues, running, done = [list(gens), list(secondary)], [], set()
    while any(queues) or running:
        for queue in queues:
            for g in queue:
                if all(id(d) in done for d in deps.get(id(g), ())):
                    queue.remove(g)
                    running.append(g)
                    break
        assert running, "dependency cycle"
        for g in list(running):
            try:
                next(g)
            except StopIteration:
                running.remove(g)
                done.add(id(g))


def _prenorm(x_ref, rows, mod_ref, g_ref):
    x = x_ref[rows, :]
    return (_rms(x, g_ref[...]) * (1.0 + mod_ref[1:2, :]) + mod_ref[0:1, :]).astype(BF16)


def _adaln_kernel(c_ref, w_ref, b_ref, o_ref):
    c = c_ref[...]
    s = c * jax.nn.sigmoid(c)
    o_ref[...] = _dot(s.astype(BF16), w_ref[...].astype(BF16)) + b_ref[...]


def _adaln(cvec, w_ada, b_ada):
    rows = cvec.shape[0]
    n = w_ada.shape[1]
    tn = 1536
    return pl.pallas_call(
        _adaln_kernel,
        out_shape=jax.ShapeDtypeStruct((rows, n), F32),
        grid=(n // tn,),
        in_specs=[
            pl.BlockSpec((rows, D_MODEL), lambda j: (0, 0)),
            pl.BlockSpec((D_MODEL, tn), lambda j: (0, j)),
            pl.BlockSpec((1, tn), lambda j: (0, j)),
        ],
        out_specs=pl.BlockSpec((rows, tn), lambda j: (0, j)),
        compiler_params=pltpu.CompilerParams(
            dimension_semantics=("arbitrary",), vmem_limit_bytes=VMEM_LIMIT),
        name="adaln",
    )(cvec, w_ada, b_ada)


def _attn_kernel(*refs, cached, seq):
    if cached:
        (x_ref, mod_ref, g_ref, w_ref, cos_ref, sa_ref, sb_ref, ck_ref, cv_ref, lam_ref, dn_ref,
         o_ref, qkv_s) = refs
    else:
        x_ref, mod_ref, g_ref, w_ref, lam_ref, dn_ref, o_ref, nk_ref, nv_ref, qkv_s = refs
        assert seq == ROW_CHUNK
    rc = ROW_CHUNK
    n_rows = x_ref.shape[0]
    lp = lam_ref[...]
    lam = (jnp.exp(jnp.sum(lp[0:1] * lp[1:2], axis=-1, keepdims=True))
           - jnp.exp(jnp.sum(lp[2:3] * lp[3:4], axis=-1, keepdims=True)) + LAM_INIT)
    lane = lax.broadcasted_iota(jnp.int32, (1, 128), 1)
    scale = DK_A ** -0.5
    m1 = jnp.where(lane < DK_A, scale, 0.0).astype(BF16)
    m2 = jnp.where(lane >= DK_A, scale, 0.0).astype(BF16)
    n_keys = (n_rows + ck_ref.shape[1]) if cached else rc
    ones = jnp.ones((n_keys, 128), BF16)

    def proj(r0):
        rows = slice(r0, r0 + rc)
        h = _prenorm(x_ref, rows, mod_ref, g_ref)
        yield
        pa = _dot(h, w_ref[:, C_QA:C_VA])
        pv = _dot(h, w_ref[:, C_VA:C_QB])
        for blk in range(8):
            t = pa[:, blk * 128:(blk + 1) * 128]
            if cached:
                t = (t * cos_ref[rows, :] + pltpu.roll(t, 112, 1) * sa_ref[rows, :]
                     + pltpu.roll(t, 16, 1) * sb_ref[rows, :])
            elif blk >= 4:
                nk_ref[r0 // seq, blk - 4] = t
            qkv_s[rows, blk * 128:(blk + 1) * 128] = t.astype(BF16)
        qkv_s[rows, C_VA:C_QB] = pv.astype(BF16)
        if not cached:
            for hh in range(H_A):
                nv_ref[r0 // seq, hh] = pv[:, hh * 128:(hh + 1) * 128]

    kv = {}
    res = {}

    def chain(r0, h, branch):
        rows = slice(r0, r0 + rc)
        cols = slice(h * 128, (h + 1) * 128)
        kcols = slice(C_KA + h * 128, C_KA + (h + 1) * 128)
        vcols = slice(C_VA + h * 128, C_VA + (h + 1) * 128)
        key = h if cached else (h, r0)
        if key not in kv:
            if cached:
                k = jnp.concatenate([ck_ref[h].astype(BF16), qkv_s[:, kcols]], axis=0)
                v = jnp.concatenate([cv_ref[h].astype(BF16), qkv_s[:, vcols]], axis=0)
            else:
                k, v = qkv_s[rows, kcols], qkv_s[rows, vcols]
            kv[key] = (k, jnp.concatenate([v, ones], axis=1))
        k, v1 = kv[key]
        s = _dot_nt(qkv_s[rows, cols] * (m2 if branch else m1), k)
        yield
        e = jnp.exp(s - jnp.max(s, axis=-1, keepdims=True)).astype(BF16)
        yield
        res[r0, h, branch] = _dot(e, v1)
        yield
        if branch:
            r1, r2 = res[r0, h, 0], res[r0, h, 1]
            o = r1[:, 0:128] / r1[:, 128:256] - lam * (r2[:, 0:128] / r2[:, 128:256])
            o_ref[rows, cols] = (_rms(o, dn_ref[...]) * (1.0 - LAM_INIT)).astype(BF16)

    projs = {r0: proj(r0) for r0 in range(0, n_rows, rc)}
    gens, deps = list(projs.values()), {}
    for r0 in projs:
        for h in range(H_A):
            for br in range(2):
                c = chain(r0, h, br)
                deps[id(c)] = list(projs.values()) if cached else [projs[r0]]
                gens.append(c)
    _run_skewed(gens, deps=deps)


def _attention(x2d, mods, g_pre, w_a, rope_tabs, cache_k, cache_v, lam_p, diff_norm, batch, seq_len):
    cached = cache_k is not None
    tq = MIX_TILE
    n_tiles = batch * seq_len // tq
    const = lambda i: (0, 0)
    in_specs = [
        pl.BlockSpec((tq, D_MODEL), lambda i: (i, 0)),
        pl.BlockSpec((None, N_MOD, D_MODEL), (lambda i: (i, 0, 0)) if cached else (lambda i: (0, 0, 0))),
        pl.BlockSpec((1, D_MODEL), const),
        pl.BlockSpec((D_MODEL, C_QB), const, pipeline_mode=pl.Buffered(1)),
    ]
    args = [x2d, mods, g_pre, w_a]
    out_shape = [jax.ShapeDtypeStruct((batch * seq_len, H_A * DV_A), BF16)]
    out_specs = [pl.BlockSpec((tq, H_A * DV_A), lambda i: (i, 0))]
    if cached:
        assert seq_len == tq and mods.shape[0] == batch
        past = cache_k.shape[3]
        cspec = pl.BlockSpec((None, None, H_A, past, 128), lambda i: (i, 0, 0, 0, 0))
        in_specs += [pl.BlockSpec((tq, 128), const)] * 3 + [cspec, cspec]
        args += list(rope_tabs) + [cache_k, cache_v]
    else:
        assert mods.shape[0] == 1
        nb = tq // seq_len
        for _ in range(2):
            out_shape.append(jax.ShapeDtypeStruct((batch, 1, H_A, seq_len, 128), F32))
            out_specs.append(pl.BlockSpec((nb, None, H_A, seq_len, 128), lambda i: (i, 0, 0, 0, 0)))
    in_specs += [pl.BlockSpec((4, DK_A), const), pl.BlockSpec((1, DV_A), const)]
    args += [lam_p, diff_norm]
    return pl.pallas_call(
        functools.partial(_attn_kernel, cached=cached, seq=seq_len),
        out_shape=out_shape,
        grid=(n_tiles,),
        in_specs=in_specs,
        out_specs=out_specs,
        scratch_shapes=[pltpu.VMEM((tq, C_QB), BF16)],
        compiler_params=pltpu.CompilerParams(
            dimension_semantics=("arbitrary",), vmem_limit_bytes=VMEM_LIMIT),
        name="attn_cached" if cached else "attn_ctx",
    )(*args)


def _split3(g):
    hi = g.astype(BF16)
    r1 = g - hi.astype(F32)
    mid = r1.astype(BF16)
    lo = (r1 - mid.astype(F32)).astype(BF16)
    return hi, mid, lo


def _gla_block_stages(q_ref, k_ref, g_ref, v_ref, in_rows, qc, vc, vt_cache, blk, reverse, oa_s, qe_s, ut_s, dec_s):
    C = GLA_CHUNK
    R = in_rows.stop - in_rows.start
    cpb = R // C
    rows = slice(blk * R, (blk + 1) * R)
    ri = lax.broadcasted_iota(jnp.int32, (R, R), 0)
    ci = lax.broadcasted_iota(jnp.int32, (R, R), 1)
    same_chunk = jnp.right_shift(ri, 6) == jnp.right_shift(ci, 6)
    if reverse:
        keep = same_chunk & (ci >= ri)
        last, ref = 0, C // 2
    else:
        keep = same_chunk & (ci <= ri)
        last, ref = C - 1, C // 2 - 1
    tri = jnp.where(keep, 1.0, 0.0).astype(BF16)
    lane = lax.broadcasted_iota(jnp.int32, (1, 2 * DK_B), 1)
    h0 = jnp.where(lane < DK_B, 1.0, 0.0).astype(BF16)
    h1 = jnp.where(lane >= DK_B, 1.0, 0.0).astype(BF16)

    hi, mid, lo = _split3(g_ref[in_rows, qc])
    b3 = _dot(tri, jnp.concatenate([hi, mid, lo], axis=1))
    yield
    b = b3[:, 0:128] + b3[:, 128:256] + b3[:, 256:384]
    q = q_ref[in_rows, qc]
    k = k_ref[in_rows, qc]
    qt, kt, qe, kd = [], [], [], []
    zeros = jnp.zeros((C, 2 * DK_B), BF16)
    for c in range(cpb):
        s = slice(c * C, (c + 1) * C)
        bc = b[s]
        bm = bc[ref:ref + 1]
        bl = bc[last:last + 1]
        qt.append((q[s] * jnp.exp(bc - bm)).astype(BF16))
        kt.append((k[s] * jnp.exp(bm - bc)).astype(BF16))
        qe.append((q[s] * jnp.exp(bc)).astype(BF16))
        kdc = (k[s] * jnp.exp(bl - bc)).astype(BF16)
        kd.append(jnp.concatenate([kdc if j == c else zeros for j in range(cpb)], axis=1))
        dec_s[blk * cpb + c:blk * cpb + c + 1, :] = jnp.exp(bl)
    qe_s[rows, :] = jnp.concatenate(qe, axis=0)
    qt = jnp.concatenate(qt, axis=0)
    kt = jnp.concatenate(kt, axis=0)
    yield
    s0 = _dot_nt(qt * h0, kt)
    s1 = _dot_nt(qt * h1, kt)
    v = v_ref[in_rows, vc]
    if (in_rows.start, vc.start) not in vt_cache:
        vt_cache[in_rows.start, vc.start] = v.T
    ut = _dot(vt_cache[in_rows.start, vc.start], jnp.concatenate(kd, axis=0))
    yield
    a0 = jnp.where(keep, s0, 0.0).astype(BF16)
    a1 = jnp.where(keep, s1, 0.0).astype(BF16)
    br = lax.broadcasted_iota(jnp.int32, ut.shape, 0)
    bcol = lax.broadcasted_iota(jnp.int32, ut.shape, 1)
    ut = jnp.where((br < DV_B) == ((bcol & (2 * DK_B - 1)) < DK_B), ut, 0.0)
    for c in range(cpb):
        ut_s[blk * cpb + c] = ut[:, c * 2 * DK_B:(c + 1) * 2 * DK_B]
    yield
    oa_s[rows, :] = jnp.concatenate([_dot(a0, v[:, 0:DV_B]), _dot(a1, v[:, DV_B:2 * DV_B])], axis=1)


def _state_to_t(s_pair):
    z = jnp.zeros((DK_B, DV_B), F32)
    a = jnp.concatenate([s_pair[0], z], axis=0).T
    b = jnp.concatenate([z, s_pair[1]], axis=0).T
    return jnp.concatenate([a, b], axis=0)


def _gla_kernel(*refs, n_chunks, n_elems, has_state):
    if has_state:
        (x_ref, mod_ref, g_ref, w_ref, wg_ref, bg_ref, gn_ref, sf_ref, sb_ref, og_ref, *scratch) = refs
    else:
        (x_ref, mod_ref, g_ref, w_ref, wg_ref, bg_ref, gn_ref, og_ref, nsf_ref, nsb_ref, *scratch) = refs
    q_s, k_s, gf_s, gb_s, rb_s, v_s, oa, oi, st, qe, ut, dec = scratch
    C = GLA_CHUNK
    R = ROW_CHUNK
    cpb = R // C
    L = n_chunks * C
    n_blk = L // R
    n_pairs = H_B // 2

    def proj(r0):
        rows = slice(r0, r0 + R)
        h = _prenorm(x_ref, rows, mod_ref, g_ref)
        yield
        pg = _dot(h, w_ref[:, B_RB:B_COLS])
        pb = _dot(h, w_ref[:, B_QB:B_RB])
        z = _dot(pg[:, B_GL - B_RB:B_COLS - B_RB].astype(BF16), wg_ref[...]) + bg_ref[...]
        ls = (jnp.minimum(z, 0.0) - jnp.log(1.0 + jnp.exp(-jnp.abs(z)))) * (1.0 / GATE_NORM)
        gf_s[rows, :] = ls[:, 0:256]
        gb_s[rows, :] = ls[:, 256:512]
        rb_s[rows, :] = pg[:, 0:512]
        q_s[rows, :] = pb[:, B_QB:B_KB] * (DK_B ** -0.5)
        k_s[rows, :] = pb[:, B_KB:B_VB]
        v_s[rows, :] = pb[:, B_VB:B_RB].astype(BF16)

    def scan_stage(e, p, d, blk):
        ch = 2 * p + d
        if blk == (n_blk - 1 if d else 0):
            if has_state:
                assert n_elems == 1
                st[e, ch] = _state_to_t((sb_ref if d else sf_ref)[2 * p:2 * p + 2])
            else:
                st[e, ch] = jnp.zeros(st.shape[2:], F32)
        for i in range(cpb):
            c = blk * cpb + (cpb - 1 - i if d else i)
            rows = slice(c * C, (c + 1) * C)
            s = st[e, ch]
            oi[e, ch, rows, :] = _dot_nt(qe[e, ch, rows, :], s.astype(BF16))
            st[e, ch] = s * dec[e, ch, c:c + 1, :] + ut[e, ch, c]
            yield
        if not has_state and blk == (0 if d else n_blk - 1):
            s = st[e, ch]
            dst_ref = nsb_ref if d else nsf_ref
            dst_ref[e, 2 * p] = s[0:DV_B, :].T[0:DK_B, :]
            dst_ref[e, 2 * p + 1] = s[DV_B:2 * DV_B, :].T[DK_B:2 * DK_B, :]

    def fin_stage(e, p, blk):
        rows = slice(blk * R, (blk + 1) * R)
        out_rows = slice(e * L + blk * R, e * L + (blk + 1) * R)
        o = ((oa[e, 2 * p, rows, :] + oi[e, 2 * p, rows, :])
             + (oa[e, 2 * p + 1, rows, :] + oi[e, 2 * p + 1, rows, :]))
        for j in range(2):
            cols = slice((2 * p + j) * DV_B, (2 * p + j + 1) * DV_B)
            oj = _rms(o[:, j * DV_B:(j + 1) * DV_B], gn_ref[...])
            rj = rb_s[out_rows, cols]
            og_ref[out_rows, cols] = (oj * (rj * jax.nn.sigmoid(rj))).astype(BF16)
        yield

    vt_cache = {}
    projs, blocks, scans, deps = {}, [], {}, {}
    for e in range(n_elems):
        for t in range(n_blk):
            blk = t // 2 if t % 2 == 0 else n_blk - 1 - t // 2
            projs[e, blk] = proj(e * L + blk * R)
    for e in range(n_elems):
        for t in range(n_blk):
            for p in range(n_pairs):
                for d in range(2):
                    blk = n_blk - 1 - t if d else t
                    ch = 2 * p + d
                    in_rows = slice(e * L + blk * R, e * L + (blk + 1) * R)
                    g = _gla_block_stages(
                        q_s, k_s, gb_s if d else gf_s, v_s, in_rows,
                        slice(p * 2 * DK_B, (p + 1) * 2 * DK_B), slice(p * 2 * DV_B, (p + 1) * 2 * DV_B),
                        vt_cache, blk, bool(d), oa.at[e, ch], qe.at[e, ch], ut.at[e, ch], dec.at[e, ch])
                    deps[id(g)] = [projs[e, blk]]
                    blocks.append(g)
                    s = scan_stage(e, p, d, blk)
                    prev = scans.get((e, p, d, blk + 1 if d else blk - 1))
                    deps[id(s)] = [g] + ([prev] if prev is not None else [])
                    scans[e, p, d, blk] = s
    aux = []
    for e in range(n_elems):
        for t in range(n_blk):
            for p in range(n_pairs):
                aux.append(scans[e, p, 0, t])
                aux.append(scans[e, p, 1, n_blk - 1 - t])
        for blk in range(n_blk):
            for p in range(n_pairs):
                f = fin_stage(e, p, blk)
                deps[id(f)] = [scans[e, p, 0, blk], scans[e, p, 1, blk]]
                aux.append(f)
    _run_skewed(list(projs.values()) + blocks, aux, deps)


def _gla(x2d, mods, g_pre, w_b, wg, bg, gla_norm, state_f, state_b, batch, seq_len):
    has_state = state_f is not None
    n_chunks = seq_len // GLA_CHUNK
    L = seq_len
    rows = MIX_TILE
    n_elems = rows // L
    n_ch = H_B
    wk, wv = H_B * DK_B, H_B * DV_B
    const = lambda i: (0, 0)
    in_specs = [
        pl.BlockSpec((rows, D_MODEL), lambda i: (i, 0)),
        pl.BlockSpec((None, N_MOD, D_MODEL), (lambda i: (i, 0, 0)) if has_state else (lambda i: (0, 0, 0))),
        pl.BlockSpec((1, D_MODEL), const),
        pl.BlockSpec((D_MODEL, B_COLS), const, pipeline_mode=pl.Buffered(1)),
        pl.BlockSpec((2 * GATE_RANK, 2 * wk), const),
        pl.BlockSpec((1, 2 * wk), const),
        pl.BlockSpec((1, DV_B), const),
    ]
    args = [x2d, mods, g_pre, w_b, wg, bg, gla_norm]
    out_shape = [jax.ShapeDtypeStruct((batch * L, wv), BF16)]
    out_specs = [pl.BlockSpec((rows, wv), lambda i: (i, 0))]
    if has_state:
        assert n_elems == 1 and mods.shape[0] == batch
        st_spec = pl.BlockSpec((None, None, H_B, DK_B, DV_B), lambda i: (i, 0, 0, 0, 0))
        in_specs += [st_spec, st_spec]
        args += [state_f, state_b]
    else:
        assert mods.shape[0] == 1
        for _ in range(2):
            out_shape.append(jax.ShapeDtypeStruct((batch, 1, H_B, DK_B, DV_B), F32))
            out_specs.append(pl.BlockSpec((n_elems, None, H_B, DK_B, DV_B), lambda i: (i, 0, 0, 0, 0)))
    return pl.pallas_call(
        functools.partial(_gla_kernel, n_chunks=n_chunks, n_elems=n_elems, has_state=has_state),
        out_shape=out_shape,
        grid=(batch // n_elems,),
        in_specs=in_specs,
        out_specs=out_specs,
        scratch_shapes=[
            pltpu.VMEM((rows, wk), F32),
            pltpu.VMEM((rows, wk), F32),
            pltpu.VMEM((rows, wk), F32),
            pltpu.VMEM((rows, wk), F32),
            pltpu.VMEM((rows, wv), F32),
            pltpu.VMEM((rows, wv), BF16),
            pltpu.VMEM((n_elems, n_ch, L, 2 * DV_B), F32),
            pltpu.VMEM((n_elems, n_ch, L, 2 * DV_B), F32),
            pltpu.VMEM((n_elems, n_ch, 2 * DV_B, 2 * DK_B), F32),
            pltpu.VMEM((n_elems, n_ch, L, 2 * DK_B), BF16),
            pltpu.VMEM((n_elems, n_ch, n_chunks, 2 * DV_B, 2 * DK_B), F32),
            pltpu.VMEM((n_elems, n_ch, max(n_chunks, 8), 2 * DK_B), F32),
        ],
        compiler_params=pltpu.CompilerParams(
            dimension_semantics=("arbitrary",), vmem_limit_bytes=VMEM_LIMIT),
        name="gla_state" if has_state else "gla_ctx",
    )(*args)


def _out_mlp_kernel(xc_ref, oac_ref, ogc_ref, xl_ref, oal_ref, ogl_ref, mod_ref, wo_ref, gpost_ref, gmpre_ref,
                    gmpost_ref, w1_ref, w2_ref, yc_ref, yl_ref, *, n_ctx_tiles):
    rc = ROW_CHUNK
    tf = 1024

    def run(x_ref, oa_ref, og_ref, y_ref):
        def chunk(r0):
            rows = slice(r0, r0 + rc)
            mix = _dot(oa_ref[rows, :], wo_ref[0:512, :]) + _dot(og_ref[rows, :], wo_ref[512:1024, :])
            yield
            x1 = x_ref[rows, :] + mod_ref[2:3, :] * _rms(mix, gpost_ref[...])
            h2 = (_rms(x1, gmpre_ref[...]) * (1.0 + mod_ref[4:5, :]) + mod_ref[3:4, :]).astype(BF16)
            yield
            acc = None
            for j in range(D_FF // tf):
                u = jnp.maximum(_dot(h2, w1_ref[:, j * tf:(j + 1) * tf]), 0.0)
                part = _dot((u * u).astype(BF16), w2_ref[j * tf:(j + 1) * tf, :])
                acc = part if acc is None else acc + part
                yield
            y_ref[rows, :] = x1 + mod_ref[5:6, :] * _rms(acc, gmpost_ref[...])

        _run_skewed([chunk(r0) for r0 in range(0, x_ref.shape[0], rc)])

    is_ctx = pl.program_id(0) < n_ctx_tiles

    @pl.when(is_ctx)
    def _():
        run(xc_ref, oac_ref, ogc_ref, yc_ref)

    @pl.when(jnp.logical_not(is_ctx))
    def _():
        run(xl_ref, oal_ref, ogl_ref, yl_ref)


def _out_mlp(xc2d, oac, ogc, xl2d, oal, ogl, mods, lat_seq, w_out_b, g_post, g_mpre, g_mpost, w1_b, w2_b):
    tm = 512
    n_c, n_l = xc2d.shape[0] // tm, xl2d.shape[0] // tm
    per_b = lat_seq // tm
    ctx_row = lambda i: (jnp.minimum(i, n_c - 1), 0)
    lat_row = lambda i: (jnp.maximum(i - n_c, 0), 0)
    mod_idx = lambda i: (jnp.where(i < n_c, 0, 1 + jnp.maximum(i - n_c, 0) // per_b), 0, 0)
    const = lambda i: (0, 0)
    return pl.pallas_call(
        functools.partial(_out_mlp_kernel, n_ctx_tiles=n_c),
        out_shape=[jax.ShapeDtypeStruct(xc2d.shape, F32), jax.ShapeDtypeStruct(xl2d.shape, F32)],
        grid=(n_c + n_l,),
        in_specs=[
            pl.BlockSpec((tm, D_MODEL), ctx_row),
            pl.BlockSpec((tm, 512), ctx_row),
            pl.BlockSpec((tm, 512), ctx_row),
            pl.BlockSpec((tm, D_MODEL), lat_row),
            pl.BlockSpec((tm, 512), lat_row),
            pl.BlockSpec((tm, 512), lat_row),
            pl.BlockSpec((None, N_MOD, D_MODEL), mod_idx),
            pl.BlockSpec((D_MODEL, D_MODEL), const, pipeline_mode=pl.Buffered(1)),
            pl.BlockSpec((1, D_MODEL), const),
            pl.BlockSpec((1, D_MODEL), const),
            pl.BlockSpec((1, D_MODEL), const),
            pl.BlockSpec((D_MODEL, D_FF), const, pipeline_mode=pl.Buffered(1)),
            pl.BlockSpec((D_FF, D_MODEL), const, pipeline_mode=pl.Buffered(1)),
        ],
        out_specs=[pl.BlockSpec((tm, D_MODEL), ctx_row), pl.BlockSpec((tm, D_MODEL), lat_row)],
        compiler_params=pltpu.CompilerParams(
            dimension_semantics=("arbitrary",), vmem_limit_bytes=VMEM_LIMIT),
        name="out_mlp",
    )(xc2d, oac, ogc, xl2d, oal, ogl, mods, w_out_b, g_post, g_mpre, g_mpost, w1_b, w2_b)


def _rope_tables(n_lat):
    pos = np.arange(n_lat)
    row_pos = (pos // GRID_W).astype(np.float64)
    col_pos = (pos % GRID_W).astype(np.float64)
    half = DK_A // 4
    inv = ROPE_BASE ** (-np.arange(half, dtype=np.float64) / half)
    lane = np.arange(128)
    in64 = lane % 64
    in32 = in64 % 32
    p = np.where((in64 < 32)[None, :], row_pos[:, None], col_pos[:, None])
    ang = p * inv[in32 % half][None, :]
    cos, sin = np.cos(ang), np.sin(ang)
    first = (in32 < half)[None, :]
    sa = np.where(first, -sin, 0.0)
    sb = np.where(first, 0.0, sin)
    return tuple(jnp.asarray(t, dtype=F32) for t in (cos, sa, sb))


def _mixers(x, mods, w, rope_tabs, cache_k, cache_v, state_f, state_b):
    batch, seq_len, _ = x.shape
    x2d = x.reshape(batch * seq_len, D_MODEL)
    aouts = _attention(x2d, mods, w["g_pre"], w["w_a"], rope_tabs, cache_k, cache_v, w["lam"], w["diff_norm"],
                       batch, seq_len)
    gouts = _gla(x2d, mods, w["g_pre"], w["w_b"], w["wg"], w["bg"], w["gla_norm"], state_f, state_b,
                 batch, seq_len)
    return x2d, aouts[0], gouts[0], tuple(aouts[1:]) + tuple(gouts[1:])


def kernel(x_prompt, x_sample, c, cache_k, cache_v, state_fwd, state_bwd, c_ctx, w_ada, b_ada,
           norm_attn_pre, norm_attn_post, norm_mlp_pre, norm_mlp_post, w_in, w_gate_fwd, b_gate_fwd,
           w_gate_bwd, b_gate_bwd, lam_q1, lam_k1, lam_q2, lam_k2, diff_norm, gla_norm, w_out,
           w_mlp1, w_mlp2):
    dec_batch = c.shape[0]
    rows = 16
    cvec = jnp.zeros((rows, D_MODEL), F32).at[0].set(c_ctx).at[1:1 + dec_batch].set(c)
    m = _adaln(cvec, w_ada[0], b_ada[0][None, :])
    mods_ctx = m[0:1].reshape(1, N_MOD, D_MODEL)
    mods_lat = m[1:1 + dec_batch].reshape(dec_batch, N_MOD, D_MODEL)

    wg = jnp.zeros((2 * GATE_RANK, 512), F32)
    wg = wg.at[0:GATE_RANK, 0:256].set(w_gate_fwd[0]).at[GATE_RANK:2 * GATE_RANK, 256:512].set(w_gate_bwd[0])
    w = {
        "g_pre": norm_attn_pre[0][None, :],
        "g_post": norm_attn_post[0][None, :],
        "g_mpre": norm_mlp_pre[0][None, :],
        "g_mpost": norm_mlp_post[0][None, :],
        "w_a": w_in[0][:, C_QA:C_QB].astype(BF16),
        "w_b": w_in[0][:, C_QB:IN_COLS].astype(BF16),
        "wg": wg.astype(BF16),
        "bg": jnp.concatenate([b_gate_fwd[0], b_gate_bwd[0]])[None, :],
        "lam": jnp.stack([lam_q1[0], lam_k1[0], lam_q2[0], lam_k2[0]]),
        "diff_norm": diff_norm[0][None, :],
        "gla_norm": gla_norm[0][None, :],
        "w_out": w_out[0].astype(BF16),
        "w1": w_mlp1[0].astype(BF16),
        "w2": w_mlp2[0].astype(BF16),
    }
    xc2d, oac, ogc, (new_k, new_v, new_sf, new_sb) = _mixers(x_prompt, mods_ctx, w, None, None, None, None, None)
    xl2d, oal, ogl, _ = _mixers(x_sample, mods_lat, w, _rope_tables(x_sample.shape[1]),
                                cache_k, cache_v, state_fwd, state_bwd)
    mods_all = m[0:1 + dec_batch].reshape(1 + dec_batch, N_MOD, D_MODEL)
    y_prompt, y_sample = _out_mlp(xc2d, oac, ogc, xl2d, oal, ogl, mods_all, x_sample.shape[1],
                                  w["w_out"], w["g_post"], w["g_mpre"], w["g_mpost"], w["w1"], w["w2"])
    return (y_prompt.reshape(x_prompt.shape), y_sample.reshape(x_sample.shape), new_k, new_v, new_sf, new_sb)
```

```python
import functools
import math

import numpy as np
import jax
import jax.numpy as jnp
from jax import lax
from jax.experimental import pallas as pl
from jax.experimental.pallas import tpu as pltpu

F32 = jnp.float32
BF16 = jnp.bfloat16

D_MODEL = 1024
GRID_W = 64
H_A = 4
DV_A = 128
DK_A = 64
H_B = 4
DV_B = 128
DK_B = 64
GATE_RANK = 16
GATE_NORM = 16.0
GLA_CHUNK = 64
D_FF = 4 * D_MODEL
N_MOD = 6
ROPE_BASE = 10000.0
EPS = 1e-6
LAM_INIT = 0.8 - 0.6 * math.exp(-0.3 * 0)

C_QA, C_KA, C_VA, C_QB, C_KB, C_VB, C_RB, C_GL = 0, 512, 1024, 1536, 1792, 2048, 2560, 3072
IN_COLS = 3104
B_QB, B_KB, B_VB, B_RB, B_GL, B_COLS = (c - C_QB for c in (C_QB, C_KB, C_VB, C_RB, C_GL, IN_COLS))

V7X_VMEM_BYTES = 64 * 1024 * 1024
VMEM_LIMIT = V7X_VMEM_BYTES - 4 * 1024 * 1024
ROW_CHUNK = 256
MIX_TILE = 1024


def _rms(x, g):
    return x * lax.rsqrt(jnp.mean(x * x, axis=-1, keepdims=True) + EPS) * g


def _dot(a, b):
    return jnp.dot(a, b, preferred_element_type=F32)


def _dot_nt(a, b):
    return lax.dot_general(a, b, (((1,), (1,)), ((), ())), preferred_element_type=F32)


def _run_skewed(gens, secondary=(), deps=None):
    deps = deps or {}
    queues, running, done = [list(gens), list(secondary)], [], set()
    while any(queues) or running:
        for queue in queues:
            for g in queue:
                if all(id(d) in done for d in deps.get(id(g), ())):
                    queue.remove(g)
                    running.append(g)
                    break
        assert running, "dependency cycle"
        for g in list(running):
            try:
                next(g)
            except StopIteration:
                running.remove(g)
                done.add(id(g))


def _prenorm(x_ref, rows, mod_ref, g_ref):
    x = x_ref[rows, :]
    return (_rms(x, g_ref[...]) * (1.0 + mod_ref[1:2, :]) + mod_ref[0:1, :]).astype(BF16)


def _adaln_kernel(c_ref, w_ref, b_ref, o_ref):
    c = c_ref[...]
    s = c * jax.nn.sigmoid(c)
    o_ref[...] = _dot(s.astype(BF16), w_ref[...].astype(BF16)) + b_ref[...]


def _adaln(cvec, w_ada, b_ada):
    rows = cvec.shape[0]
    n = w_ada.shape[1]
    tn = 1536
    return pl.pallas_call(
        _adaln_kernel,
        out_shape=jax.ShapeDtypeStruct((rows, n), F32),
        grid=(n // tn,),
        in_specs=[
            pl.BlockSpec((rows, D_MODEL), lambda j: (0, 0)),
            pl.BlockSpec((D_MODEL, tn), lambda j: (0, j)),
            pl.BlockSpec((1, tn), lambda j: (0, j)),
        ],
        out_specs=pl.BlockSpec((rows, tn), lambda j: (0, j)),
        compiler_params=pltpu.CompilerParams(
            dimension_semantics=("arbitrary",), vmem_limit_bytes=VMEM_LIMIT),
        name="adaln",
    )(cvec, w_ada, b_ada)


def _attn_kernel(*refs, cached, seq):
    if cached:
        (x_ref, mod_ref, g_ref, w_ref, cos_ref, sa_ref, sb_ref, ck_ref, cv_ref, lam_ref, dn_ref,
         o_ref, qkv_s) = refs
    else:
        x_ref, mod_ref, g_ref, w_ref, lam_ref, dn_ref, o_ref, nk_ref, nv_ref, qkv_s = refs
        assert seq == ROW_CHUNK
    rc = ROW_CHUNK
    n_rows = x_ref.shape[0]
    lp = lam_ref[...]
    lam = (jnp.exp(jnp.sum(lp[0:1] * lp[1:2], axis=-1, keepdims=True))
           - jnp.exp(jnp.sum(lp[2:3] * lp[3:4], axis=-1, keepdims=True)) + LAM_INIT)
    lane = lax.broadcasted_iota(jnp.int32, (1, 128), 1)
    scale = DK_A ** -0.5
    m1 = jnp.where(lane < DK_A, scale, 0.0).astype(BF16)
    m2 = jnp.where(lane >= DK_A, scale, 0.0).astype(BF16)
    n_keys = (n_rows + ck_ref.shape[1]) if cached else rc
    ones = jnp.ones((n_keys, 128), BF16)

    def proj(r0):
        rows = slice(r0, r0 + rc)
        h = _prenorm(x_ref, rows, mod_ref, g_ref)
        yield
        pa = _dot(h, w_ref[:, C_QA:C_VA])
        pv = _dot(h, w_ref[:, C_VA:C_QB])
        for blk in range(8):
            t = pa[:, blk * 128:(blk + 1) * 128]
            if cached:
                t = (t * cos_ref[rows, :] + pltpu.roll(t, 112, 1) * sa_ref[rows, :]
                     + pltpu.roll(t, 16, 1) * sb_ref[rows, :])
            elif blk >= 4:
                nk_ref[r0 // seq, blk - 4] = t
            qkv_s[rows, blk * 128:(blk + 1) * 128] = t.astype(BF16)
        qkv_s[rows, C_VA:C_QB] = pv.astype(BF16)
        if not cached:
            for hh in range(H_A):
                nv_ref[r0 // seq, hh] = pv[:, hh * 128:(hh + 1) * 128]

    kv = {}
    res = {}

    def chain(r0, h, branch):
        rows = slice(r0, r0 + rc)
        cols = slice(h * 128, (h + 1) * 128)
        kcols = slice(C_KA + h * 128, C_KA + (h + 1) * 128)
        vcols = slice(C_VA + h * 128, C_VA + (h + 1) * 128)
        key = h if cached else (h, r0)
        if key not in kv:
            if cached:
                k = jnp.concatenate([ck_ref[h].astype(BF16), qkv_s[:, kcols]], axis=0)
                v = jnp.concatenate([cv_ref[h].astype(BF16), qkv_s[:, vcols]], axis=0)
            else:
                k, v = qkv_s[rows, kcols], qkv_s[rows, vcols]
            kv[key] = (k, jnp.concatenate([v, ones], axis=1))
        k, v1 = kv[key]
        s = _dot_nt(qkv_s[rows, cols] * (m2 if branch else m1), k)
        yield
        e = jnp.exp(s - jnp.max(s, axis=-1, keepdims=True)).astype(BF16)
        yield
        res[r0, h, branch] = _dot(e, v1)
        yield
        if branch:
            r1, r2 = res[r0, h, 0], res[r0, h, 1]
            o = r1[:, 0:128] / r1[:, 128:256] - lam * (r2[:, 0:128] / r2[:, 128:256])
            o_ref[rows, cols] = (_rms(o, dn_ref[...]) * (1.0 - LAM_INIT)).astype(BF16)

    projs = {r0: proj(r0) for r0 in range(0, n_rows, rc)}
    gens, deps = list(projs.values()), {}
    for r0 in projs:
        for h in range(H_A):
            for br in range(2):
                c = chain(r0, h, br)
                deps[id(c)] = list(projs.values()) if cached else [projs[r0]]
                gens.append(c)
    _run_skewed(gens, deps=deps)


def _attention(x2d, mods, g_pre, w_in_b, rope_tabs, cache_k, cache_v, lam_p, diff_norm, batch, seq_len):
    cached = cache_k is not None
    tq = MIX_TILE
    n_tiles = batch * seq_len // tq
    const = lambda i: (0, 0)
    in_specs = [
        pl.BlockSpec((tq, D_MODEL), lambda i: (i, 0)),
        pl.BlockSpec((None, N_MOD, D_MODEL), (lambda i: (i, 0, 0)) if cached else (lambda i: (0, 0, 0))),
        pl.BlockSpec((1, D_MODEL), const),
        pl.BlockSpec((D_MODEL, C_QB), const, pipeline_mode=pl.Buffered(1)),
    ]
    args = [x2d, mods, g_pre, w_in_b]
    out_shape = [jax.ShapeDtypeStruct((batch * seq_len, H_A * DV_A), BF16)]
    out_specs = [pl.BlockSpec((tq, H_A * DV_A), lambda i: (i, 0))]
    if cached:
        assert seq_len == tq and mods.shape[0] == batch
        past = cache_k.shape[3]
        cspec = pl.BlockSpec((None, None, H_A, past, 128), lambda i: (i, 0, 0, 0, 0))
        in_specs += [pl.BlockSpec((tq, 128), const)] * 3 + [cspec, cspec]
        args += list(rope_tabs) + [cache_k, cache_v]
    else:
        assert mods.shape[0] == 1
        nb = tq // seq_len
        for _ in range(2):
            out_shape.append(jax.ShapeDtypeStruct((batch, 1, H_A, seq_len, 128), F32))
            out_specs.append(pl.BlockSpec((nb, None, H_A, seq_len, 128), lambda i: (i, 0, 0, 0, 0)))
    in_specs += [pl.BlockSpec((4, DK_A), const), pl.BlockSpec((1, DV_A), const)]
    args += [lam_p, diff_norm]
    return pl.pallas_call(
        functools.partial(_attn_kernel, cached=cached, seq=seq_len),
        out_shape=out_shape,
        grid=(n_tiles,),
        in_specs=in_specs,
        out_specs=out_specs,
        scratch_shapes=[pltpu.VMEM((tq, C_QB), BF16)],
        compiler_params=pltpu.CompilerParams(
            dimension_semantics=("arbitrary",), vmem_limit_bytes=VMEM_LIMIT),
        name="attn_cached" if cached else "attn_ctx",
    )(*args)


def _split3(g):
    hi = g.astype(BF16)
    r1 = g - hi.astype(F32)
    mid = r1.astype(BF16)
    lo = (r1 - mid.astype(F32)).astype(BF16)
    return hi, mid, lo


def _gla_block_stages(q_ref, k_ref, g_ref, v_ref, in_rows, qc, vc, vt_cache, blk, reverse, oa_s, qe_s, ut_s, dec_s):
    C = GLA_CHUNK
    R = in_rows.stop - in_rows.start
    cpb = R // C
    rows = slice(blk * R, (blk + 1) * R)
    ri = lax.broadcasted_iota(jnp.int32, (R, R), 0)
    ci = lax.broadcasted_iota(jnp.int32, (R, R), 1)
    same_chunk = jnp.right_shift(ri, 6) == jnp.right_shift(ci, 6)
    if reverse:
        keep = same_chunk & (ci >= ri)
        last, ref = 0, C // 2
    else:
        keep = same_chunk & (ci <= ri)
        last, ref = C - 1, C // 2 - 1
    tri = jnp.where(keep, 1.0, 0.0).astype(BF16)
    lane = lax.broadcasted_iota(jnp.int32, (1, 2 * DK_B), 1)
    h0 = jnp.where(lane < DK_B, 1.0, 0.0).astype(BF16)
    h1 = jnp.where(lane >= DK_B, 1.0, 0.0).astype(BF16)

    hi, mid, lo = _split3(g_ref[in_rows, qc])
    b3 = _dot(tri, jnp.concatenate([hi, mid, lo], axis=1))
    yield
    b = b3[:, 0:128] + b3[:, 128:256] + b3[:, 256:384]
    q = q_ref[in_rows, qc]
    k = k_ref[in_rows, qc]
    qt, kt, qe, kd = [], [], [], []
    zeros = jnp.zeros((C, 2 * DK_B), BF16)
    for c in range(cpb):
        s = slice(c * C, (c + 1) * C)
        bc = b[s]
        bm = bc[ref:ref + 1]
        bl = bc[last:last + 1]
        qt.append((q[s] * jnp.exp(bc - bm)).astype(BF16))
        kt.append((k[s] * jnp.exp(bm - bc)).astype(BF16))
        qe.append((q[s] * jnp.exp(bc)).astype(BF16))
        kdc = (k[s] * jnp.exp(bl - bc)).astype(BF16)
        kd.append(jnp.concatenate([kdc if j == c else zeros for j in range(cpb)], axis=1))
        dec_s[blk * cpb + c:blk * cpb + c + 1, :] = jnp.exp(bl)
    qe_s[rows, :] = jnp.concatenate(qe, axis=0)
    qt = jnp.concatenate(qt, axis=0)
    kt = jnp.concatenate(kt, axis=0)
    yield
    s0 = _dot_nt(qt * h0, kt)
    s1 = _dot_nt(qt * h1, kt)
    v = v_ref[in_rows, vc]
    if (in_rows.start, vc.start) not in vt_cache:
        vt_cache[in_rows.start, vc.start] = v.T
    ut = _dot(vt_cache[in_rows.start, vc.start], jnp.concatenate(kd, axis=0))
    yield
    a0 = jnp.where(keep, s0, 0.0).astype(BF16)
    a1 = jnp.where(keep, s1, 0.0).astype(BF16)
    br = lax.broadcasted_iota(jnp.int32, ut.shape, 0)
    bcol = lax.broadcasted_iota(jnp.int32, ut.shape, 1)
    ut = jnp.where((br < DV_B) == ((bcol & (2 * DK_B - 1)) < DK_B), ut, 0.0)
    for c in range(cpb):
        ut_s[blk * cpb + c] = ut[:, c * 2 * DK_B:(c + 1) * 2 * DK_B]
    yield
    oa_s[rows, :] = jnp.concatenate([_dot(a0, v[:, 0:DV_B]), _dot(a1, v[:, DV_B:2 * DV_B])], axis=1)


def _state_to_t(s_pair):
    z = jnp.zeros((DK_B, DV_B), F32)
    a = jnp.concatenate([s_pair[0], z], axis=0).T
    b = jnp.concatenate([z, s_pair[1]], axis=0).T
    return jnp.concatenate([a, b], axis=0)


def _gla_kernel(*refs, n_chunks, n_elems, has_state):
    if has_state:
        (x_ref, mod_ref, g_ref, w_ref, wg_ref, bg_ref, gn_ref, sf_ref, sb_ref, og_ref, *scratch) = refs
    else:
        (x_ref, mod_ref, g_ref, w_ref, wg_ref, bg_ref, gn_ref, og_ref, nsf_ref, nsb_ref, *scratch) = refs
    q_s, k_s, gf_s, gb_s, rb_s, v_s, oa, oi, st, qe, ut, dec = scratch
    C = GLA_CHUNK
    R = ROW_CHUNK
    cpb = R // C
    L = n_chunks * C
    n_blk = L // R
    n_pairs = H_B // 2

    def proj(r0):
        rows = slice(r0, r0 + R)
        h = _prenorm(x_ref, rows, mod_ref, g_ref)
        yield
        pg = _dot(h, w_ref[:, C_RB:IN_COLS])
        pb = _dot(h, w_ref[:, C_QB:C_RB])
        z = _dot(pg[:, B_GL - B_RB:B_COLS - B_RB].astype(BF16), wg_ref[...]) + bg_ref[...]
        ls = (jnp.minimum(z, 0.0) - jnp.log(1.0 + jnp.exp(-jnp.abs(z)))) * (1.0 / GATE_NORM)
        gf_s[rows, :] = ls[:, 0:256]
        gb_s[rows, :] = ls[:, 256:512]
        rb_s[rows, :] = pg[:, 0:512]
        q_s[rows, :] = pb[:, B_QB:B_KB] * (DK_B ** -0.5)
        k_s[rows, :] = pb[:, B_KB:B_VB]
        v_s[rows, :] = pb[:, B_VB:B_RB].astype(BF16)

    def scan_stage(e, p, d, blk):
        ch = 2 * p + d
        if blk == (n_blk - 1 if d else 0):
            if has_state:
                assert n_elems == 1
                st[e, ch] = _state_to_t((sb_ref if d else sf_ref)[2 * p:2 * p + 2])
            else:
                st[e, ch] = jnp.zeros(st.shape[2:], F32)
        for i in range(cpb):
            c = blk * cpb + (cpb - 1 - i if d else i)
            rows = slice(c * C, (c + 1) * C)
            s = st[e, ch]
            oi[e, ch, rows, :] = _dot_nt(qe[e, ch, rows, :], s.astype(BF16))
            st[e, ch] = s * dec[e, ch, c:c + 1, :] + ut[e, ch, c]
            yield
        if not has_state and blk == (0 if d else n_blk - 1):
            s = st[e, ch]
            dst_ref = nsb_ref if d else nsf_ref
            dst_ref[e, 2 * p] = s[0:DV_B, :].T[0:DK_B, :]
            dst_ref[e, 2 * p + 1] = s[DV_B:2 * DV_B, :].T[DK_B:2 * DK_B, :]

    def fin_stage(e, p, blk):
        rows = slice(blk * R, (blk + 1) * R)
        out_rows = slice(e * L + blk * R, e * L + (blk + 1) * R)
        o = ((oa[e, 2 * p, rows, :] + oi[e, 2 * p, rows, :])
             + (oa[e, 2 * p + 1, rows, :] + oi[e, 2 * p + 1, rows, :]))
        for j in range(2):
            cols = slice((2 * p + j) * DV_B, (2 * p + j + 1) * DV_B)
            oj = _rms(o[:, j * DV_B:(j + 1) * DV_B], gn_ref[...])
            rj = rb_s[out_rows, cols]
            og_ref[out_rows, cols] = (oj * (rj * jax.nn.sigmoid(rj))).astype(BF16)
        yield

    vt_cache = {}
    projs, blocks, scans, deps = {}, [], {}, {}
    for e in range(n_elems):
        for t in range(n_blk):
            blk = t // 2 if t % 2 == 0 else n_blk - 1 - t // 2
            projs[e, blk] = proj(e * L + blk * R)
    for e in range(n_elems):
        for t in range(n_blk):
            for p in range(n_pairs):
                for d in range(2):
                    blk = n_blk - 1 - t if d else t
                    ch = 2 * p + d
                    in_rows = slice(e * L + blk * R, e * L + (blk + 1) * R)
                    g = _gla_block_stages(
                        q_s, k_s, gb_s if d else gf_s, v_s, in_rows,
                        slice(p * 2 * DK_B, (p + 1) * 2 * DK_B), slice(p * 2 * DV_B, (p + 1) * 2 * DV_B),
                        vt_cache, blk, bool(d), oa.at[e, ch], qe.at[e, ch], ut.at[e, ch], dec.at[e, ch])
                    deps[id(g)] = [projs[e, blk]]
                    blocks.append(g)
                    s = scan_stage(e, p, d, blk)
                    prev = scans.get((e, p, d, blk + 1 if d else blk - 1))
                    deps[id(s)] = [g] + ([prev] if prev is not None else [])
                    scans[e, p, d, blk] = s
    aux = []
    for e in range(n_elems):
        for t in range(n_blk):
            for p in range(n_pairs):
                aux.append(scans[e, p, 0, t])
                aux.append(scans[e, p, 1, n_blk - 1 - t])
        for blk in range(n_blk):
            for p in range(n_pairs):
                f = fin_stage(e, p, blk)
                deps[id(f)] = [scans[e, p, 0, blk], scans[e, p, 1, blk]]
                aux.append(f)
    _run_skewed(list(projs.values()) + blocks, aux, deps)


def _gla(x2d, mods, g_pre, w_in_b, wg, bg, gla_norm, state_f, state_b, batch, seq_len):
    has_state = state_f is not None
    n_chunks = seq_len // GLA_CHUNK
    L = seq_len
    rows = MIX_TILE
    n_elems = rows // L
    n_ch = H_B
    wk, wv = H_B * DK_B, H_B * DV_B
    const = lambda i: (0, 0)
    in_specs = [
        pl.BlockSpec((rows, D_MODEL), lambda i: (i, 0)),
        pl.BlockSpec((None, N_MOD, D_MODEL), (lambda i: (i, 0, 0)) if has_state else (lambda i: (0, 0, 0))),
        pl.BlockSpec((1, D_MODEL), const),
        pl.BlockSpec((D_MODEL, IN_COLS), const, pipeline_mode=pl.Buffered(1)),
        pl.BlockSpec((2 * GATE_RANK, 2 * wk), const),
        pl.BlockSpec((1, 2 * wk), const),
        pl.BlockSpec((1, DV_B), const),
    ]
    args = [x2d, mods, g_pre, w_in_b, wg, bg, gla_norm]
    out_shape = [jax.ShapeDtypeStruct((batch * L, wv), BF16)]
    out_specs = [pl.BlockSpec((rows, wv), lambda i: (i, 0))]
    if has_state:
        assert n_elems == 1 and mods.shape[0] == batch
        st_spec = pl.BlockSpec((None, None, H_B, DK_B, DV_B), lambda i: (i, 0, 0, 0, 0))
        in_specs += [st_spec, st_spec]
        args += [state_f, state_b]
    else:
        assert mods.shape[0] == 1
        for _ in range(2):
            out_shape.append(jax.ShapeDtypeStruct((batch, 1, H_B, DK_B, DV_B), F32))
            out_specs.append(pl.BlockSpec((n_elems, None, H_B, DK_B, DV_B), lambda i: (i, 0, 0, 0, 0)))
    return pl.pallas_call(
        functools.partial(_gla_kernel, n_chunks=n_chunks, n_elems=n_elems, has_state=has_state),
        out_shape=out_shape,
        grid=(batch // n_elems,),
        in_specs=in_specs,
        out_specs=out_specs,
        scratch_shapes=[
            pltpu.VMEM((rows, wk), F32),
            pltpu.VMEM((rows, wk), F32),
            pltpu.VMEM((rows, wk), F32),
            pltpu.VMEM((rows, wk), F32),
            pltpu.VMEM((rows, wv), F32),
            pltpu.VMEM((rows, wv), BF16),
            pltpu.VMEM((n_elems, n_ch, L, 2 * DV_B), F32),
            pltpu.VMEM((n_elems, n_ch, L, 2 * DV_B), F32),
            pltpu.VMEM((n_elems, n_ch, 2 * DV_B, 2 * DK_B), F32),
            pltpu.VMEM((n_elems, n_ch, L, 2 * DK_B), BF16),
            pltpu.VMEM((n_elems, n_ch, n_chunks, 2 * DV_B, 2 * DK_B), F32),
            pltpu.VMEM((n_elems, n_ch, max(n_chunks, 8), 2 * DK_B), F32),
        ],
        compiler_params=pltpu.CompilerParams(
            dimension_semantics=("arbitrary",), vmem_limit_bytes=VMEM_LIMIT),
        name="gla_state" if has_state else "gla_ctx",
    )(*args)


def _out_mlp_kernel(xc_ref, oac_ref, ogc_ref, xl_ref, oal_ref, ogl_ref, mod_ref, wo_ref, gpost_ref, gmpre_ref,
                    gmpost_ref, w1_ref, w2_ref, yc_ref, yl_ref, *, n_ctx_tiles):
    rc = ROW_CHUNK
    tf = 1024

    def run(x_ref, oa_ref, og_ref, y_ref):
        def chunk(r0):
            rows = slice(r0, r0 + rc)
            mix = _dot(oa_ref[rows, :], wo_ref[0:512, :]) + _dot(og_ref[rows, :], wo_ref[512:1024, :])
            yield
            x1 = x_ref[rows, :] + mod_ref[2:3, :] * _rms(mix, gpost_ref[...])
            h2 = (_rms(x1, gmpre_ref[...]) * (1.0 + mod_ref[4:5, :]) + mod_ref[3:4, :]).astype(BF16)
            yield
            acc = None
            for j in range(D_FF // tf):
                u = jnp.maximum(_dot(h2, w1_ref[:, j * tf:(j + 1) * tf]), 0.0)
                part = _dot((u * u).astype(BF16), w2_ref[j * tf:(j + 1) * tf, :])
                acc = part if acc is None else acc + part
                yield
            y_ref[rows, :] = x1 + mod_ref[5:6, :] * _rms(acc, gmpost_ref[...])

        _run_skewed([chunk(r0) for r0 in range(0, x_ref.shape[0], rc)])

    is_ctx = pl.program_id(0) < n_ctx_tiles

    @pl.when(is_ctx)
    def _():
        run(xc_ref, oac_ref, ogc_ref, yc_ref)

    @pl.when(jnp.logical_not(is_ctx))
    def _():
        run(xl_ref, oal_ref, ogl_ref, yl_ref)


def _out_mlp(xc2d, oac, ogc, xl2d, oal, ogl, mods, lat_seq, w_out_b, g_post, g_mpre, g_mpost, w1_b, w2_b):
    tm = 512
    n_c, n_l = xc2d.shape[0] // tm, xl2d.shape[0] // tm
    per_b = lat_seq // tm
    ctx_row = lambda i: (jnp.minimum(i, n_c - 1), 0)
    lat_row = lambda i: (jnp.maximum(i - n_c, 0), 0)
    mod_idx = lambda i: (jnp.where(i < n_c, 0, 1 + jnp.maximum(i - n_c, 0) // per_b), 0, 0)
    const = lambda i: (0, 0)
    return pl.pallas_call(
        functools.partial(_out_mlp_kernel, n_ctx_tiles=n_c),
        out_shape=[jax.ShapeDtypeStruct(xc2d.shape, F32), jax.ShapeDtypeStruct(xl2d.shape, F32)],
        grid=(n_c + n_l,),
        in_specs=[
            pl.BlockSpec((tm, D_MODEL), ctx_row),
            pl.BlockSpec((tm, 512), ctx_row),
            pl.BlockSpec((tm, 512), ctx_row),
            pl.BlockSpec((tm, D_MODEL), lat_row),
            pl.BlockSpec((tm, 512), lat_row),
            pl.BlockSpec((tm, 512), lat_row),
            pl.BlockSpec((None, N_MOD, D_MODEL), mod_idx),
            pl.BlockSpec((D_MODEL, D_MODEL), const, pipeline_mode=pl.Buffered(1)),
            pl.BlockSpec((1, D_MODEL), const),
            pl.BlockSpec((1, D_MODEL), const),
            pl.BlockSpec((1, D_MODEL), const),
            pl.BlockSpec((D_MODEL, D_FF), const, pipeline_mode=pl.Buffered(1)),
            pl.BlockSpec((D_FF, D_MODEL), const, pipeline_mode=pl.Buffered(1)),
        ],
        out_specs=[pl.BlockSpec((tm, D_MODEL), ctx_row), pl.BlockSpec((tm, D_MODEL), lat_row)],
        compiler_params=pltpu.CompilerParams(
            dimension_semantics=("arbitrary",), vmem_limit_bytes=VMEM_LIMIT),
        name="out_mlp",
    )(xc2d, oac, ogc, xl2d, oal, ogl, mods, w_out_b, g_post, g_mpre, g_mpost, w1_b, w2_b)


def _rope_tables(n_lat):
    pos = np.arange(n_lat)
    row_pos = (pos // GRID_W).astype(np.float64)
    col_pos = (pos % GRID_W).astype(np.float64)
    half = DK_A // 4
    inv = ROPE_BASE ** (-np.arange(half, dtype=np.float64) / half)
    lane = np.arange(128)
    in64 = lane % 64
    in32 = in64 % 32
    p = np.where((in64 < 32)[None, :], row_pos[:, None], col_pos[:, None])
    ang = p * inv[in32 % half][None, :]
    cos, sin = np.cos(ang), np.sin(ang)
    first = (in32 < half)[None, :]
    sa = np.where(first, -sin, 0.0)
    sb = np.where(first, 0.0, sin)
    return tuple(jnp.asarray(t, dtype=F32) for t in (cos, sa, sb))


def _mixers(x, mods, w, rope_tabs, cache_k, cache_v, state_f, state_b):
    batch, seq_len, _ = x.shape
    x2d = x.reshape(batch * seq_len, D_MODEL)
    aouts = _attention(x2d, mods, w["g_pre"], w["w_in"], rope_tabs, cache_k, cache_v, w["lam"], w["diff_norm"],
                       batch, seq_len)
    gouts = _gla(x2d, mods, w["g_pre"], w["w_in"], w["wg"], w["bg"], w["gla_norm"], state_f, state_b,
                 batch, seq_len)
    return x2d, aouts[0], gouts[0], tuple(aouts[1:]) + tuple(gouts[1:])


def kernel(x_prompt, x_sample, c, cache_k, cache_v, state_fwd, state_bwd, c_ctx, w_ada, b_ada,
           norm_attn_pre, norm_attn_post, norm_mlp_pre, norm_mlp_post, w_in, w_gate_fwd, b_gate_fwd,
           w_gate_bwd, b_gate_bwd, lam_q1, lam_k1, lam_q2, lam_k2, diff_norm, gla_norm, w_out,
           w_mlp1, w_mlp2):
    dec_batch = c.shape[0]
    rows = 16
    cvec = jnp.zeros((rows, D_MODEL), F32).at[0].set(c_ctx).at[1:1 + dec_batch].set(c)
    m = _adaln(cvec, w_ada[0], b_ada[0][None, :])
    mods_ctx = m[0:1].reshape(1, N_MOD, D_MODEL)
    mods_lat = m[1:1 + dec_batch].reshape(dec_batch, N_MOD, D_MODEL)

    wg = jnp.zeros((2 * GATE_RANK, 512), F32)
    wg = wg.at[0:GATE_RANK, 0:256].set(w_gate_fwd[0]).at[GATE_RANK:2 * GATE_RANK, 256:512].set(w_gate_bwd[0])
    w = {
        "g_pre": norm_attn_pre[0][None, :],
        "g_post": norm_attn_post[0][None, :],
        "g_mpre": norm_mlp_pre[0][None, :],
        "g_mpost": norm_mlp_post[0][None, :],
        "w_in": w_in[0].astype(BF16),
        "wg": wg.astype(BF16),
        "bg": jnp.concatenate([b_gate_fwd[0], b_gate_bwd[0]])[None, :],
        "lam": jnp.stack([lam_q1[0], lam_k1[0], lam_q2[0], lam_k2[0]]),
        "diff_norm": diff_norm[0][None, :],
        "gla_norm": gla_norm[0][None, :],
        "w_out": w_out[0].astype(BF16),
        "w1": w_mlp1[0].astype(BF16),
        "w2": w_mlp2[0].astype(BF16),
    }
    xc2d, oac, ogc, (new_k, new_v, new_sf, new_sb) = _mixers(x_prompt, mods_ctx, w, None, None, None, None, None)
    xl2d, oal, ogl, _ = _mixers(x_sample, mods_lat, w, _rope_tables(x_sample.shape[1]),
                                cache_k, cache_v, state_fwd, state_bwd)
    mods_all = m[0:1 + dec_batch].reshape(1 + dec_batch, N_MOD, D_MODEL)
    y_prompt, y_sample = _out_mlp(xc2d, oac, ogc, xl2d, oal, ogl, mods_all, x_sample.shape[1],
                                  w["w_out"], w["g_post"], w["g_mpre"], w["g_mpost"], w["w1"], w["w2"])
    return (y_prompt.reshape(x_prompt.shape), y_sample.reshape(x_sample.shape), new_k, new_v, new_sf, new_sb)
```

```python
import functools
import math

import numpy as np
import jax
import jax.numpy as jnp
from jax import lax
from jax.experimental import pallas as pl
from jax.experimental.pallas import tpu as pltpu

F32 = jnp.float32
BF16 = jnp.bfloat16

D_MODEL = 1024
GRID_W = 64
H_A = 4
DV_A = 128
DK_A = 64
H_B = 4
DV_B = 128
DK_B = 64
GATE_RANK = 16
GATE_NORM = 16.0
GLA_CHUNK = 64
D_FF = 4 * D_MODEL
N_MOD = 6
ROPE_BASE = 10000.0
EPS = 1e-6
LAM_INIT = 0.8 - 0.6 * math.exp(-0.3 * 0)

C_QA, C_KA, C_VA, C_QB, C_KB, C_VB, C_RB, C_GL = 0, 512, 1024, 1536, 1792, 2048, 2560, 3072
IN_COLS = 3104
B_QB, B_KB, B_VB, B_RB, B_GL, B_COLS = (c - C_QB for c in (C_QB, C_KB, C_VB, C_RB, C_GL, IN_COLS))

V7X_VMEM_BYTES = 64 * 1024 * 1024
VMEM_LIMIT = V7X_VMEM_BYTES - 4 * 1024 * 1024
ROW_CHUNK = 256
MIX_TILE = 1024
W_STAGE_ROWS = 512


def _rms(x, g):
    return x * lax.rsqrt(jnp.mean(x * x, axis=-1, keepdims=True) + EPS) * g


def _dot(a, b):
    return jnp.dot(a, b, preferred_element_type=F32)


def _dot_nt(a, b):
    return lax.dot_general(a, b, (((1,), (1,)), ((), ())), preferred_element_type=F32)


def _run_skewed(gens, secondary=(), deps=None):
    deps = deps or {}
    queues, running, done = [list(gens), list(secondary)], [], set()
    while any(queues) or running:
        for queue in queues:
            for g in queue:
                if all(id(d) in done for d in deps.get(id(g), ())):
                    queue.remove(g)
                    running.append(g)
                    break
        assert running, "dependency cycle"
        for g in list(running):
            try:
                next(g)
            except StopIteration:
                running.remove(g)
                done.add(id(g))


def _prenorm(x_ref, rows, mod_ref, g_ref):
    x = x_ref[rows, :]
    return (_rms(x, g_ref[...]) * (1.0 + mod_ref[1:2, :]) + mod_ref[0:1, :]).astype(BF16)


def _adaln_kernel(c_ref, w_ref, b_ref, o_ref):
    c = c_ref[...]
    s = c * jax.nn.sigmoid(c)
    o_ref[...] = _dot(s.astype(BF16), w_ref[...].astype(BF16)) + b_ref[...]


def _adaln(cvec, w_ada, b_ada):
    rows = cvec.shape[0]
    n = w_ada.shape[1]
    tn = 1536
    return pl.pallas_call(
        _adaln_kernel,
        out_shape=jax.ShapeDtypeStruct((rows, n), F32),
        grid=(n // tn,),
        in_specs=[
            pl.BlockSpec((rows, D_MODEL), lambda j: (0, 0)),
            pl.BlockSpec((D_MODEL, tn), lambda j: (0, j)),
            pl.BlockSpec((1, tn), lambda j: (0, j)),
        ],
        out_specs=pl.BlockSpec((rows, tn), lambda j: (0, j)),
        compiler_params=pltpu.CompilerParams(
            dimension_semantics=("arbitrary",), vmem_limit_bytes=VMEM_LIMIT),
        name="adaln",
    )(cvec, w_ada, b_ada)


def _attn_kernel(*refs, cached, seq):
    if cached:
        (x_ref, mod_ref, g_ref, w_ref, cos_ref, sa_ref, sb_ref, ck_ref, cv_ref, lam_ref, dn_ref,
         o_ref, qkv_s) = refs
    else:
        x_ref, mod_ref, g_ref, w_ref, lam_ref, dn_ref, o_ref, nk_ref, nv_ref, qkv_s = refs
        assert seq == ROW_CHUNK
    rc = ROW_CHUNK
    n_rows = x_ref.shape[0]
    lp = lam_ref[...]
    lam = (jnp.exp(jnp.sum(lp[0:1] * lp[1:2], axis=-1, keepdims=True))
           - jnp.exp(jnp.sum(lp[2:3] * lp[3:4], axis=-1, keepdims=True)) + LAM_INIT)
    lane = lax.broadcasted_iota(jnp.int32, (1, 128), 1)
    scale = DK_A ** -0.5
    m1 = jnp.where(lane < DK_A, scale, 0.0).astype(BF16)
    m2 = jnp.where(lane >= DK_A, scale, 0.0).astype(BF16)
    n_keys = (n_rows + ck_ref.shape[1]) if cached else rc
    ones = jnp.ones((n_keys, 128), BF16)

    def proj(r0):
        rows = slice(r0, r0 + rc)
        h = _prenorm(x_ref, rows, mod_ref, g_ref)
        yield
        pa = _dot(h, w_ref[:, C_QA:C_VA])
        pv = _dot(h, w_ref[:, C_VA:C_QB])
        for blk in range(8):
            t = pa[:, blk * 128:(blk + 1) * 128]
            if cached:
                t = (t * cos_ref[rows, :] + pltpu.roll(t, 112, 1) * sa_ref[rows, :]
                     + pltpu.roll(t, 16, 1) * sb_ref[rows, :])
            elif blk >= 4:
                nk_ref[r0 // seq, blk - 4] = t
            qkv_s[rows, blk * 128:(blk + 1) * 128] = t.astype(BF16)
        qkv_s[rows, C_VA:C_QB] = pv.astype(BF16)
        if not cached:
            for hh in range(H_A):
                nv_ref[r0 // seq, hh] = pv[:, hh * 128:(hh + 1) * 128]

    kv = {}
    res = {}

    def chain(r0, h, branch):
        rows = slice(r0, r0 + rc)
        cols = slice(h * 128, (h + 1) * 128)
        kcols = slice(C_KA + h * 128, C_KA + (h + 1) * 128)
        vcols = slice(C_VA + h * 128, C_VA + (h + 1) * 128)
        key = h if cached else (h, r0)
        if key not in kv:
            if cached:
                k = jnp.concatenate([ck_ref[h].astype(BF16), qkv_s[:, kcols]], axis=0)
                v = jnp.concatenate([cv_ref[h].astype(BF16), qkv_s[:, vcols]], axis=0)
            else:
                k, v = qkv_s[rows, kcols], qkv_s[rows, vcols]
            kv[key] = (k, jnp.concatenate([v, ones], axis=1))
        k, v1 = kv[key]
        s = _dot_nt(qkv_s[rows, cols] * (m2 if branch else m1), k)
        yield
        e = jnp.exp(s - jnp.max(s, axis=-1, keepdims=True)).astype(BF16)
        yield
        res[r0, h, branch] = _dot(e, v1)
        yield
        if branch:
            r1, r2 = res[r0, h, 0], res[r0, h, 1]
            o = r1[:, 0:128] / r1[:, 128:256] - lam * (r2[:, 0:128] / r2[:, 128:256])
            o_ref[rows, cols] = (_rms(o, dn_ref[...]) * (1.0 - LAM_INIT)).astype(BF16)

    projs = {r0: proj(r0) for r0 in range(0, n_rows, rc)}
    gens, deps = list(projs.values()), {}
    for r0 in projs:
        for h in range(H_A):
            for br in range(2):
                c = chain(r0, h, br)
                deps[id(c)] = list(projs.values()) if cached else [projs[r0]]
                gens.append(c)
    _run_skewed(gens, deps=deps)


def _attention(x2d, mods, g_pre, w_in_b, rope_tabs, cache_k, cache_v, lam_p, diff_norm, batch, seq_len):
    cached = cache_k is not None
    tq = MIX_TILE
    n_tiles = batch * seq_len // tq
    const = lambda i: (0, 0)
    in_specs = [
        pl.BlockSpec((tq, D_MODEL), lambda i: (i, 0)),
        pl.BlockSpec((None, N_MOD, D_MODEL), (lambda i: (i, 0, 0)) if cached else (lambda i: (0, 0, 0))),
        pl.BlockSpec((1, D_MODEL), const),
        pl.BlockSpec((D_MODEL, C_QB), const, pipeline_mode=pl.Buffered(1)),
    ]
    args = [x2d, mods, g_pre, w_in_b]
    out_shape = [jax.ShapeDtypeStruct((batch * seq_len, H_A * DV_A), BF16)]
    out_specs = [pl.BlockSpec((tq, H_A * DV_A), lambda i: (i, 0))]
    if cached:
        assert seq_len == tq and mods.shape[0] == batch
        past = cache_k.shape[3]
        cspec = pl.BlockSpec((None, None, H_A, past, 128), lambda i: (i, 0, 0, 0, 0))
        in_specs += [pl.BlockSpec((tq, 128), const)] * 3 + [cspec, cspec]
        args += list(rope_tabs) + [cache_k, cache_v]
    else:
        assert mods.shape[0] == 1
        nb = tq // seq_len
        for _ in range(2):
            out_shape.append(jax.ShapeDtypeStruct((batch, 1, H_A, seq_len, 128), F32))
            out_specs.append(pl.BlockSpec((nb, None, H_A, seq_len, 128), lambda i: (i, 0, 0, 0, 0)))
    in_specs += [pl.BlockSpec((4, DK_A), const), pl.BlockSpec((1, DV_A), const)]
    args += [lam_p, diff_norm]
    return pl.pallas_call(
        functools.partial(_attn_kernel, cached=cached, seq=seq_len),
        out_shape=out_shape,
        grid=(n_tiles,),
        in_specs=in_specs,
        out_specs=out_specs,
        scratch_shapes=[pltpu.VMEM((tq, C_QB), BF16)],
        compiler_params=pltpu.CompilerParams(
            dimension_semantics=("arbitrary",), vmem_limit_bytes=VMEM_LIMIT),
        name="attn_cached" if cached else "attn_ctx",
    )(*args)


def _split3(g):
    hi = g.astype(BF16)
    r1 = g - hi.astype(F32)
    mid = r1.astype(BF16)
    lo = (r1 - mid.astype(F32)).astype(BF16)
    return hi, mid, lo


def _gla_block_stages(q_ref, k_ref, g_ref, v_ref, in_rows, qc, vc, vt_cache, blk, reverse, oa_s, qe_s, ut_s, dec_s):
    C = GLA_CHUNK
    R = in_rows.stop - in_rows.start
    cpb = R // C
    rows = slice(blk * R, (blk + 1) * R)
    ri = lax.broadcasted_iota(jnp.int32, (R, R), 0)
    ci = lax.broadcasted_iota(jnp.int32, (R, R), 1)
    same_chunk = jnp.right_shift(ri, 6) == jnp.right_shift(ci, 6)
    if reverse:
        keep = same_chunk & (ci >= ri)
        last, ref = 0, C // 2
    else:
        keep = same_chunk & (ci <= ri)
        last, ref = C - 1, C // 2 - 1
    tri = jnp.where(keep, 1.0, 0.0).astype(BF16)
    lane = lax.broadcasted_iota(jnp.int32, (1, 2 * DK_B), 1)
    h0 = jnp.where(lane < DK_B, 1.0, 0.0).astype(BF16)
    h1 = jnp.where(lane >= DK_B, 1.0, 0.0).astype(BF16)

    hi, mid, lo = _split3(g_ref[in_rows, qc])
    b3 = _dot(tri, jnp.concatenate([hi, mid, lo], axis=1))
    yield
    b = b3[:, 0:128] + b3[:, 128:256] + b3[:, 256:384]
    q = q_ref[in_rows, qc]
    k = k_ref[in_rows, qc]
    qt, kt, qe, kd = [], [], [], []
    zeros = jnp.zeros((C, 2 * DK_B), BF16)
    for c in range(cpb):
        s = slice(c * C, (c + 1) * C)
        bc = b[s]
        bm = bc[ref:ref + 1]
        bl = bc[last:last + 1]
        qt.append((q[s] * jnp.exp(bc - bm)).astype(BF16))
        kt.append((k[s] * jnp.exp(bm - bc)).astype(BF16))
        qe.append((q[s] * jnp.exp(bc)).astype(BF16))
        kdc = (k[s] * jnp.exp(bl - bc)).astype(BF16)
        kd.append(jnp.concatenate([kdc if j == c else zeros for j in range(cpb)], axis=1))
        dec_s[blk * cpb + c:blk * cpb + c + 1, :] = jnp.exp(bl)
    qe_s[rows, :] = jnp.concatenate(qe, axis=0)
    qt = jnp.concatenate(qt, axis=0)
    kt = jnp.concatenate(kt, axis=0)
    yield
    s0 = _dot_nt(qt * h0, kt)
    s1 = _dot_nt(qt * h1, kt)
    v = v_ref[in_rows, vc]
    if (in_rows.start, vc.start) not in vt_cache:
        vt_cache[in_rows.start, vc.start] = v.T
    ut = _dot(vt_cache[in_rows.start, vc.start], jnp.concatenate(kd, axis=0))
    yield
    a0 = jnp.where(keep, s0, 0.0).astype(BF16)
    a1 = jnp.where(keep, s1, 0.0).astype(BF16)
    br = lax.broadcasted_iota(jnp.int32, ut.shape, 0)
    bcol = lax.broadcasted_iota(jnp.int32, ut.shape, 1)
    ut = jnp.where((br < DV_B) == ((bcol & (2 * DK_B - 1)) < DK_B), ut, 0.0)
    for c in range(cpb):
        ut_s[blk * cpb + c] = ut[:, c * 2 * DK_B:(c + 1) * 2 * DK_B]
    yield
    oa_s[rows, :] = jnp.concatenate([_dot(a0, v[:, 0:DV_B]), _dot(a1, v[:, DV_B:2 * DV_B])], axis=1)


def _state_to_t(s_pair):
    z = jnp.zeros((DK_B, DV_B), F32)
    a = jnp.concatenate([s_pair[0], z], axis=0).T
    b = jnp.concatenate([z, s_pair[1]], axis=0).T
    return jnp.concatenate([a, b], axis=0)


def _gla_kernel(*refs, n_chunks, n_elems, has_state):
    if has_state:
        (x_ref, mod_ref, g_ref, w_ref, wg_ref, bg_ref, gn_ref, sf_ref, sb_ref, og_ref, *scratch) = refs
    else:
        (x_ref, mod_ref, g_ref, w_ref, wg_ref, bg_ref, gn_ref, og_ref, nsf_ref, nsb_ref, *scratch) = refs
    q_s, k_s, gf_s, gb_s, rb_s, v_s, oa, oi, st, qe, ut, dec = scratch
    C = GLA_CHUNK
    R = ROW_CHUNK
    cpb = R // C
    L = n_chunks * C
    n_blk = L // R
    n_pairs = H_B // 2

    def proj(r0):
        rows = slice(r0, r0 + R)
        h = _prenorm(x_ref, rows, mod_ref, g_ref)
        yield
        pg = _dot(h, w_ref[:, C_RB:IN_COLS])
        pb = _dot(h, w_ref[:, C_QB:C_RB])
        z = _dot(pg[:, B_GL - B_RB:B_COLS - B_RB].astype(BF16), wg_ref[...]) + bg_ref[...]
        ls = (jnp.minimum(z, 0.0) - jnp.log(1.0 + jnp.exp(-jnp.abs(z)))) * (1.0 / GATE_NORM)
        gf_s[rows, :] = ls[:, 0:256]
        gb_s[rows, :] = ls[:, 256:512]
        rb_s[rows, :] = pg[:, 0:512]
        q_s[rows, :] = pb[:, B_QB:B_KB] * (DK_B ** -0.5)
        k_s[rows, :] = pb[:, B_KB:B_VB]
        v_s[rows, :] = pb[:, B_VB:B_RB].astype(BF16)

    def scan_stage(e, p, d, blk):
        ch = 2 * p + d
        if blk != (n_blk - 1 if d else 0):
            s = st[e, ch]
        elif has_state:
            assert n_elems == 1
            s = _state_to_t((sb_ref if d else sf_ref)[2 * p:2 * p + 2])
        else:
            s = jnp.zeros(st.shape[2:], F32)
        for i in range(cpb):
            c = blk * cpb + (cpb - 1 - i if d else i)
            rows = slice(c * C, (c + 1) * C)
            oi[e, ch, rows, :] = _dot_nt(qe[e, ch, rows, :], s.astype(BF16))
            s = s * dec[e, ch, c:c + 1, :] + ut[e, ch, c]
            yield
        if blk != (0 if d else n_blk - 1):
            st[e, ch] = s
        elif not has_state:
            dst_ref = nsb_ref if d else nsf_ref
            dst_ref[e, 2 * p] = s[0:DV_B, :].T[0:DK_B, :]
            dst_ref[e, 2 * p + 1] = s[DV_B:2 * DV_B, :].T[DK_B:2 * DK_B, :]

    def fin_stage(e, p, blk):
        rows = slice(blk * R, (blk + 1) * R)
        out_rows = slice(e * L + blk * R, e * L + (blk + 1) * R)
        o = ((oa[e, 2 * p, rows, :] + oi[e, 2 * p, rows, :])
             + (oa[e, 2 * p + 1, rows, :] + oi[e, 2 * p + 1, rows, :]))
        for j in range(2):
            cols = slice((2 * p + j) * DV_B, (2 * p + j + 1) * DV_B)
            oj = _rms(o[:, j * DV_B:(j + 1) * DV_B], gn_ref[...])
            rj = rb_s[out_rows, cols]
            og_ref[out_rows, cols] = (oj * (rj * jax.nn.sigmoid(rj))).astype(BF16)
        yield

    vt_cache = {}
    projs, blocks, scans, deps = {}, [], {}, {}
    for e in range(n_elems):
        for t in range(n_blk):
            blk = t // 2 if t % 2 == 0 else n_blk - 1 - t // 2
            projs[e, blk] = proj(e * L + blk * R)
    for e in range(n_elems):
        for t in range(n_blk):
            for p in range(n_pairs):
                for d in range(2):
                    blk = n_blk - 1 - t if d else t
                    ch = 2 * p + d
                    in_rows = slice(e * L + blk * R, e * L + (blk + 1) * R)
                    g = _gla_block_stages(
                        q_s, k_s, gb_s if d else gf_s, v_s, in_rows,
                        slice(p * 2 * DK_B, (p + 1) * 2 * DK_B), slice(p * 2 * DV_B, (p + 1) * 2 * DV_B),
                        vt_cache, blk, bool(d), oa.at[e, ch], qe.at[e, ch], ut.at[e, ch], dec.at[e, ch])
                    deps[id(g)] = [projs[e, blk]]
                    blocks.append(g)
                    s = scan_stage(e, p, d, blk)
                    prev = scans.get((e, p, d, blk + 1 if d else blk - 1))
                    deps[id(s)] = [g] + ([prev] if prev is not None else [])
                    scans[e, p, d, blk] = s
    aux = []
    for e in range(n_elems):
        for t in range(n_blk):
            for p in range(n_pairs):
                aux.append(scans[e, p, 0, t])
                aux.append(scans[e, p, 1, n_blk - 1 - t])
        for blk in range(n_blk):
            for p in range(n_pairs):
                f = fin_stage(e, p, blk)
                deps[id(f)] = [scans[e, p, 0, blk], scans[e, p, 1, blk]]
                aux.append(f)
    _run_skewed(list(projs.values()) + blocks, aux, deps)


def _gla(x2d, mods, g_pre, w_in_b, wg, bg, gla_norm, state_f, state_b, batch, seq_len):
    has_state = state_f is not None
    n_chunks = seq_len // GLA_CHUNK
    L = seq_len
    rows = MIX_TILE
    n_elems = rows // L
    n_ch = H_B
    wk, wv = H_B * DK_B, H_B * DV_B
    const = lambda i: (0, 0)
    in_specs = [
        pl.BlockSpec((rows, D_MODEL), lambda i: (i, 0)),
        pl.BlockSpec((None, N_MOD, D_MODEL), (lambda i: (i, 0, 0)) if has_state else (lambda i: (0, 0, 0))),
        pl.BlockSpec((1, D_MODEL), const),
        pl.BlockSpec((D_MODEL, IN_COLS), const, pipeline_mode=pl.Buffered(1)),
        pl.BlockSpec((2 * GATE_RANK, 2 * wk), const),
        pl.BlockSpec((1, 2 * wk), const),
        pl.BlockSpec((1, DV_B), const),
    ]
    args = [x2d, mods, g_pre, w_in_b, wg, bg, gla_norm]
    out_shape = [jax.ShapeDtypeStruct((batch * L, wv), BF16)]
    out_specs = [pl.BlockSpec((rows, wv), lambda i: (i, 0))]
    if has_state:
        assert n_elems == 1 and mods.shape[0] == batch
        st_spec = pl.BlockSpec((None, None, H_B, DK_B, DV_B), lambda i: (i, 0, 0, 0, 0))
        in_specs += [st_spec, st_spec]
        args += [state_f, state_b]
    else:
        assert mods.shape[0] == 1
        for _ in range(2):
            out_shape.append(jax.ShapeDtypeStruct((batch, 1, H_B, DK_B, DV_B), F32))
            out_specs.append(pl.BlockSpec((n_elems, None, H_B, DK_B, DV_B), lambda i: (i, 0, 0, 0, 0)))
    return pl.pallas_call(
        functools.partial(_gla_kernel, n_chunks=n_chunks, n_elems=n_elems, has_state=has_state),
        out_shape=out_shape,
        grid=(batch // n_elems,),
        in_specs=in_specs,
        out_specs=out_specs,
        scratch_shapes=[
            pltpu.VMEM((rows, wk), F32),
            pltpu.VMEM((rows, wk), F32),
            pltpu.VMEM((rows, wk), F32),
            pltpu.VMEM((rows, wk), F32),
            pltpu.VMEM((rows, wv), F32),
            pltpu.VMEM((rows, wv), BF16),
            pltpu.VMEM((n_elems, n_ch, L, 2 * DV_B), F32),
            pltpu.VMEM((n_elems, n_ch, L, 2 * DV_B), F32),
            pltpu.VMEM((n_elems, n_ch, 2 * DV_B, 2 * DK_B), F32),
            pltpu.VMEM((n_elems, n_ch, L, 2 * DK_B), BF16),
            pltpu.VMEM((n_elems, n_ch, n_chunks, 2 * DV_B, 2 * DK_B), F32),
            pltpu.VMEM((n_elems, n_ch, max(n_chunks, 8), 2 * DK_B), F32),
        ],
        compiler_params=pltpu.CompilerParams(
            dimension_semantics=("arbitrary",), vmem_limit_bytes=VMEM_LIMIT),
        name="gla_state" if has_state else "gla_ctx",
    )(*args)


def _weight_copy(src_hbm, r0, c0, stage, sem, slot):
    return pltpu.make_async_copy(src_hbm.at[pl.ds(r0, W_STAGE_ROWS), pl.ds(c0, D_MODEL)], stage.at[slot], sem.at[slot])


def _out_mlp_kernel(xc_ref, oac_ref, ogc_ref, xl_ref, oal_ref, ogl_ref, mod_ref, wo_hbm, gpost_ref, gmpre_ref,
                    gmpost_ref, w1_hbm, w2_hbm, yc_ref, yl_ref, wo_ref, w1_ref, w2_ref, stage, sem, *, n_ctx_tiles):
    rc = ROW_CHUNK
    tf = 1024

    @pl.when(pl.program_id(0) == 0)
    def _():
        windows = [(src, dst, r0, c0)
                   for src, dst in ((wo_hbm, wo_ref), (w1_hbm, w1_ref), (w2_hbm, w2_ref))
                   for r0 in range(0, src.shape[0], W_STAGE_ROWS) for c0 in range(0, src.shape[1], D_MODEL)]
        copies = [_weight_copy(src, r0, c0, stage, sem, n % 2) for n, (src, _, r0, c0) in enumerate(windows)]
        copies[0].start()
        for n, (_, dst, r0, c0) in enumerate(windows):
            if n + 1 < len(windows):
                copies[n + 1].start()
            copies[n].wait()
            dst[r0:r0 + W_STAGE_ROWS, c0:c0 + D_MODEL] = stage[n % 2].astype(BF16)

    def run(x_ref, oa_ref, og_ref, y_ref):
        def chunk(r0):
            rows = slice(r0, r0 + rc)
            mix = _dot(oa_ref[rows, :], wo_ref[0:512, :]) + _dot(og_ref[rows, :], wo_ref[512:1024, :])
            yield
            x1 = x_ref[rows, :] + mod_ref[2:3, :] * _rms(mix, gpost_ref[...])
            h2 = (_rms(x1, gmpre_ref[...]) * (1.0 + mod_ref[4:5, :]) + mod_ref[3:4, :]).astype(BF16)
            yield
            acc = None
            for j in range(D_FF // tf):
                u = jnp.maximum(_dot(h2, w1_ref[:, j * tf:(j + 1) * tf]), 0.0)
                part = _dot((u * u).astype(BF16), w2_ref[j * tf:(j + 1) * tf, :])
                acc = part if acc is None else acc + part
                yield
            y_ref[rows, :] = x1 + mod_ref[5:6, :] * _rms(acc, gmpost_ref[...])

        _run_skewed([chunk(r0) for r0 in range(0, x_ref.shape[0], rc)])

    is_ctx = pl.program_id(0) < n_ctx_tiles

    @pl.when(is_ctx)
    def _():
        run(xc_ref, oac_ref, ogc_ref, yc_ref)

    @pl.when(jnp.logical_not(is_ctx))
    def _():
        run(xl_ref, oal_ref, ogl_ref, yl_ref)


def _out_mlp(xc2d, oac, ogc, xl2d, oal, ogl, mods, lat_seq, w_out, g_post, g_mpre, g_mpost, w1, w2):
    tm = 512
    n_c, n_l = xc2d.shape[0] // tm, xl2d.shape[0] // tm
    per_b = lat_seq // tm
    ctx_row = lambda i: (jnp.minimum(i, n_c - 1), 0)
    lat_row = lambda i: (jnp.maximum(i - n_c, 0), 0)
    mod_idx = lambda i: (jnp.where(i < n_c, 0, 1 + jnp.maximum(i - n_c, 0) // per_b), 0, 0)
    const = lambda i: (0, 0)
    return pl.pallas_call(
        functools.partial(_out_mlp_kernel, n_ctx_tiles=n_c),
        out_shape=[jax.ShapeDtypeStruct(xc2d.shape, F32), jax.ShapeDtypeStruct(xl2d.shape, F32)],
        grid=(n_c + n_l,),
        in_specs=[
            pl.BlockSpec((tm, D_MODEL), ctx_row),
            pl.BlockSpec((tm, 512), ctx_row),
            pl.BlockSpec((tm, 512), ctx_row),
            pl.BlockSpec((tm, D_MODEL), lat_row),
            pl.BlockSpec((tm, 512), lat_row),
            pl.BlockSpec((tm, 512), lat_row),
            pl.BlockSpec((None, N_MOD, D_MODEL), mod_idx),
            pl.BlockSpec(memory_space=pl.ANY),
            pl.BlockSpec((1, D_MODEL), const),
            pl.BlockSpec((1, D_MODEL), const),
            pl.BlockSpec((1, D_MODEL), const),
            pl.BlockSpec(memory_space=pl.ANY),
            pl.BlockSpec(memory_space=pl.ANY),
        ],
        out_specs=[pl.BlockSpec((tm, D_MODEL), ctx_row), pl.BlockSpec((tm, D_MODEL), lat_row)],
        scratch_shapes=[
            pltpu.VMEM((D_MODEL, D_MODEL), BF16),
            pltpu.VMEM((D_MODEL, D_FF), BF16),
            pltpu.VMEM((D_FF, D_MODEL), BF16),
            pltpu.VMEM((2, W_STAGE_ROWS, D_MODEL), F32),
            pltpu.SemaphoreType.DMA((2,)),
        ],
        compiler_params=pltpu.CompilerParams(
            dimension_semantics=("arbitrary",), vmem_limit_bytes=VMEM_LIMIT),
        name="out_mlp",
    )(xc2d, oac, ogc, xl2d, oal, ogl, mods, w_out, g_post, g_mpre, g_mpost, w1, w2)


def _rope_tables(n_lat):
    pos = np.arange(n_lat)
    row_pos = (pos // GRID_W).astype(np.float64)
    col_pos = (pos % GRID_W).astype(np.float64)
    half = DK_A // 4
    inv = ROPE_BASE ** (-np.arange(half, dtype=np.float64) / half)
    lane = np.arange(128)
    in64 = lane % 64
    in32 = in64 % 32
    p = np.where((in64 < 32)[None, :], row_pos[:, None], col_pos[:, None])
    ang = p * inv[in32 % half][None, :]
    cos, sin = np.cos(ang), np.sin(ang)
    first = (in32 < half)[None, :]
    sa = np.where(first, -sin, 0.0)
    sb = np.where(first, 0.0, sin)
    return tuple(jnp.asarray(t, dtype=F32) for t in (cos, sa, sb))


def _mixers(x, mods, w, rope_tabs, cache_k, cache_v, state_f, state_b):
    batch, seq_len, _ = x.shape
    x2d = x.reshape(batch * seq_len, D_MODEL)
    aouts = _attention(x2d, mods, w["g_pre"], w["w_in"], rope_tabs, cache_k, cache_v, w["lam"], w["diff_norm"],
                       batch, seq_len)
    gouts = _gla(x2d, mods, w["g_pre"], w["w_in"], w["wg"], w["bg"], w["gla_norm"], state_f, state_b,
                 batch, seq_len)
    return x2d, aouts[0], gouts[0], tuple(aouts[1:]) + tuple(gouts[1:])


def kernel(x_prompt, x_sample, c, cache_k, cache_v, state_fwd, state_bwd, c_ctx, w_ada, b_ada,
           norm_attn_pre, norm_attn_post, norm_mlp_pre, norm_mlp_post, w_in, w_gate_fwd, b_gate_fwd,
           w_gate_bwd, b_gate_bwd, lam_q1, lam_k1, lam_q2, lam_k2, diff_norm, gla_norm, w_out,
           w_mlp1, w_mlp2):
    dec_batch = c.shape[0]
    rows = 16
    cvec = jnp.zeros((rows, D_MODEL), F32).at[0].set(c_ctx).at[1:1 + dec_batch].set(c)
    m = _adaln(cvec, w_ada[0], b_ada[0][None, :])
    mods_ctx = m[0:1].reshape(1, N_MOD, D_MODEL)
    mods_lat = m[1:1 + dec_batch].reshape(dec_batch, N_MOD, D_MODEL)

    wg = jnp.zeros((2 * GATE_RANK, 512), F32)
    wg = wg.at[0:GATE_RANK, 0:256].set(w_gate_fwd[0]).at[GATE_RANK:2 * GATE_RANK, 256:512].set(w_gate_bwd[0])
    w = {
        "g_pre": norm_attn_pre[0][None, :],
        "g_post": norm_attn_post[0][None, :],
        "g_mpre": norm_mlp_pre[0][None, :],
        "g_mpost": norm_mlp_post[0][None, :],
        "w_in": w_in[0].astype(BF16),
        "wg": wg.astype(BF16),
        "bg": jnp.concatenate([b_gate_fwd[0], b_gate_bwd[0]])[None, :],
        "lam": jnp.stack([lam_q1[0], lam_k1[0], lam_q2[0], lam_k2[0]]),
        "diff_norm": diff_norm[0][None, :],
        "gla_norm": gla_norm[0][None, :],
        "w_out": w_out[0],
        "w1": w_mlp1[0],
        "w2": w_mlp2[0],
    }
    xc2d, oac, ogc, (new_k, new_v, new_sf, new_sb) = _mixers(x_prompt, mods_ctx, w, None, None, None, None, None)
    xl2d, oal, ogl, _ = _mixers(x_sample, mods_lat, w, _rope_tables(x_sample.shape[1]),
                                cache_k, cache_v, state_fwd, state_bwd)
    mods_all = m[0:1 + dec_batch].reshape(1 + dec_batch, N_MOD, D_MODEL)
    y_prompt, y_sample = _out_mlp(xc2d, oac, ogc, xl2d, oal, ogl, mods_all, x_sample.shape[1],
                                  w["w_out"], w["g_post"], w["g_mpre"], w["g_mpost"], w["w1"], w["w2"])
    return (y_prompt.reshape(x_prompt.shape), y_sample.reshape(x_sample.shape), new_k, new_v, new_sf, new_sb)
```

```python
import functools
import math

import numpy as np
import jax
import jax.numpy as jnp
from jax import lax
from jax.experimental import pallas as pl
from jax.experimental.pallas import tpu as pltpu

F32 = jnp.float32
BF16 = jnp.bfloat16

D_MODEL = 1024
GRID_W = 64
H_A = 4
DV_A = 128
DK_A = 64
H_B = 4
DV_B = 128
DK_B = 64
GATE_RANK = 16
GATE_NORM = 16.0
GLA_CHUNK = 64
D_FF = 4 * D_MODEL
N_MOD = 6
ROPE_BASE = 10000.0
EPS = 1e-6
LAM_INIT = 0.8 - 0.6 * math.exp(-0.3 * 0)

C_QA, C_KA, C_VA, C_QB, C_KB, C_VB, C_RB, C_GL = 0, 512, 1024, 1536, 1792, 2048, 2560, 3072
IN_COLS = 3104
B_QB, B_KB, B_VB, B_RB, B_GL, B_COLS = (c - C_QB for c in (C_QB, C_KB, C_VB, C_RB, C_GL, IN_COLS))

V7X_VMEM_BYTES = 64 * 1024 * 1024
VMEM_LIMIT = V7X_VMEM_BYTES - 4 * 1024 * 1024
ROW_CHUNK = 256
MIX_TILE = 1024
W_STAGE_ROWS = 512
W_STAGE_SLOTS = 4


def _rms(x, g):
    return x * lax.rsqrt(jnp.mean(x * x, axis=-1, keepdims=True) + EPS) * g


def _dot(a, b):
    return jnp.dot(a, b, preferred_element_type=F32)


def _dot_nt(a, b):
    return lax.dot_general(a, b, (((1,), (1,)), ((), ())), preferred_element_type=F32)


def _run_skewed(gens, secondary=(), deps=None):
    deps = deps or {}
    queues, running, done = [list(gens), list(secondary)], [], set()
    while any(queues) or running:
        for queue in queues:
            for g in queue:
                if all(id(d) in done for d in deps.get(id(g), ())):
                    queue.remove(g)
                    running.append(g)
                    break
        assert running, "dependency cycle"
        for g in list(running):
            try:
                next(g)
            except StopIteration:
                running.remove(g)
                done.add(id(g))


def _prenorm(x_ref, rows, mod_ref, g_ref):
    x = x_ref[rows, :]
    return (_rms(x, g_ref[...]) * (1.0 + mod_ref[1:2, :]) + mod_ref[0:1, :]).astype(BF16)


def _adaln_kernel(c_ref, w_ref, b_ref, o_ref):
    c = c_ref[...]
    s = c * jax.nn.sigmoid(c)
    o_ref[...] = _dot(s.astype(BF16), w_ref[...].astype(BF16)) + b_ref[...]


def _adaln(cvec, w_ada, b_ada):
    rows = cvec.shape[0]
    n = w_ada.shape[1]
    tn = 1536
    return pl.pallas_call(
        _adaln_kernel,
        out_shape=jax.ShapeDtypeStruct((rows, n), F32),
        grid=(n // tn,),
        in_specs=[
            pl.BlockSpec((rows, D_MODEL), lambda j: (0, 0)),
            pl.BlockSpec((D_MODEL, tn), lambda j: (0, j)),
            pl.BlockSpec((1, tn), lambda j: (0, j)),
        ],
        out_specs=pl.BlockSpec((rows, tn), lambda j: (0, j)),
        compiler_params=pltpu.CompilerParams(
            dimension_semantics=("arbitrary",), vmem_limit_bytes=VMEM_LIMIT),
        name="adaln",
    )(cvec, w_ada, b_ada)


def _attn_kernel(*refs, cached, seq):
    if cached:
        (x_ref, mod_ref, g_ref, w_ref, cos_ref, sa_ref, sb_ref, ck_ref, cv_ref, lam_ref, dn_ref,
         o_ref, qkv_s) = refs
    else:
        x_ref, mod_ref, g_ref, w_ref, lam_ref, dn_ref, o_ref, nk_ref, nv_ref, qkv_s = refs
        assert seq == ROW_CHUNK
    rc = ROW_CHUNK
    n_rows = x_ref.shape[0]
    lp = lam_ref[...]
    lam = (jnp.exp(jnp.sum(lp[0:1] * lp[1:2], axis=-1, keepdims=True))
           - jnp.exp(jnp.sum(lp[2:3] * lp[3:4], axis=-1, keepdims=True)) + LAM_INIT)
    lane = lax.broadcasted_iota(jnp.int32, (1, 128), 1)
    scale = DK_A ** -0.5
    m1 = jnp.where(lane < DK_A, scale, 0.0).astype(BF16)
    m2 = jnp.where(lane >= DK_A, scale, 0.0).astype(BF16)
    n_keys = (n_rows + ck_ref.shape[1]) if cached else rc
    ones = jnp.ones((n_keys, 128), BF16)

    def proj(r0):
        rows = slice(r0, r0 + rc)
        h = _prenorm(x_ref, rows, mod_ref, g_ref)
        yield
        pa = _dot(h, w_ref[:, C_QA:C_VA])
        pv = _dot(h, w_ref[:, C_VA:C_QB])
        for blk in range(8):
            t = pa[:, blk * 128:(blk + 1) * 128]
            if cached:
                t = (t * cos_ref[rows, :] + pltpu.roll(t, 112, 1) * sa_ref[rows, :]
                     + pltpu.roll(t, 16, 1) * sb_ref[rows, :])
            elif blk >= 4:
                nk_ref[r0 // seq, blk - 4] = t
            qkv_s[rows, blk * 128:(blk + 1) * 128] = t.astype(BF16)
        qkv_s[rows, C_VA:C_QB] = pv.astype(BF16)
        if not cached:
            for hh in range(H_A):
                nv_ref[r0 // seq, hh] = pv[:, hh * 128:(hh + 1) * 128]

    kv = {}
    res = {}

    def chain(r0, h, branch):
        rows = slice(r0, r0 + rc)
        cols = slice(h * 128, (h + 1) * 128)
        kcols = slice(C_KA + h * 128, C_KA + (h + 1) * 128)
        vcols = slice(C_VA + h * 128, C_VA + (h + 1) * 128)
        key = h if cached else (h, r0)
        if key not in kv:
            if cached:
                k = jnp.concatenate([ck_ref[h].astype(BF16), qkv_s[:, kcols]], axis=0)
                v = jnp.concatenate([cv_ref[h].astype(BF16), qkv_s[:, vcols]], axis=0)
            else:
                k, v = qkv_s[rows, kcols], qkv_s[rows, vcols]
            kv[key] = (k, jnp.concatenate([v, ones], axis=1))
        k, v1 = kv[key]
        s = _dot_nt(qkv_s[rows, cols] * (m2 if branch else m1), k)
        yield
        e = jnp.exp(s - jnp.max(s, axis=-1, keepdims=True)).astype(BF16)
        yield
        res[r0, h, branch] = _dot(e, v1)
        yield
        if branch:
            r1, r2 = res[r0, h, 0], res[r0, h, 1]
            o = r1[:, 0:128] / r1[:, 128:256] - lam * (r2[:, 0:128] / r2[:, 128:256])
            o_ref[rows, cols] = (_rms(o, dn_ref[...]) * (1.0 - LAM_INIT)).astype(BF16)

    projs = {r0: proj(r0) for r0 in range(0, n_rows, rc)}
    gens, deps = list(projs.values()), {}
    for r0 in projs:
        for h in range(H_A):
            for br in range(2):
                c = chain(r0, h, br)
                deps[id(c)] = list(projs.values()) if cached else [projs[r0]]
                gens.append(c)
    _run_skewed(gens, deps=deps)


def _attention(x2d, mods, g_pre, w_in_b, rope_tabs, cache_k, cache_v, lam_p, diff_norm, batch, seq_len):
    cached = cache_k is not None
    tq = MIX_TILE
    n_tiles = batch * seq_len // tq
    const = lambda i: (0, 0)
    in_specs = [
        pl.BlockSpec((tq, D_MODEL), lambda i: (i, 0)),
        pl.BlockSpec((None, N_MOD, D_MODEL), (lambda i: (i, 0, 0)) if cached else (lambda i: (0, 0, 0))),
        pl.BlockSpec((1, D_MODEL), const),
        pl.BlockSpec((D_MODEL, C_QB), const, pipeline_mode=pl.Buffered(1)),
    ]
    args = [x2d, mods, g_pre, w_in_b]
    out_shape = [jax.ShapeDtypeStruct((batch * seq_len, H_A * DV_A), BF16)]
    out_specs = [pl.BlockSpec((tq, H_A * DV_A), lambda i: (i, 0))]
    if cached:
        assert seq_len == tq and mods.shape[0] == batch
        past = cache_k.shape[3]
        cspec = pl.BlockSpec((None, None, H_A, past, 128), lambda i: (i, 0, 0, 0, 0))
        in_specs += [pl.BlockSpec((tq, 128), const)] * 3 + [cspec, cspec]
        args += list(rope_tabs) + [cache_k, cache_v]
    else:
        assert mods.shape[0] == 1
        nb = tq // seq_len
        for _ in range(2):
            out_shape.append(jax.ShapeDtypeStruct((batch, 1, H_A, seq_len, 128), F32))
            out_specs.append(pl.BlockSpec((nb, None, H_A, seq_len, 128), lambda i: (i, 0, 0, 0, 0)))
    in_specs += [pl.BlockSpec((4, DK_A), const), pl.BlockSpec((1, DV_A), const)]
    args += [lam_p, diff_norm]
    return pl.pallas_call(
        functools.partial(_attn_kernel, cached=cached, seq=seq_len),
        out_shape=out_shape,
        grid=(n_tiles,),
        in_specs=in_specs,
        out_specs=out_specs,
        scratch_shapes=[pltpu.VMEM((tq, C_QB), BF16)],
        compiler_params=pltpu.CompilerParams(
            dimension_semantics=("arbitrary",), vmem_limit_bytes=VMEM_LIMIT),
        name="attn_cached" if cached else "attn_ctx",
    )(*args)


def _split3(g):
    hi = g.astype(BF16)
    r1 = g - hi.astype(F32)
    mid = r1.astype(BF16)
    lo = (r1 - mid.astype(F32)).astype(BF16)
    return hi, mid, lo


def _gla_block_stages(q_ref, k_ref, g_ref, v_ref, in_rows, qc, vc, vt_cache, blk, reverse, oa_s, qe_s, ut_s, dec_s):
    C = GLA_CHUNK
    R = in_rows.stop - in_rows.start
    cpb = R // C
    rows = slice(blk * R, (blk + 1) * R)
    ri = lax.broadcasted_iota(jnp.int32, (R, R), 0)
    ci = lax.broadcasted_iota(jnp.int32, (R, R), 1)
    same_chunk = jnp.right_shift(ri, 6) == jnp.right_shift(ci, 6)
    if reverse:
        keep = same_chunk & (ci >= ri)
        last, ref = 0, C // 2
    else:
        keep = same_chunk & (ci <= ri)
        last, ref = C - 1, C // 2 - 1
    tri = jnp.where(keep, 1.0, 0.0).astype(BF16)
    lane = lax.broadcasted_iota(jnp.int32, (1, 2 * DK_B), 1)
    h0 = jnp.where(lane < DK_B, 1.0, 0.0).astype(BF16)
    h1 = jnp.where(lane >= DK_B, 1.0, 0.0).astype(BF16)

    hi, mid, lo = _split3(g_ref[in_rows, qc])
    b3 = _dot(tri, jnp.concatenate([hi, mid, lo], axis=1))
    yield
    b = b3[:, 0:128] + b3[:, 128:256] + b3[:, 256:384]
    q = q_ref[in_rows, qc]
    k = k_ref[in_rows, qc]
    qt, kt, qe, kd = [], [], [], []
    zeros = jnp.zeros((C, 2 * DK_B), BF16)
    for c in range(cpb):
        s = slice(c * C, (c + 1) * C)
        bc = b[s]
        bm = bc[ref:ref + 1]
        bl = bc[last:last + 1]
        qt.append((q[s] * jnp.exp(bc - bm)).astype(BF16))
        kt.append((k[s] * jnp.exp(bm - bc)).astype(BF16))
        qe.append((q[s] * jnp.exp(bc)).astype(BF16))
        kdc = (k[s] * jnp.exp(bl - bc)).astype(BF16)
        kd.append(jnp.concatenate([kdc if j == c else zeros for j in range(cpb)], axis=1))
        dec_s[blk * cpb + c:blk * cpb + c + 1, :] = jnp.exp(bl)
    qe_s[rows, :] = jnp.concatenate(qe, axis=0)
    qt = jnp.concatenate(qt, axis=0)
    kt = jnp.concatenate(kt, axis=0)
    yield
    s0 = _dot_nt(qt * h0, kt)
    s1 = _dot_nt(qt * h1, kt)
    v = v_ref[in_rows, vc]
    if (in_rows.start, vc.start) not in vt_cache:
        vt_cache[in_rows.start, vc.start] = v.T
    ut = _dot(vt_cache[in_rows.start, vc.start], jnp.concatenate(kd, axis=0))
    yield
    a0 = jnp.where(keep, s0, 0.0).astype(BF16)
    a1 = jnp.where(keep, s1, 0.0).astype(BF16)
    br = lax.broadcasted_iota(jnp.int32, ut.shape, 0)
    bcol = lax.broadcasted_iota(jnp.int32, ut.shape, 1)
    ut = jnp.where((br < DV_B) == ((bcol & (2 * DK_B - 1)) < DK_B), ut, 0.0)
    for c in range(cpb):
        ut_s[blk * cpb + c] = ut[:, c * 2 * DK_B:(c + 1) * 2 * DK_B]
    yield
    oa_s[rows, :] = jnp.concatenate([_dot(a0, v[:, 0:DV_B]), _dot(a1, v[:, DV_B:2 * DV_B])], axis=1)


def _state_to_t(s_pair):
    z = jnp.zeros((DK_B, DV_B), F32)
    a = jnp.concatenate([s_pair[0], z], axis=0).T
    b = jnp.concatenate([z, s_pair[1]], axis=0).T
    return jnp.concatenate([a, b], axis=0)


def _gla_kernel(*refs, n_chunks, n_elems, has_state):
    if has_state:
        (x_ref, mod_ref, g_ref, w_ref, wg_ref, bg_ref, gn_ref, sf_ref, sb_ref, og_ref, *scratch) = refs
    else:
        (x_ref, mod_ref, g_ref, w_ref, wg_ref, bg_ref, gn_ref, og_ref, nsf_ref, nsb_ref, *scratch) = refs
    q_s, k_s, gf_s, gb_s, rb_s, v_s, oa, oi, st, qe, ut, dec = scratch
    C = GLA_CHUNK
    R = ROW_CHUNK
    cpb = R // C
    L = n_chunks * C
    n_blk = L // R
    n_pairs = H_B // 2

    def proj(r0):
        rows = slice(r0, r0 + R)
        h = _prenorm(x_ref, rows, mod_ref, g_ref)
        yield
        pg = _dot(h, w_ref[:, C_RB:IN_COLS])
        pb = _dot(h, w_ref[:, C_QB:C_RB])
        z = _dot(pg[:, B_GL - B_RB:B_COLS - B_RB].astype(BF16), wg_ref[...]) + bg_ref[...]
        ls = (jnp.minimum(z, 0.0) - jnp.log(1.0 + jnp.exp(-jnp.abs(z)))) * (1.0 / GATE_NORM)
        gf_s[rows, :] = ls[:, 0:256]
        gb_s[rows, :] = ls[:, 256:512]
        rb_s[rows, :] = pg[:, 0:512]
        q_s[rows, :] = pb[:, B_QB:B_KB] * (DK_B ** -0.5)
        k_s[rows, :] = pb[:, B_KB:B_VB]
        v_s[rows, :] = pb[:, B_VB:B_RB].astype(BF16)

    def scan_stage(e, p, d, blk):
        ch = 2 * p + d
        if blk != (n_blk - 1 if d else 0):
            s = st[e, ch]
        elif has_state:
            assert n_elems == 1
            s = _state_to_t((sb_ref if d else sf_ref)[2 * p:2 * p + 2])
        else:
            s = jnp.zeros(st.shape[2:], F32)
        for i in range(cpb):
            c = blk * cpb + (cpb - 1 - i if d else i)
            rows = slice(c * C, (c + 1) * C)
            oi[e, ch, rows, :] = _dot_nt(qe[e, ch, rows, :], s.astype(BF16))
            s = s * dec[e, ch, c:c + 1, :] + ut[e, ch, c]
            yield
        if blk != (0 if d else n_blk - 1):
            st[e, ch] = s
        elif not has_state:
            dst_ref = nsb_ref if d else nsf_ref
            dst_ref[e, 2 * p] = s[0:DV_B, :].T[0:DK_B, :]
            dst_ref[e, 2 * p + 1] = s[DV_B:2 * DV_B, :].T[DK_B:2 * DK_B, :]

    def fin_stage(e, p, blk):
        rows = slice(blk * R, (blk + 1) * R)
        out_rows = slice(e * L + blk * R, e * L + (blk + 1) * R)
        o = ((oa[e, 2 * p, rows, :] + oi[e, 2 * p, rows, :])
             + (oa[e, 2 * p + 1, rows, :] + oi[e, 2 * p + 1, rows, :]))
        for j in range(2):
            cols = slice((2 * p + j) * DV_B, (2 * p + j + 1) * DV_B)
            oj = _rms(o[:, j * DV_B:(j + 1) * DV_B], gn_ref[...])
            rj = rb_s[out_rows, cols]
            og_ref[out_rows, cols] = (oj * (rj * jax.nn.sigmoid(rj))).astype(BF16)
        yield

    vt_cache = {}
    projs, blocks, scans, deps = {}, [], {}, {}
    for e in range(n_elems):
        for t in range(n_blk):
            blk = t // 2 if t % 2 == 0 else n_blk - 1 - t // 2
            projs[e, blk] = proj(e * L + blk * R)
    for e in range(n_elems):
        for t in range(n_blk):
            for p in range(n_pairs):
                for d in range(2):
                    blk = n_blk - 1 - t if d else t
                    ch = 2 * p + d
                    in_rows = slice(e * L + blk * R, e * L + (blk + 1) * R)
                    g = _gla_block_stages(
                        q_s, k_s, gb_s if d else gf_s, v_s, in_rows,
                        slice(p * 2 * DK_B, (p + 1) * 2 * DK_B), slice(p * 2 * DV_B, (p + 1) * 2 * DV_B),
                        vt_cache, blk, bool(d), oa.at[e, ch], qe.at[e, ch], ut.at[e, ch], dec.at[e, ch])
                    deps[id(g)] = [projs[e, blk]]
                    blocks.append(g)
                    s = scan_stage(e, p, d, blk)
                    prev = scans.get((e, p, d, blk + 1 if d else blk - 1))
                    deps[id(s)] = [g] + ([prev] if prev is not None else [])
                    scans[e, p, d, blk] = s
    aux = []
    for e in range(n_elems):
        for t in range(n_blk):
            for p in range(n_pairs):
                aux.append(scans[e, p, 0, t])
                aux.append(scans[e, p, 1, n_blk - 1 - t])
        for blk in range(n_blk):
            for p in range(n_pairs):
                f = fin_stage(e, p, blk)
                deps[id(f)] = [scans[e, p, 0, blk], scans[e, p, 1, blk]]
                aux.append(f)
    _run_skewed(list(projs.values()) + blocks, aux, deps)


def _gla(x2d, mods, g_pre, w_in_b, wg, bg, gla_norm, state_f, state_b, batch, seq_len):
    has_state = state_f is not None
    n_chunks = seq_len // GLA_CHUNK
    L = seq_len
    rows = MIX_TILE
    n_elems = rows // L
    n_ch = H_B
    wk, wv = H_B * DK_B, H_B * DV_B
    const = lambda i: (0, 0)
    in_specs = [
        pl.BlockSpec((rows, D_MODEL), lambda i: (i, 0)),
        pl.BlockSpec((None, N_MOD, D_MODEL), (lambda i: (i, 0, 0)) if has_state else (lambda i: (0, 0, 0))),
        pl.BlockSpec((1, D_MODEL), const),
        pl.BlockSpec((D_MODEL, IN_COLS), const, pipeline_mode=pl.Buffered(1)),
        pl.BlockSpec((2 * GATE_RANK, 2 * wk), const),
        pl.BlockSpec((1, 2 * wk), const),
        pl.BlockSpec((1, DV_B), const),
    ]
    args = [x2d, mods, g_pre, w_in_b, wg, bg, gla_norm]
    out_shape = [jax.ShapeDtypeStruct((batch * L, wv), BF16)]
    out_specs = [pl.BlockSpec((rows, wv), lambda i: (i, 0))]
    if has_state:
        assert n_elems == 1 and mods.shape[0] == batch
        st_spec = pl.BlockSpec((None, None, H_B, DK_B, DV_B), lambda i: (i, 0, 0, 0, 0))
        in_specs += [st_spec, st_spec]
        args += [state_f, state_b]
    else:
        assert mods.shape[0] == 1
        for _ in range(2):
            out_shape.append(jax.ShapeDtypeStruct((batch, 1, H_B, DK_B, DV_B), F32))
            out_specs.append(pl.BlockSpec((n_elems, None, H_B, DK_B, DV_B), lambda i: (i, 0, 0, 0, 0)))
    return pl.pallas_call(
        functools.partial(_gla_kernel, n_chunks=n_chunks, n_elems=n_elems, has_state=has_state),
        out_shape=out_shape,
        grid=(batch // n_elems,),
        in_specs=in_specs,
        out_specs=out_specs,
        scratch_shapes=[
            pltpu.VMEM((rows, wk), F32),
            pltpu.VMEM((rows, wk), F32),
            pltpu.VMEM((rows, wk), F32),
            pltpu.VMEM((rows, wk), F32),
            pltpu.VMEM((rows, wv), F32),
            pltpu.VMEM((rows, wv), BF16),
            pltpu.VMEM((n_elems, n_ch, L, 2 * DV_B), F32),
            pltpu.VMEM((n_elems, n_ch, L, 2 * DV_B), F32),
            pltpu.VMEM((n_elems, n_ch, 2 * DV_B, 2 * DK_B), F32),
            pltpu.VMEM((n_elems, n_ch, L, 2 * DK_B), BF16),
            pltpu.VMEM((n_elems, n_ch, n_chunks, 2 * DV_B, 2 * DK_B), F32),
            pltpu.VMEM((n_elems, n_ch, max(n_chunks, 8), 2 * DK_B), F32),
        ],
        compiler_params=pltpu.CompilerParams(
            dimension_semantics=("arbitrary",), vmem_limit_bytes=VMEM_LIMIT),
        name="gla_state" if has_state else "gla_ctx",
    )(*args)


def _weight_copy(src_hbm, r0, c0, stage, sem, slot):
    return pltpu.make_async_copy(src_hbm.at[pl.ds(r0, W_STAGE_ROWS), pl.ds(c0, D_MODEL)], stage.at[slot], sem.at[slot])


def _out_mlp_kernel(xc_ref, oac_ref, ogc_ref, xl_ref, oal_ref, ogl_ref, mod_ref, wo_hbm, gpost_ref, gmpre_ref,
                    gmpost_ref, w1_hbm, w2_hbm, yc_ref, yl_ref, wo_ref, w1_ref, w2_ref, stage, sem, *, n_ctx_tiles):
    rc = ROW_CHUNK
    tf = 1024
    n_slots = stage.shape[0]
    per_win = tf // W_STAGE_ROWS

    def weight_stream():
        windows = [(wo_hbm, wo_ref, r0, 0) for r0 in range(0, D_MODEL, W_STAGE_ROWS)]
        for j in range(D_FF // tf):
            windows += [(w1_hbm, w1_ref, r0, j * tf) for r0 in range(0, D_MODEL, W_STAGE_ROWS)]
            windows += [(w2_hbm, w2_ref, j * tf + r0, 0) for r0 in range(0, tf, W_STAGE_ROWS)]
        copies = [_weight_copy(src, r0, c0, stage, sem, n % n_slots) for n, (src, _, r0, c0) in enumerate(windows)]
        pos = {"started": 0, "taken": 0}

        def start_below(limit):
            while pos["started"] < min(limit, len(copies)):
                copies[pos["started"]].start()
                pos["started"] += 1

        def take(k):
            for _ in range(k):
                n = pos["taken"]
                _, dst, r0, c0 = windows[n]
                copies[n].wait()
                dst[r0:r0 + W_STAGE_ROWS, c0:c0 + D_MODEL] = stage[n % n_slots].astype(BF16)
                pos["taken"] += 1
                start_below(n + n_slots + 1)

        start_below(n_slots)
        return take, lambda: pos["taken"] == len(copies)

    def run(x_ref, oa_ref, og_ref, y_ref, take=None):
        def chunk(r0):
            need = take if (take is not None and r0 == 0) else (lambda k: None)
            rows = slice(r0, r0 + rc)
            need(D_MODEL // W_STAGE_ROWS)
            mix = _dot(oa_ref[rows, :], wo_ref[0:512, :]) + _dot(og_ref[rows, :], wo_ref[512:1024, :])
            yield
            x1 = x_ref[rows, :] + mod_ref[2:3, :] * _rms(mix, gpost_ref[...])
            h2 = (_rms(x1, gmpre_ref[...]) * (1.0 + mod_ref[4:5, :]) + mod_ref[3:4, :]).astype(BF16)
            yield
            acc = None
            for j in range(D_FF // tf):
                need(D_MODEL // W_STAGE_ROWS + per_win)
                u = jnp.maximum(_dot(h2, w1_ref[:, j * tf:(j + 1) * tf]), 0.0)
                part = _dot((u * u).astype(BF16), w2_ref[j * tf:(j + 1) * tf, :])
                acc = part if acc is None else acc + part
                yield
            y_ref[rows, :] = x1 + mod_ref[5:6, :] * _rms(acc, gmpost_ref[...])

        _run_skewed([chunk(r0) for r0 in range(0, x_ref.shape[0], rc)])

    step = pl.program_id(0)
    is_ctx = step < n_ctx_tiles

    @pl.when(step == 0)
    def _():
        take, all_taken = weight_stream()
        run(xc_ref, oac_ref, ogc_ref, yc_ref, take)
        assert all_taken()

    @pl.when(jnp.logical_and(step > 0, is_ctx))
    def _():
        run(xc_ref, oac_ref, ogc_ref, yc_ref)

    @pl.when(jnp.logical_not(is_ctx))
    def _():
        run(xl_ref, oal_ref, ogl_ref, yl_ref)


def _out_mlp(xc2d, oac, ogc, xl2d, oal, ogl, mods, lat_seq, w_out, g_post, g_mpre, g_mpost, w1, w2):
    tm = 512
    n_c, n_l = xc2d.shape[0] // tm, xl2d.shape[0] // tm
    per_b = lat_seq // tm
    ctx_row = lambda i: (jnp.minimum(i, n_c - 1), 0)
    lat_row = lambda i: (jnp.maximum(i - n_c, 0), 0)
    mod_idx = lambda i: (jnp.where(i < n_c, 0, 1 + jnp.maximum(i - n_c, 0) // per_b), 0, 0)
    const = lambda i: (0, 0)
    return pl.pallas_call(
        functools.partial(_out_mlp_kernel, n_ctx_tiles=n_c),
        out_shape=[jax.ShapeDtypeStruct(xc2d.shape, F32), jax.ShapeDtypeStruct(xl2d.shape, F32)],
        grid=(n_c + n_l,),
        in_specs=[
            pl.BlockSpec((tm, D_MODEL), ctx_row),
            pl.BlockSpec((tm, 512), ctx_row),
            pl.BlockSpec((tm, 512), ctx_row),
            pl.BlockSpec((tm, D_MODEL), lat_row),
            pl.BlockSpec((tm, 512), lat_row),
            pl.BlockSpec((tm, 512), lat_row),
            pl.BlockSpec((None, N_MOD, D_MODEL), mod_idx),
            pl.BlockSpec(memory_space=pl.ANY),
            pl.BlockSpec((1, D_MODEL), const),
            pl.BlockSpec((1, D_MODEL), const),
            pl.BlockSpec((1, D_MODEL), const),
            pl.BlockSpec(memory_space=pl.ANY),
            pl.BlockSpec(memory_space=pl.ANY),
        ],
        out_specs=[pl.BlockSpec((tm, D_MODEL), ctx_row), pl.BlockSpec((tm, D_MODEL), lat_row)],
        scratch_shapes=[
            pltpu.VMEM((D_MODEL, D_MODEL), BF16),
            pltpu.VMEM((D_MODEL, D_FF), BF16),
            pltpu.VMEM((D_FF, D_MODEL), BF16),
            pltpu.VMEM((W_STAGE_SLOTS, W_STAGE_ROWS, D_MODEL), F32),
            pltpu.SemaphoreType.DMA((W_STAGE_SLOTS,)),
        ],
        compiler_params=pltpu.CompilerParams(
            dimension_semantics=("arbitrary",), vmem_limit_bytes=VMEM_LIMIT),
        name="out_mlp",
    )(xc2d, oac, ogc, xl2d, oal, ogl, mods, w_out, g_post, g_mpre, g_mpost, w1, w2)


def _rope_tables(n_lat):
    pos = np.arange(n_lat)
    row_pos = (pos // GRID_W).astype(np.float64)
    col_pos = (pos % GRID_W).astype(np.float64)
    half = DK_A // 4
    inv = ROPE_BASE ** (-np.arange(half, dtype=np.float64) / half)
    lane = np.arange(128)
    in64 = lane % 64
    in32 = in64 % 32
    p = np.where((in64 < 32)[None, :], row_pos[:, None], col_pos[:, None])
    ang = p * inv[in32 % half][None, :]
    cos, sin = np.cos(ang), np.sin(ang)
    first = (in32 < half)[None, :]
    sa = np.where(first, -sin, 0.0)
    sb = np.where(first, 0.0, sin)
    return tuple(jnp.asarray(t, dtype=F32) for t in (cos, sa, sb))


def _mixers(x, mods, w, rope_tabs, cache_k, cache_v, state_f, state_b):
    batch, seq_len, _ = x.shape
    x2d = x.reshape(batch * seq_len, D_MODEL)
    aouts = _attention(x2d, mods, w["g_pre"], w["w_in"], rope_tabs, cache_k, cache_v, w["lam"], w["diff_norm"],
                       batch, seq_len)
    gouts = _gla(x2d, mods, w["g_pre"], w["w_in"], w["wg"], w["bg"], w["gla_norm"], state_f, state_b,
                 batch, seq_len)
    return x2d, aouts[0], gouts[0], tuple(aouts[1:]) + tuple(gouts[1:])


def kernel(x_prompt, x_sample, c, cache_k, cache_v, state_fwd, state_bwd, c_ctx, w_ada, b_ada,
           norm_attn_pre, norm_attn_post, norm_mlp_pre, norm_mlp_post, w_in, w_gate_fwd, b_gate_fwd,
           w_gate_bwd, b_gate_bwd, lam_q1, lam_k1, lam_q2, lam_k2, diff_norm, gla_norm, w_out,
           w_mlp1, w_mlp2):
    dec_batch = c.shape[0]
    rows = 16
    cvec = jnp.zeros((rows, D_MODEL), F32).at[0].set(c_ctx).at[1:1 + dec_batch].set(c)
    m = _adaln(cvec, w_ada[0], b_ada[0][None, :])
    mods_ctx = m[0:1].reshape(1, N_MOD, D_MODEL)
    mods_lat = m[1:1 + dec_batch].reshape(dec_batch, N_MOD, D_MODEL)

    wg = jnp.zeros((2 * GATE_RANK, 512), F32)
    wg = wg.at[0:GATE_RANK, 0:256].set(w_gate_fwd[0]).at[GATE_RANK:2 * GATE_RANK, 256:512].set(w_gate_bwd[0])
    w = {
        "g_pre": norm_attn_pre[0][None, :],
        "g_post": norm_attn_post[0][None, :],
        "g_mpre": norm_mlp_pre[0][None, :],
        "g_mpost": norm_mlp_post[0][None, :],
        "w_in": w_in[0].astype(BF16),
        "wg": wg.astype(BF16),
        "bg": jnp.concatenate([b_gate_fwd[0], b_gate_bwd[0]])[None, :],
        "lam": jnp.stack([lam_q1[0], lam_k1[0], lam_q2[0], lam_k2[0]]),
        "diff_norm": diff_norm[0][None, :],
        "gla_norm": gla_norm[0][None, :],
        "w_out": w_out[0],
        "w1": w_mlp1[0],
        "w2": w_mlp2[0],
    }
    xc2d, oac, ogc, (new_k, new_v, new_sf, new_sb) = _mixers(x_prompt, mods_ctx, w, None, None, None, None, None)
    xl2d, oal, ogl, _ = _mixers(x_sample, mods_lat, w, _rope_tables(x_sample.shape[1]),
                                cache_k, cache_v, state_fwd, state_bwd)
    mods_all = m[0:1 + dec_batch].reshape(1 + dec_batch, N_MOD, D_MODEL)
    y_prompt, y_sample = _out_mlp(xc2d, oac, ogc, xl2d, oal, ogl, mods_all, x_sample.shape[1],
                                  w["w_out"], w["g_post"], w["g_mpre"], w["g_mpost"], w["w1"], w["w2"])
    return (y_prompt.reshape(x_prompt.shape), y_sample.reshape(x_sample.shape), new_k, new_v, new_sf, new_sb)
```

```python
import functools
import math

import numpy as np
import jax
import jax.numpy as jnp
from jax import lax
from jax.experimental import pallas as pl
from jax.experimental.pallas import tpu as pltpu

F32 = jnp.float32
BF16 = jnp.bfloat16

D_MODEL = 1024
GRID_W = 64
H_A = 4
DV_A = 128
DK_A = 64
H_B = 4
DV_B = 128
DK_B = 64
GATE_RANK = 16
GATE_NORM = 16.0
GLA_CHUNK = 64
D_FF = 4 * D_MODEL
N_MOD = 6
ROPE_BASE = 10000.0
EPS = 1e-6
LAM_INIT = 0.8 - 0.6 * math.exp(-0.3 * 0)

C_QA, C_KA, C_VA, C_QB, C_KB, C_VB, C_RB, C_GL = 0, 512, 1024, 1536, 1792, 2048, 2560, 3072
IN_COLS = 3104
B_QB, B_KB, B_VB, B_RB, B_GL, B_COLS = (c - C_QB for c in (C_QB, C_KB, C_VB, C_RB, C_GL, IN_COLS))

V7X_VMEM_BYTES = 64 * 1024 * 1024
VMEM_LIMIT = V7X_VMEM_BYTES - 4 * 1024 * 1024
ROW_CHUNK = 256
MIX_TILE = 1024
W_STAGE_ROWS = 512
W_STAGE_SLOTS = 4


def _rms(x, g):
    return x * lax.rsqrt(jnp.mean(x * x, axis=-1, keepdims=True) + EPS) * g


def _dot(a, b):
    return jnp.dot(a, b, preferred_element_type=F32)


def _dot_nt(a, b):
    return lax.dot_general(a, b, (((1,), (1,)), ((), ())), preferred_element_type=F32)


def _run_skewed(gens, secondary=(), deps=None):
    deps = deps or {}
    queues, running, done = [list(gens), list(secondary)], [], set()
    while any(queues) or running:
        for queue in queues:
            for g in queue:
                if all(id(d) in done for d in deps.get(id(g), ())):
                    queue.remove(g)
                    running.append(g)
                    break
        assert running, "dependency cycle"
        for g in list(running):
            try:
                next(g)
            except StopIteration:
                running.remove(g)
                done.add(id(g))


def _prenorm(x_ref, rows, mod_ref, g_ref):
    x = x_ref[rows, :]
    return (_rms(x, g_ref[...]) * (1.0 + mod_ref[1:2, :]) + mod_ref[0:1, :]).astype(BF16)


def _weight_stream(windows, stage, sem):
    n_slots = stage.shape[0]

    def copy(n):
        src, _, r0, c0, _, width = windows[n]
        slot = n % n_slots
        return pltpu.make_async_copy(src.at[pl.ds(r0, W_STAGE_ROWS), pl.ds(c0, width)],
                                     stage.at[slot, pl.ds(0, W_STAGE_ROWS), pl.ds(0, width)], sem.at[slot])

    pos = {"started": 0, "taken": 0}

    def start_below(limit):
        while pos["started"] < min(limit, len(windows)):
            copy(pos["started"]).start()
            pos["started"] += 1

    def take(k):
        for _ in range(k):
            n = pos["taken"]
            _, dst, r0, _, d0, width = windows[n]
            copy(n).wait()
            dst[r0:r0 + W_STAGE_ROWS, d0:d0 + width] = stage[n % n_slots, :, 0:width].astype(BF16)
            pos["taken"] += 1
            start_below(n + n_slots + 1)

    start_below(n_slots)
    return take, lambda: pos["taken"] == len(windows)


def _step0_weights(windows, stage, sem):
    first = pl.program_id(0) == 0
    stream = {}

    @pl.when(first)
    def _():
        stream["take"], stream["all_taken"] = _weight_stream(windows, stage, sem)

    def need(k):
        @pl.when(first)
        def _():
            stream["take"](k)

    return need, lambda: stream["all_taken"]()


def _adaln_kernel(c_ref, w_ref, b_ref, o_ref):
    c = c_ref[...]
    s = c * jax.nn.sigmoid(c)
    o_ref[...] = _dot(s.astype(BF16), w_ref[...].astype(BF16)) + b_ref[...]


def _adaln(cvec, w_ada, b_ada):
    rows = cvec.shape[0]
    n = w_ada.shape[1]
    tn = 1536
    return pl.pallas_call(
        _adaln_kernel,
        out_shape=jax.ShapeDtypeStruct((rows, n), F32),
        grid=(n // tn,),
        in_specs=[
            pl.BlockSpec((rows, D_MODEL), lambda j: (0, 0)),
            pl.BlockSpec((D_MODEL, tn), lambda j: (0, j)),
            pl.BlockSpec((1, tn), lambda j: (0, j)),
        ],
        out_specs=pl.BlockSpec((rows, tn), lambda j: (0, j)),
        compiler_params=pltpu.CompilerParams(
            dimension_semantics=("arbitrary",), vmem_limit_bytes=VMEM_LIMIT),
        name="adaln",
    )(cvec, w_ada, b_ada)


def _attn_kernel(*refs, cached, seq):
    if cached:
        (x_ref, mod_ref, g_ref, w_hbm, cos_ref, sa_ref, sb_ref, ck_ref, cv_ref, lam_ref, dn_ref,
         o_ref, qkv_s, w_ref, stage, sem) = refs
    else:
        x_ref, mod_ref, g_ref, w_hbm, lam_ref, dn_ref, o_ref, nk_ref, nv_ref, qkv_s, w_ref, stage, sem = refs
        assert seq == ROW_CHUNK
    rc = ROW_CHUNK
    n_rows = x_ref.shape[0]
    need, all_taken = _step0_weights(
        [(w_hbm, w_ref, r0, c0, c0, width) for c0, width in ((C_QA, C_VA - C_QA), (C_VA, C_QB - C_VA))
         for r0 in range(0, D_MODEL, W_STAGE_ROWS)], stage, sem)
    per_group = D_MODEL // W_STAGE_ROWS
    lp = lam_ref[...]
    lam = (jnp.exp(jnp.sum(lp[0:1] * lp[1:2], axis=-1, keepdims=True))
           - jnp.exp(jnp.sum(lp[2:3] * lp[3:4], axis=-1, keepdims=True)) + LAM_INIT)
    lane = lax.broadcasted_iota(jnp.int32, (1, 128), 1)
    scale = DK_A ** -0.5
    m1 = jnp.where(lane < DK_A, scale, 0.0).astype(BF16)
    m2 = jnp.where(lane >= DK_A, scale, 0.0).astype(BF16)
    n_keys = (n_rows + ck_ref.shape[1]) if cached else rc
    ones = jnp.ones((n_keys, 128), BF16)

    def proj(r0):
        rows = slice(r0, r0 + rc)
        h = _prenorm(x_ref, rows, mod_ref, g_ref)
        yield
        if r0 == 0:
            need(per_group)
        pa = _dot(h, w_ref[:, C_QA:C_VA])
        if r0 == 0:
            need(per_group)
        pv = _dot(h, w_ref[:, C_VA:C_QB])
        for blk in range(8):
            t = pa[:, blk * 128:(blk + 1) * 128]
            if cached:
                t = (t * cos_ref[rows, :] + pltpu.roll(t, 112, 1) * sa_ref[rows, :]
                     + pltpu.roll(t, 16, 1) * sb_ref[rows, :])
            elif blk >= 4:
                nk_ref[r0 // seq, blk - 4] = t
            qkv_s[rows, blk * 128:(blk + 1) * 128] = t.astype(BF16)
        qkv_s[rows, C_VA:C_QB] = pv.astype(BF16)
        if not cached:
            for hh in range(H_A):
                nv_ref[r0 // seq, hh] = pv[:, hh * 128:(hh + 1) * 128]

    kv = {}
    res = {}

    def chain(r0, h, branch):
        rows = slice(r0, r0 + rc)
        cols = slice(h * 128, (h + 1) * 128)
        kcols = slice(C_KA + h * 128, C_KA + (h + 1) * 128)
        vcols = slice(C_VA + h * 128, C_VA + (h + 1) * 128)
        key = h if cached else (h, r0)
        if key not in kv:
            if cached:
                k = jnp.concatenate([ck_ref[h].astype(BF16), qkv_s[:, kcols]], axis=0)
                v = jnp.concatenate([cv_ref[h].astype(BF16), qkv_s[:, vcols]], axis=0)
            else:
                k, v = qkv_s[rows, kcols], qkv_s[rows, vcols]
            kv[key] = (k, jnp.concatenate([v, ones], axis=1))
        k, v1 = kv[key]
        s = _dot_nt(qkv_s[rows, cols] * (m2 if branch else m1), k)
        yield
        e = jnp.exp(s - jnp.max(s, axis=-1, keepdims=True)).astype(BF16)
        yield
        res[r0, h, branch] = _dot(e, v1)
        yield
        if branch:
            r1, r2 = res[r0, h, 0], res[r0, h, 1]
            o = r1[:, 0:128] / r1[:, 128:256] - lam * (r2[:, 0:128] / r2[:, 128:256])
            o_ref[rows, cols] = (_rms(o, dn_ref[...]) * (1.0 - LAM_INIT)).astype(BF16)

    projs = {r0: proj(r0) for r0 in range(0, n_rows, rc)}
    gens, deps = list(projs.values()), {}
    for r0 in projs:
        for h in range(H_A):
            for br in range(2):
                c = chain(r0, h, br)
                deps[id(c)] = list(projs.values()) if cached else [projs[r0]]
                gens.append(c)
    _run_skewed(gens, deps=deps)
    assert all_taken()


def _attention(x2d, mods, g_pre, w_in_f32, rope_tabs, cache_k, cache_v, lam_p, diff_norm, batch, seq_len):
    cached = cache_k is not None
    tq = MIX_TILE
    n_tiles = batch * seq_len // tq
    const = lambda i: (0, 0)
    in_specs = [
        pl.BlockSpec((tq, D_MODEL), lambda i: (i, 0)),
        pl.BlockSpec((None, N_MOD, D_MODEL), (lambda i: (i, 0, 0)) if cached else (lambda i: (0, 0, 0))),
        pl.BlockSpec((1, D_MODEL), const),
        pl.BlockSpec(memory_space=pl.ANY),
    ]
    args = [x2d, mods, g_pre, w_in_f32]
    out_shape = [jax.ShapeDtypeStruct((batch * seq_len, H_A * DV_A), BF16)]
    out_specs = [pl.BlockSpec((tq, H_A * DV_A), lambda i: (i, 0))]
    if cached:
        assert seq_len == tq and mods.shape[0] == batch
        past = cache_k.shape[3]
        cspec = pl.BlockSpec((None, None, H_A, past, 128), lambda i: (i, 0, 0, 0, 0))
        in_specs += [pl.BlockSpec((tq, 128), const)] * 3 + [cspec, cspec]
        args += list(rope_tabs) + [cache_k, cache_v]
    else:
        assert mods.shape[0] == 1
        nb = tq // seq_len
        for _ in range(2):
            out_shape.append(jax.ShapeDtypeStruct((batch, 1, H_A, seq_len, 128), F32))
            out_specs.append(pl.BlockSpec((nb, None, H_A, seq_len, 128), lambda i: (i, 0, 0, 0, 0)))
    in_specs += [pl.BlockSpec((4, DK_A), const), pl.BlockSpec((1, DV_A), const)]
    args += [lam_p, diff_norm]
    return pl.pallas_call(
        functools.partial(_attn_kernel, cached=cached, seq=seq_len),
        out_shape=out_shape,
        grid=(n_tiles,),
        in_specs=in_specs,
        out_specs=out_specs,
        scratch_shapes=[
            pltpu.VMEM((tq, C_QB), BF16),
            pltpu.VMEM((D_MODEL, C_QB), BF16),
            pltpu.VMEM((W_STAGE_SLOTS, W_STAGE_ROWS, D_MODEL), F32),
            pltpu.SemaphoreType.DMA((W_STAGE_SLOTS,)),
        ],
        compiler_params=pltpu.CompilerParams(
            dimension_semantics=("arbitrary",), vmem_limit_bytes=VMEM_LIMIT),
        name="attn_cached" if cached else "attn_ctx",
    )(*args)


def _split3(g):
    hi = g.astype(BF16)
    r1 = g - hi.astype(F32)
    mid = r1.astype(BF16)
    lo = (r1 - mid.astype(F32)).astype(BF16)
    return hi, mid, lo


def _gla_block_stages(q_ref, k_ref, g_ref, v_ref, in_rows, qc, vc, vt_cache, blk, reverse, oa_s, qe_s, ut_s, dec_s):
    C = GLA_CHUNK
    R = in_rows.stop - in_rows.start
    cpb = R // C
    rows = slice(blk * R, (blk + 1) * R)
    ri = lax.broadcasted_iota(jnp.int32, (R, R), 0)
    ci = lax.broadcasted_iota(jnp.int32, (R, R), 1)
    same_chunk = jnp.right_shift(ri, 6) == jnp.right_shift(ci, 6)
    if reverse:
        keep = same_chunk & (ci >= ri)
        last, ref = 0, C // 2
    else:
        keep = same_chunk & (ci <= ri)
        last, ref = C - 1, C // 2 - 1
    tri = jnp.where(keep, 1.0, 0.0).astype(BF16)
    lane = lax.broadcasted_iota(jnp.int32, (1, 2 * DK_B), 1)
    h0 = jnp.where(lane < DK_B, 1.0, 0.0).astype(BF16)
    h1 = jnp.where(lane >= DK_B, 1.0, 0.0).astype(BF16)

    hi, mid, lo = _split3(g_ref[in_rows, qc])
    b3 = _dot(tri, jnp.concatenate([hi, mid, lo], axis=1))
    yield
    b = b3[:, 0:128] + b3[:, 128:256] + b3[:, 256:384]
    q = q_ref[in_rows, qc]
    k = k_ref[in_rows, qc]
    qt, kt, qe, kd = [], [], [], []
    zeros = jnp.zeros((C, 2 * DK_B), BF16)
    for c in range(cpb):
        s = slice(c * C, (c + 1) * C)
        bc = b[s]
        bm = bc[ref:ref + 1]
        bl = bc[last:last + 1]
        qt.append((q[s] * jnp.exp(bc - bm)).astype(BF16))
        kt.append((k[s] * jnp.exp(bm - bc)).astype(BF16))
        qe.append((q[s] * jnp.exp(bc)).astype(BF16))
        kdc = (k[s] * jnp.exp(bl - bc)).astype(BF16)
        kd.append(jnp.concatenate([kdc if j == c else zeros for j in range(cpb)], axis=1))
        dec_s[blk * cpb + c:blk * cpb + c + 1, :] = jnp.exp(bl)
    qe_s[rows, :] = jnp.concatenate(qe, axis=0)
    qt = jnp.concatenate(qt, axis=0)
    kt = jnp.concatenate(kt, axis=0)
    yield
    s0 = _dot_nt(qt * h0, kt)
    s1 = _dot_nt(qt * h1, kt)
    v = v_ref[in_rows, vc]
    if (in_rows.start, vc.start) not in vt_cache:
        vt_cache[in_rows.start, vc.start] = v.T
    ut = _dot(vt_cache[in_rows.start, vc.start], jnp.concatenate(kd, axis=0))
    yield
    a0 = jnp.where(keep, s0, 0.0).astype(BF16)
    a1 = jnp.where(keep, s1, 0.0).astype(BF16)
    br = lax.broadcasted_iota(jnp.int32, ut.shape, 0)
    bcol = lax.broadcasted_iota(jnp.int32, ut.shape, 1)
    ut = jnp.where((br < DV_B) == ((bcol & (2 * DK_B - 1)) < DK_B), ut, 0.0)
    for c in range(cpb):
        ut_s[blk * cpb + c] = ut[:, c * 2 * DK_B:(c + 1) * 2 * DK_B]
    yield
    oa_s[rows, :] = jnp.concatenate([_dot(a0, v[:, 0:DV_B]), _dot(a1, v[:, DV_B:2 * DV_B])], axis=1)


def _state_to_t(s_pair):
    z = jnp.zeros((DK_B, DV_B), F32)
    a = jnp.concatenate([s_pair[0], z], axis=0).T
    b = jnp.concatenate([z, s_pair[1]], axis=0).T
    return jnp.concatenate([a, b], axis=0)


def _gla_kernel(*refs, n_chunks, n_elems, has_state):
    if has_state:
        (x_ref, mod_ref, g_ref, w_hbm, wgl_ref, wg_ref, bg_ref, gn_ref, sf_ref, sb_ref, og_ref, *scratch) = refs
    else:
        (x_ref, mod_ref, g_ref, w_hbm, wgl_ref, wg_ref, bg_ref, gn_ref, og_ref, nsf_ref, nsb_ref, *scratch) = refs
    q_s, k_s, gf_s, gb_s, rb_s, v_s, oa, oi, st, qe, ut, dec, w_ref, stage, sem = scratch
    C = GLA_CHUNK
    R = ROW_CHUNK
    cpb = R // C
    L = n_chunks * C
    n_blk = L // R
    n_pairs = H_B // 2
    need, all_taken = _step0_weights(
        [(w_hbm, w_ref, r0, C_QB + b0, b0, width) for b0, width in ((B_RB, B_GL - B_RB), (B_QB, B_RB - B_QB))
         for r0 in range(0, D_MODEL, W_STAGE_ROWS)], stage, sem)
    per_group = D_MODEL // W_STAGE_ROWS

    def proj(r0):
        rows = slice(r0, r0 + R)
        h = _prenorm(x_ref, rows, mod_ref, g_ref)
        yield
        if r0 == 0:
            need(per_group)
        pr = _dot(h, w_ref[:, B_RB:B_GL])
        pgl = _dot(h, wgl_ref[...])
        if r0 == 0:
            need(per_group)
        pb = _dot(h, w_ref[:, B_QB:B_RB])
        z = _dot(pgl.astype(BF16), wg_ref[...]) + bg_ref[...]
        ls = (jnp.minimum(z, 0.0) - jnp.log(1.0 + jnp.exp(-jnp.abs(z)))) * (1.0 / GATE_NORM)
        gf_s[rows, :] = ls[:, 0:256]
        gb_s[rows, :] = ls[:, 256:512]
        rb_s[rows, :] = pr
        q_s[rows, :] = pb[:, B_QB:B_KB] * (DK_B ** -0.5)
        k_s[rows, :] = pb[:, B_KB:B_VB]
        v_s[rows, :] = pb[:, B_VB:B_RB].astype(BF16)

    def scan_stage(e, p, d, blk):
        ch = 2 * p + d
        if blk != (n_blk - 1 if d else 0):
            s = st[e, ch]
        elif has_state:
            assert n_elems == 1
            s = _state_to_t((sb_ref if d else sf_ref)[2 * p:2 * p + 2])
        else:
            s = jnp.zeros(st.shape[2:], F32)
        for i in range(cpb):
            c = blk * cpb + (cpb - 1 - i if d else i)
            rows = slice(c * C, (c + 1) * C)
            oi[e, ch, rows, :] = _dot_nt(qe[e, ch, rows, :], s.astype(BF16))
            s = s * dec[e, ch, c:c + 1, :] + ut[e, ch, c]
            yield
        if blk != (0 if d else n_blk - 1):
            st[e, ch] = s
        elif not has_state:
            dst_ref = nsb_ref if d else nsf_ref
            dst_ref[e, 2 * p] = s[0:DV_B, :].T[0:DK_B, :]
            dst_ref[e, 2 * p + 1] = s[DV_B:2 * DV_B, :].T[DK_B:2 * DK_B, :]

    def fin_stage(e, p, blk):
        rows = slice(blk * R, (blk + 1) * R)
        out_rows = slice(e * L + blk * R, e * L + (blk + 1) * R)
        o = ((oa[e, 2 * p, rows, :] + oi[e, 2 * p, rows, :])
             + (oa[e, 2 * p + 1, rows, :] + oi[e, 2 * p + 1, rows, :]))
        for j in range(2):
            cols = slice((2 * p + j) * DV_B, (2 * p + j + 1) * DV_B)
            oj = _rms(o[:, j * DV_B:(j + 1) * DV_B], gn_ref[...])
            rj = rb_s[out_rows, cols]
            og_ref[out_rows, cols] = (oj * (rj * jax.nn.sigmoid(rj))).astype(BF16)
        yield

    vt_cache = {}
    projs, blocks, scans, deps = {}, [], {}, {}
    for e in range(n_elems):
        for t in range(n_blk):
            blk = t // 2 if t % 2 == 0 else n_blk - 1 - t // 2
            projs[e, blk] = proj(e * L + blk * R)
    for e in range(n_elems):
        for t in range(n_blk):
            for p in range(n_pairs):
                for d in range(2):
                    blk = n_blk - 1 - t if d else t
                    ch = 2 * p + d
                    in_rows = slice(e * L + blk * R, e * L + (blk + 1) * R)
                    g = _gla_block_stages(
                        q_s, k_s, gb_s if d else gf_s, v_s, in_rows,
                        slice(p * 2 * DK_B, (p + 1) * 2 * DK_B), slice(p * 2 * DV_B, (p + 1) * 2 * DV_B),
                        vt_cache, blk, bool(d), oa.at[e, ch], qe.at[e, ch], ut.at[e, ch], dec.at[e, ch])
                    deps[id(g)] = [projs[e, blk]]
                    blocks.append(g)
                    s = scan_stage(e, p, d, blk)
                    prev = scans.get((e, p, d, blk + 1 if d else blk - 1))
                    deps[id(s)] = [g] + ([prev] if prev is not None else [])
                    scans[e, p, d, blk] = s
    aux = []
    for e in range(n_elems):
        for t in range(n_blk):
            for p in range(n_pairs):
                aux.append(scans[e, p, 0, t])
                aux.append(scans[e, p, 1, n_blk - 1 - t])
        for blk in range(n_blk):
            for p in range(n_pairs):
                f = fin_stage(e, p, blk)
                deps[id(f)] = [scans[e, p, 0, blk], scans[e, p, 1, blk]]
                aux.append(f)
    _run_skewed(list(projs.values()) + blocks, aux, deps)
    assert all_taken()


def _gla(x2d, mods, g_pre, w_in_f32, w_gl, wg, bg, gla_norm, state_f, state_b, batch, seq_len):
    has_state = state_f is not None
    n_chunks = seq_len // GLA_CHUNK
    L = seq_len
    rows = MIX_TILE
    n_elems = rows // L
    n_ch = H_B
    wk, wv = H_B * DK_B, H_B * DV_B
    const = lambda i: (0, 0)
    in_specs = [
        pl.BlockSpec((rows, D_MODEL), lambda i: (i, 0)),
        pl.BlockSpec((None, N_MOD, D_MODEL), (lambda i: (i, 0, 0)) if has_state else (lambda i: (0, 0, 0))),
        pl.BlockSpec((1, D_MODEL), const),
        pl.BlockSpec(memory_space=pl.ANY),
        pl.BlockSpec((D_MODEL, 2 * GATE_RANK), const),
        pl.BlockSpec((2 * GATE_RANK, 2 * wk), const),
        pl.BlockSpec((1, 2 * wk), const),
        pl.BlockSpec((1, DV_B), const),
    ]
    args = [x2d, mods, g_pre, w_in_f32, w_gl, wg, bg, gla_norm]
    out_shape = [jax.ShapeDtypeStruct((batch * L, wv), BF16)]
    out_specs = [pl.BlockSpec((rows, wv), lambda i: (i, 0))]
    if has_state:
        assert n_elems == 1 and mods.shape[0] == batch
        st_spec = pl.BlockSpec((None, None, H_B, DK_B, DV_B), lambda i: (i, 0, 0, 0, 0))
        in_specs += [st_spec, st_spec]
        args += [state_f, state_b]
    else:
        assert mods.shape[0] == 1
        for _ in range(2):
            out_shape.append(jax.ShapeDtypeStruct((batch, 1, H_B, DK_B, DV_B), F32))
            out_specs.append(pl.BlockSpec((n_elems, None, H_B, DK_B, DV_B), lambda i: (i, 0, 0, 0, 0)))
    return pl.pallas_call(
        functools.partial(_gla_kernel, n_chunks=n_chunks, n_elems=n_elems, has_state=has_state),
        out_shape=out_shape,
        grid=(batch // n_elems,),
        in_specs=in_specs,
        out_specs=out_specs,
        scratch_shapes=[
            pltpu.VMEM((rows, wk), F32),
            pltpu.VMEM((rows, wk), F32),
            pltpu.VMEM((rows, wk), F32),
            pltpu.VMEM((rows, wk), F32),
            pltpu.VMEM((rows, wv), F32),
            pltpu.VMEM((rows, wv), BF16),
            pltpu.VMEM((n_elems, n_ch, L, 2 * DV_B), F32),
            pltpu.VMEM((n_elems, n_ch, L, 2 * DV_B), F32),
            pltpu.VMEM((n_elems, n_ch, 2 * DV_B, 2 * DK_B), F32),
            pltpu.VMEM((n_elems, n_ch, L, 2 * DK_B), BF16),
            pltpu.VMEM((n_elems, n_ch, n_chunks, 2 * DV_B, 2 * DK_B), F32),
            pltpu.VMEM((n_elems, n_ch, max(n_chunks, 8), 2 * DK_B), F32),
            pltpu.VMEM((D_MODEL, B_GL), BF16),
            pltpu.VMEM((W_STAGE_SLOTS, W_STAGE_ROWS, D_MODEL), F32),
            pltpu.SemaphoreType.DMA((W_STAGE_SLOTS,)),
        ],
        compiler_params=pltpu.CompilerParams(
            dimension_semantics=("arbitrary",), vmem_limit_bytes=VMEM_LIMIT),
        name="gla_state" if has_state else "gla_ctx",
    )(*args)


def _out_mlp_kernel(xc_ref, oac_ref, ogc_ref, xl_ref, oal_ref, ogl_ref, mod_ref, wo_hbm, gpost_ref, gmpre_ref,
                    gmpost_ref, w1_hbm, w2_hbm, yc_ref, yl_ref, wo_ref, w1_ref, w2_ref, stage, sem, *, n_ctx_tiles):
    rc = ROW_CHUNK
    tf = 1024
    per_win = tf // W_STAGE_ROWS

    def weight_windows():
        windows = [(wo_hbm, wo_ref, r0, 0, 0, D_MODEL) for r0 in range(0, D_MODEL, W_STAGE_ROWS)]
        for j in range(D_FF // tf):
            windows += [(w1_hbm, w1_ref, r0, j * tf, j * tf, tf) for r0 in range(0, D_MODEL, W_STAGE_ROWS)]
            windows += [(w2_hbm, w2_ref, j * tf + r0, 0, 0, D_MODEL) for r0 in range(0, tf, W_STAGE_ROWS)]
        return windows

    def run(x_ref, oa_ref, og_ref, y_ref, take=None):
        def chunk(r0):
            need = take if (take is not None and r0 == 0) else (lambda k: None)
            rows = slice(r0, r0 + rc)
            need(D_MODEL // W_STAGE_ROWS)
            mix = _dot(oa_ref[rows, :], wo_ref[0:512, :]) + _dot(og_ref[rows, :], wo_ref[512:1024, :])
            yield
            x1 = x_ref[rows, :] + mod_ref[2:3, :] * _rms(mix, gpost_ref[...])
            h2 = (_rms(x1, gmpre_ref[...]) * (1.0 + mod_ref[4:5, :]) + mod_ref[3:4, :]).astype(BF16)
            yield
            acc = None
            for j in range(D_FF // tf):
                need(D_MODEL // W_STAGE_ROWS + per_win)
                u = jnp.maximum(_dot(h2, w1_ref[:, j * tf:(j + 1) * tf]), 0.0)
                part = _dot((u * u).astype(BF16), w2_ref[j * tf:(j + 1) * tf, :])
                acc = part if acc is None else acc + part
                yield
            y_ref[rows, :] = x1 + mod_ref[5:6, :] * _rms(acc, gmpost_ref[...])

        _run_skewed([chunk(r0) for r0 in range(0, x_ref.shape[0], rc)])

    step = pl.program_id(0)
    is_ctx = step < n_ctx_tiles

    @pl.when(step == 0)
    def _():
        take, all_taken = _weight_stream(weight_windows(), stage, sem)
        run(xc_ref, oac_ref, ogc_ref, yc_ref, take)
        assert all_taken()

    @pl.when(jnp.logical_and(step > 0, is_ctx))
    def _():
        run(xc_ref, oac_ref, ogc_ref, yc_ref)

    @pl.when(jnp.logical_not(is_ctx))
    def _():
        run(xl_ref, oal_ref, ogl_ref, yl_ref)


def _out_mlp(xc2d, oac, ogc, xl2d, oal, ogl, mods, lat_seq, w_out, g_post, g_mpre, g_mpost, w1, w2):
    tm = 512
    n_c, n_l = xc2d.shape[0] // tm, xl2d.shape[0] // tm
    per_b = lat_seq // tm
    ctx_row = lambda i: (jnp.minimum(i, n_c - 1), 0)
    lat_row = lambda i: (jnp.maximum(i - n_c, 0), 0)
    mod_idx = lambda i: (jnp.where(i < n_c, 0, 1 + jnp.maximum(i - n_c, 0) // per_b), 0, 0)
    const = lambda i: (0, 0)
    return pl.pallas_call(
        functools.partial(_out_mlp_kernel, n_ctx_tiles=n_c),
        out_shape=[jax.ShapeDtypeStruct(xc2d.shape, F32), jax.ShapeDtypeStruct(xl2d.shape, F32)],
        grid=(n_c + n_l,),
        in_specs=[
            pl.BlockSpec((tm, D_MODEL), ctx_row),
            pl.BlockSpec((tm, 512), ctx_row),
            pl.BlockSpec((tm, 512), ctx_row),
            pl.BlockSpec((tm, D_MODEL), lat_row),
            pl.BlockSpec((tm, 512), lat_row),
            pl.BlockSpec((tm, 512), lat_row),
            pl.BlockSpec((None, N_MOD, D_MODEL), mod_idx),
            pl.BlockSpec(memory_space=pl.ANY),
            pl.BlockSpec((1, D_MODEL), const),
            pl.BlockSpec((1, D_MODEL), const),
            pl.BlockSpec((1, D_MODEL), const),
            pl.BlockSpec(memory_space=pl.ANY),
            pl.BlockSpec(memory_space=pl.ANY),
        ],
        out_specs=[pl.BlockSpec((tm, D_MODEL), ctx_row), pl.BlockSpec((tm, D_MODEL), lat_row)],
        scratch_shapes=[
            pltpu.VMEM((D_MODEL, D_MODEL), BF16),
            pltpu.VMEM((D_MODEL, D_FF), BF16),
            pltpu.VMEM((D_FF, D_MODEL), BF16),
            pltpu.VMEM((W_STAGE_SLOTS, W_STAGE_ROWS, D_MODEL), F32),
            pltpu.SemaphoreType.DMA((W_STAGE_SLOTS,)),
        ],
        compiler_params=pltpu.CompilerParams(
            dimension_semantics=("arbitrary",), vmem_limit_bytes=VMEM_LIMIT),
        name="out_mlp",
    )(xc2d, oac, ogc, xl2d, oal, ogl, mods, w_out, g_post, g_mpre, g_mpost, w1, w2)


def _rope_tables(n_lat):
    pos = np.arange(n_lat)
    row_pos = (pos // GRID_W).astype(np.float64)
    col_pos = (pos % GRID_W).astype(np.float64)
    half = DK_A // 4
    inv = ROPE_BASE ** (-np.arange(half, dtype=np.float64) / half)
    lane = np.arange(128)
    in64 = lane % 64
    in32 = in64 % 32
    p = np.where((in64 < 32)[None, :], row_pos[:, None], col_pos[:, None])
    ang = p * inv[in32 % half][None, :]
    cos, sin = np.cos(ang), np.sin(ang)
    first = (in32 < half)[None, :]
    sa = np.where(first, -sin, 0.0)
    sb = np.where(first, 0.0, sin)
    return tuple(jnp.asarray(t, dtype=F32) for t in (cos, sa, sb))


def _mixers(x, mods, w, rope_tabs, cache_k, cache_v, state_f, state_b):
    batch, seq_len, _ = x.shape
    x2d = x.reshape(batch * seq_len, D_MODEL)
    aouts = _attention(x2d, mods, w["g_pre"], w["w_in"], rope_tabs, cache_k, cache_v, w["lam"], w["diff_norm"],
                       batch, seq_len)
    gouts = _gla(x2d, mods, w["g_pre"], w["w_in"], w["w_gl"], w["wg"], w["bg"], w["gla_norm"], state_f, state_b,
                 batch, seq_len)
    return x2d, aouts[0], gouts[0], tuple(aouts[1:]) + tuple(gouts[1:])


def kernel(x_prompt, x_sample, c, cache_k, cache_v, state_fwd, state_bwd, c_ctx, w_ada, b_ada,
           norm_attn_pre, norm_attn_post, norm_mlp_pre, norm_mlp_post, w_in, w_gate_fwd, b_gate_fwd,
           w_gate_bwd, b_gate_bwd, lam_q1, lam_k1, lam_q2, lam_k2, diff_norm, gla_norm, w_out,
           w_mlp1, w_mlp2):
    dec_batch = c.shape[0]
    rows = 16
    cvec = jnp.zeros((rows, D_MODEL), F32).at[0].set(c_ctx).at[1:1 + dec_batch].set(c)
    m = _adaln(cvec, w_ada[0], b_ada[0][None, :])
    mods_ctx = m[0:1].reshape(1, N_MOD, D_MODEL)
    mods_lat = m[1:1 + dec_batch].reshape(dec_batch, N_MOD, D_MODEL)

    wg = jnp.zeros((2 * GATE_RANK, 512), F32)
    wg = wg.at[0:GATE_RANK, 0:256].set(w_gate_fwd[0]).at[GATE_RANK:2 * GATE_RANK, 256:512].set(w_gate_bwd[0])
    w = {
        "g_pre": norm_attn_pre[0][None, :],
        "g_post": norm_attn_post[0][None, :],
        "g_mpre": norm_mlp_pre[0][None, :],
        "g_mpost": norm_mlp_post[0][None, :],
        "w_in": w_in[0],
        "w_gl": w_in[0][:, C_GL:IN_COLS].astype(BF16),
        "wg": wg.astype(BF16),
        "bg": jnp.concatenate([b_gate_fwd[0], b_gate_bwd[0]])[None, :],
        "lam": jnp.stack([lam_q1[0], lam_k1[0], lam_q2[0], lam_k2[0]]),
        "diff_norm": diff_norm[0][None, :],
        "gla_norm": gla_norm[0][None, :],
        "w_out": w_out[0],
        "w1": w_mlp1[0],
        "w2": w_mlp2[0],
    }
    xc2d, oac, ogc, (new_k, new_v, new_sf, new_sb) = _mixers(x_prompt, mods_ctx, w, None, None, None, None, None)
    xl2d, oal, ogl, _ = _mixers(x_sample, mods_lat, w, _rope_tables(x_sample.shape[1]),
                                cache_k, cache_v, state_fwd, state_bwd)
    mods_all = m[0:1 + dec_batch].reshape(1 + dec_batch, N_MOD, D_MODEL)
    y_prompt, y_sample = _out_mlp(xc2d, oac, ogc, xl2d, oal, ogl, mods_all, x_sample.shape[1],
                                  w["w_out"], w["g_post"], w["g_mpre"], w["g_mpost"], w["w1"], w["w2"])
    return (y_prompt.reshape(x_prompt.shape), y_sample.reshape(x_sample.shape), new_k, new_v, new_sf, new_sb)
```

```python
import functools
import math

import numpy as np
import jax
import jax.numpy as jnp
from jax import lax
from jax.experimental import pallas as pl
from jax.experimental.pallas import tpu as pltpu

F32 = jnp.float32
BF16 = jnp.bfloat16

D_MODEL = 1024
GRID_W = 64
H_A = 4
DV_A = 128
DK_A = 64
H_B = 4
DV_B = 128
DK_B = 64
GATE_RANK = 16
GATE_NORM = 16.0
GLA_CHUNK = 64
D_FF = 4 * D_MODEL
N_MOD = 6
ROPE_BASE = 10000.0
EPS = 1e-6
LAM_INIT = 0.8 - 0.6 * math.exp(-0.3 * 0)

C_QA, C_KA, C_VA, C_QB, C_KB, C_VB, C_RB, C_GL = 0, 512, 1024, 1536, 1792, 2048, 2560, 3072
IN_COLS = 3104
B_QB, B_KB, B_VB, B_RB, B_GL, B_COLS = (c - C_QB for c in (C_QB, C_KB, C_VB, C_RB, C_GL, IN_COLS))

V7X_VMEM_BYTES = 64 * 1024 * 1024
VMEM_LIMIT = V7X_VMEM_BYTES - 4 * 1024 * 1024
ROW_CHUNK = 256
MIX_TILE = 1024
W_STAGE_ROWS = 512
W_STAGE_SLOTS = 4


def _rms(x, g):
    return x * lax.rsqrt(jnp.mean(x * x, axis=-1, keepdims=True) + EPS) * g


def _dot(a, b):
    return jnp.dot(a, b, preferred_element_type=F32)


def _dot_nt(a, b):
    return lax.dot_general(a, b, (((1,), (1,)), ((), ())), preferred_element_type=F32)


def _run_skewed(gens, secondary=(), deps=None):
    deps = deps or {}
    queues, running, done = [list(gens), list(secondary)], [], set()
    while any(queues) or running:
        for queue in queues:
            for g in queue:
                if all(id(d) in done for d in deps.get(id(g), ())):
                    queue.remove(g)
                    running.append(g)
                    break
        assert running, "dependency cycle"
        for g in list(running):
            try:
                next(g)
            except StopIteration:
                running.remove(g)
                done.add(id(g))


def _prenorm(x_ref, rows, mod_ref, g_ref):
    x = x_ref[rows, :]
    return (_rms(x, g_ref[...]) * (1.0 + mod_ref[1:2, :]) + mod_ref[0:1, :]).astype(BF16)


def _weight_stream(windows, stage, sem):
    n_slots = stage.shape[0]

    def copy(n):
        src, _, r0, c0, _, _, width = windows[n]
        slot = n % n_slots
        return pltpu.make_async_copy(src.at[pl.ds(r0, W_STAGE_ROWS), pl.ds(c0, width)],
                                     stage.at[slot, pl.ds(0, W_STAGE_ROWS), pl.ds(0, width)], sem.at[slot])

    pos = {"started": 0, "taken": 0}

    def start_below(limit):
        while pos["started"] < min(limit, len(windows)):
            copy(pos["started"]).start()
            pos["started"] += 1

    def take(k):
        for _ in range(k):
            n = pos["taken"]
            _, dst, _, _, r0, d0, width = windows[n]
            copy(n).wait()
            dst[r0:r0 + W_STAGE_ROWS, d0:d0 + width] = stage[n % n_slots, :, 0:width].astype(BF16)
            pos["taken"] += 1
            start_below(n + n_slots + 1)

    start_below(n_slots)
    return take, lambda: pos["taken"] == len(windows)


def _step0_weights(windows, stage, sem):
    first = pl.program_id(0) == 0
    stream = {}

    @pl.when(first)
    def _():
        stream["take"], stream["all_taken"] = _weight_stream(windows, stage, sem)

    def need(k):
        @pl.when(first)
        def _():
            stream["take"](k)

    return need, lambda: stream["all_taken"]()


def _adaln_kernel(c_ref, w_ref, b_ref, o_ref):
    c = c_ref[...]
    s = c * jax.nn.sigmoid(c)
    o_ref[...] = _dot(s.astype(BF16), w_ref[...].astype(BF16)) + b_ref[...]


def _adaln(cvec, w_ada, b_ada):
    rows = cvec.shape[0]
    n = w_ada.shape[1]
    tn = 1536
    return pl.pallas_call(
        _adaln_kernel,
        out_shape=jax.ShapeDtypeStruct((rows, n), F32),
        grid=(n // tn,),
        in_specs=[
            pl.BlockSpec((rows, D_MODEL), lambda j: (0, 0)),
            pl.BlockSpec((D_MODEL, tn), lambda j: (0, j)),
            pl.BlockSpec((1, tn), lambda j: (0, j)),
        ],
        out_specs=pl.BlockSpec((rows, tn), lambda j: (0, j)),
        compiler_params=pltpu.CompilerParams(
            dimension_semantics=("arbitrary",), vmem_limit_bytes=VMEM_LIMIT),
        name="adaln",
    )(cvec, w_ada, b_ada)


def _attn_kernel(*refs, cached, seq):
    if cached:
        (x_ref, mod_ref, g_ref, w_hbm, cos_ref, sa_ref, sb_ref, ck_ref, cv_ref, lam_ref, dn_ref,
         o_ref, qkv_s, w_ref, stage, sem) = refs
    else:
        x_ref, mod_ref, g_ref, w_hbm, lam_ref, dn_ref, o_ref, nk_ref, nv_ref, qkv_s, w_ref, stage, sem = refs
        assert seq == ROW_CHUNK
    rc = ROW_CHUNK
    n_rows = x_ref.shape[0]
    need, all_taken = _step0_weights(
        [(w_hbm, w_ref, r0, 0, r0, 0, D_MODEL) for r0 in range(C_QA, C_QB, W_STAGE_ROWS)], stage, sem)
    lp = lam_ref[...]
    lam = (jnp.exp(jnp.sum(lp[0:1] * lp[1:2], axis=-1, keepdims=True))
           - jnp.exp(jnp.sum(lp[2:3] * lp[3:4], axis=-1, keepdims=True)) + LAM_INIT)
    lane = lax.broadcasted_iota(jnp.int32, (1, 128), 1)
    scale = DK_A ** -0.5
    m1 = jnp.where(lane < DK_A, scale, 0.0).astype(BF16)
    m2 = jnp.where(lane >= DK_A, scale, 0.0).astype(BF16)
    n_keys = (n_rows + ck_ref.shape[1]) if cached else rc
    ones = jnp.ones((n_keys, 128), BF16)

    def proj(r0):
        rows = slice(r0, r0 + rc)
        h = _prenorm(x_ref, rows, mod_ref, g_ref)
        yield
        if r0 == 0:
            need((C_VA - C_QA) // W_STAGE_ROWS)
        pa = _dot_nt(h, w_ref[C_QA:C_VA, :])
        if r0 == 0:
            need((C_QB - C_VA) // W_STAGE_ROWS)
        pv = _dot_nt(h, w_ref[C_VA:C_QB, :])
        for blk in range(8):
            t = pa[:, blk * 128:(blk + 1) * 128]
            if cached:
                t = (t * cos_ref[rows, :] + pltpu.roll(t, 112, 1) * sa_ref[rows, :]
                     + pltpu.roll(t, 16, 1) * sb_ref[rows, :])
            elif blk >= 4:
                nk_ref[r0 // seq, blk - 4] = t
            qkv_s[rows, blk * 128:(blk + 1) * 128] = t.astype(BF16)
        qkv_s[rows, C_VA:C_QB] = pv.astype(BF16)
        if not cached:
            for hh in range(H_A):
                nv_ref[r0 // seq, hh] = pv[:, hh * 128:(hh + 1) * 128]

    kv = {}
    res = {}

    def chain(r0, h, branch):
        rows = slice(r0, r0 + rc)
        cols = slice(h * 128, (h + 1) * 128)
        kcols = slice(C_KA + h * 128, C_KA + (h + 1) * 128)
        vcols = slice(C_VA + h * 128, C_VA + (h + 1) * 128)
        key = h if cached else (h, r0)
        if key not in kv:
            if cached:
                k = jnp.concatenate([ck_ref[h].astype(BF16), qkv_s[:, kcols]], axis=0)
                v = jnp.concatenate([cv_ref[h].astype(BF16), qkv_s[:, vcols]], axis=0)
            else:
                k, v = qkv_s[rows, kcols], qkv_s[rows, vcols]
            kv[key] = (k, jnp.concatenate([v, ones], axis=1))
        k, v1 = kv[key]
        s = _dot_nt(qkv_s[rows, cols] * (m2 if branch else m1), k)
        yield
        e = jnp.exp(s - jnp.max(s, axis=-1, keepdims=True)).astype(BF16)
        yield
        res[r0, h, branch] = _dot(e, v1)
        yield
        if branch:
            r1, r2 = res[r0, h, 0], res[r0, h, 1]
            o = r1[:, 0:128] / r1[:, 128:256] - lam * (r2[:, 0:128] / r2[:, 128:256])
            o_ref[rows, cols] = (_rms(o, dn_ref[...]) * (1.0 - LAM_INIT)).astype(BF16)

    projs = {r0: proj(r0) for r0 in range(0, n_rows, rc)}
    gens, deps = list(projs.values()), {}
    for r0 in projs:
        for h in range(H_A):
            for br in range(2):
                c = chain(r0, h, br)
                deps[id(c)] = list(projs.values()) if cached else [projs[r0]]
                gens.append(c)
    _run_skewed(gens, deps=deps)
    assert all_taken()


def _attention(x2d, mods, g_pre, w_in_f32, rope_tabs, cache_k, cache_v, lam_p, diff_norm, batch, seq_len):
    cached = cache_k is not None
    tq = MIX_TILE
    n_tiles = batch * seq_len // tq
    const = lambda i: (0, 0)
    in_specs = [
        pl.BlockSpec((tq, D_MODEL), lambda i: (i, 0)),
        pl.BlockSpec((None, N_MOD, D_MODEL), (lambda i: (i, 0, 0)) if cached else (lambda i: (0, 0, 0))),
        pl.BlockSpec((1, D_MODEL), const),
        pl.BlockSpec(memory_space=pl.ANY),
    ]
    args = [x2d, mods, g_pre, w_in_f32]
    out_shape = [jax.ShapeDtypeStruct((batch * seq_len, H_A * DV_A), BF16)]
    out_specs = [pl.BlockSpec((tq, H_A * DV_A), lambda i: (i, 0))]
    if cached:
        assert seq_len == tq and mods.shape[0] == batch
        past = cache_k.shape[3]
        cspec = pl.BlockSpec((None, None, H_A, past, 128), lambda i: (i, 0, 0, 0, 0))
        in_specs += [pl.BlockSpec((tq, 128), const)] * 3 + [cspec, cspec]
        args += list(rope_tabs) + [cache_k, cache_v]
    else:
        assert mods.shape[0] == 1
        nb = tq // seq_len
        for _ in range(2):
            out_shape.append(jax.ShapeDtypeStruct((batch, 1, H_A, seq_len, 128), F32))
            out_specs.append(pl.BlockSpec((nb, None, H_A, seq_len, 128), lambda i: (i, 0, 0, 0, 0)))
    in_specs += [pl.BlockSpec((4, DK_A), const), pl.BlockSpec((1, DV_A), const)]
    args += [lam_p, diff_norm]
    return pl.pallas_call(
        functools.partial(_attn_kernel, cached=cached, seq=seq_len),
        out_shape=out_shape,
        grid=(n_tiles,),
        in_specs=in_specs,
        out_specs=out_specs,
        scratch_shapes=[
            pltpu.VMEM((tq, C_QB), BF16),
            pltpu.VMEM((C_QB, D_MODEL), BF16),
            pltpu.VMEM((W_STAGE_SLOTS, W_STAGE_ROWS, D_MODEL), F32),
            pltpu.SemaphoreType.DMA((W_STAGE_SLOTS,)),
        ],
        compiler_params=pltpu.CompilerParams(
            dimension_semantics=("arbitrary",), vmem_limit_bytes=VMEM_LIMIT),
        name="attn_cached" if cached else "attn_ctx",
    )(*args)


def _split3(g):
    hi = g.astype(BF16)
    r1 = g - hi.astype(F32)
    mid = r1.astype(BF16)
    lo = (r1 - mid.astype(F32)).astype(BF16)
    return hi, mid, lo


def _gla_block_stages(q_ref, k_ref, g_ref, v_ref, in_rows, qc, vc, vt_cache, blk, reverse, oa_s, qe_s, ut_s, dec_s):
    C = GLA_CHUNK
    R = in_rows.stop - in_rows.start
    cpb = R // C
    rows = slice(blk * R, (blk + 1) * R)
    ri = lax.broadcasted_iota(jnp.int32, (R, R), 0)
    ci = lax.broadcasted_iota(jnp.int32, (R, R), 1)
    same_chunk = jnp.right_shift(ri, 6) == jnp.right_shift(ci, 6)
    if reverse:
        keep = same_chunk & (ci >= ri)
        last, ref = 0, C // 2
    else:
        keep = same_chunk & (ci <= ri)
        last, ref = C - 1, C // 2 - 1
    tri = jnp.where(keep, 1.0, 0.0).astype(BF16)
    lane = lax.broadcasted_iota(jnp.int32, (1, 2 * DK_B), 1)
    h0 = jnp.where(lane < DK_B, 1.0, 0.0).astype(BF16)
    h1 = jnp.where(lane >= DK_B, 1.0, 0.0).astype(BF16)

    hi, mid, lo = _split3(g_ref[in_rows, qc])
    b3 = _dot(tri, jnp.concatenate([hi, mid, lo], axis=1))
    yield
    b = b3[:, 0:128] + b3[:, 128:256] + b3[:, 256:384]
    q = q_ref[in_rows, qc]
    k = k_ref[in_rows, qc]
    qt, kt, qe, kd = [], [], [], []
    zeros = jnp.zeros((C, 2 * DK_B), BF16)
    for c in range(cpb):
        s = slice(c * C, (c + 1) * C)
        bc = b[s]
        bm = bc[ref:ref + 1]
        bl = bc[last:last + 1]
        qt.append((q[s] * jnp.exp(bc - bm)).astype(BF16))
        kt.append((k[s] * jnp.exp(bm - bc)).astype(BF16))
        qe.append((q[s] * jnp.exp(bc)).astype(BF16))
        kdc = (k[s] * jnp.exp(bl - bc)).astype(BF16)
        kd.append(jnp.concatenate([kdc if j == c else zeros for j in range(cpb)], axis=1))
        dec_s[blk * cpb + c:blk * cpb + c + 1, :] = jnp.exp(bl)
    qe_s[rows, :] = jnp.concatenate(qe, axis=0)
    qt = jnp.concatenate(qt, axis=0)
    kt = jnp.concatenate(kt, axis=0)
    yield
    s0 = _dot_nt(qt * h0, kt)
    s1 = _dot_nt(qt * h1, kt)
    v = v_ref[in_rows, vc]
    if (in_rows.start, vc.start) not in vt_cache:
        vt_cache[in_rows.start, vc.start] = v.T
    ut = _dot(vt_cache[in_rows.start, vc.start], jnp.concatenate(kd, axis=0))
    yield
    a0 = jnp.where(keep, s0, 0.0).astype(BF16)
    a1 = jnp.where(keep, s1, 0.0).astype(BF16)
    br = lax.broadcasted_iota(jnp.int32, ut.shape, 0)
    bcol = lax.broadcasted_iota(jnp.int32, ut.shape, 1)
    ut = jnp.where((br < DV_B) == ((bcol & (2 * DK_B - 1)) < DK_B), ut, 0.0)
    for c in range(cpb):
        ut_s[blk * cpb + c] = ut[:, c * 2 * DK_B:(c + 1) * 2 * DK_B]
    yield
    oa_s[rows, :] = jnp.concatenate([_dot(a0, v[:, 0:DV_B]), _dot(a1, v[:, DV_B:2 * DV_B])], axis=1)


def _state_to_t(s_pair):
    z = jnp.zeros((DK_B, DV_B), F32)
    a = jnp.concatenate([s_pair[0], z], axis=0).T
    b = jnp.concatenate([z, s_pair[1]], axis=0).T
    return jnp.concatenate([a, b], axis=0)


def _gla_kernel(*refs, n_chunks, n_elems, has_state):
    if has_state:
        (x_ref, mod_ref, g_ref, w_hbm, wgl_ref, wg_ref, bg_ref, gn_ref, sf_ref, sb_ref, og_ref, *scratch) = refs
    else:
        (x_ref, mod_ref, g_ref, w_hbm, wgl_ref, wg_ref, bg_ref, gn_ref, og_ref, nsf_ref, nsb_ref, *scratch) = refs
    q_s, k_s, gf_s, gb_s, rb_s, v_s, oa, oi, st, qe, ut, dec, w_ref, stage, sem = scratch
    C = GLA_CHUNK
    R = ROW_CHUNK
    cpb = R // C
    L = n_chunks * C
    n_blk = L // R
    n_pairs = H_B // 2
    need, all_taken = _step0_weights(
        [(w_hbm, w_ref, C_QB + b0, 0, b0, 0, D_MODEL)
         for b0 in list(range(B_RB, B_GL, W_STAGE_ROWS)) + list(range(B_QB, B_RB, W_STAGE_ROWS))], stage, sem)

    def proj(r0):
        rows = slice(r0, r0 + R)
        h = _prenorm(x_ref, rows, mod_ref, g_ref)
        yield
        if r0 == 0:
            need((B_GL - B_RB) // W_STAGE_ROWS)
        pr = _dot_nt(h, w_ref[B_RB:B_GL, :])
        pgl = _dot_nt(h, wgl_ref[...])
        if r0 == 0:
            need((B_RB - B_QB) // W_STAGE_ROWS)
        pb = _dot_nt(h, w_ref[B_QB:B_RB, :])
        z = _dot(pgl.astype(BF16), wg_ref[...]) + bg_ref[...]
        ls = (jnp.minimum(z, 0.0) - jnp.log(1.0 + jnp.exp(-jnp.abs(z)))) * (1.0 / GATE_NORM)
        gf_s[rows, :] = ls[:, 0:256]
        gb_s[rows, :] = ls[:, 256:512]
        rb_s[rows, :] = pr
        q_s[rows, :] = pb[:, B_QB:B_KB] * (DK_B ** -0.5)
        k_s[rows, :] = pb[:, B_KB:B_VB]
        v_s[rows, :] = pb[:, B_VB:B_RB].astype(BF16)

    def scan_stage(e, p, d, blk):
        ch = 2 * p + d
        if blk != (n_blk - 1 if d else 0):
            s = st[e, ch]
        elif has_state:
            assert n_elems == 1
            s = _state_to_t((sb_ref if d else sf_ref)[2 * p:2 * p + 2])
        else:
            s = jnp.zeros(st.shape[2:], F32)
        for i in range(cpb):
            c = blk * cpb + (cpb - 1 - i if d else i)
            rows = slice(c * C, (c + 1) * C)
            oi[e, ch, rows, :] = _dot_nt(qe[e, ch, rows, :], s.astype(BF16))
            s = s * dec[e, ch, c:c + 1, :] + ut[e, ch, c]
            yield
        if blk != (0 if d else n_blk - 1):
            st[e, ch] = s
        elif not has_state:
            dst_ref = nsb_ref if d else nsf_ref
            dst_ref[e, 2 * p] = s[0:DV_B, :].T[0:DK_B, :]
            dst_ref[e, 2 * p + 1] = s[DV_B:2 * DV_B, :].T[DK_B:2 * DK_B, :]

    def fin_stage(e, p, blk):
        rows = slice(blk * R, (blk + 1) * R)
        out_rows = slice(e * L + blk * R, e * L + (blk + 1) * R)
        o = ((oa[e, 2 * p, rows, :] + oi[e, 2 * p, rows, :])
             + (oa[e, 2 * p + 1, rows, :] + oi[e, 2 * p + 1, rows, :]))
        for j in range(2):
            cols = slice((2 * p + j) * DV_B, (2 * p + j + 1) * DV_B)
            oj = _rms(o[:, j * DV_B:(j + 1) * DV_B], gn_ref[...])
            rj = rb_s[out_rows, cols]
            og_ref[out_rows, cols] = (oj * (rj * jax.nn.sigmoid(rj))).astype(BF16)
        yield

    vt_cache = {}
    projs, blocks, scans, deps = {}, [], {}, {}
    for e in range(n_elems):
        for t in range(n_blk):
            blk = t // 2 if t % 2 == 0 else n_blk - 1 - t // 2
            projs[e, blk] = proj(e * L + blk * R)
    for e in range(n_elems):
        for t in range(n_blk):
            for p in range(n_pairs):
                for d in range(2):
                    blk = n_blk - 1 - t if d else t
                    ch = 2 * p + d
                    in_rows = slice(e * L + blk * R, e * L + (blk + 1) * R)
                    g = _gla_block_stages(
                        q_s, k_s, gb_s if d else gf_s, v_s, in_rows,
                        slice(p * 2 * DK_B, (p + 1) * 2 * DK_B), slice(p * 2 * DV_B, (p + 1) * 2 * DV_B),
                        vt_cache, blk, bool(d), oa.at[e, ch], qe.at[e, ch], ut.at[e, ch], dec.at[e, ch])
                    deps[id(g)] = [projs[e, blk]]
                    blocks.append(g)
                    s = scan_stage(e, p, d, blk)
                    prev = scans.get((e, p, d, blk + 1 if d else blk - 1))
                    deps[id(s)] = [g] + ([prev] if prev is not None else [])
                    scans[e, p, d, blk] = s
    aux = []
    for e in range(n_elems):
        for t in range(n_blk):
            for p in range(n_pairs):
                aux.append(scans[e, p, 0, t])
                aux.append(scans[e, p, 1, n_blk - 1 - t])
        for blk in range(n_blk):
            for p in range(n_pairs):
                f = fin_stage(e, p, blk)
                deps[id(f)] = [scans[e, p, 0, blk], scans[e, p, 1, blk]]
                aux.append(f)
    _run_skewed(list(projs.values()) + blocks, aux, deps)
    assert all_taken()


def _gla(x2d, mods, g_pre, w_in_f32, w_gl, wg, bg, gla_norm, state_f, state_b, batch, seq_len):
    has_state = state_f is not None
    n_chunks = seq_len // GLA_CHUNK
    L = seq_len
    rows = MIX_TILE
    n_elems = rows // L
    n_ch = H_B
    wk, wv = H_B * DK_B, H_B * DV_B
    const = lambda i: (0, 0)
    in_specs = [
        pl.BlockSpec((rows, D_MODEL), lambda i: (i, 0)),
        pl.BlockSpec((None, N_MOD, D_MODEL), (lambda i: (i, 0, 0)) if has_state else (lambda i: (0, 0, 0))),
        pl.BlockSpec((1, D_MODEL), const),
        pl.BlockSpec(memory_space=pl.ANY),
        pl.BlockSpec((2 * GATE_RANK, D_MODEL), const),
        pl.BlockSpec((2 * GATE_RANK, 2 * wk), const),
        pl.BlockSpec((1, 2 * wk), const),
        pl.BlockSpec((1, DV_B), const),
    ]
    args = [x2d, mods, g_pre, w_in_f32, w_gl, wg, bg, gla_norm]
    out_shape = [jax.ShapeDtypeStruct((batch * L, wv), BF16)]
    out_specs = [pl.BlockSpec((rows, wv), lambda i: (i, 0))]
    if has_state:
        assert n_elems == 1 and mods.shape[0] == batch
        st_spec = pl.BlockSpec((None, None, H_B, DK_B, DV_B), lambda i: (i, 0, 0, 0, 0))
        in_specs += [st_spec, st_spec]
        args += [state_f, state_b]
    else:
        assert mods.shape[0] == 1
        for _ in range(2):
            out_shape.append(jax.ShapeDtypeStruct((batch, 1, H_B, DK_B, DV_B), F32))
            out_specs.append(pl.BlockSpec((n_elems, None, H_B, DK_B, DV_B), lambda i: (i, 0, 0, 0, 0)))
    return pl.pallas_call(
        functools.partial(_gla_kernel, n_chunks=n_chunks, n_elems=n_elems, has_state=has_state),
        out_shape=out_shape,
        grid=(batch // n_elems,),
        in_specs=in_specs,
        out_specs=out_specs,
        scratch_shapes=[
            pltpu.VMEM((rows, wk), F32),
            pltpu.VMEM((rows, wk), F32),
            pltpu.VMEM((rows, wk), F32),
            pltpu.VMEM((rows, wk), F32),
            pltpu.VMEM((rows, wv), F32),
            pltpu.VMEM((rows, wv), BF16),
            pltpu.VMEM((n_elems, n_ch, L, 2 * DV_B), F32),
            pltpu.VMEM((n_elems, n_ch, L, 2 * DV_B), F32),
            pltpu.VMEM((n_elems, n_ch, 2 * DV_B, 2 * DK_B), F32),
            pltpu.VMEM((n_elems, n_ch, L, 2 * DK_B), BF16),
            pltpu.VMEM((n_elems, n_ch, n_chunks, 2 * DV_B, 2 * DK_B), F32),
            pltpu.VMEM((n_elems, n_ch, max(n_chunks, 8), 2 * DK_B), F32),
            pltpu.VMEM((B_GL, D_MODEL), BF16),
            pltpu.VMEM((W_STAGE_SLOTS, W_STAGE_ROWS, D_MODEL), F32),
            pltpu.SemaphoreType.DMA((W_STAGE_SLOTS,)),
        ],
        compiler_params=pltpu.CompilerParams(
            dimension_semantics=("arbitrary",), vmem_limit_bytes=VMEM_LIMIT),
        name="gla_state" if has_state else "gla_ctx",
    )(*args)


def _out_mlp_kernel(xc_ref, oac_ref, ogc_ref, xl_ref, oal_ref, ogl_ref, mod_ref, wo_hbm, gpost_ref, gmpre_ref,
                    gmpost_ref, w1_hbm, w2_hbm, yc_ref, yl_ref, wo_ref, w1_ref, w2_ref, stage, sem, *, n_ctx_tiles):
    rc = ROW_CHUNK
    tf = 1024
    per_win = tf // W_STAGE_ROWS

    def weight_windows():
        windows = [(wo_hbm, wo_ref, r0, 0, r0, 0, D_MODEL) for r0 in range(0, D_MODEL, W_STAGE_ROWS)]
        for j in range(D_FF // tf):
            windows += [(w1_hbm, w1_ref, r0, j * tf, r0, j * tf, tf) for r0 in range(0, D_MODEL, W_STAGE_ROWS)]
            windows += [(w2_hbm, w2_ref, j * tf + r0, 0, j * tf + r0, 0, D_MODEL) for r0 in range(0, tf, W_STAGE_ROWS)]
        return windows

    def run(x_ref, oa_ref, og_ref, y_ref, take=None):
        def chunk(r0):
            need = take if (take is not None and r0 == 0) else (lambda k: None)
            rows = slice(r0, r0 + rc)
            need(D_MODEL // W_STAGE_ROWS)
            mix = _dot(oa_ref[rows, :], wo_ref[0:512, :]) + _dot(og_ref[rows, :], wo_ref[512:1024, :])
            yield
            x1 = x_ref[rows, :] + mod_ref[2:3, :] * _rms(mix, gpost_ref[...])
            h2 = (_rms(x1, gmpre_ref[...]) * (1.0 + mod_ref[4:5, :]) + mod_ref[3:4, :]).astype(BF16)
            yield
            acc = None
            for j in range(D_FF // tf):
                need(D_MODEL // W_STAGE_ROWS + per_win)
                u = jnp.maximum(_dot(h2, w1_ref[:, j * tf:(j + 1) * tf]), 0.0)
                part = _dot((u * u).astype(BF16), w2_ref[j * tf:(j + 1) * tf, :])
                acc = part if acc is None else acc + part
                yield
            y_ref[rows, :] = x1 + mod_ref[5:6, :] * _rms(acc, gmpost_ref[...])

        _run_skewed([chunk(r0) for r0 in range(0, x_ref.shape[0], rc)])

    step = pl.program_id(0)
    is_ctx = step < n_ctx_tiles

    @pl.when(step == 0)
    def _():
        take, all_taken = _weight_stream(weight_windows(), stage, sem)
        run(xc_ref, oac_ref, ogc_ref, yc_ref, take)
        assert all_taken()

    @pl.when(jnp.logical_and(step > 0, is_ctx))
    def _():
        run(xc_ref, oac_ref, ogc_ref, yc_ref)

    @pl.when(jnp.logical_not(is_ctx))
    def _():
        run(xl_ref, oal_ref, ogl_ref, yl_ref)


def _out_mlp(xc2d, oac, ogc, xl2d, oal, ogl, mods, lat_seq, w_out, g_post, g_mpre, g_mpost, w1, w2):
    tm = 512
    n_c, n_l = xc2d.shape[0] // tm, xl2d.shape[0] // tm
    per_b = lat_seq // tm
    ctx_row = lambda i: (jnp.minimum(i, n_c - 1), 0)
    lat_row = lambda i: (jnp.maximum(i - n_c, 0), 0)
    mod_idx = lambda i: (jnp.where(i < n_c, 0, 1 + jnp.maximum(i - n_c, 0) // per_b), 0, 0)
    const = lambda i: (0, 0)
    return pl.pallas_call(
        functools.partial(_out_mlp_kernel, n_ctx_tiles=n_c),
        out_shape=[jax.ShapeDtypeStruct(xc2d.shape, F32), jax.ShapeDtypeStruct(xl2d.shape, F32)],
        grid=(n_c + n_l,),
        in_specs=[
            pl.BlockSpec((tm, D_MODEL), ctx_row),
            pl.BlockSpec((tm, 512), ctx_row),
            pl.BlockSpec((tm, 512), ctx_row),
            pl.BlockSpec((tm, D_MODEL), lat_row),
            pl.BlockSpec((tm, 512), lat_row),
            pl.BlockSpec((tm, 512), lat_row),
            pl.BlockSpec((None, N_MOD, D_MODEL), mod_idx),
            pl.BlockSpec(memory_space=pl.ANY),
            pl.BlockSpec((1, D_MODEL), const),
            pl.BlockSpec((1, D_MODEL), const),
            pl.BlockSpec((1, D_MODEL), const),
            pl.BlockSpec(memory_space=pl.ANY),
            pl.BlockSpec(memory_space=pl.ANY),
        ],
        out_specs=[pl.BlockSpec((tm, D_MODEL), ctx_row), pl.BlockSpec((tm, D_MODEL), lat_row)],
        scratch_shapes=[
            pltpu.VMEM((D_MODEL, D_MODEL), BF16),
            pltpu.VMEM((D_MODEL, D_FF), BF16),
            pltpu.VMEM((D_FF, D_MODEL), BF16),
            pltpu.VMEM((W_STAGE_SLOTS, W_STAGE_ROWS, D_MODEL), F32),
            pltpu.SemaphoreType.DMA((W_STAGE_SLOTS,)),
        ],
        compiler_params=pltpu.CompilerParams(
            dimension_semantics=("arbitrary",), vmem_limit_bytes=VMEM_LIMIT),
        name="out_mlp",
    )(xc2d, oac, ogc, xl2d, oal, ogl, mods, w_out, g_post, g_mpre, g_mpost, w1, w2)


def _rope_tables(n_lat):
    pos = np.arange(n_lat)
    row_pos = (pos // GRID_W).astype(np.float64)
    col_pos = (pos % GRID_W).astype(np.float64)
    half = DK_A // 4
    inv = ROPE_BASE ** (-np.arange(half, dtype=np.float64) / half)
    lane = np.arange(128)
    in64 = lane % 64
    in32 = in64 % 32
    p = np.where((in64 < 32)[None, :], row_pos[:, None], col_pos[:, None])
    ang = p * inv[in32 % half][None, :]
    cos, sin = np.cos(ang), np.sin(ang)
    first = (in32 < half)[None, :]
    sa = np.where(first, -sin, 0.0)
    sb = np.where(first, 0.0, sin)
    return tuple(jnp.asarray(t, dtype=F32) for t in (cos, sa, sb))


def _mixers(x, mods, w, rope_tabs, cache_k, cache_v, state_f, state_b):
    batch, seq_len, _ = x.shape
    x2d = x.reshape(batch * seq_len, D_MODEL)
    aouts = _attention(x2d, mods, w["g_pre"], w["w_in"], rope_tabs, cache_k, cache_v, w["lam"], w["diff_norm"],
                       batch, seq_len)
    gouts = _gla(x2d, mods, w["g_pre"], w["w_in"], w["w_gl"], w["wg"], w["bg"], w["gla_norm"], state_f, state_b,
                 batch, seq_len)
    return x2d, aouts[0], gouts[0], tuple(aouts[1:]) + tuple(gouts[1:])


def kernel(x_prompt, x_sample, c, cache_k, cache_v, state_fwd, state_bwd, c_ctx, w_ada, b_ada,
           norm_attn_pre, norm_attn_post, norm_mlp_pre, norm_mlp_post, w_in, w_gate_fwd, b_gate_fwd,
           w_gate_bwd, b_gate_bwd, lam_q1, lam_k1, lam_q2, lam_k2, diff_norm, gla_norm, w_out,
           w_mlp1, w_mlp2):
    dec_batch = c.shape[0]
    rows = 16
    cvec = jnp.zeros((rows, D_MODEL), F32).at[0].set(c_ctx).at[1:1 + dec_batch].set(c)
    m = _adaln(cvec, w_ada[0], b_ada[0][None, :])
    mods_ctx = m[0:1].reshape(1, N_MOD, D_MODEL)
    mods_lat = m[1:1 + dec_batch].reshape(dec_batch, N_MOD, D_MODEL)

    wg = jnp.zeros((2 * GATE_RANK, 512), F32)
    wg = wg.at[0:GATE_RANK, 0:256].set(w_gate_fwd[0]).at[GATE_RANK:2 * GATE_RANK, 256:512].set(w_gate_bwd[0])
    w = {
        "g_pre": norm_attn_pre[0][None, :],
        "g_post": norm_attn_post[0][None, :],
        "g_mpre": norm_mlp_pre[0][None, :],
        "g_mpost": norm_mlp_post[0][None, :],
        "w_in": jnp.transpose(w_in[0]),
        "w_gl": jnp.transpose(w_in[0][:, C_GL:IN_COLS]).astype(BF16),
        "wg": wg.astype(BF16),
        "bg": jnp.concatenate([b_gate_fwd[0], b_gate_bwd[0]])[None, :],
        "lam": jnp.stack([lam_q1[0], lam_k1[0], lam_q2[0], lam_k2[0]]),
        "diff_norm": diff_norm[0][None, :],
        "gla_norm": gla_norm[0][None, :],
        "w_out": w_out[0],
        "w1": w_mlp1[0],
        "w2": w_mlp2[0],
    }
    xc2d, oac, ogc, (new_k, new_v, new_sf, new_sb) = _mixers(x_prompt, mods_ctx, w, None, None, None, None, None)
    xl2d, oal, ogl, _ = _mixers(x_sample, mods_lat, w, _rope_tables(x_sample.shape[1]),
                                cache_k, cache_v, state_fwd, state_bwd)
    mods_all = m[0:1 + dec_batch].reshape(1 + dec_batch, N_MOD, D_MODEL)
    y_prompt, y_sample = _out_mlp(xc2d, oac, ogc, xl2d, oal, ogl, mods_all, x_sample.shape[1],
                                  w["w_out"], w["g_post"], w["g_mpre"], w["g_mpost"], w["w1"], w["w2"])
    return (y_prompt.reshape(x_prompt.shape), y_sample.reshape(x_sample.shape), new_k, new_v, new_sf, new_sb)
```

```python
import functools
import math

import numpy as np
import jax
import jax.numpy as jnp
from jax import lax
from jax.experimental import pallas as pl
from jax.experimental.pallas import tpu as pltpu

F32 = jnp.float32
BF16 = jnp.bfloat16

D_MODEL = 1024
GRID_W = 64
H_A = 4
DV_A = 128
DK_A = 64
H_B = 4
DV_B = 128
DK_B = 64
GATE_RANK = 16
GATE_NORM = 16.0
GLA_CHUNK = 64
D_FF = 4 * D_MODEL
N_MOD = 6
ROPE_BASE = 10000.0
EPS = 1e-6
LAM_INIT = 0.8 - 0.6 * math.exp(-0.3 * 0)

C_QA, C_KA, C_VA, C_QB, C_KB, C_VB, C_RB, C_GL = 0, 512, 1024, 1536, 1792, 2048, 2560, 3072
IN_COLS = 3104
B_QB, B_KB, B_VB, B_RB, B_GL, B_COLS = (c - C_QB for c in (C_QB, C_KB, C_VB, C_RB, C_GL, IN_COLS))

V7X_VMEM_BYTES = 64 * 1024 * 1024
VMEM_LIMIT = V7X_VMEM_BYTES - 4 * 1024 * 1024
ROW_CHUNK = 256
MIX_TILE = 1024
W_STAGE_ROWS = 512
W_STAGE_SLOTS = 4


def _rms(x, g):
    return x * lax.rsqrt(jnp.mean(x * x, axis=-1, keepdims=True) + EPS) * g


def _dot(a, b):
    return jnp.dot(a, b, preferred_element_type=F32)


def _dot_nt(a, b):
    return lax.dot_general(a, b, (((1,), (1,)), ((), ())), preferred_element_type=F32)


def _run_skewed(gens, secondary=(), deps=None):
    deps = deps or {}
    queues, running, done = [list(gens), list(secondary)], [], set()
    while any(queues) or running:
        for queue in queues:
            for g in queue:
                if all(id(d) in done for d in deps.get(id(g), ())):
                    queue.remove(g)
                    running.append(g)
                    break
        assert running, "dependency cycle"
        for g in list(running):
            try:
                next(g)
            except StopIteration:
                running.remove(g)
                done.add(id(g))


def _mod(mod_ref, mrow, k):
    row = slice(mrow, mrow + 1) if isinstance(mrow, int) else pl.ds(mrow, 1)
    return mod_ref[row, k * D_MODEL:(k + 1) * D_MODEL]


def _prenorm(x_ref, rows, mod_ref, mrow, g_ref):
    x = x_ref[rows, :]
    return (_rms(x, g_ref[...]) * (1.0 + _mod(mod_ref, mrow, 1)) + _mod(mod_ref, mrow, 0)).astype(BF16)


def _weight_stream(windows, stage, sem):
    n_slots = stage.shape[0]

    def copy(n):
        src, _, r0, c0, _, _, width = windows[n]
        slot = n % n_slots
        return pltpu.make_async_copy(src.at[pl.ds(r0, W_STAGE_ROWS), pl.ds(c0, width)],
                                     stage.at[slot, pl.ds(0, W_STAGE_ROWS), pl.ds(0, width)], sem.at[slot])

    pos = {"started": 0, "taken": 0}

    def start_below(limit):
        while pos["started"] < min(limit, len(windows)):
            copy(pos["started"]).start()
            pos["started"] += 1

    def take(k):
        for _ in range(k):
            n = pos["taken"]
            _, dst, _, _, r0, d0, width = windows[n]
            copy(n).wait()
            dst[r0:r0 + W_STAGE_ROWS, d0:d0 + width] = stage[n % n_slots, :, 0:width].astype(BF16)
            pos["taken"] += 1
            start_below(n + n_slots + 1)

    start_below(n_slots)
    return take, lambda: pos["taken"] == len(windows)


def _step0_weights(windows, stage, sem):
    first = pl.program_id(0) == 0
    stream = {}

    @pl.when(first)
    def _():
        stream["take"], stream["all_taken"] = _weight_stream(windows, stage, sem)

    def need(k):
        @pl.when(first)
        def _():
            stream["take"](k)

    return need, lambda: stream["all_taken"]()


def _adaln_kernel(c_ref, w_ref, b_ref, o_ref):
    c = c_ref[...]
    s = c * jax.nn.sigmoid(c)
    o_ref[...] = _dot(s.astype(BF16), w_ref[...].astype(BF16)) + b_ref[...]


def _adaln(cvec, w_ada, b_ada):
    rows = cvec.shape[0]
    n = w_ada.shape[1]
    tn = 1536
    return pl.pallas_call(
        _adaln_kernel,
        out_shape=jax.ShapeDtypeStruct((rows, n), F32),
        grid=(n // tn,),
        in_specs=[
            pl.BlockSpec((rows, D_MODEL), lambda j: (0, 0)),
            pl.BlockSpec((D_MODEL, tn), lambda j: (0, j)),
            pl.BlockSpec((1, tn), lambda j: (0, j)),
        ],
        out_specs=pl.BlockSpec((rows, tn), lambda j: (0, j)),
        compiler_params=pltpu.CompilerParams(
            dimension_semantics=("arbitrary",), vmem_limit_bytes=VMEM_LIMIT),
        name="adaln",
    )(cvec, w_ada, b_ada)


def _attn_kernel(*refs, cached, seq):
    if cached:
        (x_ref, mod_ref, g_ref, w_hbm, cos_ref, sa_ref, sb_ref, ck_ref, cv_ref, lam_ref, dn_ref,
         o_ref, qkv_s, w_ref, stage, sem) = refs
    else:
        x_ref, mod_ref, g_ref, w_hbm, lam_ref, dn_ref, o_ref, nk_ref, nv_ref, qkv_s, w_ref, stage, sem = refs
        assert seq == ROW_CHUNK
    rc = ROW_CHUNK
    n_rows = x_ref.shape[0]
    mrow = (1 + pl.program_id(0)) if cached else 0
    need, all_taken = _step0_weights(
        [(w_hbm, w_ref, r0, 0, r0, 0, D_MODEL) for r0 in range(C_QA, C_QB, W_STAGE_ROWS)], stage, sem)
    lp = lam_ref[...]
    lam = (jnp.exp(jnp.sum(lp[0:1] * lp[1:2], axis=-1, keepdims=True))
           - jnp.exp(jnp.sum(lp[2:3] * lp[3:4], axis=-1, keepdims=True)) + LAM_INIT)
    lane = lax.broadcasted_iota(jnp.int32, (1, 128), 1)
    scale = DK_A ** -0.5
    m1 = jnp.where(lane < DK_A, scale, 0.0).astype(BF16)
    m2 = jnp.where(lane >= DK_A, scale, 0.0).astype(BF16)
    n_keys = (n_rows + ck_ref.shape[1]) if cached else rc
    ones = jnp.ones((n_keys, 128), BF16)

    def proj(r0):
        rows = slice(r0, r0 + rc)
        h = _prenorm(x_ref, rows, mod_ref, mrow, g_ref)
        yield
        if r0 == 0:
            need((C_VA - C_QA) // W_STAGE_ROWS)
        pa = _dot_nt(h, w_ref[C_QA:C_VA, :])
        if r0 == 0:
            need((C_QB - C_VA) // W_STAGE_ROWS)
        pv = _dot_nt(h, w_ref[C_VA:C_QB, :])
        for blk in range(8):
            t = pa[:, blk * 128:(blk + 1) * 128]
            if cached:
                t = (t * cos_ref[rows, :] + pltpu.roll(t, 112, 1) * sa_ref[rows, :]
                     + pltpu.roll(t, 16, 1) * sb_ref[rows, :])
            elif blk >= 4:
                nk_ref[r0 // seq, blk - 4] = t
            qkv_s[rows, blk * 128:(blk + 1) * 128] = t.astype(BF16)
        qkv_s[rows, C_VA:C_QB] = pv.astype(BF16)
        if not cached:
            for hh in range(H_A):
                nv_ref[r0 // seq, hh] = pv[:, hh * 128:(hh + 1) * 128]

    kv = {}
    res = {}

    def chain(r0, h, branch):
        rows = slice(r0, r0 + rc)
        cols = slice(h * 128, (h + 1) * 128)
        kcols = slice(C_KA + h * 128, C_KA + (h + 1) * 128)
        vcols = slice(C_VA + h * 128, C_VA + (h + 1) * 128)
        key = h if cached else (h, r0)
        if key not in kv:
            if cached:
                k = jnp.concatenate([ck_ref[h].astype(BF16), qkv_s[:, kcols]], axis=0)
                v = jnp.concatenate([cv_ref[h].astype(BF16), qkv_s[:, vcols]], axis=0)
            else:
                k, v = qkv_s[rows, kcols], qkv_s[rows, vcols]
            kv[key] = (k, jnp.concatenate([v, ones], axis=1))
        k, v1 = kv[key]
        s = _dot_nt(qkv_s[rows, cols] * (m2 if branch else m1), k)
        yield
        e = jnp.exp(s - jnp.max(s, axis=-1, keepdims=True)).astype(BF16)
        yield
        res[r0, h, branch] = _dot(e, v1)
        yield
        if branch:
            r1, r2 = res[r0, h, 0], res[r0, h, 1]
            o = r1[:, 0:128] / r1[:, 128:256] - lam * (r2[:, 0:128] / r2[:, 128:256])
            o_ref[rows, cols] = (_rms(o, dn_ref[...]) * (1.0 - LAM_INIT)).astype(BF16)

    projs = {r0: proj(r0) for r0 in range(0, n_rows, rc)}
    gens, deps = list(projs.values()), {}
    for r0 in projs:
        for h in range(H_A):
            for br in range(2):
                c = chain(r0, h, br)
                deps[id(c)] = list(projs.values()) if cached else [projs[r0]]
                gens.append(c)
    _run_skewed(gens, deps=deps)
    assert all_taken()


def _attention(x2d, mods, g_pre, w_in_f32, rope_tabs, cache_k, cache_v, lam_p, diff_norm, batch, seq_len):
    cached = cache_k is not None
    tq = MIX_TILE
    n_tiles = batch * seq_len // tq
    const = lambda i: (0, 0)
    in_specs = [
        pl.BlockSpec((tq, D_MODEL), lambda i: (i, 0)),
        pl.BlockSpec(mods.shape, const),
        pl.BlockSpec((1, D_MODEL), const),
        pl.BlockSpec(memory_space=pl.ANY),
    ]
    args = [x2d, mods, g_pre, w_in_f32]
    out_shape = [jax.ShapeDtypeStruct((batch * seq_len, H_A * DV_A), BF16)]
    out_specs = [pl.BlockSpec((tq, H_A * DV_A), lambda i: (i, 0))]
    if cached:
        assert seq_len == tq
        past = cache_k.shape[3]
        cspec = pl.BlockSpec((None, None, H_A, past, 128), lambda i: (i, 0, 0, 0, 0))
        in_specs += [pl.BlockSpec((tq, 128), const)] * 3 + [cspec, cspec]
        args += list(rope_tabs) + [cache_k, cache_v]
    else:
        nb = tq // seq_len
        for _ in range(2):
            out_shape.append(jax.ShapeDtypeStruct((batch, 1, H_A, seq_len, 128), F32))
            out_specs.append(pl.BlockSpec((nb, None, H_A, seq_len, 128), lambda i: (i, 0, 0, 0, 0)))
    in_specs += [pl.BlockSpec((4, DK_A), const), pl.BlockSpec((1, DV_A), const)]
    args += [lam_p, diff_norm]
    return pl.pallas_call(
        functools.partial(_attn_kernel, cached=cached, seq=seq_len),
        out_shape=out_shape,
        grid=(n_tiles,),
        in_specs=in_specs,
        out_specs=out_specs,
        scratch_shapes=[
            pltpu.VMEM((tq, C_QB), BF16),
            pltpu.VMEM((C_QB, D_MODEL), BF16),
            pltpu.VMEM((W_STAGE_SLOTS, W_STAGE_ROWS, D_MODEL), F32),
            pltpu.SemaphoreType.DMA((W_STAGE_SLOTS,)),
        ],
        compiler_params=pltpu.CompilerParams(
            dimension_semantics=("arbitrary",), vmem_limit_bytes=VMEM_LIMIT),
        name="attn_cached" if cached else "attn_ctx",
    )(*args)


def _split3(g):
    hi = g.astype(BF16)
    r1 = g - hi.astype(F32)
    mid = r1.astype(BF16)
    lo = (r1 - mid.astype(F32)).astype(BF16)
    return hi, mid, lo


def _gla_block_stages(q_ref, k_ref, g_ref, v_ref, in_rows, qc, vc, vt_cache, blk, reverse, oa_s, qe_s, ut_s, dec_s):
    C = GLA_CHUNK
    R = in_rows.stop - in_rows.start
    cpb = R // C
    rows = slice(blk * R, (blk + 1) * R)
    ri = lax.broadcasted_iota(jnp.int32, (R, R), 0)
    ci = lax.broadcasted_iota(jnp.int32, (R, R), 1)
    same_chunk = jnp.right_shift(ri, 6) == jnp.right_shift(ci, 6)
    if reverse:
        keep = same_chunk & (ci >= ri)
        last, ref = 0, C // 2
    else:
        keep = same_chunk & (ci <= ri)
        last, ref = C - 1, C // 2 - 1
    tri = jnp.where(keep, 1.0, 0.0).astype(BF16)
    lane = lax.broadcasted_iota(jnp.int32, (1, 2 * DK_B), 1)
    h0 = jnp.where(lane < DK_B, 1.0, 0.0).astype(BF16)
    h1 = jnp.where(lane >= DK_B, 1.0, 0.0).astype(BF16)

    hi, mid, lo = _split3(g_ref[in_rows, qc])
    b3 = _dot(tri, jnp.concatenate([hi, mid, lo], axis=1))
    yield
    b = b3[:, 0:128] + b3[:, 128:256] + b3[:, 256:384]
    q = q_ref[in_rows, qc]
    k = k_ref[in_rows, qc]
    qt, kt, qe, kd = [], [], [], []
    zeros = jnp.zeros((C, 2 * DK_B), BF16)
    for c in range(cpb):
        s = slice(c * C, (c + 1) * C)
        bc = b[s]
        bm = bc[ref:ref + 1]
        bl = bc[last:last + 1]
        qt.append((q[s] * jnp.exp(bc - bm)).astype(BF16))
        kt.append((k[s] * jnp.exp(bm - bc)).astype(BF16))
        qe.append((q[s] * jnp.exp(bc)).astype(BF16))
        kdc = (k[s] * jnp.exp(bl - bc)).astype(BF16)
        kd.append(jnp.concatenate([kdc if j == c else zeros for j in range(cpb)], axis=1))
        dec_s[blk * cpb + c:blk * cpb + c + 1, :] = jnp.exp(bl)
    qe_s[rows, :] = jnp.concatenate(qe, axis=0)
    qt = jnp.concatenate(qt, axis=0)
    kt = jnp.concatenate(kt, axis=0)
    yield
    s0 = _dot_nt(qt * h0, kt)
    s1 = _dot_nt(qt * h1, kt)
    v = v_ref[in_rows, vc]
    if (in_rows.start, vc.start) not in vt_cache:
        vt_cache[in_rows.start, vc.start] = v.T
    ut = _dot(vt_cache[in_rows.start, vc.start], jnp.concatenate(kd, axis=0))
    yield
    a0 = jnp.where(keep, s0, 0.0).astype(BF16)
    a1 = jnp.where(keep, s1, 0.0).astype(BF16)
    br = lax.broadcasted_iota(jnp.int32, ut.shape, 0)
    bcol = lax.broadcasted_iota(jnp.int32, ut.shape, 1)
    ut = jnp.where((br < DV_B) == ((bcol & (2 * DK_B - 1)) < DK_B), ut, 0.0)
    for c in range(cpb):
        ut_s[blk * cpb + c] = ut[:, c * 2 * DK_B:(c + 1) * 2 * DK_B]
    yield
    oa_s[rows, :] = jnp.concatenate([_dot(a0, v[:, 0:DV_B]), _dot(a1, v[:, DV_B:2 * DV_B])], axis=1)


def _state_to_t(s_pair):
    z = jnp.zeros((DK_B, DV_B), F32)
    a = jnp.concatenate([s_pair[0], z], axis=0).T
    b = jnp.concatenate([z, s_pair[1]], axis=0).T
    return jnp.concatenate([a, b], axis=0)


def _gla_kernel(*refs, n_chunks, n_elems, has_state):
    if has_state:
        (x_ref, mod_ref, g_ref, w_hbm, wgl_ref, wg_ref, bg_ref, gn_ref, sf_ref, sb_ref, og_ref, *scratch) = refs
    else:
        (x_ref, mod_ref, g_ref, w_hbm, wgl_ref, wg_ref, bg_ref, gn_ref, og_ref, nsf_ref, nsb_ref, *scratch) = refs
    q_s, k_s, gf_s, gb_s, rb_s, v_s, oa, oi, st, qe, ut, dec, w_ref, stage, sem = scratch
    C = GLA_CHUNK
    R = ROW_CHUNK
    cpb = R // C
    L = n_chunks * C
    n_blk = L // R
    n_pairs = H_B // 2
    mrow = (1 + pl.program_id(0)) if has_state else 0
    need, all_taken = _step0_weights(
        [(w_hbm, w_ref, C_QB + b0, 0, b0, 0, D_MODEL)
         for b0 in list(range(B_RB, B_GL, W_STAGE_ROWS)) + list(range(B_QB, B_RB, W_STAGE_ROWS))], stage, sem)

    def proj(r0):
        rows = slice(r0, r0 + R)
        h = _prenorm(x_ref, rows, mod_ref, mrow, g_ref)
        yield
        if r0 == 0:
            need((B_GL - B_RB) // W_STAGE_ROWS)
        pr = _dot_nt(h, w_ref[B_RB:B_GL, :])
        pgl = _dot_nt(h, wgl_ref[...])
        if r0 == 0:
            need((B_RB - B_QB) // W_STAGE_ROWS)
        pb = _dot_nt(h, w_ref[B_QB:B_RB, :])
        z = _dot(pgl.astype(BF16), wg_ref[...]) + bg_ref[...]
        ls = (jnp.minimum(z, 0.0) - jnp.log(1.0 + jnp.exp(-jnp.abs(z)))) * (1.0 / GATE_NORM)
        gf_s[rows, :] = ls[:, 0:256]
        gb_s[rows, :] = ls[:, 256:512]
        rb_s[rows, :] = pr
        q_s[rows, :] = pb[:, B_QB:B_KB] * (DK_B ** -0.5)
        k_s[rows, :] = pb[:, B_KB:B_VB]
        v_s[rows, :] = pb[:, B_VB:B_RB].astype(BF16)

    def scan_stage(e, p, d, blk):
        ch = 2 * p + d
        if blk != (n_blk - 1 if d else 0):
            s = st[e, ch]
        elif has_state:
            assert n_elems == 1
            s = _state_to_t((sb_ref if d else sf_ref)[2 * p:2 * p + 2])
        else:
            s = jnp.zeros(st.shape[2:], F32)
        for i in range(cpb):
            c = blk * cpb + (cpb - 1 - i if d else i)
            rows = slice(c * C, (c + 1) * C)
            oi[e, ch, rows, :] = _dot_nt(qe[e, ch, rows, :], s.astype(BF16))
            s = s * dec[e, ch, c:c + 1, :] + ut[e, ch, c]
            yield
        if blk != (0 if d else n_blk - 1):
            st[e, ch] = s
        elif not has_state:
            dst_ref = nsb_ref if d else nsf_ref
            dst_ref[e, 2 * p] = s[0:DV_B, :].T[0:DK_B, :]
            dst_ref[e, 2 * p + 1] = s[DV_B:2 * DV_B, :].T[DK_B:2 * DK_B, :]

    def fin_stage(e, p, blk):
        rows = slice(blk * R, (blk + 1) * R)
        out_rows = slice(e * L + blk * R, e * L + (blk + 1) * R)
        o = ((oa[e, 2 * p, rows, :] + oi[e, 2 * p, rows, :])
             + (oa[e, 2 * p + 1, rows, :] + oi[e, 2 * p + 1, rows, :]))
        for j in range(2):
            cols = slice((2 * p + j) * DV_B, (2 * p + j + 1) * DV_B)
            oj = _rms(o[:, j * DV_B:(j + 1) * DV_B], gn_ref[...])
            rj = rb_s[out_rows, cols]
            og_ref[out_rows, cols] = (oj * (rj * jax.nn.sigmoid(rj))).astype(BF16)
        yield

    vt_cache = {}
    projs, blocks, scans, deps = {}, [], {}, {}
    for e in range(n_elems):
        for t in range(n_blk):
            blk = t // 2 if t % 2 == 0 else n_blk - 1 - t // 2
            projs[e, blk] = proj(e * L + blk * R)
    for e in range(n_elems):
        for t in range(n_blk):
            for p in range(n_pairs):
                for d in range(2):
                    blk = n_blk - 1 - t if d else t
                    ch = 2 * p + d
                    in_rows = slice(e * L + blk * R, e * L + (blk + 1) * R)
                    g = _gla_block_stages(
                        q_s, k_s, gb_s if d else gf_s, v_s, in_rows,
                        slice(p * 2 * DK_B, (p + 1) * 2 * DK_B), slice(p * 2 * DV_B, (p + 1) * 2 * DV_B),
                        vt_cache, blk, bool(d), oa.at[e, ch], qe.at[e, ch], ut.at[e, ch], dec.at[e, ch])
                    deps[id(g)] = [projs[e, blk]]
                    blocks.append(g)
                    s = scan_stage(e, p, d, blk)
                    prev = scans.get((e, p, d, blk + 1 if d else blk - 1))
                    deps[id(s)] = [g] + ([prev] if prev is not None else [])
                    scans[e, p, d, blk] = s
    aux = []
    for e in range(n_elems):
        for t in range(n_blk):
            for p in range(n_pairs):
                aux.append(scans[e, p, 0, t])
                aux.append(scans[e, p, 1, n_blk - 1 - t])
        for blk in range(n_blk):
            for p in range(n_pairs):
                f = fin_stage(e, p, blk)
                deps[id(f)] = [scans[e, p, 0, blk], scans[e, p, 1, blk]]
                aux.append(f)
    _run_skewed(list(projs.values()) + blocks, aux, deps)
    assert all_taken()


def _gla(x2d, mods, g_pre, w_in_f32, w_gl, wg, bg, gla_norm, state_f, state_b, batch, seq_len):
    has_state = state_f is not None
    n_chunks = seq_len // GLA_CHUNK
    L = seq_len
    rows = MIX_TILE
    n_elems = rows // L
    n_ch = H_B
    wk, wv = H_B * DK_B, H_B * DV_B
    const = lambda i: (0, 0)
    in_specs = [
        pl.BlockSpec((rows, D_MODEL), lambda i: (i, 0)),
        pl.BlockSpec(mods.shape, const),
        pl.BlockSpec((1, D_MODEL), const),
        pl.BlockSpec(memory_space=pl.ANY),
        pl.BlockSpec((2 * GATE_RANK, D_MODEL), const),
        pl.BlockSpec((2 * GATE_RANK, 2 * wk), const),
        pl.BlockSpec((1, 2 * wk), const),
        pl.BlockSpec((1, DV_B), const),
    ]
    args = [x2d, mods, g_pre, w_in_f32, w_gl, wg, bg, gla_norm]
    out_shape = [jax.ShapeDtypeStruct((batch * L, wv), BF16)]
    out_specs = [pl.BlockSpec((rows, wv), lambda i: (i, 0))]
    if has_state:
        assert n_elems == 1
        st_spec = pl.BlockSpec((None, None, H_B, DK_B, DV_B), lambda i: (i, 0, 0, 0, 0))
        in_specs += [st_spec, st_spec]
        args += [state_f, state_b]
    else:
        for _ in range(2):
            out_shape.append(jax.ShapeDtypeStruct((batch, 1, H_B, DK_B, DV_B), F32))
            out_specs.append(pl.BlockSpec((n_elems, None, H_B, DK_B, DV_B), lambda i: (i, 0, 0, 0, 0)))
    return pl.pallas_call(
        functools.partial(_gla_kernel, n_chunks=n_chunks, n_elems=n_elems, has_state=has_state),
        out_shape=out_shape,
        grid=(batch // n_elems,),
        in_specs=in_specs,
        out_specs=out_specs,
        scratch_shapes=[
            pltpu.VMEM((rows, wk), F32),
            pltpu.VMEM((rows, wk), F32),
            pltpu.VMEM((rows, wk), F32),
            pltpu.VMEM((rows, wk), F32),
            pltpu.VMEM((rows, wv), F32),
            pltpu.VMEM((rows, wv), BF16),
            pltpu.VMEM((n_elems, n_ch, L, 2 * DV_B), F32),
            pltpu.VMEM((n_elems, n_ch, L, 2 * DV_B), F32),
            pltpu.VMEM((n_elems, n_ch, 2 * DV_B, 2 * DK_B), F32),
            pltpu.VMEM((n_elems, n_ch, L, 2 * DK_B), BF16),
            pltpu.VMEM((n_elems, n_ch, n_chunks, 2 * DV_B, 2 * DK_B), F32),
            pltpu.VMEM((n_elems, n_ch, max(n_chunks, 8), 2 * DK_B), F32),
            pltpu.VMEM((B_GL, D_MODEL), BF16),
            pltpu.VMEM((W_STAGE_SLOTS, W_STAGE_ROWS, D_MODEL), F32),
            pltpu.SemaphoreType.DMA((W_STAGE_SLOTS,)),
        ],
        compiler_params=pltpu.CompilerParams(
            dimension_semantics=("arbitrary",), vmem_limit_bytes=VMEM_LIMIT),
        name="gla_state" if has_state else "gla_ctx",
    )(*args)


def _out_mlp_kernel(xc_ref, oac_ref, ogc_ref, xl_ref, oal_ref, ogl_ref, mod_ref, wo_hbm, gpost_ref, gmpre_ref,
                    gmpost_ref, w1_hbm, w2_hbm, yc_ref, yl_ref, wo_ref, w1_ref, w2_ref, stage, sem, *,
                    n_ctx_tiles, lat_tiles_per_sample):
    rc = ROW_CHUNK
    tf = 1024
    per_win = tf // W_STAGE_ROWS

    def weight_windows():
        windows = [(wo_hbm, wo_ref, r0, 0, r0, 0, D_MODEL) for r0 in range(0, D_MODEL, W_STAGE_ROWS)]
        for j in range(D_FF // tf):
            windows += [(w1_hbm, w1_ref, r0, j * tf, r0, j * tf, tf) for r0 in range(0, D_MODEL, W_STAGE_ROWS)]
            windows += [(w2_hbm, w2_ref, j * tf + r0, 0, j * tf + r0, 0, D_MODEL) for r0 in range(0, tf, W_STAGE_ROWS)]
        return windows

    def run(x_ref, oa_ref, og_ref, y_ref, mrow, take=None):
        def chunk(r0):
            need = take if (take is not None and r0 == 0) else (lambda k: None)
            rows = slice(r0, r0 + rc)
            need(D_MODEL // W_STAGE_ROWS)
            mix = _dot(oa_ref[rows, :], wo_ref[0:512, :]) + _dot(og_ref[rows, :], wo_ref[512:1024, :])
            yield
            x1 = x_ref[rows, :] + _mod(mod_ref, mrow, 2) * _rms(mix, gpost_ref[...])
            h2 = (_rms(x1, gmpre_ref[...]) * (1.0 + _mod(mod_ref, mrow, 4)) + _mod(mod_ref, mrow, 3)).astype(BF16)
            yield
            acc = None
            for j in range(D_FF // tf):
                need(D_MODEL // W_STAGE_ROWS + per_win)
                u = jnp.maximum(_dot(h2, w1_ref[:, j * tf:(j + 1) * tf]), 0.0)
                part = _dot((u * u).astype(BF16), w2_ref[j * tf:(j + 1) * tf, :])
                acc = part if acc is None else acc + part
                yield
            y_ref[rows, :] = x1 + _mod(mod_ref, mrow, 5) * _rms(acc, gmpost_ref[...])

        _run_skewed([chunk(r0) for r0 in range(0, x_ref.shape[0], rc)])

    step = pl.program_id(0)
    is_ctx = step < n_ctx_tiles

    @pl.when(step == 0)
    def _():
        take, all_taken = _weight_stream(weight_windows(), stage, sem)
        run(xc_ref, oac_ref, ogc_ref, yc_ref, 0, take)
        assert all_taken()

    @pl.when(jnp.logical_and(step > 0, is_ctx))
    def _():
        run(xc_ref, oac_ref, ogc_ref, yc_ref, 0)

    @pl.when(jnp.logical_not(is_ctx))
    def _():
        run(xl_ref, oal_ref, ogl_ref, yl_ref, 1 + (step - n_ctx_tiles) // lat_tiles_per_sample)


def _out_mlp(xc2d, oac, ogc, xl2d, oal, ogl, mods, lat_seq, w_out, g_post, g_mpre, g_mpost, w1, w2):
    tm = 512
    n_c, n_l = xc2d.shape[0] // tm, xl2d.shape[0] // tm
    per_b = lat_seq // tm
    ctx_row = lambda i: (jnp.minimum(i, n_c - 1), 0)
    lat_row = lambda i: (jnp.maximum(i - n_c, 0), 0)
    const = lambda i: (0, 0)
    return pl.pallas_call(
        functools.partial(_out_mlp_kernel, n_ctx_tiles=n_c, lat_tiles_per_sample=per_b),
        out_shape=[jax.ShapeDtypeStruct(xc2d.shape, F32), jax.ShapeDtypeStruct(xl2d.shape, F32)],
        grid=(n_c + n_l,),
        in_specs=[
            pl.BlockSpec((tm, D_MODEL), ctx_row),
            pl.BlockSpec((tm, 512), ctx_row),
            pl.BlockSpec((tm, 512), ctx_row),
            pl.BlockSpec((tm, D_MODEL), lat_row),
            pl.BlockSpec((tm, 512), lat_row),
            pl.BlockSpec((tm, 512), lat_row),
            pl.BlockSpec(mods.shape, const),
            pl.BlockSpec(memory_space=pl.ANY),
            pl.BlockSpec((1, D_MODEL), const),
            pl.BlockSpec((1, D_MODEL), const),
            pl.BlockSpec((1, D_MODEL), const),
            pl.BlockSpec(memory_space=pl.ANY),
            pl.BlockSpec(memory_space=pl.ANY),
        ],
        out_specs=[pl.BlockSpec((tm, D_MODEL), ctx_row), pl.BlockSpec((tm, D_MODEL), lat_row)],
        scratch_shapes=[
            pltpu.VMEM((D_MODEL, D_MODEL), BF16),
            pltpu.VMEM((D_MODEL, D_FF), BF16),
            pltpu.VMEM((D_FF, D_MODEL), BF16),
            pltpu.VMEM((W_STAGE_SLOTS, W_STAGE_ROWS, D_MODEL), F32),
            pltpu.SemaphoreType.DMA((W_STAGE_SLOTS,)),
        ],
        compiler_params=pltpu.CompilerParams(
            dimension_semantics=("arbitrary",), vmem_limit_bytes=VMEM_LIMIT),
        name="out_mlp",
    )(xc2d, oac, ogc, xl2d, oal, ogl, mods, w_out, g_post, g_mpre, g_mpost, w1, w2)


def _rope_tables(n_lat):
    pos = np.arange(n_lat)
    row_pos = (pos // GRID_W).astype(np.float64)
    col_pos = (pos % GRID_W).astype(np.float64)
    half = DK_A // 4
    inv = ROPE_BASE ** (-np.arange(half, dtype=np.float64) / half)
    lane = np.arange(128)
    in64 = lane % 64
    in32 = in64 % 32
    p = np.where((in64 < 32)[None, :], row_pos[:, None], col_pos[:, None])
    ang = p * inv[in32 % half][None, :]
    cos, sin = np.cos(ang), np.sin(ang)
    first = (in32 < half)[None, :]
    sa = np.where(first, -sin, 0.0)
    sb = np.where(first, 0.0, sin)
    return tuple(jnp.asarray(t, dtype=F32) for t in (cos, sa, sb))


def _mixers(x, mods, w, rope_tabs, cache_k, cache_v, state_f, state_b):
    batch, seq_len, _ = x.shape
    x2d = x.reshape(batch * seq_len, D_MODEL)
    aouts = _attention(x2d, mods, w["g_pre"], w["w_in"], rope_tabs, cache_k, cache_v, w["lam"], w["diff_norm"],
                       batch, seq_len)
    gouts = _gla(x2d, mods, w["g_pre"], w["w_in"], w["w_gl"], w["wg"], w["bg"], w["gla_norm"], state_f, state_b,
                 batch, seq_len)
    return x2d, aouts[0], gouts[0], tuple(aouts[1:]) + tuple(gouts[1:])


def kernel(x_prompt, x_sample, c, cache_k, cache_v, state_fwd, state_bwd, c_ctx, w_ada, b_ada,
           norm_attn_pre, norm_attn_post, norm_mlp_pre, norm_mlp_post, w_in, w_gate_fwd, b_gate_fwd,
           w_gate_bwd, b_gate_bwd, lam_q1, lam_k1, lam_q2, lam_k2, diff_norm, gla_norm, w_out,
           w_mlp1, w_mlp2):
    dec_batch = c.shape[0]
    rows = 16
    cvec = jnp.concatenate([c_ctx[None, :], c, jnp.zeros((rows - 1 - dec_batch, D_MODEL), F32)], axis=0)
    m = _adaln(cvec, w_ada[0], b_ada[0][None, :])

    wg = jnp.zeros((2 * GATE_RANK, 512), F32)
    wg = wg.at[0:GATE_RANK, 0:256].set(w_gate_fwd[0]).at[GATE_RANK:2 * GATE_RANK, 256:512].set(w_gate_bwd[0])
    w = {
        "g_pre": norm_attn_pre[0][None, :],
        "g_post": norm_attn_post[0][None, :],
        "g_mpre": norm_mlp_pre[0][None, :],
        "g_mpost": norm_mlp_post[0][None, :],
        "w_in": jnp.transpose(w_in[0]),
        "w_gl": jnp.transpose(w_in[0][:, C_GL:IN_COLS]).astype(BF16),
        "wg": wg.astype(BF16),
        "bg": jnp.concatenate([b_gate_fwd[0], b_gate_bwd[0]])[None, :],
        "lam": jnp.stack([lam_q1[0], lam_k1[0], lam_q2[0], lam_k2[0]]),
        "diff_norm": diff_norm[0][None, :],
        "gla_norm": gla_norm[0][None, :],
        "w_out": w_out[0],
        "w1": w_mlp1[0],
        "w2": w_mlp2[0],
    }
    xc2d, oac, ogc, (new_k, new_v, new_sf, new_sb) = _mixers(x_prompt, m, w, None, None, None, None, None)
    xl2d, oal, ogl, _ = _mixers(x_sample, m, w, _rope_tables(x_sample.shape[1]),
                                cache_k, cache_v, state_fwd, state_bwd)
    y_prompt, y_sample = _out_mlp(xc2d, oac, ogc, xl2d, oal, ogl, m, x_sample.shape[1],
                                  w["w_out"], w["g_post"], w["g_mpre"], w["g_mpost"], w["w1"], w["w2"])
    return (y_prompt.reshape(x_prompt.shape), y_sample.reshape(x_sample.shape), new_k, new_v, new_sf, new_sb)
```

```python
import functools
import math

import numpy as np
import jax
import jax.numpy as jnp
from jax import lax
from jax.experimental import pallas as pl
from jax.experimental.pallas import tpu as pltpu

F32 = jnp.float32
BF16 = jnp.bfloat16

D_MODEL = 1024
GRID_W = 64
H_A = 4
DV_A = 128
DK_A = 64
H_B = 4
DV_B = 128
DK_B = 64
GATE_RANK = 16
GATE_NORM = 16.0
GLA_CHUNK = 64
D_FF = 4 * D_MODEL
N_MOD = 6
ROPE_BASE = 10000.0
EPS = 1e-6
LAM_INIT = 0.8 - 0.6 * math.exp(-0.3 * 0)

C_QA, C_KA, C_VA, C_QB, C_KB, C_VB, C_RB, C_GL = 0, 512, 1024, 1536, 1792, 2048, 2560, 3072
IN_COLS = 3104
B_QB, B_KB, B_VB, B_RB, B_GL, B_COLS = (c - C_QB for c in (C_QB, C_KB, C_VB, C_RB, C_GL, IN_COLS))

V7X_VMEM_BYTES = 64 * 1024 * 1024
VMEM_LIMIT = V7X_VMEM_BYTES - 4 * 1024 * 1024
ROW_CHUNK = 256
MIX_TILE = 1024
W_STAGE_ROWS = 512
W_STAGE_SLOTS = 4


def _rms(x, g):
    return x * lax.rsqrt(jnp.mean(x * x, axis=-1, keepdims=True) + EPS) * g


def _dot(a, b):
    return jnp.dot(a, b, preferred_element_type=F32)


def _dot_nt(a, b):
    return lax.dot_general(a, b, (((1,), (1,)), ((), ())), preferred_element_type=F32)


def _run_skewed(gens, secondary=(), deps=None):
    deps = deps or {}
    queues, running, done = [list(gens), list(secondary)], [], set()
    while any(queues) or running:
        for queue in queues:
            for g in queue:
                if all(id(d) in done for d in deps.get(id(g), ())):
                    queue.remove(g)
                    running.append(g)
                    break
        assert running, "dependency cycle"
        for g in list(running):
            try:
                next(g)
            except StopIteration:
                running.remove(g)
                done.add(id(g))


def _mod(mod_ref, mrow, k):
    row = slice(mrow, mrow + 1) if isinstance(mrow, int) else pl.ds(mrow, 1)
    return mod_ref[row, k * D_MODEL:(k + 1) * D_MODEL]


def _prenorm(x_ref, rows, mod_ref, mrow, g_ref):
    x = x_ref[rows, :]
    return (_rms(x, g_ref[...]) * (1.0 + _mod(mod_ref, mrow, 1)) + _mod(mod_ref, mrow, 0)).astype(BF16)


def _weight_stream(windows, stage, sem):
    n_slots = stage.shape[0]

    def copy(n):
        src, _, r0, c0, _, _, width = windows[n]
        slot = n % n_slots
        return pltpu.make_async_copy(src.at[pl.ds(r0, W_STAGE_ROWS), pl.ds(c0, width)],
                                     stage.at[slot, pl.ds(0, W_STAGE_ROWS), pl.ds(0, width)], sem.at[slot])

    pos = {"started": 0, "taken": 0}

    def start_below(limit):
        while pos["started"] < min(limit, len(windows)):
            copy(pos["started"]).start()
            pos["started"] += 1

    def take(k):
        for _ in range(k):
            n = pos["taken"]
            _, dst, _, _, r0, d0, width = windows[n]
            copy(n).wait()
            dst[r0:r0 + W_STAGE_ROWS, d0:d0 + width] = stage[n % n_slots, :, 0:width].astype(BF16)
            pos["taken"] += 1
            start_below(n + n_slots + 1)

    start_below(n_slots)
    return take, lambda: pos["taken"] == len(windows)


def _step0_weights(windows, stage, sem):
    first = pl.program_id(0) == 0
    stream = {}

    @pl.when(first)
    def _():
        stream["take"], stream["all_taken"] = _weight_stream(windows, stage, sem)

    def need(k):
        @pl.when(first)
        def _():
            stream["take"](k)

    return need, lambda: stream["all_taken"]()


def _adaln_kernel(c_ref, w_ref, b_ref, o_ref):
    c = c_ref[...]
    s = c * jax.nn.sigmoid(c)
    o_ref[...] = _dot(s.astype(BF16), w_ref[...].astype(BF16)) + b_ref[...]


def _adaln(cvec, w_ada, b_ada):
    rows = cvec.shape[0]
    n = w_ada.shape[1]
    tn = 1536
    return pl.pallas_call(
        _adaln_kernel,
        out_shape=jax.ShapeDtypeStruct((rows, n), F32),
        grid=(n // tn,),
        in_specs=[
            pl.BlockSpec((rows, D_MODEL), lambda j: (0, 0)),
            pl.BlockSpec((D_MODEL, tn), lambda j: (0, j)),
            pl.BlockSpec((1, tn), lambda j: (0, j)),
        ],
        out_specs=pl.BlockSpec((rows, tn), lambda j: (0, j)),
        compiler_params=pltpu.CompilerParams(
            dimension_semantics=("arbitrary",), vmem_limit_bytes=VMEM_LIMIT),
        name="adaln",
    )(cvec, w_ada, b_ada)


def _attn_kernel(*refs, cached, seq):
    if cached:
        (x_ref, mod_ref, g_ref, w_hbm, cos_ref, sa_ref, sb_ref, ck_ref, cv_ref, lam_ref, dn_ref,
         o_ref, qkv_s, w_ref, stage, sem) = refs
    else:
        x_ref, mod_ref, g_ref, w_hbm, lam_ref, dn_ref, o_ref, nk_ref, nv_ref, qkv_s, w_ref, stage, sem = refs
        assert seq == ROW_CHUNK
    rc = ROW_CHUNK
    n_rows = x_ref.shape[0]
    mrow = (1 + pl.program_id(0)) if cached else 0
    need, all_taken = _step0_weights(
        [(w_hbm, w_ref, r0, 0, r0, 0, D_MODEL) for r0 in range(C_QA, C_QB, W_STAGE_ROWS)], stage, sem)
    lp = lam_ref[...]
    lam = (jnp.exp(jnp.sum(lp[0:1] * lp[1:2], axis=-1, keepdims=True))
           - jnp.exp(jnp.sum(lp[2:3] * lp[3:4], axis=-1, keepdims=True)) + LAM_INIT)
    lane = lax.broadcasted_iota(jnp.int32, (1, 128), 1)
    scale = DK_A ** -0.5
    m1 = jnp.where(lane < DK_A, scale, 0.0).astype(BF16)
    m2 = jnp.where(lane >= DK_A, scale, 0.0).astype(BF16)
    n_keys = (n_rows + ck_ref.shape[1]) if cached else rc
    ones = jnp.ones((n_keys, 128), BF16)

    def proj(r0):
        rows = slice(r0, r0 + rc)
        h = _prenorm(x_ref, rows, mod_ref, mrow, g_ref)
        yield
        if r0 == 0:
            need((C_VA - C_QA) // W_STAGE_ROWS)
        pa = _dot_nt(h, w_ref[C_QA:C_VA, :])
        if r0 == 0:
            need((C_QB - C_VA) // W_STAGE_ROWS)
        pv = _dot_nt(h, w_ref[C_VA:C_QB, :])
        for blk in range(8):
            t = pa[:, blk * 128:(blk + 1) * 128]
            if cached:
                t = (t * cos_ref[rows, :] + pltpu.roll(t, 112, 1) * sa_ref[rows, :]
                     + pltpu.roll(t, 16, 1) * sb_ref[rows, :])
            elif blk >= 4:
                nk_ref[r0 // seq, blk - 4] = t
            qkv_s[rows, blk * 128:(blk + 1) * 128] = t.astype(BF16)
        qkv_s[rows, C_VA:C_QB] = pv.astype(BF16)
        if not cached:
            for hh in range(H_A):
                nv_ref[r0 // seq, hh] = pv[:, hh * 128:(hh + 1) * 128]

    kv = {}
    res = {}

    def chain(r0, h, branch):
        rows = slice(r0, r0 + rc)
        cols = slice(h * 128, (h + 1) * 128)
        kcols = slice(C_KA + h * 128, C_KA + (h + 1) * 128)
        vcols = slice(C_VA + h * 128, C_VA + (h + 1) * 128)
        key = h if cached else (h, r0)
        if key not in kv:
            if cached:
                k = jnp.concatenate([ck_ref[h].astype(BF16), qkv_s[:, kcols]], axis=0)
                v = jnp.concatenate([cv_ref[h].astype(BF16), qkv_s[:, vcols]], axis=0)
            else:
                k, v = qkv_s[rows, kcols], qkv_s[rows, vcols]
            kv[key] = (k, jnp.concatenate([v, ones], axis=1))
        k, v1 = kv[key]
        s = _dot_nt(qkv_s[rows, cols] * (m2 if branch else m1), k)
        yield
        e = jnp.exp(s - jnp.max(s, axis=-1, keepdims=True)).astype(BF16)
        yield
        res[r0, h, branch] = _dot(e, v1)
        yield
        if branch:
            r1, r2 = res[r0, h, 0], res[r0, h, 1]
            o = r1[:, 0:128] / r1[:, 128:256] - lam * (r2[:, 0:128] / r2[:, 128:256])
            o_ref[rows, cols] = (_rms(o, dn_ref[...]) * (1.0 - LAM_INIT)).astype(BF16)

    projs = {r0: proj(r0) for r0 in range(0, n_rows, rc)}
    gens, deps = list(projs.values()), {}
    for r0 in projs:
        for h in range(H_A):
            for br in range(2):
                c = chain(r0, h, br)
                deps[id(c)] = list(projs.values()) if cached else [projs[r0]]
                gens.append(c)
    _run_skewed(gens, deps=deps)
    assert all_taken()


def _attention(x2d, mods, g_pre, w_in_f32, rope_tabs, cache_k, cache_v, lam_p, diff_norm, batch, seq_len):
    cached = cache_k is not None
    tq = MIX_TILE
    n_tiles = batch * seq_len // tq
    const = lambda i: (0, 0)
    in_specs = [
        pl.BlockSpec((tq, D_MODEL), lambda i: (i, 0)),
        pl.BlockSpec(mods.shape, const),
        pl.BlockSpec((1, D_MODEL), const),
        pl.BlockSpec(memory_space=pl.ANY),
    ]
    args = [x2d, mods, g_pre, w_in_f32]
    out_shape = [jax.ShapeDtypeStruct((batch * seq_len, H_A * DV_A), BF16)]
    out_specs = [pl.BlockSpec((tq, H_A * DV_A), lambda i: (i, 0))]
    if cached:
        assert seq_len == tq
        past = cache_k.shape[3]
        cspec = pl.BlockSpec((None, None, H_A, past, 128), lambda i: (i, 0, 0, 0, 0))
        in_specs += [pl.BlockSpec((tq, 128), const)] * 3 + [cspec, cspec]
        args += list(rope_tabs) + [cache_k, cache_v]
    else:
        nb = tq // seq_len
        for _ in range(2):
            out_shape.append(jax.ShapeDtypeStruct((batch, 1, H_A, seq_len, 128), F32))
            out_specs.append(pl.BlockSpec((nb, None, H_A, seq_len, 128), lambda i: (i, 0, 0, 0, 0)))
    in_specs += [pl.BlockSpec((4, DK_A), const), pl.BlockSpec((1, DV_A), const)]
    args += [lam_p, diff_norm]
    return pl.pallas_call(
        functools.partial(_attn_kernel, cached=cached, seq=seq_len),
        out_shape=out_shape,
        grid=(n_tiles,),
        in_specs=in_specs,
        out_specs=out_specs,
        scratch_shapes=[
            pltpu.VMEM((tq, C_QB), BF16),
            pltpu.VMEM((C_QB, D_MODEL), BF16),
            pltpu.VMEM((W_STAGE_SLOTS, W_STAGE_ROWS, D_MODEL), F32),
            pltpu.SemaphoreType.DMA((W_STAGE_SLOTS,)),
        ],
        compiler_params=pltpu.CompilerParams(
            dimension_semantics=("arbitrary",), vmem_limit_bytes=VMEM_LIMIT),
        name="attn_cached" if cached else "attn_ctx",
    )(*args)


def _split3(g):
    hi = g.astype(BF16)
    r1 = g - hi.astype(F32)
    mid = r1.astype(BF16)
    lo = (r1 - mid.astype(F32)).astype(BF16)
    return hi, mid, lo


def _gla_block_stages(q_ref, k_ref, g_ref, v_ref, in_rows, qc, vc, vt_cache, blk, reverse, oa_s, qe_s, ut_s, dec_s):
    C = GLA_CHUNK
    R = in_rows.stop - in_rows.start
    cpb = R // C
    rows = slice(blk * R, (blk + 1) * R)
    ri = lax.broadcasted_iota(jnp.int32, (R, R), 0)
    ci = lax.broadcasted_iota(jnp.int32, (R, R), 1)
    same_chunk = jnp.right_shift(ri, 6) == jnp.right_shift(ci, 6)
    if reverse:
        keep = same_chunk & (ci >= ri)
        last, ref = 0, C // 2
    else:
        keep = same_chunk & (ci <= ri)
        last, ref = C - 1, C // 2 - 1
    tri = jnp.where(keep, 1.0, 0.0).astype(BF16)
    lane = lax.broadcasted_iota(jnp.int32, (1, 2 * DK_B), 1)
    h0 = jnp.where(lane < DK_B, 1.0, 0.0).astype(BF16)
    h1 = jnp.where(lane >= DK_B, 1.0, 0.0).astype(BF16)

    hi, mid, lo = _split3(g_ref[in_rows, qc])
    b3 = _dot(tri, jnp.concatenate([hi, mid, lo], axis=1))
    yield
    b = b3[:, 0:128] + b3[:, 128:256] + b3[:, 256:384]
    q = q_ref[in_rows, qc]
    k = k_ref[in_rows, qc]
    qt, kt, qe, kd = [], [], [], []
    zeros = jnp.zeros((C, 2 * DK_B), BF16)
    for c in range(cpb):
        s = slice(c * C, (c + 1) * C)
        bc = b[s]
        bm = bc[ref:ref + 1]
        bl = bc[last:last + 1]
        qt.append((q[s] * jnp.exp(bc - bm)).astype(BF16))
        kt.append((k[s] * jnp.exp(bm - bc)).astype(BF16))
        qe.append((q[s] * jnp.exp(bc)).astype(BF16))
        kdc = (k[s] * jnp.exp(bl - bc)).astype(BF16)
        kd.append(jnp.concatenate([kdc if j == c else zeros for j in range(cpb)], axis=1))
        dec_s[blk * cpb + c:blk * cpb + c + 1, :] = jnp.exp(bl)
    qe_s[rows, :] = jnp.concatenate(qe, axis=0)
    qt = jnp.concatenate(qt, axis=0)
    kt = jnp.concatenate(kt, axis=0)
    yield
    s0 = _dot_nt(qt * h0, kt)
    s1 = _dot_nt(qt * h1, kt)
    v = v_ref[in_rows, vc]
    if (in_rows.start, vc.start) not in vt_cache:
        vt_cache[in_rows.start, vc.start] = v.T
    ut = _dot(vt_cache[in_rows.start, vc.start], jnp.concatenate(kd, axis=0))
    yield
    a0 = jnp.where(keep, s0, 0.0).astype(BF16)
    a1 = jnp.where(keep, s1, 0.0).astype(BF16)
    br = lax.broadcasted_iota(jnp.int32, ut.shape, 0)
    bcol = lax.broadcasted_iota(jnp.int32, ut.shape, 1)
    ut = jnp.where((br < DV_B) == ((bcol & (2 * DK_B - 1)) < DK_B), ut, 0.0)
    for c in range(cpb):
        ut_s[blk * cpb + c] = ut[:, c * 2 * DK_B:(c + 1) * 2 * DK_B]
    yield
    oa_s[rows, :] = jnp.concatenate([_dot(a0, v[:, 0:DV_B]), _dot(a1, v[:, DV_B:2 * DV_B])], axis=1)


def _state_to_t(s_pair):
    z = jnp.zeros((DK_B, DV_B), F32)
    a = jnp.concatenate([s_pair[0], z], axis=0).T
    b = jnp.concatenate([z, s_pair[1]], axis=0).T
    return jnp.concatenate([a, b], axis=0)


def _gla_kernel(*refs, n_chunks, n_elems, has_state):
    if has_state:
        (x_ref, mod_ref, g_ref, w_hbm, wgl_ref, wgf_ref, wgb_ref, bgf_ref, bgb_ref, gn_ref, sf_ref, sb_ref, og_ref,
         *scratch) = refs
    else:
        (x_ref, mod_ref, g_ref, w_hbm, wgl_ref, wgf_ref, wgb_ref, bgf_ref, bgb_ref, gn_ref, og_ref, nsf_ref, nsb_ref,
         *scratch) = refs
    q_s, k_s, gf_s, gb_s, rb_s, v_s, oa, oi, st, qe, ut, dec, w_ref, stage, sem = scratch
    C = GLA_CHUNK
    R = ROW_CHUNK
    cpb = R // C
    L = n_chunks * C
    n_blk = L // R
    n_pairs = H_B // 2
    mrow = (1 + pl.program_id(0)) if has_state else 0
    need, all_taken = _step0_weights(
        [(w_hbm, w_ref, C_QB + b0, 0, b0, 0, D_MODEL)
         for b0 in list(range(B_RB, B_GL, W_STAGE_ROWS)) + list(range(B_QB, B_RB, W_STAGE_ROWS))], stage, sem)

    def proj(r0):
        rows = slice(r0, r0 + R)
        h = _prenorm(x_ref, rows, mod_ref, mrow, g_ref)
        yield
        if r0 == 0:
            need((B_GL - B_RB) // W_STAGE_ROWS)
        pr = _dot_nt(h, w_ref[B_RB:B_GL, :])
        pgl = _dot_nt(h, wgl_ref[...])
        if r0 == 0:
            need((B_RB - B_QB) // W_STAGE_ROWS)
        pb = _dot_nt(h, w_ref[B_QB:B_RB, :])
        zero = jnp.zeros(wgf_ref.shape, F32)
        wg = jnp.concatenate([jnp.concatenate([wgf_ref[...], zero], axis=1),
                              jnp.concatenate([zero, wgb_ref[...]], axis=1)], axis=0).astype(BF16)
        z = _dot(pgl.astype(BF16), wg) + jnp.concatenate([bgf_ref[...], bgb_ref[...]], axis=1)
        ls = (jnp.minimum(z, 0.0) - jnp.log(1.0 + jnp.exp(-jnp.abs(z)))) * (1.0 / GATE_NORM)
        gf_s[rows, :] = ls[:, 0:256]
        gb_s[rows, :] = ls[:, 256:512]
        rb_s[rows, :] = pr
        q_s[rows, :] = pb[:, B_QB:B_KB] * (DK_B ** -0.5)
        k_s[rows, :] = pb[:, B_KB:B_VB]
        v_s[rows, :] = pb[:, B_VB:B_RB].astype(BF16)

    def scan_stage(e, p, d, blk):
        ch = 2 * p + d
        if blk != (n_blk - 1 if d else 0):
            s = st[e, ch]
        elif has_state:
            assert n_elems == 1
            s = _state_to_t((sb_ref if d else sf_ref)[2 * p:2 * p + 2])
        else:
            s = jnp.zeros(st.shape[2:], F32)
        for i in range(cpb):
            c = blk * cpb + (cpb - 1 - i if d else i)
            rows = slice(c * C, (c + 1) * C)
            oi[e, ch, rows, :] = _dot_nt(qe[e, ch, rows, :], s.astype(BF16))
            s = s * dec[e, ch, c:c + 1, :] + ut[e, ch, c]
            yield
        if blk != (0 if d else n_blk - 1):
            st[e, ch] = s
        elif not has_state:
            dst_ref = nsb_ref if d else nsf_ref
            dst_ref[e, 2 * p] = s[0:DV_B, :].T[0:DK_B, :]
            dst_ref[e, 2 * p + 1] = s[DV_B:2 * DV_B, :].T[DK_B:2 * DK_B, :]

    def fin_stage(e, p, blk):
        rows = slice(blk * R, (blk + 1) * R)
        out_rows = slice(e * L + blk * R, e * L + (blk + 1) * R)
        o = ((oa[e, 2 * p, rows, :] + oi[e, 2 * p, rows, :])
             + (oa[e, 2 * p + 1, rows, :] + oi[e, 2 * p + 1, rows, :]))
        for j in range(2):
            cols = slice((2 * p + j) * DV_B, (2 * p + j + 1) * DV_B)
            oj = _rms(o[:, j * DV_B:(j + 1) * DV_B], gn_ref[...])
            rj = rb_s[out_rows, cols]
            og_ref[out_rows, cols] = (oj * (rj * jax.nn.sigmoid(rj))).astype(BF16)
        yield

    vt_cache = {}
    projs, blocks, scans, deps = {}, [], {}, {}
    for e in range(n_elems):
        for t in range(n_blk):
            blk = t // 2 if t % 2 == 0 else n_blk - 1 - t // 2
            projs[e, blk] = proj(e * L + blk * R)
    for e in range(n_elems):
        for t in range(n_blk):
            for p in range(n_pairs):
                for d in range(2):
                    blk = n_blk - 1 - t if d else t
                    ch = 2 * p + d
                    in_rows = slice(e * L + blk * R, e * L + (blk + 1) * R)
                    g = _gla_block_stages(
                        q_s, k_s, gb_s if d else gf_s, v_s, in_rows,
                        slice(p * 2 * DK_B, (p + 1) * 2 * DK_B), slice(p * 2 * DV_B, (p + 1) * 2 * DV_B),
                        vt_cache, blk, bool(d), oa.at[e, ch], qe.at[e, ch], ut.at[e, ch], dec.at[e, ch])
                    deps[id(g)] = [projs[e, blk]]
                    blocks.append(g)
                    s = scan_stage(e, p, d, blk)
                    prev = scans.get((e, p, d, blk + 1 if d else blk - 1))
                    deps[id(s)] = [g] + ([prev] if prev is not None else [])
                    scans[e, p, d, blk] = s
    aux = []
    for e in range(n_elems):
        for t in range(n_blk):
            for p in range(n_pairs):
                aux.append(scans[e, p, 0, t])
                aux.append(scans[e, p, 1, n_blk - 1 - t])
        for blk in range(n_blk):
            for p in range(n_pairs):
                f = fin_stage(e, p, blk)
                deps[id(f)] = [scans[e, p, 0, blk], scans[e, p, 1, blk]]
                aux.append(f)
    _run_skewed(list(projs.values()) + blocks, aux, deps)
    assert all_taken()


def _gla(x2d, mods, g_pre, w_in_f32, w_gl, wgf, wgb, bgf, bgb, gla_norm, state_f, state_b, batch, seq_len):
    has_state = state_f is not None
    n_chunks = seq_len // GLA_CHUNK
    L = seq_len
    rows = MIX_TILE
    n_elems = rows // L
    n_ch = H_B
    wk, wv = H_B * DK_B, H_B * DV_B
    const = lambda i: (0, 0)
    in_specs = [
        pl.BlockSpec((rows, D_MODEL), lambda i: (i, 0)),
        pl.BlockSpec(mods.shape, const),
        pl.BlockSpec((1, D_MODEL), const),
        pl.BlockSpec(memory_space=pl.ANY),
        pl.BlockSpec((2 * GATE_RANK, D_MODEL), const),
        pl.BlockSpec((GATE_RANK, wk), const),
        pl.BlockSpec((GATE_RANK, wk), const),
        pl.BlockSpec((1, wk), const),
        pl.BlockSpec((1, wk), const),
        pl.BlockSpec((1, DV_B), const),
    ]
    args = [x2d, mods, g_pre, w_in_f32, w_gl, wgf, wgb, bgf, bgb, gla_norm]
    out_shape = [jax.ShapeDtypeStruct((batch * L, wv), BF16)]
    out_specs = [pl.BlockSpec((rows, wv), lambda i: (i, 0))]
    if has_state:
        assert n_elems == 1
        st_spec = pl.BlockSpec((None, None, H_B, DK_B, DV_B), lambda i: (i, 0, 0, 0, 0))
        in_specs += [st_spec, st_spec]
        args += [state_f, state_b]
    else:
        for _ in range(2):
            out_shape.append(jax.ShapeDtypeStruct((batch, 1, H_B, DK_B, DV_B), F32))
            out_specs.append(pl.BlockSpec((n_elems, None, H_B, DK_B, DV_B), lambda i: (i, 0, 0, 0, 0)))
    return pl.pallas_call(
        functools.partial(_gla_kernel, n_chunks=n_chunks, n_elems=n_elems, has_state=has_state),
        out_shape=out_shape,
        grid=(batch // n_elems,),
        in_specs=in_specs,
        out_specs=out_specs,
        scratch_shapes=[
            pltpu.VMEM((rows, wk), F32),
            pltpu.VMEM((rows, wk), F32),
            pltpu.VMEM((rows, wk), F32),
            pltpu.VMEM((rows, wk), F32),
            pltpu.VMEM((rows, wv), F32),
            pltpu.VMEM((rows, wv), BF16),
            pltpu.VMEM((n_elems, n_ch, L, 2 * DV_B), F32),
            pltpu.VMEM((n_elems, n_ch, L, 2 * DV_B), F32),
            pltpu.VMEM((n_elems, n_ch, 2 * DV_B, 2 * DK_B), F32),
            pltpu.VMEM((n_elems, n_ch, L, 2 * DK_B), BF16),
            pltpu.VMEM((n_elems, n_ch, n_chunks, 2 * DV_B, 2 * DK_B), F32),
            pltpu.VMEM((n_elems, n_ch, max(n_chunks, 8), 2 * DK_B), F32),
            pltpu.VMEM((B_GL, D_MODEL), BF16),
            pltpu.VMEM((W_STAGE_SLOTS, W_STAGE_ROWS, D_MODEL), F32),
            pltpu.SemaphoreType.DMA((W_STAGE_SLOTS,)),
        ],
        compiler_params=pltpu.CompilerParams(
            dimension_semantics=("arbitrary",), vmem_limit_bytes=VMEM_LIMIT),
        name="gla_state" if has_state else "gla_ctx",
    )(*args)


def _out_mlp_kernel(xc_ref, oac_ref, ogc_ref, xl_ref, oal_ref, ogl_ref, mod_ref, wo_hbm, gpost_ref, gmpre_ref,
                    gmpost_ref, w1_hbm, w2_hbm, yc_ref, yl_ref, wo_ref, w1_ref, w2_ref, stage, sem, *,
                    n_ctx_tiles, lat_tiles_per_sample):
    rc = ROW_CHUNK
    tf = 1024
    per_win = tf // W_STAGE_ROWS

    def weight_windows():
        windows = [(wo_hbm, wo_ref, r0, 0, r0, 0, D_MODEL) for r0 in range(0, D_MODEL, W_STAGE_ROWS)]
        for j in range(D_FF // tf):
            windows += [(w1_hbm, w1_ref, r0, j * tf, r0, j * tf, tf) for r0 in range(0, D_MODEL, W_STAGE_ROWS)]
            windows += [(w2_hbm, w2_ref, j * tf + r0, 0, j * tf + r0, 0, D_MODEL) for r0 in range(0, tf, W_STAGE_ROWS)]
        return windows

    def run(x_ref, oa_ref, og_ref, y_ref, mrow, take=None):
        def chunk(r0):
            need = take if (take is not None and r0 == 0) else (lambda k: None)
            rows = slice(r0, r0 + rc)
            need(D_MODEL // W_STAGE_ROWS)
            mix = _dot(oa_ref[rows, :], wo_ref[0:512, :]) + _dot(og_ref[rows, :], wo_ref[512:1024, :])
            yield
            x1 = x_ref[rows, :] + _mod(mod_ref, mrow, 2) * _rms(mix, gpost_ref[...])
            h2 = (_rms(x1, gmpre_ref[...]) * (1.0 + _mod(mod_ref, mrow, 4)) + _mod(mod_ref, mrow, 3)).astype(BF16)
            yield
            acc = None
            for j in range(D_FF // tf):
                need(D_MODEL // W_STAGE_ROWS + per_win)
                u = jnp.maximum(_dot(h2, w1_ref[:, j * tf:(j + 1) * tf]), 0.0)
                part = _dot((u * u).astype(BF16), w2_ref[j * tf:(j + 1) * tf, :])
                acc = part if acc is None else acc + part
                yield
            y_ref[rows, :] = x1 + _mod(mod_ref, mrow, 5) * _rms(acc, gmpost_ref[...])

        _run_skewed([chunk(r0) for r0 in range(0, x_ref.shape[0], rc)])

    step = pl.program_id(0)
    is_ctx = step < n_ctx_tiles

    @pl.when(step == 0)
    def _():
        take, all_taken = _weight_stream(weight_windows(), stage, sem)
        run(xc_ref, oac_ref, ogc_ref, yc_ref, 0, take)
        assert all_taken()

    @pl.when(jnp.logical_and(step > 0, is_ctx))
    def _():
        run(xc_ref, oac_ref, ogc_ref, yc_ref, 0)

    @pl.when(jnp.logical_not(is_ctx))
    def _():
        run(xl_ref, oal_ref, ogl_ref, yl_ref, 1 + (step - n_ctx_tiles) // lat_tiles_per_sample)


def _out_mlp(xc2d, oac, ogc, xl2d, oal, ogl, mods, lat_seq, w_out, g_post, g_mpre, g_mpost, w1, w2):
    tm = 512
    n_c, n_l = xc2d.shape[0] // tm, xl2d.shape[0] // tm
    per_b = lat_seq // tm
    ctx_row = lambda i: (jnp.minimum(i, n_c - 1), 0)
    lat_row = lambda i: (jnp.maximum(i - n_c, 0), 0)
    const = lambda i: (0, 0)
    return pl.pallas_call(
        functools.partial(_out_mlp_kernel, n_ctx_tiles=n_c, lat_tiles_per_sample=per_b),
        out_shape=[jax.ShapeDtypeStruct(xc2d.shape, F32), jax.ShapeDtypeStruct(xl2d.shape, F32)],
        grid=(n_c + n_l,),
        in_specs=[
            pl.BlockSpec((tm, D_MODEL), ctx_row),
            pl.BlockSpec((tm, 512), ctx_row),
            pl.BlockSpec((tm, 512), ctx_row),
            pl.BlockSpec((tm, D_MODEL), lat_row),
            pl.BlockSpec((tm, 512), lat_row),
            pl.BlockSpec((tm, 512), lat_row),
            pl.BlockSpec(mods.shape, const),
            pl.BlockSpec(memory_space=pl.ANY),
            pl.BlockSpec((1, D_MODEL), const),
            pl.BlockSpec((1, D_MODEL), const),
            pl.BlockSpec((1, D_MODEL), const),
            pl.BlockSpec(memory_space=pl.ANY),
            pl.BlockSpec(memory_space=pl.ANY),
        ],
        out_specs=[pl.BlockSpec((tm, D_MODEL), ctx_row), pl.BlockSpec((tm, D_MODEL), lat_row)],
        scratch_shapes=[
            pltpu.VMEM((D_MODEL, D_MODEL), BF16),
            pltpu.VMEM((D_MODEL, D_FF), BF16),
            pltpu.VMEM((D_FF, D_MODEL), BF16),
            pltpu.VMEM((W_STAGE_SLOTS, W_STAGE_ROWS, D_MODEL), F32),
            pltpu.SemaphoreType.DMA((W_STAGE_SLOTS,)),
        ],
        compiler_params=pltpu.CompilerParams(
            dimension_semantics=("arbitrary",), vmem_limit_bytes=VMEM_LIMIT),
        name="out_mlp",
    )(xc2d, oac, ogc, xl2d, oal, ogl, mods, w_out, g_post, g_mpre, g_mpost, w1, w2)


def _rope_tables(n_lat):
    pos = np.arange(n_lat)
    row_pos = (pos // GRID_W).astype(np.float64)
    col_pos = (pos % GRID_W).astype(np.float64)
    half = DK_A // 4
    inv = ROPE_BASE ** (-np.arange(half, dtype=np.float64) / half)
    lane = np.arange(128)
    in64 = lane % 64
    in32 = in64 % 32
    p = np.where((in64 < 32)[None, :], row_pos[:, None], col_pos[:, None])
    ang = p * inv[in32 % half][None, :]
    cos, sin = np.cos(ang), np.sin(ang)
    first = (in32 < half)[None, :]
    sa = np.where(first, -sin, 0.0)
    sb = np.where(first, 0.0, sin)
    return tuple(jnp.asarray(t, dtype=F32) for t in (cos, sa, sb))


def _mixers(x, mods, w, rope_tabs, cache_k, cache_v, state_f, state_b):
    batch, seq_len, _ = x.shape
    x2d = x.reshape(batch * seq_len, D_MODEL)
    aouts = _attention(x2d, mods, w["g_pre"], w["w_in"], rope_tabs, cache_k, cache_v, w["lam"], w["diff_norm"],
                       batch, seq_len)
    gouts = _gla(x2d, mods, w["g_pre"], w["w_in"], w["w_gl"], w["wgf"], w["wgb"], w["bgf"], w["bgb"], w["gla_norm"],
                 state_f, state_b, batch, seq_len)
    return x2d, aouts[0], gouts[0], tuple(aouts[1:]) + tuple(gouts[1:])


def kernel(x_prompt, x_sample, c, cache_k, cache_v, state_fwd, state_bwd, c_ctx, w_ada, b_ada,
           norm_attn_pre, norm_attn_post, norm_mlp_pre, norm_mlp_post, w_in, w_gate_fwd, b_gate_fwd,
           w_gate_bwd, b_gate_bwd, lam_q1, lam_k1, lam_q2, lam_k2, diff_norm, gla_norm, w_out,
           w_mlp1, w_mlp2):
    dec_batch = c.shape[0]
    rows = 16
    cvec = jnp.concatenate([c_ctx[None, :], c, jnp.zeros((rows - 1 - dec_batch, D_MODEL), F32)], axis=0)
    m = _adaln(cvec, w_ada[0], b_ada[0][None, :])

    w = {
        "g_pre": norm_attn_pre[0][None, :],
        "g_post": norm_attn_post[0][None, :],
        "g_mpre": norm_mlp_pre[0][None, :],
        "g_mpost": norm_mlp_post[0][None, :],
        "w_in": jnp.transpose(w_in[0]),
        "w_gl": jnp.transpose(w_in[0][:, C_GL:IN_COLS]).astype(BF16),
        "wgf": w_gate_fwd[0],
        "wgb": w_gate_bwd[0],
        "bgf": b_gate_fwd,
        "bgb": b_gate_bwd,
        "lam": jnp.stack([lam_q1[0], lam_k1[0], lam_q2[0], lam_k2[0]]),
        "diff_norm": diff_norm[0][None, :],
        "gla_norm": gla_norm[0][None, :],
        "w_out": w_out[0],
        "w1": w_mlp1[0],
        "w2": w_mlp2[0],
    }
    xc2d, oac, ogc, (new_k, new_v, new_sf, new_sb) = _mixers(x_prompt, m, w, None, None, None, None, None)
    xl2d, oal, ogl, _ = _mixers(x_sample, m, w, _rope_tables(x_sample.shape[1]),
                                cache_k, cache_v, state_fwd, state_bwd)
    y_prompt, y_sample = _out_mlp(xc2d, oac, ogc, xl2d, oal, ogl, m, x_sample.shape[1],
                                  w["w_out"], w["g_post"], w["g_mpre"], w["g_mpost"], w["w1"], w["w2"])
    return (y_prompt.reshape(x_prompt.shape), y_sample.reshape(x_sample.shape), new_k, new_v, new_sf, new_sb)
```

```python
import functools
import math

import numpy as np
import jax
import jax.numpy as jnp
from jax import lax
from jax.experimental import pallas as pl
from jax.experimental.pallas import tpu as pltpu

F32 = jnp.float32
BF16 = jnp.bfloat16

D_MODEL = 1024
GRID_W = 64
H_A = 4
DV_A = 128
DK_A = 64
H_B = 4
DV_B = 128
DK_B = 64
GATE_RANK = 16
GATE_NORM = 16.0
GLA_CHUNK = 64
D_FF = 4 * D_MODEL
N_MOD = 6
ROPE_BASE = 10000.0
EPS = 1e-6
LAM_INIT = 0.8 - 0.6 * math.exp(-0.3 * 0)

C_QA, C_KA, C_VA, C_QB, C_KB, C_VB, C_RB, C_GL = 0, 512, 1024, 1536, 1792, 2048, 2560, 3072
IN_COLS = 3104
B_QB, B_KB, B_VB, B_RB, B_GL, B_COLS = (c - C_QB for c in (C_QB, C_KB, C_VB, C_RB, C_GL, IN_COLS))

V7X_VMEM_BYTES = 64 * 1024 * 1024
VMEM_LIMIT = V7X_VMEM_BYTES - 4 * 1024 * 1024
ROW_CHUNK = 256
MIX_TILE = 1024
W_STAGE_ROWS = 512
W_STAGE_SLOTS = 4


def _rms(x, g):
    return x * lax.rsqrt(jnp.mean(x * x, axis=-1, keepdims=True) + EPS) * g


def _dot(a, b):
    return jnp.dot(a, b, preferred_element_type=F32)


def _dot_nt(a, b):
    return lax.dot_general(a, b, (((1,), (1,)), ((), ())), preferred_element_type=F32)


def _run_skewed(gens, secondary=(), deps=None):
    deps = deps or {}
    queues, running, done = [list(gens), list(secondary)], [], set()
    while any(queues) or running:
        for queue in queues:
            for g in queue:
                if all(id(d) in done for d in deps.get(id(g), ())):
                    queue.remove(g)
                    running.append(g)
                    break
        assert running, "dependency cycle"
        for g in list(running):
            try:
                next(g)
            except StopIteration:
                running.remove(g)
                done.add(id(g))


def _mod(mod_ref, mrow, k):
    row = slice(mrow, mrow + 1) if isinstance(mrow, int) else pl.ds(mrow, 1)
    return mod_ref[row, k * D_MODEL:(k + 1) * D_MODEL]


def _prenorm(x_ref, rows, mod_ref, mrow, g_ref):
    x = x_ref[rows, :]
    return (_rms(x, g_ref[...]) * (1.0 + _mod(mod_ref, mrow, 1)) + _mod(mod_ref, mrow, 0)).astype(BF16)


def _weight_stream(windows, stage, sem):
    n_slots = stage.shape[0]

    def copy(n):
        src, _, r0, c0, _, _, n_rows, width = windows[n]
        slot = n % n_slots
        return pltpu.make_async_copy(src.at[pl.ds(r0, n_rows), pl.ds(c0, width)],
                                     stage.at[slot, pl.ds(0, n_rows), pl.ds(0, width)], sem.at[slot])

    pos = {"started": 0, "taken": 0}

    def start_below(limit):
        while pos["started"] < min(limit, len(windows)):
            copy(pos["started"]).start()
            pos["started"] += 1

    def take(k):
        for _ in range(k):
            n = pos["taken"]
            _, dst, _, _, r0, d0, n_rows, width = windows[n]
            copy(n).wait()
            dst[r0:r0 + n_rows, d0:d0 + width] = stage[n % n_slots, 0:n_rows, 0:width].astype(BF16)
            pos["taken"] += 1
            start_below(n + n_slots + 1)

    start_below(n_slots)
    return take, lambda: pos["taken"] == len(windows)


def _step0_weights(windows, stage, sem):
    first = pl.program_id(0) == 0
    stream = {}

    @pl.when(first)
    def _():
        stream["take"], stream["all_taken"] = _weight_stream(windows, stage, sem)

    def need(k):
        @pl.when(first)
        def _():
            stream["take"](k)

    return need, lambda: stream["all_taken"]()


def _adaln_kernel(c_ref, w_ref, b_ref, o_ref):
    c = c_ref[...]
    s = c * jax.nn.sigmoid(c)
    o_ref[...] = _dot(s.astype(BF16), w_ref[...].astype(BF16)) + b_ref[...]


def _adaln(cvec, w_ada, b_ada):
    rows = cvec.shape[0]
    n = w_ada.shape[1]
    tn = 1536
    return pl.pallas_call(
        _adaln_kernel,
        out_shape=jax.ShapeDtypeStruct((rows, n), F32),
        grid=(n // tn,),
        in_specs=[
            pl.BlockSpec((rows, D_MODEL), lambda j: (0, 0)),
            pl.BlockSpec((D_MODEL, tn), lambda j: (0, j)),
            pl.BlockSpec((1, tn), lambda j: (0, j)),
        ],
        out_specs=pl.BlockSpec((rows, tn), lambda j: (0, j)),
        compiler_params=pltpu.CompilerParams(
            dimension_semantics=("arbitrary",), vmem_limit_bytes=VMEM_LIMIT),
        name="adaln",
    )(cvec, w_ada, b_ada)


def _attn_kernel(*refs, cached, seq):
    if cached:
        (x_ref, mod_ref, g_ref, w_hbm, cos_ref, sa_ref, sb_ref, ck_ref, cv_ref, lq1_ref, lk1_ref, lq2_ref, lk2_ref,
         dn_ref, o_ref, qkv_s, w_ref, stage, sem) = refs
    else:
        (x_ref, mod_ref, g_ref, w_hbm, lq1_ref, lk1_ref, lq2_ref, lk2_ref, dn_ref, o_ref, nk_ref, nv_ref,
         qkv_s, w_ref, stage, sem) = refs
        assert seq == ROW_CHUNK
    rc = ROW_CHUNK
    n_rows = x_ref.shape[0]
    mrow = (1 + pl.program_id(0)) if cached else 0
    need, all_taken = _step0_weights(
        [(w_hbm, w_ref, r0, 0, r0, 0, W_STAGE_ROWS, D_MODEL) for r0 in range(C_QA, C_QB, W_STAGE_ROWS)], stage, sem)
    lam = (jnp.exp(jnp.sum(lq1_ref[...] * lk1_ref[...], axis=-1, keepdims=True))
           - jnp.exp(jnp.sum(lq2_ref[...] * lk2_ref[...], axis=-1, keepdims=True)) + LAM_INIT)
    lane = lax.broadcasted_iota(jnp.int32, (1, 128), 1)
    scale = DK_A ** -0.5
    m1 = jnp.where(lane < DK_A, scale, 0.0).astype(BF16)
    m2 = jnp.where(lane >= DK_A, scale, 0.0).astype(BF16)
    n_keys = (n_rows + ck_ref.shape[1]) if cached else rc
    ones = jnp.ones((n_keys, 128), BF16)

    def proj(r0):
        rows = slice(r0, r0 + rc)
        h = _prenorm(x_ref, rows, mod_ref, mrow, g_ref)
        yield
        if r0 == 0:
            need((C_VA - C_QA) // W_STAGE_ROWS)
        pa = _dot_nt(h, w_ref[C_QA:C_VA, :])
        if r0 == 0:
            need((C_QB - C_VA) // W_STAGE_ROWS)
        pv = _dot_nt(h, w_ref[C_VA:C_QB, :])
        for blk in range(8):
            t = pa[:, blk * 128:(blk + 1) * 128]
            if cached:
                t = (t * cos_ref[rows, :] + pltpu.roll(t, 112, 1) * sa_ref[rows, :]
                     + pltpu.roll(t, 16, 1) * sb_ref[rows, :])
            elif blk >= 4:
                nk_ref[r0 // seq, blk - 4] = t
            qkv_s[rows, blk * 128:(blk + 1) * 128] = t.astype(BF16)
        qkv_s[rows, C_VA:C_QB] = pv.astype(BF16)
        if not cached:
            for hh in range(H_A):
                nv_ref[r0 // seq, hh] = pv[:, hh * 128:(hh + 1) * 128]

    kv = {}
    res = {}

    def chain(r0, h, branch):
        rows = slice(r0, r0 + rc)
        cols = slice(h * 128, (h + 1) * 128)
        kcols = slice(C_KA + h * 128, C_KA + (h + 1) * 128)
        vcols = slice(C_VA + h * 128, C_VA + (h + 1) * 128)
        key = h if cached else (h, r0)
        if key not in kv:
            if cached:
                k = jnp.concatenate([ck_ref[h].astype(BF16), qkv_s[:, kcols]], axis=0)
                v = jnp.concatenate([cv_ref[h].astype(BF16), qkv_s[:, vcols]], axis=0)
            else:
                k, v = qkv_s[rows, kcols], qkv_s[rows, vcols]
            kv[key] = (k, jnp.concatenate([v, ones], axis=1))
        k, v1 = kv[key]
        s = _dot_nt(qkv_s[rows, cols] * (m2 if branch else m1), k)
        yield
        e = jnp.exp(s - jnp.max(s, axis=-1, keepdims=True)).astype(BF16)
        yield
        res[r0, h, branch] = _dot(e, v1)
        yield
        if branch:
            r1, r2 = res[r0, h, 0], res[r0, h, 1]
            o = r1[:, 0:128] / r1[:, 128:256] - lam * (r2[:, 0:128] / r2[:, 128:256])
            o_ref[rows, cols] = (_rms(o, dn_ref[...]) * (1.0 - LAM_INIT)).astype(BF16)

    projs = {r0: proj(r0) for r0 in range(0, n_rows, rc)}
    gens, deps = list(projs.values()), {}
    for r0 in projs:
        for h in range(H_A):
            for br in range(2):
                c = chain(r0, h, br)
                deps[id(c)] = list(projs.values()) if cached else [projs[r0]]
                gens.append(c)
    _run_skewed(gens, deps=deps)
    assert all_taken()


def _attention(x2d, mods, g_pre, w_in_f32, rope_tabs, cache_k, cache_v, lam_params, diff_norm, batch, seq_len):
    cached = cache_k is not None
    tq = MIX_TILE
    n_tiles = batch * seq_len // tq
    const = lambda i: (0, 0)
    in_specs = [
        pl.BlockSpec((tq, D_MODEL), lambda i: (i, 0)),
        pl.BlockSpec(mods.shape, const),
        pl.BlockSpec((1, D_MODEL), const),
        pl.BlockSpec(memory_space=pl.ANY),
    ]
    args = [x2d, mods, g_pre, w_in_f32]
    out_shape = [jax.ShapeDtypeStruct((batch * seq_len, H_A * DV_A), BF16)]
    out_specs = [pl.BlockSpec((tq, H_A * DV_A), lambda i: (i, 0))]
    if cached:
        assert seq_len == tq
        past = cache_k.shape[3]
        cspec = pl.BlockSpec((None, None, H_A, past, 128), lambda i: (i, 0, 0, 0, 0))
        in_specs += [pl.BlockSpec((tq, 128), const)] * 3 + [cspec, cspec]
        args += list(rope_tabs) + [cache_k, cache_v]
    else:
        nb = tq // seq_len
        for _ in range(2):
            out_shape.append(jax.ShapeDtypeStruct((batch, 1, H_A, seq_len, 128), F32))
            out_specs.append(pl.BlockSpec((nb, None, H_A, seq_len, 128), lambda i: (i, 0, 0, 0, 0)))
    in_specs += [pl.BlockSpec((1, DK_A), const)] * 4 + [pl.BlockSpec((1, DV_A), const)]
    args += list(lam_params) + [diff_norm]
    return pl.pallas_call(
        functools.partial(_attn_kernel, cached=cached, seq=seq_len),
        out_shape=out_shape,
        grid=(n_tiles,),
        in_specs=in_specs,
        out_specs=out_specs,
        scratch_shapes=[
            pltpu.VMEM((tq, C_QB), BF16),
            pltpu.VMEM((C_QB, D_MODEL), BF16),
            pltpu.VMEM((W_STAGE_SLOTS, W_STAGE_ROWS, D_MODEL), F32),
            pltpu.SemaphoreType.DMA((W_STAGE_SLOTS,)),
        ],
        compiler_params=pltpu.CompilerParams(
            dimension_semantics=("arbitrary",), vmem_limit_bytes=VMEM_LIMIT),
        name="attn_cached" if cached else "attn_ctx",
    )(*args)


def _split3(g):
    hi = g.astype(BF16)
    r1 = g - hi.astype(F32)
    mid = r1.astype(BF16)
    lo = (r1 - mid.astype(F32)).astype(BF16)
    return hi, mid, lo


def _gla_block_stages(q_ref, k_ref, g_ref, v_ref, in_rows, qc, vc, vt_cache, blk, reverse, oa_s, qe_s, ut_s, dec_s):
    C = GLA_CHUNK
    R = in_rows.stop - in_rows.start
    cpb = R // C
    rows = slice(blk * R, (blk + 1) * R)
    ri = lax.broadcasted_iota(jnp.int32, (R, R), 0)
    ci = lax.broadcasted_iota(jnp.int32, (R, R), 1)
    same_chunk = jnp.right_shift(ri, 6) == jnp.right_shift(ci, 6)
    if reverse:
        keep = same_chunk & (ci >= ri)
        last, ref = 0, C // 2
    else:
        keep = same_chunk & (ci <= ri)
        last, ref = C - 1, C // 2 - 1
    tri = jnp.where(keep, 1.0, 0.0).astype(BF16)
    lane = lax.broadcasted_iota(jnp.int32, (1, 2 * DK_B), 1)
    h0 = jnp.where(lane < DK_B, 1.0, 0.0).astype(BF16)
    h1 = jnp.where(lane >= DK_B, 1.0, 0.0).astype(BF16)

    hi, mid, lo = _split3(g_ref[in_rows, qc])
    b3 = _dot(tri, jnp.concatenate([hi, mid, lo], axis=1))
    yield
    b = b3[:, 0:128] + b3[:, 128:256] + b3[:, 256:384]
    q = q_ref[in_rows, qc]
    k = k_ref[in_rows, qc]
    qt, kt, qe, kd = [], [], [], []
    zeros = jnp.zeros((C, 2 * DK_B), BF16)
    for c in range(cpb):
        s = slice(c * C, (c + 1) * C)
        bc = b[s]
        bm = bc[ref:ref + 1]
        bl = bc[last:last + 1]
        qt.append((q[s] * jnp.exp(bc - bm)).astype(BF16))
        kt.append((k[s] * jnp.exp(bm - bc)).astype(BF16))
        qe.append((q[s] * jnp.exp(bc)).astype(BF16))
        kdc = (k[s] * jnp.exp(bl - bc)).astype(BF16)
        kd.append(jnp.concatenate([kdc if j == c else zeros for j in range(cpb)], axis=1))
        dec_s[blk * cpb + c:blk * cpb + c + 1, :] = jnp.exp(bl)
    qe_s[rows, :] = jnp.concatenate(qe, axis=0)
    qt = jnp.concatenate(qt, axis=0)
    kt = jnp.concatenate(kt, axis=0)
    yield
    s0 = _dot_nt(qt * h0, kt)
    s1 = _dot_nt(qt * h1, kt)
    v = v_ref[in_rows, vc]
    if (in_rows.start, vc.start) not in vt_cache:
        vt_cache[in_rows.start, vc.start] = v.T
    ut = _dot(vt_cache[in_rows.start, vc.start], jnp.concatenate(kd, axis=0))
    yield
    a0 = jnp.where(keep, s0, 0.0).astype(BF16)
    a1 = jnp.where(keep, s1, 0.0).astype(BF16)
    br = lax.broadcasted_iota(jnp.int32, ut.shape, 0)
    bcol = lax.broadcasted_iota(jnp.int32, ut.shape, 1)
    ut = jnp.where((br < DV_B) == ((bcol & (2 * DK_B - 1)) < DK_B), ut, 0.0)
    for c in range(cpb):
        ut_s[blk * cpb + c] = ut[:, c * 2 * DK_B:(c + 1) * 2 * DK_B]
    yield
    oa_s[rows, :] = jnp.concatenate([_dot(a0, v[:, 0:DV_B]), _dot(a1, v[:, DV_B:2 * DV_B])], axis=1)


def _state_to_t(s_pair):
    z = jnp.zeros((DK_B, DV_B), F32)
    a = jnp.concatenate([s_pair[0], z], axis=0).T
    b = jnp.concatenate([z, s_pair[1]], axis=0).T
    return jnp.concatenate([a, b], axis=0)


def _gla_kernel(*refs, n_chunks, n_elems, has_state):
    if has_state:
        (x_ref, mod_ref, g_ref, w_hbm, wgf_ref, wgb_ref, bgf_ref, bgb_ref, gn_ref, sf_ref, sb_ref, og_ref,
         *scratch) = refs
    else:
        (x_ref, mod_ref, g_ref, w_hbm, wgf_ref, wgb_ref, bgf_ref, bgb_ref, gn_ref, og_ref, nsf_ref, nsb_ref,
         *scratch) = refs
    q_s, k_s, gf_s, gb_s, rb_s, v_s, oa, oi, st, qe, ut, dec, w_ref, wgl_ref, stage, sem = scratch
    C = GLA_CHUNK
    R = ROW_CHUNK
    cpb = R // C
    L = n_chunks * C
    n_blk = L // R
    n_pairs = H_B // 2
    mrow = (1 + pl.program_id(0)) if has_state else 0
    need, all_taken = _step0_weights(
        [(w_hbm, w_ref, C_QB + b0, 0, b0, 0, W_STAGE_ROWS, D_MODEL) for b0 in range(B_RB, B_GL, W_STAGE_ROWS)]
        + [(w_hbm, wgl_ref, C_GL, 0, 0, 0, IN_COLS - C_GL, D_MODEL)]
        + [(w_hbm, w_ref, C_QB + b0, 0, b0, 0, W_STAGE_ROWS, D_MODEL) for b0 in range(B_QB, B_RB, W_STAGE_ROWS)],
        stage, sem)

    def proj(r0):
        rows = slice(r0, r0 + R)
        h = _prenorm(x_ref, rows, mod_ref, mrow, g_ref)
        yield
        if r0 == 0:
            need((B_GL - B_RB) // W_STAGE_ROWS)
        pr = _dot_nt(h, w_ref[B_RB:B_GL, :])
        if r0 == 0:
            need(1)
        pgl = _dot_nt(h, wgl_ref[...])
        if r0 == 0:
            need((B_RB - B_QB) // W_STAGE_ROWS)
        pb = _dot_nt(h, w_ref[B_QB:B_RB, :])
        zero = jnp.zeros(wgf_ref.shape, F32)
        wg = jnp.concatenate([jnp.concatenate([wgf_ref[...], zero], axis=1),
                              jnp.concatenate([zero, wgb_ref[...]], axis=1)], axis=0).astype(BF16)
        z = _dot(pgl.astype(BF16), wg) + jnp.concatenate([bgf_ref[...], bgb_ref[...]], axis=1)
        ls = (jnp.minimum(z, 0.0) - jnp.log(1.0 + jnp.exp(-jnp.abs(z)))) * (1.0 / GATE_NORM)
        gf_s[rows, :] = ls[:, 0:256]
        gb_s[rows, :] = ls[:, 256:512]
        rb_s[rows, :] = pr
        q_s[rows, :] = pb[:, B_QB:B_KB] * (DK_B ** -0.5)
        k_s[rows, :] = pb[:, B_KB:B_VB]
        v_s[rows, :] = pb[:, B_VB:B_RB].astype(BF16)

    def scan_stage(e, p, d, blk):
        ch = 2 * p + d
        if blk != (n_blk - 1 if d else 0):
            s = st[e, ch]
        elif has_state:
            assert n_elems == 1
            s = _state_to_t((sb_ref if d else sf_ref)[2 * p:2 * p + 2])
        else:
            s = jnp.zeros(st.shape[2:], F32)
        for i in range(cpb):
            c = blk * cpb + (cpb - 1 - i if d else i)
            rows = slice(c * C, (c + 1) * C)
            oi[e, ch, rows, :] = _dot_nt(qe[e, ch, rows, :], s.astype(BF16))
            s = s * dec[e, ch, c:c + 1, :] + ut[e, ch, c]
            yield
        if blk != (0 if d else n_blk - 1):
            st[e, ch] = s
        elif not has_state:
            dst_ref = nsb_ref if d else nsf_ref
            dst_ref[e, 2 * p] = s[0:DV_B, :].T[0:DK_B, :]
            dst_ref[e, 2 * p + 1] = s[DV_B:2 * DV_B, :].T[DK_B:2 * DK_B, :]

    def fin_stage(e, p, blk):
        rows = slice(blk * R, (blk + 1) * R)
        out_rows = slice(e * L + blk * R, e * L + (blk + 1) * R)
        o = ((oa[e, 2 * p, rows, :] + oi[e, 2 * p, rows, :])
             + (oa[e, 2 * p + 1, rows, :] + oi[e, 2 * p + 1, rows, :]))
        for j in range(2):
            cols = slice((2 * p + j) * DV_B, (2 * p + j + 1) * DV_B)
            oj = _rms(o[:, j * DV_B:(j + 1) * DV_B], gn_ref[...])
            rj = rb_s[out_rows, cols]
            og_ref[out_rows, cols] = (oj * (rj * jax.nn.sigmoid(rj))).astype(BF16)
        yield

    vt_cache = {}
    projs, blocks, scans, deps = {}, [], {}, {}
    for e in range(n_elems):
        for t in range(n_blk):
            blk = t // 2 if t % 2 == 0 else n_blk - 1 - t // 2
            projs[e, blk] = proj(e * L + blk * R)
    for e in range(n_elems):
        for t in range(n_blk):
            for p in range(n_pairs):
                for d in range(2):
                    blk = n_blk - 1 - t if d else t
                    ch = 2 * p + d
                    in_rows = slice(e * L + blk * R, e * L + (blk + 1) * R)
                    g = _gla_block_stages(
                        q_s, k_s, gb_s if d else gf_s, v_s, in_rows,
                        slice(p * 2 * DK_B, (p + 1) * 2 * DK_B), slice(p * 2 * DV_B, (p + 1) * 2 * DV_B),
                        vt_cache, blk, bool(d), oa.at[e, ch], qe.at[e, ch], ut.at[e, ch], dec.at[e, ch])
                    deps[id(g)] = [projs[e, blk]]
                    blocks.append(g)
                    s = scan_stage(e, p, d, blk)
                    prev = scans.get((e, p, d, blk + 1 if d else blk - 1))
                    deps[id(s)] = [g] + ([prev] if prev is not None else [])
                    scans[e, p, d, blk] = s
    aux = []
    for e in range(n_elems):
        for t in range(n_blk):
            for p in range(n_pairs):
                aux.append(scans[e, p, 0, t])
                aux.append(scans[e, p, 1, n_blk - 1 - t])
        for blk in range(n_blk):
            for p in range(n_pairs):
                f = fin_stage(e, p, blk)
                deps[id(f)] = [scans[e, p, 0, blk], scans[e, p, 1, blk]]
                aux.append(f)
    _run_skewed(list(projs.values()) + blocks, aux, deps)
    assert all_taken()


def _gla(x2d, mods, g_pre, w_in_f32, wgf, wgb, bgf, bgb, gla_norm, state_f, state_b, batch, seq_len):
    has_state = state_f is not None
    n_chunks = seq_len // GLA_CHUNK
    L = seq_len
    rows = MIX_TILE
    n_elems = rows // L
    n_ch = H_B
    wk, wv = H_B * DK_B, H_B * DV_B
    const = lambda i: (0, 0)
    in_specs = [
        pl.BlockSpec((rows, D_MODEL), lambda i: (i, 0)),
        pl.BlockSpec(mods.shape, const),
        pl.BlockSpec((1, D_MODEL), const),
        pl.BlockSpec(memory_space=pl.ANY),
        pl.BlockSpec((GATE_RANK, wk), const),
        pl.BlockSpec((GATE_RANK, wk), const),
        pl.BlockSpec((1, wk), const),
        pl.BlockSpec((1, wk), const),
        pl.BlockSpec((1, DV_B), const),
    ]
    args = [x2d, mods, g_pre, w_in_f32, wgf, wgb, bgf, bgb, gla_norm]
    out_shape = [jax.ShapeDtypeStruct((batch * L, wv), BF16)]
    out_specs = [pl.BlockSpec((rows, wv), lambda i: (i, 0))]
    if has_state:
        assert n_elems == 1
        st_spec = pl.BlockSpec((None, None, H_B, DK_B, DV_B), lambda i: (i, 0, 0, 0, 0))
        in_specs += [st_spec, st_spec]
        args += [state_f, state_b]
    else:
        for _ in range(2):
            out_shape.append(jax.ShapeDtypeStruct((batch, 1, H_B, DK_B, DV_B), F32))
            out_specs.append(pl.BlockSpec((n_elems, None, H_B, DK_B, DV_B), lambda i: (i, 0, 0, 0, 0)))
    return pl.pallas_call(
        functools.partial(_gla_kernel, n_chunks=n_chunks, n_elems=n_elems, has_state=has_state),
        out_shape=out_shape,
        grid=(batch // n_elems,),
        in_specs=in_specs,
        out_specs=out_specs,
        scratch_shapes=[
            pltpu.VMEM((rows, wk), F32),
            pltpu.VMEM((rows, wk), F32),
            pltpu.VMEM((rows, wk), F32),
            pltpu.VMEM((rows, wk), F32),
            pltpu.VMEM((rows, wv), F32),
            pltpu.VMEM((rows, wv), BF16),
            pltpu.VMEM((n_elems, n_ch, L, 2 * DV_B), F32),
            pltpu.VMEM((n_elems, n_ch, L, 2 * DV_B), F32),
            pltpu.VMEM((n_elems, n_ch, 2 * DV_B, 2 * DK_B), F32),
            pltpu.VMEM((n_elems, n_ch, L, 2 * DK_B), BF16),
            pltpu.VMEM((n_elems, n_ch, n_chunks, 2 * DV_B, 2 * DK_B), F32),
            pltpu.VMEM((n_elems, n_ch, max(n_chunks, 8), 2 * DK_B), F32),
            pltpu.VMEM((B_GL, D_MODEL), BF16),
            pltpu.VMEM((IN_COLS - C_GL, D_MODEL), BF16),
            pltpu.VMEM((W_STAGE_SLOTS, W_STAGE_ROWS, D_MODEL), F32),
            pltpu.SemaphoreType.DMA((W_STAGE_SLOTS,)),
        ],
        compiler_params=pltpu.CompilerParams(
            dimension_semantics=("arbitrary",), vmem_limit_bytes=VMEM_LIMIT),
        name="gla_state" if has_state else "gla_ctx",
    )(*args)


def _out_mlp_kernel(xc_ref, oac_ref, ogc_ref, xl_ref, oal_ref, ogl_ref, mod_ref, wo_hbm, gpost_ref, gmpre_ref,
                    gmpost_ref, w1_hbm, w2_hbm, yc_ref, yl_ref, wo_ref, w1_ref, w2_ref, stage, sem, *,
                    n_ctx_tiles, lat_tiles_per_sample):
    rc = ROW_CHUNK
    tf = 1024
    per_win = tf // W_STAGE_ROWS

    def weight_windows():
        windows = [(wo_hbm, wo_ref, r0, 0, r0, 0, W_STAGE_ROWS, D_MODEL) for r0 in range(0, D_MODEL, W_STAGE_ROWS)]
        for j in range(D_FF // tf):
            windows += [(w1_hbm, w1_ref, r0, j * tf, r0, j * tf, W_STAGE_ROWS, tf)
                        for r0 in range(0, D_MODEL, W_STAGE_ROWS)]
            windows += [(w2_hbm, w2_ref, j * tf + r0, 0, j * tf + r0, 0, W_STAGE_ROWS, D_MODEL)
                        for r0 in range(0, tf, W_STAGE_ROWS)]
        return windows

    def run(x_ref, oa_ref, og_ref, y_ref, mrow, take=None):
        def chunk(r0):
            need = take if (take is not None and r0 == 0) else (lambda k: None)
            rows = slice(r0, r0 + rc)
            need(D_MODEL // W_STAGE_ROWS)
            mix = _dot(oa_ref[rows, :], wo_ref[0:512, :]) + _dot(og_ref[rows, :], wo_ref[512:1024, :])
            yield
            x1 = x_ref[rows, :] + _mod(mod_ref, mrow, 2) * _rms(mix, gpost_ref[...])
            h2 = (_rms(x1, gmpre_ref[...]) * (1.0 + _mod(mod_ref, mrow, 4)) + _mod(mod_ref, mrow, 3)).astype(BF16)
            yield
            acc = None
            for j in range(D_FF // tf):
                need(D_MODEL // W_STAGE_ROWS + per_win)
                u = jnp.maximum(_dot(h2, w1_ref[:, j * tf:(j + 1) * tf]), 0.0)
                part = _dot((u * u).astype(BF16), w2_ref[j * tf:(j + 1) * tf, :])
                acc = part if acc is None else acc + part
                yield
            y_ref[rows, :] = x1 + _mod(mod_ref, mrow, 5) * _rms(acc, gmpost_ref[...])

        _run_skewed([chunk(r0) for r0 in range(0, x_ref.shape[0], rc)])

    step = pl.program_id(0)
    is_ctx = step < n_ctx_tiles

    @pl.when(step == 0)
    def _():
        take, all_taken = _weight_stream(weight_windows(), stage, sem)
        run(xc_ref, oac_ref, ogc_ref, yc_ref, 0, take)
        assert all_taken()

    @pl.when(jnp.logical_and(step > 0, is_ctx))
    def _():
        run(xc_ref, oac_ref, ogc_ref, yc_ref, 0)

    @pl.when(jnp.logical_not(is_ctx))
    def _():
        run(xl_ref, oal_ref, ogl_ref, yl_ref, 1 + (step - n_ctx_tiles) // lat_tiles_per_sample)


def _out_mlp(xc2d, oac, ogc, xl2d, oal, ogl, mods, lat_seq, w_out, g_post, g_mpre, g_mpost, w1, w2):
    tm = 512
    n_c, n_l = xc2d.shape[0] // tm, xl2d.shape[0] // tm
    per_b = lat_seq // tm
    ctx_row = lambda i: (jnp.minimum(i, n_c - 1), 0)
    lat_row = lambda i: (jnp.maximum(i - n_c, 0), 0)
    const = lambda i: (0, 0)
    return pl.pallas_call(
        functools.partial(_out_mlp_kernel, n_ctx_tiles=n_c, lat_tiles_per_sample=per_b),
        out_shape=[jax.ShapeDtypeStruct(xc2d.shape, F32), jax.ShapeDtypeStruct(xl2d.shape, F32)],
        grid=(n_c + n_l,),
        in_specs=[
            pl.BlockSpec((tm, D_MODEL), ctx_row),
            pl.BlockSpec((tm, 512), ctx_row),
            pl.BlockSpec((tm, 512), ctx_row),
            pl.BlockSpec((tm, D_MODEL), lat_row),
            pl.BlockSpec((tm, 512), lat_row),
            pl.BlockSpec((tm, 512), lat_row),
            pl.BlockSpec(mods.shape, const),
            pl.BlockSpec(memory_space=pl.ANY),
            pl.BlockSpec((1, D_MODEL), const),
            pl.BlockSpec((1, D_MODEL), const),
            pl.BlockSpec((1, D_MODEL), const),
            pl.BlockSpec(memory_space=pl.ANY),
            pl.BlockSpec(memory_space=pl.ANY),
        ],
        out_specs=[pl.BlockSpec((tm, D_MODEL), ctx_row), pl.BlockSpec((tm, D_MODEL), lat_row)],
        scratch_shapes=[
            pltpu.VMEM((D_MODEL, D_MODEL), BF16),
            pltpu.VMEM((D_MODEL, D_FF), BF16),
            pltpu.VMEM((D_FF, D_MODEL), BF16),
            pltpu.VMEM((W_STAGE_SLOTS, W_STAGE_ROWS, D_MODEL), F32),
            pltpu.SemaphoreType.DMA((W_STAGE_SLOTS,)),
        ],
        compiler_params=pltpu.CompilerParams(
            dimension_semantics=("arbitrary",), vmem_limit_bytes=VMEM_LIMIT),
        name="out_mlp",
    )(xc2d, oac, ogc, xl2d, oal, ogl, mods, w_out, g_post, g_mpre, g_mpost, w1, w2)


def _rope_tables(n_lat):
    pos = np.arange(n_lat)
    row_pos = (pos // GRID_W).astype(np.float64)
    col_pos = (pos % GRID_W).astype(np.float64)
    half = DK_A // 4
    inv = ROPE_BASE ** (-np.arange(half, dtype=np.float64) / half)
    lane = np.arange(128)
    in64 = lane % 64
    in32 = in64 % 32
    p = np.where((in64 < 32)[None, :], row_pos[:, None], col_pos[:, None])
    ang = p * inv[in32 % half][None, :]
    cos, sin = np.cos(ang), np.sin(ang)
    first = (in32 < half)[None, :]
    sa = np.where(first, -sin, 0.0)
    sb = np.where(first, 0.0, sin)
    return tuple(jnp.asarray(t, dtype=F32) for t in (cos, sa, sb))


def _mixers(x, mods, w, rope_tabs, cache_k, cache_v, state_f, state_b):
    batch, seq_len, _ = x.shape
    x2d = x.reshape(batch * seq_len, D_MODEL)
    aouts = _attention(x2d, mods, w["g_pre"], w["w_in"], rope_tabs, cache_k, cache_v, w["lam"], w["diff_norm"],
                       batch, seq_len)
    gouts = _gla(x2d, mods, w["g_pre"], w["w_in"], w["wgf"], w["wgb"], w["bgf"], w["bgb"], w["gla_norm"],
                 state_f, state_b, batch, seq_len)
    return x2d, aouts[0], gouts[0], tuple(aouts[1:]) + tuple(gouts[1:])


def kernel(x_prompt, x_sample, c, cache_k, cache_v, state_fwd, state_bwd, c_ctx, w_ada, b_ada,
           norm_attn_pre, norm_attn_post, norm_mlp_pre, norm_mlp_post, w_in, w_gate_fwd, b_gate_fwd,
           w_gate_bwd, b_gate_bwd, lam_q1, lam_k1, lam_q2, lam_k2, diff_norm, gla_norm, w_out,
           w_mlp1, w_mlp2):
    dec_batch = c.shape[0]
    rows = 16
    cvec = jnp.concatenate([c_ctx[None, :], c, jnp.zeros((rows - 1 - dec_batch, D_MODEL), F32)], axis=0)
    m = _adaln(cvec, w_ada[0], b_ada[0][None, :])

    w = {
        "g_pre": norm_attn_pre[0][None, :],
        "g_post": norm_attn_post[0][None, :],
        "g_mpre": norm_mlp_pre[0][None, :],
        "g_mpost": norm_mlp_post[0][None, :],
        "w_in": jnp.transpose(w_in[0]),
        "wgf": w_gate_fwd[0],
        "wgb": w_gate_bwd[0],
        "bgf": b_gate_fwd,
        "bgb": b_gate_bwd,
        "lam": (lam_q1, lam_k1, lam_q2, lam_k2),
        "diff_norm": diff_norm[0][None, :],
        "gla_norm": gla_norm[0][None, :],
        "w_out": w_out[0],
        "w1": w_mlp1[0],
        "w2": w_mlp2[0],
    }
    xc2d, oac, ogc, (new_k, new_v, new_sf, new_sb) = _mixers(x_prompt, m, w, None, None, None, None, None)
    xl2d, oal, ogl, _ = _mixers(x_sample, m, w, _rope_tables(x_sample.shape[1]),
                                cache_k, cache_v, state_fwd, state_bwd)
    y_prompt, y_sample = _out_mlp(xc2d, oac, ogc, xl2d, oal, ogl, m, x_sample.shape[1],
                                  w["w_out"], w["g_post"], w["g_mpre"], w["g_mpost"], w["w1"], w["w2"])
    return (y_prompt.reshape(x_prompt.shape), y_sample.reshape(x_sample.shape), new_k, new_v, new_sf, new_sb)
```

```python
import functools
import math

import numpy as np
import jax
import jax.numpy as jnp
from jax import lax
from jax.experimental import pallas as pl
from jax.experimental.pallas import tpu as pltpu

F32 = jnp.float32
BF16 = jnp.bfloat16

D_MODEL = 1024
GRID_W = 64
H_A = 4
DV_A = 128
DK_A = 64
H_B = 4
DV_B = 128
DK_B = 64
GATE_RANK = 16
GATE_NORM = 16.0
GLA_CHUNK = 64
D_FF = 4 * D_MODEL
N_MOD = 6
ROPE_BASE = 10000.0
EPS = 1e-6
LAM_INIT = 0.8 - 0.6 * math.exp(-0.3 * 0)

C_QA, C_KA, C_VA, C_QB, C_KB, C_VB, C_RB, C_GL = 0, 512, 1024, 1536, 1792, 2048, 2560, 3072
IN_COLS = 3104
B_QB, B_KB, B_VB, B_RB, B_GL = (c - C_QB for c in (C_QB, C_KB, C_VB, C_RB, C_GL))

V7X_VMEM_BYTES = 64 * 1024 * 1024
VMEM_LIMIT = V7X_VMEM_BYTES - 4 * 1024 * 1024
ROW_CHUNK = 256
MIX_TILE = 1024
W_STAGE_ROWS = 512
W_STAGE_SLOTS = 4


def _rms(x, g):
    return x * lax.rsqrt(jnp.mean(x * x, axis=-1, keepdims=True) + EPS) * g


def _dot(a, b):
    return jnp.dot(a, b, preferred_element_type=F32)


def _dot_nt(a, b):
    return lax.dot_general(a, b, (((1,), (1,)), ((), ())), preferred_element_type=F32)


def _run_skewed(gens, secondary=(), deps=None):
    deps = deps or {}
    queues, running, done = [list(gens), list(secondary)], [], set()
    while any(queues) or running:
        for queue in queues:
            for g in queue:
                if all(id(d) in done for d in deps.get(id(g), ())):
                    queue.remove(g)
                    running.append(g)
                    break
        assert running, "dependency cycle"
        for g in list(running):
            try:
                next(g)
            except StopIteration:
                running.remove(g)
                done.add(id(g))


def _mod(mod_ref, mrow, k):
    row = slice(mrow, mrow + 1) if isinstance(mrow, int) else pl.ds(mrow, 1)
    return mod_ref[row, k * D_MODEL:(k + 1) * D_MODEL]


def _prenorm(x_ref, rows, mod_ref, mrow, g_ref):
    x = x_ref[rows, :]
    return (_rms(x, g_ref[...]) * (1.0 + _mod(mod_ref, mrow, 1)) + _mod(mod_ref, mrow, 0)).astype(BF16)


def _weight_stream(windows, stage, sem):
    n_slots = stage.shape[0]

    def copy(n):
        src, _, r0, c0, _, _, n_rows, width = windows[n]
        slot = n % n_slots
        return pltpu.make_async_copy(src.at[pl.ds(r0, n_rows), pl.ds(c0, width)],
                                     stage.at[slot, pl.ds(0, n_rows), pl.ds(0, width)], sem.at[slot])

    pos = {"started": 0, "taken": 0}

    def start_below(limit):
        while pos["started"] < min(limit, len(windows)):
            copy(pos["started"]).start()
            pos["started"] += 1

    def take(k):
        for _ in range(k):
            n = pos["taken"]
            _, dst, _, _, r0, d0, n_rows, width = windows[n]
            copy(n).wait()
            dst[r0:r0 + n_rows, d0:d0 + width] = stage[n % n_slots, 0:n_rows, 0:width].astype(BF16)
            pos["taken"] += 1
            start_below(n + n_slots + 1)

    start_below(n_slots)
    return take, lambda: pos["taken"] == len(windows)


def _step0_weights(windows, stage, sem):
    first = pl.program_id(0) == 0
    stream = {}

    @pl.when(first)
    def _():
        stream["take"], stream["all_taken"] = _weight_stream(windows, stage, sem)

    def need(k):
        @pl.when(first)
        def _():
            stream["take"](k)

    return need, lambda: stream["all_taken"]()


def _adaln_kernel(c_ref, cctx_ref, w_ref, b_ref, o_ref):
    c = jnp.concatenate([c_ref[...], jnp.broadcast_to(cctx_ref[...], c_ref.shape)], axis=0)
    s = c * jax.nn.sigmoid(c)
    o_ref[...] = _dot(s.astype(BF16), w_ref[...].astype(BF16)) + b_ref[...]


def _adaln(c, c_ctx, w_ada, b_ada):
    n_lat = c.shape[0]
    assert n_lat % 8 == 0
    n = w_ada.shape[1]
    tn = 1536
    return pl.pallas_call(
        _adaln_kernel,
        out_shape=jax.ShapeDtypeStruct((2 * n_lat, n), F32),
        grid=(n // tn,),
        in_specs=[
            pl.BlockSpec((n_lat, D_MODEL), lambda j: (0, 0)),
            pl.BlockSpec((1, D_MODEL), lambda j: (0, 0)),
            pl.BlockSpec((D_MODEL, tn), lambda j: (0, j)),
            pl.BlockSpec((1, tn), lambda j: (0, j)),
        ],
        out_specs=pl.BlockSpec((2 * n_lat, tn), lambda j: (0, j)),
        compiler_params=pltpu.CompilerParams(
            dimension_semantics=("arbitrary",), vmem_limit_bytes=VMEM_LIMIT),
        name="adaln",
    )(c, c_ctx, w_ada, b_ada)


def _attn_kernel(*refs, cached, seq, ctx_row):
    if cached:
        (x_ref, mod_ref, g_ref, w_hbm, cos_ref, sa_ref, sb_ref, ck_ref, cv_ref, lq1_ref, lk1_ref, lq2_ref, lk2_ref,
         dn_ref, o_ref, qkv_s, w_ref, stage, sem) = refs
    else:
        (x_ref, mod_ref, g_ref, w_hbm, lq1_ref, lk1_ref, lq2_ref, lk2_ref, dn_ref, o_ref, nk_ref, nv_ref,
         qkv_s, w_ref, stage, sem) = refs
        assert seq == ROW_CHUNK
    rc = ROW_CHUNK
    n_rows = x_ref.shape[0]
    mrow = pl.program_id(0) if cached else ctx_row
    need, all_taken = _step0_weights(
        [(w_hbm, w_ref, r0, 0, r0, 0, W_STAGE_ROWS, D_MODEL) for r0 in range(C_QA, C_QB, W_STAGE_ROWS)], stage, sem)
    lam = (jnp.exp(jnp.sum(lq1_ref[...] * lk1_ref[...], axis=-1, keepdims=True))
           - jnp.exp(jnp.sum(lq2_ref[...] * lk2_ref[...], axis=-1, keepdims=True)) + LAM_INIT)
    lane = lax.broadcasted_iota(jnp.int32, (1, 128), 1)
    scale = DK_A ** -0.5
    m1 = jnp.where(lane < DK_A, scale, 0.0).astype(BF16)
    m2 = jnp.where(lane >= DK_A, scale, 0.0).astype(BF16)
    n_keys = (n_rows + ck_ref.shape[1]) if cached else rc
    ones = jnp.ones((n_keys, 128), BF16)

    def proj(r0):
        rows = slice(r0, r0 + rc)
        h = _prenorm(x_ref, rows, mod_ref, mrow, g_ref)
        yield
        if r0 == 0:
            need((C_VA - C_QA) // W_STAGE_ROWS)
        pa = _dot_nt(h, w_ref[C_QA:C_VA, :])
        if r0 == 0:
            need((C_QB - C_VA) // W_STAGE_ROWS)
        pv = _dot_nt(h, w_ref[C_VA:C_QB, :])
        for blk in range(8):
            t = pa[:, blk * 128:(blk + 1) * 128]
            if cached:
                t = (t * cos_ref[rows, :] + pltpu.roll(t, 112, 1) * sa_ref[rows, :]
                     + pltpu.roll(t, 16, 1) * sb_ref[rows, :])
            elif blk >= 4:
                nk_ref[r0 // seq, blk - 4] = t
            qkv_s[rows, blk * 128:(blk + 1) * 128] = t.astype(BF16)
        qkv_s[rows, C_VA:C_QB] = pv.astype(BF16)
        if not cached:
            for hh in range(H_A):
                nv_ref[r0 // seq, hh] = pv[:, hh * 128:(hh + 1) * 128]

    kv = {}
    res = {}

    def chain(r0, h, branch):
        rows = slice(r0, r0 + rc)
        cols = slice(h * 128, (h + 1) * 128)
        kcols = slice(C_KA + h * 128, C_KA + (h + 1) * 128)
        vcols = slice(C_VA + h * 128, C_VA + (h + 1) * 128)
        key = h if cached else (h, r0)
        if key not in kv:
            if cached:
                k = jnp.concatenate([ck_ref[h].astype(BF16), qkv_s[:, kcols]], axis=0)
                v = jnp.concatenate([cv_ref[h].astype(BF16), qkv_s[:, vcols]], axis=0)
            else:
                k, v = qkv_s[rows, kcols], qkv_s[rows, vcols]
            kv[key] = (k, jnp.concatenate([v, ones], axis=1))
        k, v1 = kv[key]
        s = _dot_nt(qkv_s[rows, cols] * (m2 if branch else m1), k)
        yield
        e = jnp.exp(s - jnp.max(s, axis=-1, keepdims=True)).astype(BF16)
        yield
        res[r0, h, branch] = _dot(e, v1)
        yield
        if branch:
            r1, r2 = res[r0, h, 0], res[r0, h, 1]
            o = r1[:, 0:128] / r1[:, 128:256] - lam * (r2[:, 0:128] / r2[:, 128:256])
            o_ref[rows, cols] = (_rms(o, dn_ref[...]) * (1.0 - LAM_INIT)).astype(BF16)

    projs = {r0: proj(r0) for r0 in range(0, n_rows, rc)}
    gens, deps = list(projs.values()), {}
    for r0 in projs:
        for h in range(H_A):
            for br in range(2):
                c = chain(r0, h, br)
                deps[id(c)] = list(projs.values()) if cached else [projs[r0]]
                gens.append(c)
    _run_skewed(gens, deps=deps)
    assert all_taken()


def _attention(x2d, mods, g_pre, w_in_f32, rope_tabs, cache_k, cache_v, lam_params, diff_norm, batch, seq_len):
    cached = cache_k is not None
    tq = MIX_TILE
    n_tiles = batch * seq_len // tq
    const = lambda i: (0, 0)
    in_specs = [
        pl.BlockSpec((tq, D_MODEL), lambda i: (i, 0)),
        pl.BlockSpec(mods.shape, const),
        pl.BlockSpec((1, D_MODEL), const),
        pl.BlockSpec(memory_space=pl.ANY),
    ]
    args = [x2d, mods, g_pre, w_in_f32]
    out_shape = [jax.ShapeDtypeStruct((batch * seq_len, H_A * DV_A), BF16)]
    out_specs = [pl.BlockSpec((tq, H_A * DV_A), lambda i: (i, 0))]
    if cached:
        assert seq_len == tq
        past = cache_k.shape[3]
        cspec = pl.BlockSpec((None, None, H_A, past, 128), lambda i: (i, 0, 0, 0, 0))
        in_specs += [pl.BlockSpec((tq, 128), const)] * 3 + [cspec, cspec]
        args += list(rope_tabs) + [cache_k, cache_v]
    else:
        nb = tq // seq_len
        for _ in range(2):
            out_shape.append(jax.ShapeDtypeStruct((batch, 1, H_A, seq_len, 128), F32))
            out_specs.append(pl.BlockSpec((nb, None, H_A, seq_len, 128), lambda i: (i, 0, 0, 0, 0)))
    in_specs += [pl.BlockSpec((1, DK_A), const)] * 4 + [pl.BlockSpec((1, DV_A), const)]
    args += list(lam_params) + [diff_norm]
    return pl.pallas_call(
        functools.partial(_attn_kernel, cached=cached, seq=seq_len, ctx_row=mods.shape[0] // 2),
        out_shape=out_shape,
        grid=(n_tiles,),
        in_specs=in_specs,
        out_specs=out_specs,
        scratch_shapes=[
            pltpu.VMEM((tq, C_QB), BF16),
            pltpu.VMEM((C_QB, D_MODEL), BF16),
            pltpu.VMEM((W_STAGE_SLOTS, W_STAGE_ROWS, D_MODEL), F32),
            pltpu.SemaphoreType.DMA((W_STAGE_SLOTS,)),
        ],
        compiler_params=pltpu.CompilerParams(
            dimension_semantics=("arbitrary",), vmem_limit_bytes=VMEM_LIMIT),
        name="attn_cached" if cached else "attn_ctx",
    )(*args)


def _split3(g):
    hi = g.astype(BF16)
    r1 = g - hi.astype(F32)
    mid = r1.astype(BF16)
    lo = (r1 - mid.astype(F32)).astype(BF16)
    return hi, mid, lo


def _gla_block_stages(q_ref, k_ref, g_ref, v_ref, in_rows, qc, vc, vt_cache, blk, reverse, oa_s, qe_s, ut_s, dec_s):
    C = GLA_CHUNK
    R = in_rows.stop - in_rows.start
    cpb = R // C
    rows = slice(blk * R, (blk + 1) * R)
    ri = lax.broadcasted_iota(jnp.int32, (R, R), 0)
    ci = lax.broadcasted_iota(jnp.int32, (R, R), 1)
    same_chunk = jnp.right_shift(ri, 6) == jnp.right_shift(ci, 6)
    if reverse:
        keep = same_chunk & (ci >= ri)
        last, ref = 0, C // 2
    else:
        keep = same_chunk & (ci <= ri)
        last, ref = C - 1, C // 2 - 1
    tri = jnp.where(keep, 1.0, 0.0).astype(BF16)
    lane = lax.broadcasted_iota(jnp.int32, (1, 2 * DK_B), 1)
    h0 = jnp.where(lane < DK_B, 1.0, 0.0).astype(BF16)
    h1 = jnp.where(lane >= DK_B, 1.0, 0.0).astype(BF16)

    hi, mid, lo = _split3(g_ref[in_rows, qc])
    b3 = _dot(tri, jnp.concatenate([hi, mid, lo], axis=1))
    yield
    b = b3[:, 0:128] + b3[:, 128:256] + b3[:, 256:384]
    q = q_ref[in_rows, qc]
    k = k_ref[in_rows, qc]
    qt, kt, qe, kd = [], [], [], []
    zeros = jnp.zeros((C, 2 * DK_B), BF16)
    for c in range(cpb):
        s = slice(c * C, (c + 1) * C)
        bc = b[s]
        bm = bc[ref:ref + 1]
        bl = bc[last:last + 1]
        qt.append((q[s] * jnp.exp(bc - bm)).astype(BF16))
        kt.append((k[s] * jnp.exp(bm - bc)).astype(BF16))
        qe.append((q[s] * jnp.exp(bc)).astype(BF16))
        kdc = (k[s] * jnp.exp(bl - bc)).astype(BF16)
        kd.append(jnp.concatenate([kdc if j == c else zeros for j in range(cpb)], axis=1))
        dec_s[blk * cpb + c:blk * cpb + c + 1, :] = jnp.exp(bl)
    qe_s[rows, :] = jnp.concatenate(qe, axis=0)
    qt = jnp.concatenate(qt, axis=0)
    kt = jnp.concatenate(kt, axis=0)
    yield
    s0 = _dot_nt(qt * h0, kt)
    s1 = _dot_nt(qt * h1, kt)
    v = v_ref[in_rows, vc]
    if (in_rows.start, vc.start) not in vt_cache:
        vt_cache[in_rows.start, vc.start] = v.T
    ut = _dot(vt_cache[in_rows.start, vc.start], jnp.concatenate(kd, axis=0))
    yield
    a0 = jnp.where(keep, s0, 0.0).astype(BF16)
    a1 = jnp.where(keep, s1, 0.0).astype(BF16)
    br = lax.broadcasted_iota(jnp.int32, ut.shape, 0)
    bcol = lax.broadcasted_iota(jnp.int32, ut.shape, 1)
    ut = jnp.where((br < DV_B) == ((bcol & (2 * DK_B - 1)) < DK_B), ut, 0.0)
    for c in range(cpb):
        ut_s[blk * cpb + c] = ut[:, c * 2 * DK_B:(c + 1) * 2 * DK_B]
    yield
    oa_s[rows, :] = jnp.concatenate([_dot(a0, v[:, 0:DV_B]), _dot(a1, v[:, DV_B:2 * DV_B])], axis=1)


def _state_to_t(s_pair):
    z = jnp.zeros((DK_B, DV_B), F32)
    a = jnp.concatenate([s_pair[0], z], axis=0).T
    b = jnp.concatenate([z, s_pair[1]], axis=0).T
    return jnp.concatenate([a, b], axis=0)


def _gla_kernel(*refs, n_chunks, n_elems, has_state, ctx_row):
    if has_state:
        (x_ref, mod_ref, g_ref, w_hbm, wgf_ref, wgb_ref, bgf_ref, bgb_ref, gn_ref, sf_ref, sb_ref, og_ref,
         *scratch) = refs
    else:
        (x_ref, mod_ref, g_ref, w_hbm, wgf_ref, wgb_ref, bgf_ref, bgb_ref, gn_ref, og_ref, nsf_ref, nsb_ref,
         *scratch) = refs
    q_s, k_s, gf_s, gb_s, rb_s, v_s, oa, oi, st, qe, ut, dec, w_ref, wgl_ref, stage, sem = scratch
    C = GLA_CHUNK
    R = ROW_CHUNK
    cpb = R // C
    L = n_chunks * C
    n_blk = L // R
    n_pairs = H_B // 2
    mrow = pl.program_id(0) if has_state else ctx_row
    need, all_taken = _step0_weights(
        [(w_hbm, w_ref, C_QB + b0, 0, b0, 0, W_STAGE_ROWS, D_MODEL) for b0 in range(B_RB, B_GL, W_STAGE_ROWS)]
        + [(w_hbm, wgl_ref, C_GL, 0, 0, 0, IN_COLS - C_GL, D_MODEL)]
        + [(w_hbm, w_ref, C_QB + b0, 0, b0, 0, W_STAGE_ROWS, D_MODEL) for b0 in range(B_QB, B_RB, W_STAGE_ROWS)],
        stage, sem)

    def proj(r0):
        rows = slice(r0, r0 + R)
        h = _prenorm(x_ref, rows, mod_ref, mrow, g_ref)
        yield
        if r0 == 0:
            need((B_GL - B_RB) // W_STAGE_ROWS)
        pr = _dot_nt(h, w_ref[B_RB:B_GL, :])
        if r0 == 0:
            need(1)
        pgl = _dot_nt(h, wgl_ref[...])
        if r0 == 0:
            need((B_RB - B_QB) // W_STAGE_ROWS)
        pb = _dot_nt(h, w_ref[B_QB:B_RB, :])
        zero = jnp.zeros(wgf_ref.shape, F32)
        wg = jnp.concatenate([jnp.concatenate([wgf_ref[...], zero], axis=1),
                              jnp.concatenate([zero, wgb_ref[...]], axis=1)], axis=0).astype(BF16)
        z = _dot(pgl.astype(BF16), wg) + jnp.concatenate([bgf_ref[...], bgb_ref[...]], axis=1)
        ls = (jnp.minimum(z, 0.0) - jnp.log(1.0 + jnp.exp(-jnp.abs(z)))) * (1.0 / GATE_NORM)
        gf_s[rows, :] = ls[:, 0:256]
        gb_s[rows, :] = ls[:, 256:512]
        rb_s[rows, :] = pr
        q_s[rows, :] = pb[:, B_QB:B_KB] * (DK_B ** -0.5)
        k_s[rows, :] = pb[:, B_KB:B_VB]
        v_s[rows, :] = pb[:, B_VB:B_RB].astype(BF16)

    def scan_stage(e, p, d, blk):
        ch = 2 * p + d
        if blk != (n_blk - 1 if d else 0):
            s = st[e, ch]
        elif has_state:
            assert n_elems == 1
            s = _state_to_t((sb_ref if d else sf_ref)[2 * p:2 * p + 2])
        else:
            s = jnp.zeros(st.shape[2:], F32)
        for i in range(cpb):
            c = blk * cpb + (cpb - 1 - i if d else i)
            rows = slice(c * C, (c + 1) * C)
            oi[e, ch, rows, :] = _dot_nt(qe[e, ch, rows, :], s.astype(BF16))
            s = s * dec[e, ch, c:c + 1, :] + ut[e, ch, c]
            yield
        if blk != (0 if d else n_blk - 1):
            st[e, ch] = s
        elif not has_state:
            dst_ref = nsb_ref if d else nsf_ref
            dst_ref[e, 2 * p] = s[0:DV_B, :].T[0:DK_B, :]
            dst_ref[e, 2 * p + 1] = s[DV_B:2 * DV_B, :].T[DK_B:2 * DK_B, :]

    def fin_stage(e, p, blk):
        rows = slice(blk * R, (blk + 1) * R)
        out_rows = slice(e * L + blk * R, e * L + (blk + 1) * R)
        o = ((oa[e, 2 * p, rows, :] + oi[e, 2 * p, rows, :])
             + (oa[e, 2 * p + 1, rows, :] + oi[e, 2 * p + 1, rows, :]))
        for j in range(2):
            cols = slice((2 * p + j) * DV_B, (2 * p + j + 1) * DV_B)
            oj = _rms(o[:, j * DV_B:(j + 1) * DV_B], gn_ref[...])
            rj = rb_s[out_rows, cols]
            og_ref[out_rows, cols] = (oj * (rj * jax.nn.sigmoid(rj))).astype(BF16)
        yield

    vt_cache = {}
    projs, blocks, scans, deps = {}, [], {}, {}
    for e in range(n_elems):
        for t in range(n_blk):
            blk = t // 2 if t % 2 == 0 else n_blk - 1 - t // 2
            projs[e, blk] = proj(e * L + blk * R)
    for e in range(n_elems):
        for t in range(n_blk):
            for p in range(n_pairs):
                for d in range(2):
                    blk = n_blk - 1 - t if d else t
                    ch = 2 * p + d
                    in_rows = slice(e * L + blk * R, e * L + (blk + 1) * R)
                    g = _gla_block_stages(
                        q_s, k_s, gb_s if d else gf_s, v_s, in_rows,
                        slice(p * 2 * DK_B, (p + 1) * 2 * DK_B), slice(p * 2 * DV_B, (p + 1) * 2 * DV_B),
                        vt_cache, blk, bool(d), oa.at[e, ch], qe.at[e, ch], ut.at[e, ch], dec.at[e, ch])
                    deps[id(g)] = [projs[e, blk]]
                    blocks.append(g)
                    s = scan_stage(e, p, d, blk)
                    prev = scans.get((e, p, d, blk + 1 if d else blk - 1))
                    deps[id(s)] = [g] + ([prev] if prev is not None else [])
                    scans[e, p, d, blk] = s
    aux = []
    for e in range(n_elems):
        for t in range(n_blk):
            for p in range(n_pairs):
                aux.append(scans[e, p, 0, t])
                aux.append(scans[e, p, 1, n_blk - 1 - t])
        for blk in range(n_blk):
            for p in range(n_pairs):
                f = fin_stage(e, p, blk)
                deps[id(f)] = [scans[e, p, 0, blk], scans[e, p, 1, blk]]
                aux.append(f)
    _run_skewed(list(projs.values()) + blocks, aux, deps)
    assert all_taken()


def _gla(x2d, mods, g_pre, w_in_f32, wgf, wgb, bgf, bgb, gla_norm, state_f, state_b, batch, seq_len):
    has_state = state_f is not None
    n_chunks = seq_len // GLA_CHUNK
    L = seq_len
    rows = MIX_TILE
    n_elems = rows // L
    n_ch = H_B
    wk, wv = H_B * DK_B, H_B * DV_B
    const = lambda i: (0, 0)
    in_specs = [
        pl.BlockSpec((rows, D_MODEL), lambda i: (i, 0)),
        pl.BlockSpec(mods.shape, const),
        pl.BlockSpec((1, D_MODEL), const),
        pl.BlockSpec(memory_space=pl.ANY),
        pl.BlockSpec((GATE_RANK, wk), const),
        pl.BlockSpec((GATE_RANK, wk), const),
        pl.BlockSpec((1, wk), const),
        pl.BlockSpec((1, wk), const),
        pl.BlockSpec((1, DV_B), const),
    ]
    args = [x2d, mods, g_pre, w_in_f32, wgf, wgb, bgf, bgb, gla_norm]
    out_shape = [jax.ShapeDtypeStruct((batch * L, wv), BF16)]
    out_specs = [pl.BlockSpec((rows, wv), lambda i: (i, 0))]
    if has_state:
        assert n_elems == 1
        st_spec = pl.BlockSpec((None, None, H_B, DK_B, DV_B), lambda i: (i, 0, 0, 0, 0))
        in_specs += [st_spec, st_spec]
        args += [state_f, state_b]
    else:
        for _ in range(2):
            out_shape.append(jax.ShapeDtypeStruct((batch, 1, H_B, DK_B, DV_B), F32))
            out_specs.append(pl.BlockSpec((n_elems, None, H_B, DK_B, DV_B), lambda i: (i, 0, 0, 0, 0)))
    return pl.pallas_call(
        functools.partial(_gla_kernel, n_chunks=n_chunks, n_elems=n_elems, has_state=has_state,
                          ctx_row=mods.shape[0] // 2),
        out_shape=out_shape,
        grid=(batch // n_elems,),
        in_specs=in_specs,
        out_specs=out_specs,
        scratch_shapes=[
            pltpu.VMEM((rows, wk), F32),
            pltpu.VMEM((rows, wk), F32),
            pltpu.VMEM((rows, wk), F32),
            pltpu.VMEM((rows, wk), F32),
            pltpu.VMEM((rows, wv), F32),
            pltpu.VMEM((rows, wv), BF16),
            pltpu.VMEM((n_elems, n_ch, L, 2 * DV_B), F32),
            pltpu.VMEM((n_elems, n_ch, L, 2 * DV_B), F32),
            pltpu.VMEM((n_elems, n_ch, 2 * DV_B, 2 * DK_B), F32),
            pltpu.VMEM((n_elems, n_ch, L, 2 * DK_B), BF16),
            pltpu.VMEM((n_elems, n_ch, n_chunks, 2 * DV_B, 2 * DK_B), F32),
            pltpu.VMEM((n_elems, n_ch, max(n_chunks, 8), 2 * DK_B), F32),
            pltpu.VMEM((B_GL, D_MODEL), BF16),
            pltpu.VMEM((IN_COLS - C_GL, D_MODEL), BF16),
            pltpu.VMEM((W_STAGE_SLOTS, W_STAGE_ROWS, D_MODEL), F32),
            pltpu.SemaphoreType.DMA((W_STAGE_SLOTS,)),
        ],
        compiler_params=pltpu.CompilerParams(
            dimension_semantics=("arbitrary",), vmem_limit_bytes=VMEM_LIMIT),
        name="gla_state" if has_state else "gla_ctx",
    )(*args)


def _out_mlp_kernel(xc_ref, oac_ref, ogc_ref, xl_ref, oal_ref, ogl_ref, mod_ref, wo_hbm, gpost_ref, gmpre_ref,
                    gmpost_ref, w1_hbm, w2_hbm, yc_ref, yl_ref, wo_ref, w1_ref, w2_ref, stage, sem, *,
                    n_ctx_tiles, lat_tiles_per_sample, ctx_row):
    rc = ROW_CHUNK
    tf = 1024
    per_win = tf // W_STAGE_ROWS

    def weight_windows():
        windows = [(wo_hbm, wo_ref, r0, 0, r0, 0, W_STAGE_ROWS, D_MODEL) for r0 in range(0, D_MODEL, W_STAGE_ROWS)]
        for j in range(D_FF // tf):
            windows += [(w1_hbm, w1_ref, r0, j * tf, r0, j * tf, W_STAGE_ROWS, tf)
                        for r0 in range(0, D_MODEL, W_STAGE_ROWS)]
            windows += [(w2_hbm, w2_ref, j * tf + r0, 0, j * tf + r0, 0, W_STAGE_ROWS, D_MODEL)
                        for r0 in range(0, tf, W_STAGE_ROWS)]
        return windows

    def run(x_ref, oa_ref, og_ref, y_ref, mrow, take=None):
        def chunk(r0):
            need = take if (take is not None and r0 == 0) else (lambda k: None)
            rows = slice(r0, r0 + rc)
            need(D_MODEL // W_STAGE_ROWS)
            mix = _dot(oa_ref[rows, :], wo_ref[0:512, :]) + _dot(og_ref[rows, :], wo_ref[512:1024, :])
            yield
            x1 = x_ref[rows, :] + _mod(mod_ref, mrow, 2) * _rms(mix, gpost_ref[...])
            h2 = (_rms(x1, gmpre_ref[...]) * (1.0 + _mod(mod_ref, mrow, 4)) + _mod(mod_ref, mrow, 3)).astype(BF16)
            yield
            acc = None
            for j in range(D_FF // tf):
                need(D_MODEL // W_STAGE_ROWS + per_win)
                u = jnp.maximum(_dot(h2, w1_ref[:, j * tf:(j + 1) * tf]), 0.0)
                part = _dot((u * u).astype(BF16), w2_ref[j * tf:(j + 1) * tf, :])
                acc = part if acc is None else acc + part
                yield
            y_ref[rows, :] = x1 + _mod(mod_ref, mrow, 5) * _rms(acc, gmpost_ref[...])

        _run_skewed([chunk(r0) for r0 in range(0, x_ref.shape[0], rc)])

    step = pl.program_id(0)
    is_ctx = step < n_ctx_tiles

    @pl.when(step == 0)
    def _():
        take, all_taken = _weight_stream(weight_windows(), stage, sem)
        run(xc_ref, oac_ref, ogc_ref, yc_ref, ctx_row, take)
        assert all_taken()

    @pl.when(jnp.logical_and(step > 0, is_ctx))
    def _():
        run(xc_ref, oac_ref, ogc_ref, yc_ref, ctx_row)

    @pl.when(jnp.logical_not(is_ctx))
    def _():
        run(xl_ref, oal_ref, ogl_ref, yl_ref, (step - n_ctx_tiles) // lat_tiles_per_sample)


def _out_mlp(xc2d, oac, ogc, xl2d, oal, ogl, mods, lat_seq, w_out, g_post, g_mpre, g_mpost, w1, w2):
    tm = 512
    n_c, n_l = xc2d.shape[0] // tm, xl2d.shape[0] // tm
    per_b = lat_seq // tm
    ctx_row = lambda i: (jnp.minimum(i, n_c - 1), 0)
    lat_row = lambda i: (jnp.maximum(i - n_c, 0), 0)
    const = lambda i: (0, 0)
    return pl.pallas_call(
        functools.partial(_out_mlp_kernel, n_ctx_tiles=n_c, lat_tiles_per_sample=per_b, ctx_row=mods.shape[0] // 2),
        out_shape=[jax.ShapeDtypeStruct(xc2d.shape, F32), jax.ShapeDtypeStruct(xl2d.shape, F32)],
        grid=(n_c + n_l,),
        in_specs=[
            pl.BlockSpec((tm, D_MODEL), ctx_row),
            pl.BlockSpec((tm, 512), ctx_row),
            pl.BlockSpec((tm, 512), ctx_row),
            pl.BlockSpec((tm, D_MODEL), lat_row),
            pl.BlockSpec((tm, 512), lat_row),
            pl.BlockSpec((tm, 512), lat_row),
            pl.BlockSpec(mods.shape, const),
            pl.BlockSpec(memory_space=pl.ANY),
            pl.BlockSpec((1, D_MODEL), const),
            pl.BlockSpec((1, D_MODEL), const),
            pl.BlockSpec((1, D_MODEL), const),
            pl.BlockSpec(memory_space=pl.ANY),
            pl.BlockSpec(memory_space=pl.ANY),
        ],
        out_specs=[pl.BlockSpec((tm, D_MODEL), ctx_row), pl.BlockSpec((tm, D_MODEL), lat_row)],
        scratch_shapes=[
            pltpu.VMEM((D_MODEL, D_MODEL), BF16),
            pltpu.VMEM((D_MODEL, D_FF), BF16),
            pltpu.VMEM((D_FF, D_MODEL), BF16),
            pltpu.VMEM((W_STAGE_SLOTS, W_STAGE_ROWS, D_MODEL), F32),
            pltpu.SemaphoreType.DMA((W_STAGE_SLOTS,)),
        ],
        compiler_params=pltpu.CompilerParams(
            dimension_semantics=("arbitrary",), vmem_limit_bytes=VMEM_LIMIT),
        name="out_mlp",
    )(xc2d, oac, ogc, xl2d, oal, ogl, mods, w_out, g_post, g_mpre, g_mpost, w1, w2)


def _rope_tables(n_lat):
    pos = np.arange(n_lat)
    row_pos = (pos // GRID_W).astype(np.float64)
    col_pos = (pos % GRID_W).astype(np.float64)
    half = DK_A // 4
    inv = ROPE_BASE ** (-np.arange(half, dtype=np.float64) / half)
    lane = np.arange(128)
    in64 = lane % 64
    in32 = in64 % 32
    p = np.where((in64 < 32)[None, :], row_pos[:, None], col_pos[:, None])
    ang = p * inv[in32 % half][None, :]
    cos, sin = np.cos(ang), np.sin(ang)
    first = (in32 < half)[None, :]
    sa = np.where(first, -sin, 0.0)
    sb = np.where(first, 0.0, sin)
    return tuple(jnp.asarray(t, dtype=F32) for t in (cos, sa, sb))


def _mixers(x, mods, w, rope_tabs, cache_k, cache_v, state_f, state_b):
    batch, seq_len, _ = x.shape
    x2d = x.reshape(batch * seq_len, D_MODEL)
    aouts = _attention(x2d, mods, w["g_pre"], w["w_in"], rope_tabs, cache_k, cache_v, w["lam"], w["diff_norm"],
                       batch, seq_len)
    gouts = _gla(x2d, mods, w["g_pre"], w["w_in"], w["wgf"], w["wgb"], w["bgf"], w["bgb"], w["gla_norm"],
                 state_f, state_b, batch, seq_len)
    return x2d, aouts[0], gouts[0], tuple(aouts[1:]) + tuple(gouts[1:])


def kernel(x_prompt, x_sample, c, cache_k, cache_v, state_fwd, state_bwd, c_ctx, w_ada, b_ada,
           norm_attn_pre, norm_attn_post, norm_mlp_pre, norm_mlp_post, w_in, w_gate_fwd, b_gate_fwd,
           w_gate_bwd, b_gate_bwd, lam_q1, lam_k1, lam_q2, lam_k2, diff_norm, gla_norm, w_out,
           w_mlp1, w_mlp2):
    m = _adaln(c, c_ctx[None, :], w_ada[0], b_ada)

    w = {
        "g_pre": norm_attn_pre[0][None, :],
        "g_post": norm_attn_post[0][None, :],
        "g_mpre": norm_mlp_pre[0][None, :],
        "g_mpost": norm_mlp_post[0][None, :],
        "w_in": jnp.transpose(w_in[0]),
        "wgf": w_gate_fwd[0],
        "wgb": w_gate_bwd[0],
        "bgf": b_gate_fwd,
        "bgb": b_gate_bwd,
        "lam": (lam_q1, lam_k1, lam_q2, lam_k2),
        "diff_norm": diff_norm[0][None, :],
        "gla_norm": gla_norm[0][None, :],
        "w_out": w_out[0],
        "w1": w_mlp1[0],
        "w2": w_mlp2[0],
    }
    xc2d, oac, ogc, (new_k, new_v, new_sf, new_sb) = _mixers(x_prompt, m, w, None, None, None, None, None)
    xl2d, oal, ogl, _ = _mixers(x_sample, m, w, _rope_tables(x_sample.shape[1]),
                                cache_k, cache_v, state_fwd, state_bwd)
    y_prompt, y_sample = _out_mlp(xc2d, oac, ogc, xl2d, oal, ogl, m, x_sample.shape[1],
                                  w["w_out"], w["g_post"], w["g_mpre"], w["g_mpost"], w["w1"], w["w2"])
    return (y_prompt.reshape(x_prompt.shape), y_sample.reshape(x_sample.shape), new_k, new_v, new_sf, new_sb)
```

```python
import functools
import math

import numpy as np
import jax
import jax.numpy as jnp
from jax import lax
from jax.experimental import pallas as pl
from jax.experimental.pallas import tpu as pltpu

F32 = jnp.float32
BF16 = jnp.bfloat16

D_MODEL = 1024
GRID_W = 64
H_A = 4
DV_A = 128
DK_A = 64
H_B = 4
DV_B = 128
DK_B = 64
GATE_RANK = 16
GATE_NORM = 16.0
GLA_CHUNK = 64
D_FF = 4 * D_MODEL
N_MOD = 6
ROPE_BASE = 10000.0
EPS = 1e-6
LAM_INIT = 0.8 - 0.6 * math.exp(-0.3 * 0)

C_QA, C_KA, C_VA, C_QB, C_KB, C_VB, C_RB, C_GL = 0, 512, 1024, 1536, 1792, 2048, 2560, 3072
IN_COLS = 3104
B_QB, B_KB, B_VB, B_RB, B_GL = (c - C_QB for c in (C_QB, C_KB, C_VB, C_RB, C_GL))

V7X_VMEM_BYTES = 64 * 1024 * 1024
VMEM_LIMIT = V7X_VMEM_BYTES - 4 * 1024 * 1024
ROW_CHUNK = 256
MIX_TILE = 1024
W_STAGE_ROWS = 512
W_STAGE_SLOTS = 4


def _rms(x, g):
    return x * lax.rsqrt(jnp.mean(x * x, axis=-1, keepdims=True) + EPS) * g


def _dot(a, b):
    return jnp.dot(a, b, preferred_element_type=F32)


def _dot_nt(a, b):
    return lax.dot_general(a, b, (((1,), (1,)), ((), ())), preferred_element_type=F32)


def _run_skewed(gens, secondary=(), deps=None):
    deps = deps or {}
    queues, running, done = [list(gens), list(secondary)], [], set()
    while any(queues) or running:
        for queue in queues:
            for g in queue:
                if all(id(d) in done for d in deps.get(id(g), ())):
                    queue.remove(g)
                    running.append(g)
                    break
        assert running, "dependency cycle"
        for g in list(running):
            try:
                next(g)
            except StopIteration:
                running.remove(g)
                done.add(id(g))


def _mod(mod_ref, mrow, k):
    row = slice(mrow, mrow + 1) if isinstance(mrow, int) else pl.ds(mrow, 1)
    return mod_ref[row, k * D_MODEL:(k + 1) * D_MODEL]


def _prenorm(x_ref, rows, mod_ref, mrow, g_ref):
    x = x_ref[rows, :]
    return (_rms(x, g_ref[...]) * (1.0 + _mod(mod_ref, mrow, 1)) + _mod(mod_ref, mrow, 0)).astype(BF16)


def _weight_stream(windows, stage, sem):
    n_slots = stage.shape[0]

    def copy(n):
        src, _, r0, c0, _, _, n_rows, width = windows[n]
        slot = n % n_slots
        return pltpu.make_async_copy(src.at[pl.ds(r0, n_rows), pl.ds(c0, width)],
                                     stage.at[slot, pl.ds(0, n_rows), pl.ds(0, width)], sem.at[slot])

    pos = {"started": 0, "taken": 0}

    def start_below(limit):
        while pos["started"] < min(limit, len(windows)):
            copy(pos["started"]).start()
            pos["started"] += 1

    def take(k):
        for _ in range(k):
            n = pos["taken"]
            _, dst, _, _, r0, d0, n_rows, width = windows[n]
            copy(n).wait()
            dst[r0:r0 + n_rows, d0:d0 + width] = stage[n % n_slots, 0:n_rows, 0:width].astype(BF16)
            pos["taken"] += 1
            start_below(n + n_slots + 1)

    start_below(n_slots)
    return take, lambda: pos["taken"] == len(windows)


def _step0_weights(windows, stage, sem):
    first = pl.program_id(0) == 0
    stream = {}

    @pl.when(first)
    def _():
        stream["take"], stream["all_taken"] = _weight_stream(windows, stage, sem)

    def need(k):
        @pl.when(first)
        def _():
            stream["take"](k)

    return need, lambda: stream["all_taken"]()


def _adaln_kernel(c_ref, cctx_ref, w_ref, b_ref, o_ref):
    c = jnp.concatenate([c_ref[...], jnp.broadcast_to(cctx_ref[...], c_ref.shape)], axis=0)
    s = c * jax.nn.sigmoid(c)
    o_ref[...] = _dot(s.astype(BF16), w_ref[...].astype(BF16)) + b_ref[...]


def _adaln(c, c_ctx, w_ada, b_ada):
    n_lat = c.shape[0]
    assert n_lat % 8 == 0
    n = w_ada.shape[1]
    tn = 1536
    return pl.pallas_call(
        _adaln_kernel,
        out_shape=jax.ShapeDtypeStruct((2 * n_lat, n), F32),
        grid=(n // tn,),
        in_specs=[
            pl.BlockSpec((n_lat, D_MODEL), lambda j: (0, 0)),
            pl.BlockSpec((1, D_MODEL), lambda j: (0, 0)),
            pl.BlockSpec((D_MODEL, tn), lambda j: (0, j)),
            pl.BlockSpec((1, tn), lambda j: (0, j)),
        ],
        out_specs=pl.BlockSpec((2 * n_lat, tn), lambda j: (0, j)),
        compiler_params=pltpu.CompilerParams(
            dimension_semantics=("arbitrary",), vmem_limit_bytes=VMEM_LIMIT),
        name="adaln",
    )(c, c_ctx, w_ada, b_ada)


def _attn_kernel(*refs, cached, seq, ctx_row):
    if cached:
        (x_ref, mod_ref, g_ref, w_hbm, cos_ref, sa_ref, sb_ref, ck_ref, cv_ref, lq1_ref, lk1_ref, lq2_ref, lk2_ref,
         dn_ref, o_ref, qkv_s, w_ref, stage, sem) = refs
    else:
        (x_ref, mod_ref, g_ref, w_hbm, lq1_ref, lk1_ref, lq2_ref, lk2_ref, dn_ref, o_ref, nk_ref, nv_ref,
         qkv_s, w_ref, stage, sem) = refs
        assert seq == ROW_CHUNK
    rc = ROW_CHUNK
    n_rows = x_ref.shape[0]
    mrow = pl.program_id(0) if cached else ctx_row
    first_rows = list(range(C_KA, C_QB, W_STAGE_ROWS)) if cached else []
    need, all_taken = _step0_weights(
        [(w_hbm, w_ref, r0, 0, r0, 0, W_STAGE_ROWS, D_MODEL)
         for r0 in first_rows + [r for r in range(C_QA, C_QB, W_STAGE_ROWS) if r not in first_rows]], stage, sem)
    lam = (jnp.exp(jnp.sum(lq1_ref[...] * lk1_ref[...], axis=-1, keepdims=True))
           - jnp.exp(jnp.sum(lq2_ref[...] * lk2_ref[...], axis=-1, keepdims=True)) + LAM_INIT)
    lane = lax.broadcasted_iota(jnp.int32, (1, 128), 1)
    scale = DK_A ** -0.5
    m1 = jnp.where(lane < DK_A, scale, 0.0).astype(BF16)
    m2 = jnp.where(lane >= DK_A, scale, 0.0).astype(BF16)
    n_keys = (n_rows + ck_ref.shape[1]) if cached else rc
    ones = jnp.ones((n_keys, 128), BF16)

    def proj(r0):
        rows = slice(r0, r0 + rc)
        h = _prenorm(x_ref, rows, mod_ref, mrow, g_ref)
        yield
        if r0 == 0:
            need((C_VA - C_QA) // W_STAGE_ROWS)
        pa = _dot_nt(h, w_ref[C_QA:C_VA, :])
        if r0 == 0:
            need((C_QB - C_VA) // W_STAGE_ROWS)
        pv = _dot_nt(h, w_ref[C_VA:C_QB, :])
        qkv_s[rows, C_QA:C_VA] = pa.astype(BF16)
        qkv_s[rows, C_VA:C_QB] = pv.astype(BF16)
        for hh in range(H_A):
            nk_ref[r0 // seq, hh] = pa[:, C_KA + hh * 128:C_KA + (hh + 1) * 128]
            nv_ref[r0 // seq, hh] = pv[:, hh * 128:(hh + 1) * 128]

    def rotary(t, rows):
        return t * cos_ref[rows, :] + pltpu.roll(t, 112, 1) * sa_ref[rows, :] + pltpu.roll(t, 16, 1) * sb_ref[rows, :]

    hs = {}

    def proj_kv(r0):
        rows = slice(r0, r0 + rc)
        hs[r0] = _prenorm(x_ref, rows, mod_ref, mrow, g_ref)
        yield
        if r0 == 0:
            need((C_VA - C_KA) // W_STAGE_ROWS)
        pk = _dot_nt(hs[r0], w_ref[C_KA:C_VA, :])
        if r0 == 0:
            need((C_QB - C_VA) // W_STAGE_ROWS)
        pv = _dot_nt(hs[r0], w_ref[C_VA:C_QB, :])
        for hh in range(H_A):
            cols = slice(C_KA + hh * 128, C_KA + (hh + 1) * 128)
            qkv_s[rows, cols] = rotary(pk[:, hh * 128:(hh + 1) * 128], rows).astype(BF16)
        qkv_s[rows, C_VA:C_QB] = pv.astype(BF16)

    def proj_q(r0):
        rows = slice(r0, r0 + rc)
        if r0 == 0:
            need((C_KA - C_QA) // W_STAGE_ROWS)
        pq = _dot_nt(hs[r0], w_ref[C_QA:C_KA, :])
        for hh in range(H_A):
            qkv_s[rows, hh * 128:(hh + 1) * 128] = rotary(pq[:, hh * 128:(hh + 1) * 128], rows).astype(BF16)
        yield

    kv = {}
    res = {}

    def chain(r0, h, branch):
        rows = slice(r0, r0 + rc)
        cols = slice(h * 128, (h + 1) * 128)
        kcols = slice(C_KA + h * 128, C_KA + (h + 1) * 128)
        vcols = slice(C_VA + h * 128, C_VA + (h + 1) * 128)
        key = h if cached else (h, r0)
        if key not in kv:
            if cached:
                k = jnp.concatenate([ck_ref[h].astype(BF16), qkv_s[:, kcols]], axis=0)
                v = jnp.concatenate([cv_ref[h].astype(BF16), qkv_s[:, vcols]], axis=0)
            else:
                k, v = qkv_s[rows, kcols], qkv_s[rows, vcols]
            kv[key] = (k, jnp.concatenate([v, ones], axis=1))
        k, v1 = kv[key]
        s = _dot_nt(qkv_s[rows, cols] * (m2 if branch else m1), k)
        yield
        e = jnp.exp(s - jnp.max(s, axis=-1, keepdims=True)).astype(BF16)
        yield
        res[r0, h, branch] = _dot(e, v1)
        yield
        if branch:
            r1, r2 = res[r0, h, 0], res[r0, h, 1]
            o = r1[:, 0:128] / r1[:, 128:256] - lam * (r2[:, 0:128] / r2[:, 128:256])
            o_ref[rows, cols] = (_rms(o, dn_ref[...]) * (1.0 - LAM_INIT)).astype(BF16)

    starts = range(0, n_rows, rc)
    gens, deps = [], {}
    if cached:
        kvs = {r0: proj_kv(r0) for r0 in starts}
        gens += list(kvs.values())
    for r0 in starts:
        if cached:
            p = proj_q(r0)
            deps[id(p)] = [kvs[r0]]
            need_first = list(kvs.values()) + [p]
        else:
            p = proj(r0)
            need_first = [p]
        gens.append(p)
        for h in range(H_A):
            for br in range(2):
                c = chain(r0, h, br)
                deps[id(c)] = need_first
                gens.append(c)
    _run_skewed(gens, deps=deps)
    assert all_taken()


def _attention(x2d, mods, g_pre, w_in_f32, rope_tabs, cache_k, cache_v, lam_params, diff_norm, batch, seq_len):
    cached = cache_k is not None
    tq = MIX_TILE
    n_tiles = batch * seq_len // tq
    const = lambda i: (0, 0)
    in_specs = [
        pl.BlockSpec((tq, D_MODEL), lambda i: (i, 0)),
        pl.BlockSpec(mods.shape, const),
        pl.BlockSpec((1, D_MODEL), const),
        pl.BlockSpec(memory_space=pl.ANY),
    ]
    args = [x2d, mods, g_pre, w_in_f32]
    out_shape = [jax.ShapeDtypeStruct((batch * seq_len, H_A * DV_A), BF16)]
    out_specs = [pl.BlockSpec((tq, H_A * DV_A), lambda i: (i, 0))]
    if cached:
        assert seq_len == tq
        past = cache_k.shape[3]
        cspec = pl.BlockSpec((None, None, H_A, past, 128), lambda i: (i, 0, 0, 0, 0))
        in_specs += [pl.BlockSpec((tq, 128), const)] * 3 + [cspec, cspec]
        args += list(rope_tabs) + [cache_k, cache_v]
    else:
        nb = tq // seq_len
        for _ in range(2):
            out_shape.append(jax.ShapeDtypeStruct((batch, 1, H_A, seq_len, 128), F32))
            out_specs.append(pl.BlockSpec((nb, None, H_A, seq_len, 128), lambda i: (i, 0, 0, 0, 0)))
    in_specs += [pl.BlockSpec((1, DK_A), const)] * 4 + [pl.BlockSpec((1, DV_A), const)]
    args += list(lam_params) + [diff_norm]
    return pl.pallas_call(
        functools.partial(_attn_kernel, cached=cached, seq=seq_len, ctx_row=mods.shape[0] // 2),
        out_shape=out_shape,
        grid=(n_tiles,),
        in_specs=in_specs,
        out_specs=out_specs,
        scratch_shapes=[
            pltpu.VMEM((tq, C_QB), BF16),
            pltpu.VMEM((C_QB, D_MODEL), BF16),
            pltpu.VMEM((W_STAGE_SLOTS, W_STAGE_ROWS, D_MODEL), F32),
            pltpu.SemaphoreType.DMA((W_STAGE_SLOTS,)),
        ],
        compiler_params=pltpu.CompilerParams(
            dimension_semantics=("arbitrary",), vmem_limit_bytes=VMEM_LIMIT),
        name="attn_cached" if cached else "attn_ctx",
    )(*args)


def _split3(g):
    hi = g.astype(BF16)
    r1 = g - hi.astype(F32)
    mid = r1.astype(BF16)
    lo = (r1 - mid.astype(F32)).astype(BF16)
    return hi, mid, lo


def _gla_block_stages(q_ref, k_ref, g_ref, v_ref, in_rows, qc, vc, vt_cache, blk, reverse, oa_s, qe_s, ut_s, dec_s):
    C = GLA_CHUNK
    R = in_rows.stop - in_rows.start
    cpb = R // C
    rows = slice(blk * R, (blk + 1) * R)
    ri = lax.broadcasted_iota(jnp.int32, (R, R), 0)
    ci = lax.broadcasted_iota(jnp.int32, (R, R), 1)
    same_chunk = jnp.right_shift(ri, 6) == jnp.right_shift(ci, 6)
    if reverse:
        keep = same_chunk & (ci >= ri)
        last, ref = 0, C // 2
    else:
        keep = same_chunk & (ci <= ri)
        last, ref = C - 1, C // 2 - 1
    tri = jnp.where(keep, 1.0, 0.0).astype(BF16)
    lane = lax.broadcasted_iota(jnp.int32, (1, 2 * DK_B), 1)
    h0 = jnp.where(lane < DK_B, 1.0, 0.0).astype(BF16)
    h1 = jnp.where(lane >= DK_B, 1.0, 0.0).astype(BF16)

    hi, mid, lo = _split3(g_ref[in_rows, qc])
    b3 = _dot(tri, jnp.concatenate([hi, mid, lo], axis=1))
    yield
    b = b3[:, 0:128] + b3[:, 128:256] + b3[:, 256:384]
    q = q_ref[in_rows, qc]
    k = k_ref[in_rows, qc]
    qt, kt, qe, kd = [], [], [], []
    zeros = jnp.zeros((C, 2 * DK_B), BF16)
    for c in range(cpb):
        s = slice(c * C, (c + 1) * C)
        bc = b[s]
        bm = bc[ref:ref + 1]
        bl = bc[last:last + 1]
        qt.append((q[s] * jnp.exp(bc - bm)).astype(BF16))
        kt.append((k[s] * jnp.exp(bm - bc)).astype(BF16))
        qe.append((q[s] * jnp.exp(bc)).astype(BF16))
        kdc = (k[s] * jnp.exp(bl - bc)).astype(BF16)
        kd.append(jnp.concatenate([kdc if j == c else zeros for j in range(cpb)], axis=1))
        dec_s[blk * cpb + c:blk * cpb + c + 1, :] = jnp.exp(bl)
    qe_s[rows, :] = jnp.concatenate(qe, axis=0)
    qt = jnp.concatenate(qt, axis=0)
    kt = jnp.concatenate(kt, axis=0)
    yield
    s0 = _dot_nt(qt * h0, kt)
    s1 = _dot_nt(qt * h1, kt)
    v = v_ref[in_rows, vc]
    if (in_rows.start, vc.start) not in vt_cache:
        vt_cache[in_rows.start, vc.start] = v.T
    ut = _dot(vt_cache[in_rows.start, vc.start], jnp.concatenate(kd, axis=0))
    yield
    a0 = jnp.where(keep, s0, 0.0).astype(BF16)
    a1 = jnp.where(keep, s1, 0.0).astype(BF16)
    br = lax.broadcasted_iota(jnp.int32, ut.shape, 0)
    bcol = lax.broadcasted_iota(jnp.int32, ut.shape, 1)
    ut = jnp.where((br < DV_B) == ((bcol & (2 * DK_B - 1)) < DK_B), ut, 0.0)
    for c in range(cpb):
        ut_s[blk * cpb + c] = ut[:, c * 2 * DK_B:(c + 1) * 2 * DK_B]
    yield
    oa_s[rows, :] = jnp.concatenate([_dot(a0, v[:, 0:DV_B]), _dot(a1, v[:, DV_B:2 * DV_B])], axis=1)


def _state_to_t(s_pair):
    z = jnp.zeros((DK_B, DV_B), F32)
    a = jnp.concatenate([s_pair[0], z], axis=0).T
    b = jnp.concatenate([z, s_pair[1]], axis=0).T
    return jnp.concatenate([a, b], axis=0)


def _gla_kernel(*refs, n_chunks, n_elems, has_state, ctx_row):
    if has_state:
        (x_ref, mod_ref, g_ref, w_hbm, wgf_ref, wgb_ref, bgf_ref, bgb_ref, gn_ref, sf_ref, sb_ref, og_ref,
         *scratch) = refs
    else:
        (x_ref, mod_ref, g_ref, w_hbm, wgf_ref, wgb_ref, bgf_ref, bgb_ref, gn_ref, og_ref, nsf_ref, nsb_ref,
         *scratch) = refs
    q_s, k_s, gf_s, gb_s, rb_s, v_s, oa, oi, st, qe, ut, dec, w_ref, wgl_ref, stage, sem = scratch
    C = GLA_CHUNK
    R = ROW_CHUNK
    cpb = R // C
    L = n_chunks * C
    n_blk = L // R
    n_pairs = H_B // 2
    mrow = pl.program_id(0) if has_state else ctx_row
    need, all_taken = _step0_weights(
        [(w_hbm, w_ref, C_QB + b0, 0, b0, 0, W_STAGE_ROWS, D_MODEL) for b0 in range(B_RB, B_GL, W_STAGE_ROWS)]
        + [(w_hbm, wgl_ref, C_GL, 0, 0, 0, IN_COLS - C_GL, D_MODEL)]
        + [(w_hbm, w_ref, C_QB + b0, 0, b0, 0, W_STAGE_ROWS, D_MODEL) for b0 in range(B_QB, B_RB, W_STAGE_ROWS)],
        stage, sem)

    def proj(r0):
        rows = slice(r0, r0 + R)
        h = _prenorm(x_ref, rows, mod_ref, mrow, g_ref)
        yield
        if r0 == 0:
            need((B_GL - B_RB) // W_STAGE_ROWS)
        pr = _dot_nt(h, w_ref[B_RB:B_GL, :])
        if r0 == 0:
            need(1)
        pgl = _dot_nt(h, wgl_ref[...])
        if r0 == 0:
            need((B_RB - B_QB) // W_STAGE_ROWS)
        pb = _dot_nt(h, w_ref[B_QB:B_RB, :])
        zero = jnp.zeros(wgf_ref.shape, F32)
        wg = jnp.concatenate([jnp.concatenate([wgf_ref[...], zero], axis=1),
                              jnp.concatenate([zero, wgb_ref[...]], axis=1)], axis=0).astype(BF16)
        z = _dot(pgl.astype(BF16), wg) + jnp.concatenate([bgf_ref[...], bgb_ref[...]], axis=1)
        ls = (jnp.minimum(z, 0.0) - jnp.log(1.0 + jnp.exp(-jnp.abs(z)))) * (1.0 / GATE_NORM)
        gf_s[rows, :] = ls[:, 0:256]
        gb_s[rows, :] = ls[:, 256:512]
        rb_s[rows, :] = pr
        q_s[rows, :] = pb[:, B_QB:B_KB] * (DK_B ** -0.5)
        k_s[rows, :] = pb[:, B_KB:B_VB]
        v_s[rows, :] = pb[:, B_VB:B_RB].astype(BF16)

    def scan_stage(e, p, d, blk):
        ch = 2 * p + d
        if blk != (n_blk - 1 if d else 0):
            s = st[e, ch]
        elif has_state:
            assert n_elems == 1
            s = _state_to_t((sb_ref if d else sf_ref)[2 * p:2 * p + 2])
        else:
            s = jnp.zeros(st.shape[2:], F32)
        for i in range(cpb):
            c = blk * cpb + (cpb - 1 - i if d else i)
            rows = slice(c * C, (c + 1) * C)
            oi[e, ch, rows, :] = _dot_nt(qe[e, ch, rows, :], s.astype(BF16))
            s = s * dec[e, ch, c:c + 1, :] + ut[e, ch, c]
            yield
        if blk != (0 if d else n_blk - 1):
            st[e, ch] = s
        elif not has_state:
            dst_ref = nsb_ref if d else nsf_ref
            dst_ref[e, 2 * p] = s[0:DV_B, :].T[0:DK_B, :]
            dst_ref[e, 2 * p + 1] = s[DV_B:2 * DV_B, :].T[DK_B:2 * DK_B, :]

    def fin_stage(e, p, blk):
        rows = slice(blk * R, (blk + 1) * R)
        out_rows = slice(e * L + blk * R, e * L + (blk + 1) * R)
        o = ((oa[e, 2 * p, rows, :] + oi[e, 2 * p, rows, :])
             + (oa[e, 2 * p + 1, rows, :] + oi[e, 2 * p + 1, rows, :]))
        for j in range(2):
            cols = slice((2 * p + j) * DV_B, (2 * p + j + 1) * DV_B)
            oj = _rms(o[:, j * DV_B:(j + 1) * DV_B], gn_ref[...])
            rj = rb_s[out_rows, cols]
            og_ref[out_rows, cols] = (oj * (rj * jax.nn.sigmoid(rj))).astype(BF16)
        yield

    vt_cache = {}
    projs, blocks, scans, deps = {}, [], {}, {}
    for e in range(n_elems):
        for t in range(n_blk):
            blk = t // 2 if t % 2 == 0 else n_blk - 1 - t // 2
            projs[e, blk] = proj(e * L + blk * R)
    for e in range(n_elems):
        for t in range(n_blk):
            for p in range(n_pairs):
                for d in range(2):
                    blk = n_blk - 1 - t if d else t
                    ch = 2 * p + d
                    in_rows = slice(e * L + blk * R, e * L + (blk + 1) * R)
                    g = _gla_block_stages(
                        q_s, k_s, gb_s if d else gf_s, v_s, in_rows,
                        slice(p * 2 * DK_B, (p + 1) * 2 * DK_B), slice(p * 2 * DV_B, (p + 1) * 2 * DV_B),
                        vt_cache, blk, bool(d), oa.at[e, ch], qe.at[e, ch], ut.at[e, ch], dec.at[e, ch])
                    deps[id(g)] = [projs[e, blk]]
                    blocks.append(g)
                    s = scan_stage(e, p, d, blk)
                    prev = scans.get((e, p, d, blk + 1 if d else blk - 1))
                    deps[id(s)] = [g] + ([prev] if prev is not None else [])
                    scans[e, p, d, blk] = s
    aux = []
    for e in range(n_elems):
        for t in range(n_blk):
            for p in range(n_pairs):
                aux.append(scans[e, p, 0, t])
                aux.append(scans[e, p, 1, n_blk - 1 - t])
        for blk in range(n_blk):
            for p in range(n_pairs):
                f = fin_stage(e, p, blk)
                deps[id(f)] = [scans[e, p, 0, blk], scans[e, p, 1, blk]]
                aux.append(f)
    _run_skewed(list(projs.values()) + blocks, aux, deps)
    assert all_taken()


def _gla(x2d, mods, g_pre, w_in_f32, wgf, wgb, bgf, bgb, gla_norm, state_f, state_b, batch, seq_len):
    has_state = state_f is not None
    n_chunks = seq_len // GLA_CHUNK
    L = seq_len
    rows = MIX_TILE
    n_elems = rows // L
    n_ch = H_B
    wk, wv = H_B * DK_B, H_B * DV_B
    const = lambda i: (0, 0)
    in_specs = [
        pl.BlockSpec((rows, D_MODEL), lambda i: (i, 0)),
        pl.BlockSpec(mods.shape, const),
        pl.BlockSpec((1, D_MODEL), const),
        pl.BlockSpec(memory_space=pl.ANY),
        pl.BlockSpec((GATE_RANK, wk), const),
        pl.BlockSpec((GATE_RANK, wk), const),
        pl.BlockSpec((1, wk), const),
        pl.BlockSpec((1, wk), const),
        pl.BlockSpec((1, DV_B), const),
    ]
    args = [x2d, mods, g_pre, w_in_f32, wgf, wgb, bgf, bgb, gla_norm]
    out_shape = [jax.ShapeDtypeStruct((batch * L, wv), BF16)]
    out_specs = [pl.BlockSpec((rows, wv), lambda i: (i, 0))]
    if has_state:
        assert n_elems == 1
        st_spec = pl.BlockSpec((None, None, H_B, DK_B, DV_B), lambda i: (i, 0, 0, 0, 0))
        in_specs += [st_spec, st_spec]
        args += [state_f, state_b]
    else:
        for _ in range(2):
            out_shape.append(jax.ShapeDtypeStruct((batch, 1, H_B, DK_B, DV_B), F32))
            out_specs.append(pl.BlockSpec((n_elems, None, H_B, DK_B, DV_B), lambda i: (i, 0, 0, 0, 0)))
    return pl.pallas_call(
        functools.partial(_gla_kernel, n_chunks=n_chunks, n_elems=n_elems, has_state=has_state,
                          ctx_row=mods.shape[0] // 2),
        out_shape=out_shape,
        grid=(batch // n_elems,),
        in_specs=in_specs,
        out_specs=out_specs,
        scratch_shapes=[
            pltpu.VMEM((rows, wk), F32),
            pltpu.VMEM((rows, wk), F32),
            pltpu.VMEM((rows, wk), F32),
            pltpu.VMEM((rows, wk), F32),
            pltpu.VMEM((rows, wv), F32),
            pltpu.VMEM((rows, wv), BF16),
            pltpu.VMEM((n_elems, n_ch, L, 2 * DV_B), F32),
            pltpu.VMEM((n_elems, n_ch, L, 2 * DV_B), F32),
            pltpu.VMEM((n_elems, n_ch, 2 * DV_B, 2 * DK_B), F32),
            pltpu.VMEM((n_elems, n_ch, L, 2 * DK_B), BF16),
            pltpu.VMEM((n_elems, n_ch, n_chunks, 2 * DV_B, 2 * DK_B), F32),
            pltpu.VMEM((n_elems, n_ch, max(n_chunks, 8), 2 * DK_B), F32),
            pltpu.VMEM((B_GL, D_MODEL), BF16),
            pltpu.VMEM((IN_COLS - C_GL, D_MODEL), BF16),
            pltpu.VMEM((W_STAGE_SLOTS, W_STAGE_ROWS, D_MODEL), F32),
            pltpu.SemaphoreType.DMA((W_STAGE_SLOTS,)),
        ],
        compiler_params=pltpu.CompilerParams(
            dimension_semantics=("arbitrary",), vmem_limit_bytes=VMEM_LIMIT),
        name="gla_state" if has_state else "gla_ctx",
    )(*args)


def _out_mlp_kernel(xc_ref, oac_ref, ogc_ref, xl_ref, oal_ref, ogl_ref, mod_ref, wo_hbm, gpost_ref, gmpre_ref,
                    gmpost_ref, w1_hbm, w2_hbm, yc_ref, yl_ref, wo_ref, w1_ref, w2_ref, stage, sem, *,
                    n_ctx_tiles, lat_tiles_per_sample, ctx_row):
    rc = ROW_CHUNK
    tf = 1024
    per_win = tf // W_STAGE_ROWS

    def weight_windows():
        windows = [(wo_hbm, wo_ref, r0, 0, r0, 0, W_STAGE_ROWS, D_MODEL) for r0 in range(0, D_MODEL, W_STAGE_ROWS)]
        for j in range(D_FF // tf):
            windows += [(w1_hbm, w1_ref, r0, j * tf, r0, j * tf, W_STAGE_ROWS, tf)
                        for r0 in range(0, D_MODEL, W_STAGE_ROWS)]
            windows += [(w2_hbm, w2_ref, j * tf + r0, 0, j * tf + r0, 0, W_STAGE_ROWS, D_MODEL)
                        for r0 in range(0, tf, W_STAGE_ROWS)]
        return windows

    def run(x_ref, oa_ref, og_ref, y_ref, mrow, take=None):
        def chunk(r0):
            need = take if (take is not None and r0 == 0) else (lambda k: None)
            rows = slice(r0, r0 + rc)
            need(D_MODEL // W_STAGE_ROWS)
            mix = _dot(oa_ref[rows, :], wo_ref[0:512, :]) + _dot(og_ref[rows, :], wo_ref[512:1024, :])
            yield
            x1 = x_ref[rows, :] + _mod(mod_ref, mrow, 2) * _rms(mix, gpost_ref[...])
            h2 = (_rms(x1, gmpre_ref[...]) * (1.0 + _mod(mod_ref, mrow, 4)) + _mod(mod_ref, mrow, 3)).astype(BF16)
            yield
            acc = None
            for j in range(D_FF // tf):
                need(D_MODEL // W_STAGE_ROWS + per_win)
                u = jnp.maximum(_dot(h2, w1_ref[:, j * tf:(j + 1) * tf]), 0.0)
                part = _dot((u * u).astype(BF16), w2_ref[j * tf:(j + 1) * tf, :])
                acc = part if acc is None else acc + part
                yield
            y_ref[rows, :] = x1 + _mod(mod_ref, mrow, 5) * _rms(acc, gmpost_ref[...])

        _run_skewed([chunk(r0) for r0 in range(0, x_ref.shape[0], rc)])

    step = pl.program_id(0)
    is_ctx = step < n_ctx_tiles

    @pl.when(step == 0)
    def _():
        take, all_taken = _weight_stream(weight_windows(), stage, sem)
        run(xc_ref, oac_ref, ogc_ref, yc_ref, ctx_row, take)
        assert all_taken()

    @pl.when(jnp.logical_and(step > 0, is_ctx))
    def _():
        run(xc_ref, oac_ref, ogc_ref, yc_ref, ctx_row)

    @pl.when(jnp.logical_not(is_ctx))
    def _():
        run(xl_ref, oal_ref, ogl_ref, yl_ref, (step - n_ctx_tiles) // lat_tiles_per_sample)


def _out_mlp(xc2d, oac, ogc, xl2d, oal, ogl, mods, lat_seq, w_out, g_post, g_mpre, g_mpost, w1, w2):
    tm = 512
    n_c, n_l = xc2d.shape[0] // tm, xl2d.shape[0] // tm
    per_b = lat_seq // tm
    ctx_row = lambda i: (jnp.minimum(i, n_c - 1), 0)
    lat_row = lambda i: (jnp.maximum(i - n_c, 0), 0)
    const = lambda i: (0, 0)
    return pl.pallas_call(
        functools.partial(_out_mlp_kernel, n_ctx_tiles=n_c, lat_tiles_per_sample=per_b, ctx_row=mods.shape[0] // 2),
        out_shape=[jax.ShapeDtypeStruct(xc2d.shape, F32), jax.ShapeDtypeStruct(xl2d.shape, F32)],
        grid=(n_c + n_l,),
        in_specs=[
            pl.BlockSpec((tm, D_MODEL), ctx_row),
            pl.BlockSpec((tm, 512), ctx_row),
            pl.BlockSpec((tm, 512), ctx_row),
            pl.BlockSpec((tm, D_MODEL), lat_row),
            pl.BlockSpec((tm, 512), lat_row),
            pl.BlockSpec((tm, 512), lat_row),
            pl.BlockSpec(mods.shape, const),
            pl.BlockSpec(memory_space=pl.ANY),
            pl.BlockSpec((1, D_MODEL), const),
            pl.BlockSpec((1, D_MODEL), const),
            pl.BlockSpec((1, D_MODEL), const),
            pl.BlockSpec(memory_space=pl.ANY),
            pl.BlockSpec(memory_space=pl.ANY),
        ],
        out_specs=[pl.BlockSpec((tm, D_MODEL), ctx_row), pl.BlockSpec((tm, D_MODEL), lat_row)],
        scratch_shapes=[
            pltpu.VMEM((D_MODEL, D_MODEL), BF16),
            pltpu.VMEM((D_MODEL, D_FF), BF16),
            pltpu.VMEM((D_FF, D_MODEL), BF16),
            pltpu.VMEM((W_STAGE_SLOTS, W_STAGE_ROWS, D_MODEL), F32),
            pltpu.SemaphoreType.DMA((W_STAGE_SLOTS,)),
        ],
        compiler_params=pltpu.CompilerParams(
            dimension_semantics=("arbitrary",), vmem_limit_bytes=VMEM_LIMIT),
        name="out_mlp",
    )(xc2d, oac, ogc, xl2d, oal, ogl, mods, w_out, g_post, g_mpre, g_mpost, w1, w2)


def _rope_tables(n_lat):
    pos = np.arange(n_lat)
    row_pos = (pos // GRID_W).astype(np.float64)
    col_pos = (pos % GRID_W).astype(np.float64)
    half = DK_A // 4
    inv = ROPE_BASE ** (-np.arange(half, dtype=np.float64) / half)
    lane = np.arange(128)
    in64 = lane % 64
    in32 = in64 % 32
    p = np.where((in64 < 32)[None, :], row_pos[:, None], col_pos[:, None])
    ang = p * inv[in32 % half][None, :]
    cos, sin = np.cos(ang), np.sin(ang)
    first = (in32 < half)[None, :]
    sa = np.where(first, -sin, 0.0)
    sb = np.where(first, 0.0, sin)
    return tuple(jnp.asarray(t, dtype=F32) for t in (cos, sa, sb))


def _mixers(x, mods, w, rope_tabs, cache_k, cache_v, state_f, state_b):
    batch, seq_len, _ = x.shape
    x2d = x.reshape(batch * seq_len, D_MODEL)
    aouts = _attention(x2d, mods, w["g_pre"], w["w_in"], rope_tabs, cache_k, cache_v, w["lam"], w["diff_norm"],
                       batch, seq_len)
    gouts = _gla(x2d, mods, w["g_pre"], w["w_in"], w["wgf"], w["wgb"], w["bgf"], w["bgb"], w["gla_norm"],
                 state_f, state_b, batch, seq_len)
    return x2d, aouts[0], gouts[0], tuple(aouts[1:]) + tuple(gouts[1:])


def kernel(x_prompt, x_sample, c, cache_k, cache_v, state_fwd, state_bwd, c_ctx, w_ada, b_ada,
           norm_attn_pre, norm_attn_post, norm_mlp_pre, norm_mlp_post, w_in, w_gate_fwd, b_gate_fwd,
           w_gate_bwd, b_gate_bwd, lam_q1, lam_k1, lam_q2, lam_k2, diff_norm, gla_norm, w_out,
           w_mlp1, w_mlp2):
    m = _adaln(c, c_ctx[None, :], w_ada[0], b_ada)

    w = {
        "g_pre": norm_attn_pre[0][None, :],
        "g_post": norm_attn_post[0][None, :],
        "g_mpre": norm_mlp_pre[0][None, :],
        "g_mpost": norm_mlp_post[0][None, :],
        "w_in": jnp.transpose(w_in[0]),
        "wgf": w_gate_fwd[0],
        "wgb": w_gate_bwd[0],
        "bgf": b_gate_fwd,
        "bgb": b_gate_bwd,
        "lam": (lam_q1, lam_k1, lam_q2, lam_k2),
        "diff_norm": diff_norm[0][None, :],
        "gla_norm": gla_norm[0][None, :],
        "w_out": w_out[0],
        "w1": w_mlp1[0],
        "w2": w_mlp2[0],
    }
    xc2d, oac, ogc, (new_k, new_v, new_sf, new_sb) = _mixers(x_prompt, m, w, None, None, None, None, None)
    xl2d, oal, ogl, _ = _mixers(x_sample, m, w, _rope_tables(x_sample.shape[1]),
                                cache_k, cache_v, state_fwd, state_bwd)
    y_prompt, y_sample = _out_mlp(xc2d, oac, ogc, xl2d, oal, ogl, m, x_sample.shape[1],
                                  w["w_out"], w["g_post"], w["g_mpre"], w["g_mpost"], w["w1"], w["w2"])
    return (y_prompt.reshape(x_prompt.shape), y_sample.reshape(x_sample.shape), new_k, new_v, new_sf, new_sb)
```

```python
import functools
import math

import numpy as np
import jax
import jax.numpy as jnp
from jax import lax
from jax.experimental import pallas as pl
from jax.experimental.pallas import tpu as pltpu

F32 = jnp.float32
BF16 = jnp.bfloat16

D_MODEL = 1024
GRID_W = 64
H_A = 4
DV_A = 128
DK_A = 64
H_B = 4
DV_B = 128
DK_B = 64
GATE_RANK = 16
GATE_NORM = 16.0
GLA_CHUNK = 64
D_FF = 4 * D_MODEL
N_MOD = 6
ROPE_BASE = 10000.0
EPS = 1e-6
LAM_INIT = 0.8 - 0.6 * math.exp(-0.3 * 0)

C_QA, C_KA, C_VA, C_QB, C_KB, C_VB, C_RB, C_GL = 0, 512, 1024, 1536, 1792, 2048, 2560, 3072
IN_COLS = 3104
B_QB, B_KB, B_VB, B_RB, B_GL, B_COLS = (c - C_QB for c in (C_QB, C_KB, C_VB, C_RB, C_GL, IN_COLS))

V7X_VMEM_BYTES = 64 * 1024 * 1024
VMEM_LIMIT = V7X_VMEM_BYTES - 4 * 1024 * 1024
ROW_CHUNK = 256
MIX_TILE = 1024
W_STAGE_ROWS = 512
W_STAGE_SLOTS = 4


def _rms(x, g):
    return x * lax.rsqrt(jnp.mean(x * x, axis=-1, keepdims=True) + EPS) * g


def _dot(a, b):
    return jnp.dot(a, b, preferred_element_type=F32)


def _dot_nt(a, b):
    return lax.dot_general(a, b, (((1,), (1,)), ((), ())), preferred_element_type=F32)


def _run_skewed(gens, secondary=(), deps=None):
    deps = deps or {}
    queues, running, done = [list(gens), list(secondary)], [], set()
    while any(queues) or running:
        for queue in queues:
            for g in queue:
                if all(id(d) in done for d in deps.get(id(g), ())):
                    queue.remove(g)
                    running.append(g)
                    break
        assert running, "dependency cycle"
        for g in list(running):
            try:
                next(g)
            except StopIteration:
                running.remove(g)
                done.add(id(g))


def _prenorm(x_ref, rows, mod_ref, g_ref):
    x = x_ref[rows, :]
    return (_rms(x, g_ref[...]) * (1.0 + mod_ref[1:2, :]) + mod_ref[0:1, :]).astype(BF16)


def _adaln_kernel(c_ref, w_ref, b_ref, o_ref):
    c = c_ref[...]
    s = c * jax.nn.sigmoid(c)
    o_ref[...] = _dot(s.astype(BF16), w_ref[...].astype(BF16)) + b_ref[...]


def _adaln(cvec, w_ada, b_ada):
    rows = cvec.shape[0]
    n = w_ada.shape[1]
    tn = 1536
    return pl.pallas_call(
        _adaln_kernel,
        out_shape=jax.ShapeDtypeStruct((rows, n), F32),
        grid=(n // tn,),
        in_specs=[
            pl.BlockSpec((rows, D_MODEL), lambda j: (0, 0)),
            pl.BlockSpec((D_MODEL, tn), lambda j: (0, j)),
            pl.BlockSpec((1, tn), lambda j: (0, j)),
        ],
        out_specs=pl.BlockSpec((rows, tn), lambda j: (0, j)),
        compiler_params=pltpu.CompilerParams(
            dimension_semantics=("arbitrary",), vmem_limit_bytes=VMEM_LIMIT),
        name="adaln",
    )(cvec, w_ada, b_ada)


def _attn_kernel(*refs, cached, seq):
    if cached:
        (x_ref, mod_ref, g_ref, w_ref, cos_ref, sa_ref, sb_ref, ck_ref, cv_ref, lam_ref, dn_ref,
         o_ref, qkv_s) = refs
    else:
        x_ref, mod_ref, g_ref, w_ref, lam_ref, dn_ref, o_ref, nk_ref, nv_ref, qkv_s = refs
        assert seq == ROW_CHUNK
    rc = ROW_CHUNK
    n_rows = x_ref.shape[0]
    lp = lam_ref[...]
    lam = (jnp.exp(jnp.sum(lp[0:1] * lp[1:2], axis=-1, keepdims=True))
           - jnp.exp(jnp.sum(lp[2:3] * lp[3:4], axis=-1, keepdims=True)) + LAM_INIT)
    lane = lax.broadcasted_iota(jnp.int32, (1, 128), 1)
    scale = DK_A ** -0.5
    m1 = jnp.where(lane < DK_A, scale, 0.0).astype(BF16)
    m2 = jnp.where(lane >= DK_A, scale, 0.0).astype(BF16)
    n_keys = (n_rows + ck_ref.shape[1]) if cached else rc
    ones = jnp.ones((n_keys, 128), BF16)

    def proj(r0):
        rows = slice(r0, r0 + rc)
        h = _prenorm(x_ref, rows, mod_ref, g_ref)
        yield
        pa = _dot(h, w_ref[:, C_QA:C_VA])
        pv = _dot(h, w_ref[:, C_VA:C_QB])
        for blk in range(8):
            t = pa[:, blk * 128:(blk + 1) * 128]
            if cached:
                t = (t * cos_ref[rows, :] + pltpu.roll(t, 112, 1) * sa_ref[rows, :]
                     + pltpu.roll(t, 16, 1) * sb_ref[rows, :])
            elif blk >= 4:
                nk_ref[r0 // seq, blk - 4] = t
            qkv_s[rows, blk * 128:(blk + 1) * 128] = t.astype(BF16)
        qkv_s[rows, C_VA:C_QB] = pv.astype(BF16)
        if not cached:
            for hh in range(H_A):
                nv_ref[r0 // seq, hh] = pv[:, hh * 128:(hh + 1) * 128]

    kv = {}

    def chain(r0, h):
        rows = slice(r0, r0 + rc)
        cols = slice(h * 128, (h + 1) * 128)
        kcols = slice(C_KA + h * 128, C_KA + (h + 1) * 128)
        vcols = slice(C_VA + h * 128, C_VA + (h + 1) * 128)
        key = h if cached else (h, r0)
        if key not in kv:
            if cached:
                k = jnp.concatenate([ck_ref[h].astype(BF16), qkv_s[:, kcols]], axis=0)
                v = jnp.concatenate([cv_ref[h].astype(BF16), qkv_s[:, vcols]], axis=0)
            else:
                k, v = qkv_s[rows, kcols], qkv_s[rows, vcols]
            kv[key] = (k, jnp.concatenate([v, ones], axis=1))
        k, v1 = kv[key]
        q = qkv_s[rows, cols]
        s = _dot_nt(jnp.concatenate([q * m1, q * m2], axis=0), k)
        yield
        e = jnp.exp(s - jnp.max(s, axis=-1, keepdims=True)).astype(BF16)
        yield
        r = _dot(e, v1)
        yield
        r1, r2 = r[0:rc], r[rc:2 * rc]
        o = r1[:, 0:128] / r1[:, 128:256] - lam * (r2[:, 0:128] / r2[:, 128:256])
        o_ref[rows, cols] = (_rms(o, dn_ref[...]) * (1.0 - LAM_INIT)).astype(BF16)

    projs = {r0: proj(r0) for r0 in range(0, n_rows, rc)}
    gens, deps = list(projs.values()), {}
    for r0 in projs:
        for h in range(H_A):
            c = chain(r0, h)
            deps[id(c)] = list(projs.values()) if cached else [projs[r0]]
            gens.append(c)
    _run_skewed(gens, deps=deps)


def _attention(x2d, mods, g_pre, w_in_b, rope_tabs, cache_k, cache_v, lam_p, diff_norm, batch, seq_len):
    cached = cache_k is not None
    tq = MIX_TILE
    n_tiles = batch * seq_len // tq
    const = lambda i: (0, 0)
    in_specs = [
        pl.BlockSpec((tq, D_MODEL), lambda i: (i, 0)),
        pl.BlockSpec((None, N_MOD, D_MODEL), (lambda i: (i, 0, 0)) if cached else (lambda i: (0, 0, 0))),
        pl.BlockSpec((1, D_MODEL), const),
        pl.BlockSpec((D_MODEL, C_QB), const, pipeline_mode=pl.Buffered(1)),
    ]
    args = [x2d, mods, g_pre, w_in_b]
    out_shape = [jax.ShapeDtypeStruct((batch * seq_len, H_A * DV_A), BF16)]
    out_specs = [pl.BlockSpec((tq, H_A * DV_A), lambda i: (i, 0))]
    if cached:
        assert seq_len == tq and mods.shape[0] == batch
        past = cache_k.shape[3]
        cspec = pl.BlockSpec((None, None, H_A, past, 128), lambda i: (i, 0, 0, 0, 0))
        in_specs += [pl.BlockSpec((tq, 128), const)] * 3 + [cspec, cspec]
        args += list(rope_tabs) + [cache_k, cache_v]
    else:
        assert mods.shape[0] == 1
        nb = tq // seq_len
        for _ in range(2):
            out_shape.append(jax.ShapeDtypeStruct((batch, 1, H_A, seq_len, 128), F32))
            out_specs.append(pl.BlockSpec((nb, None, H_A, seq_len, 128), lambda i: (i, 0, 0, 0, 0)))
    in_specs += [pl.BlockSpec((4, DK_A), const), pl.BlockSpec((1, DV_A), const)]
    args += [lam_p, diff_norm]
    return pl.pallas_call(
        functools.partial(_attn_kernel, cached=cached, seq=seq_len),
        out_shape=out_shape,
        grid=(n_tiles,),
        in_specs=in_specs,
        out_specs=out_specs,
        scratch_shapes=[pltpu.VMEM((tq, C_QB), BF16)],
        compiler_params=pltpu.CompilerParams(
            dimension_semantics=("arbitrary",), vmem_limit_bytes=VMEM_LIMIT),
        name="attn_cached" if cached else "attn_ctx",
    )(*args)


def _split3(g):
    hi = g.astype(BF16)
    r1 = g - hi.astype(F32)
    mid = r1.astype(BF16)
    lo = (r1 - mid.astype(F32)).astype(BF16)
    return hi, mid, lo


def _gla_block_stages(q_ref, k_ref, g_ref, v_ref, in_rows, qc, vc, vt_cache, blk, reverse, oa_s, qe_s, ut_s, dec_s):
    C = GLA_CHUNK
    R = in_rows.stop - in_rows.start
    cpb = R // C
    rows = slice(blk * R, (blk + 1) * R)
    ri = lax.broadcasted_iota(jnp.int32, (R, R), 0)
    ci = lax.broadcasted_iota(jnp.int32, (R, R), 1)
    same_chunk = jnp.right_shift(ri, 6) == jnp.right_shift(ci, 6)
    if reverse:
        keep = same_chunk & (ci >= ri)
        last, ref = 0, C // 2
    else:
        keep = same_chunk & (ci <= ri)
        last, ref = C - 1, C // 2 - 1
    tri = jnp.where(keep, 1.0, 0.0).astype(BF16)
    lane = lax.broadcasted_iota(jnp.int32, (1, 2 * DK_B), 1)
    h0 = jnp.where(lane < DK_B, 1.0, 0.0).astype(BF16)
    h1 = jnp.where(lane >= DK_B, 1.0, 0.0).astype(BF16)

    hi, mid, lo = _split3(g_ref[in_rows, qc])
    b3 = _dot(tri, jnp.concatenate([hi, mid, lo], axis=1))
    yield
    b = b3[:, 0:128] + b3[:, 128:256] + b3[:, 256:384]
    q = q_ref[in_rows, qc]
    k = k_ref[in_rows, qc]
    qt, kt, qe, kd = [], [], [], []
    zeros = jnp.zeros((C, 2 * DK_B), BF16)
    for c in range(cpb):
        s = slice(c * C, (c + 1) * C)
        bc = b[s]
        bm = bc[ref:ref + 1]
        bl = bc[last:last + 1]
        qt.append((q[s] * jnp.exp(bc - bm)).astype(BF16))
        kt.append((k[s] * jnp.exp(bm - bc)).astype(BF16))
        qe.append((q[s] * jnp.exp(bc)).astype(BF16))
        kdc = (k[s] * jnp.exp(bl - bc)).astype(BF16)
        kd.append(jnp.concatenate([kdc if j == c else zeros for j in range(cpb)], axis=1))
        dec_s[blk * cpb + c:blk * cpb + c + 1, :] = jnp.exp(bl)
    qe_s[rows, :] = jnp.concatenate(qe, axis=0)
    qt = jnp.concatenate(qt, axis=0)
    kt = jnp.concatenate(kt, axis=0)
    yield
    s0 = _dot_nt(qt * h0, kt)
    s1 = _dot_nt(qt * h1, kt)
    v = v_ref[in_rows, vc]
    if (in_rows.start, vc.start) not in vt_cache:
        vt_cache[in_rows.start, vc.start] = v.T
    ut = _dot(vt_cache[in_rows.start, vc.start], jnp.concatenate(kd, axis=0))
    yield
    a0 = jnp.where(keep, s0, 0.0).astype(BF16)
    a1 = jnp.where(keep, s1, 0.0).astype(BF16)
    br = lax.broadcasted_iota(jnp.int32, ut.shape, 0)
    bcol = lax.broadcasted_iota(jnp.int32, ut.shape, 1)
    ut = jnp.where((br < DV_B) == ((bcol & (2 * DK_B - 1)) < DK_B), ut, 0.0)
    for c in range(cpb):
        ut_s[blk * cpb + c] = ut[:, c * 2 * DK_B:(c + 1) * 2 * DK_B]
    yield
    oa_s[rows, :] = jnp.concatenate([_dot(a0, v[:, 0:DV_B]), _dot(a1, v[:, DV_B:2 * DV_B])], axis=1)


def _state_to_t(s_pair):
    z = jnp.zeros((DK_B, DV_B), F32)
    a = jnp.concatenate([s_pair[0], z], axis=0).T
    b = jnp.concatenate([z, s_pair[1]], axis=0).T
    return jnp.concatenate([a, b], axis=0)


def _gla_kernel(*refs, n_chunks, n_elems, has_state):
    if has_state:
        (x_ref, mod_ref, g_ref, w_ref, wg_ref, bg_ref, gn_ref, sf_ref, sb_ref, og_ref, *scratch) = refs
    else:
        (x_ref, mod_ref, g_ref, w_ref, wg_ref, bg_ref, gn_ref, og_ref, nsf_ref, nsb_ref, *scratch) = refs
    q_s, k_s, gf_s, gb_s, rb_s, v_s, oa, oi, st, qe, ut, dec = scratch
    C = GLA_CHUNK
    R = ROW_CHUNK
    cpb = R // C
    L = n_chunks * C
    n_blk = L // R
    n_pairs = H_B // 2

    def proj(r0):
        rows = slice(r0, r0 + R)
        h = _prenorm(x_ref, rows, mod_ref, g_ref)
        yield
        pg = _dot(h, w_ref[:, C_RB:IN_COLS])
        pb = _dot(h, w_ref[:, C_QB:C_RB])
        z = _dot(pg[:, B_GL - B_RB:B_COLS - B_RB].astype(BF16), wg_ref[...]) + bg_ref[...]
        ls = (jnp.minimum(z, 0.0) - jnp.log(1.0 + jnp.exp(-jnp.abs(z)))) * (1.0 / GATE_NORM)
        gf_s[rows, :] = ls[:, 0:256]
        gb_s[rows, :] = ls[:, 256:512]
        rb_s[rows, :] = pg[:, 0:512]
        q_s[rows, :] = pb[:, B_QB:B_KB] * (DK_B ** -0.5)
        k_s[rows, :] = pb[:, B_KB:B_VB]
        v_s[rows, :] = pb[:, B_VB:B_RB].astype(BF16)

    def scan_stage(e, p, d, blk):
        ch = 2 * p + d
        if blk != (n_blk - 1 if d else 0):
            s = st[e, ch]
        elif has_state:
            assert n_elems == 1
            s = _state_to_t((sb_ref if d else sf_ref)[2 * p:2 * p + 2])
        else:
            s = jnp.zeros(st.shape[2:], F32)
        for i in range(cpb):
            c = blk * cpb + (cpb - 1 - i if d else i)
            rows = slice(c * C, (c + 1) * C)
            oi[e, ch, rows, :] = _dot_nt(qe[e, ch, rows, :], s.astype(BF16))
            s = s * dec[e, ch, c:c + 1, :] + ut[e, ch, c]
            yield
        if blk != (0 if d else n_blk - 1):
            st[e, ch] = s
        elif not has_state:
            dst_ref = nsb_ref if d else nsf_ref
            dst_ref[e, 2 * p] = s[0:DV_B, :].T[0:DK_B, :]
            dst_ref[e, 2 * p + 1] = s[DV_B:2 * DV_B, :].T[DK_B:2 * DK_B, :]

    def fin_stage(e, p, blk):
        rows = slice(blk * R, (blk + 1) * R)
        out_rows = slice(e * L + blk * R, e * L + (blk + 1) * R)
        o = ((oa[e, 2 * p, rows, :] + oi[e, 2 * p, rows, :])
             + (oa[e, 2 * p + 1, rows, :] + oi[e, 2 * p + 1, rows, :]))
        for j in range(2):
            cols = slice((2 * p + j) * DV_B, (2 * p + j + 1) * DV_B)
            oj = _rms(o[:, j * DV_B:(j + 1) * DV_B], gn_ref[...])
            rj = rb_s[out_rows, cols]
            og_ref[out_rows, cols] = (oj * (rj * jax.nn.sigmoid(rj))).astype(BF16)
        yield

    vt_cache = {}
    projs, blocks, scans, deps = {}, [], {}, {}
    for e in range(n_elems):
        for t in range(n_blk):
            blk = t // 2 if t % 2 == 0 else n_blk - 1 - t // 2
            projs[e, blk] = proj(e * L + blk * R)
    for e in range(n_elems):
        for t in range(n_blk):
            for p in range(n_pairs):
                for d in range(2):
                    blk = n_blk - 1 - t if d else t
                    ch = 2 * p + d
                    in_rows = slice(e * L + blk * R, e * L + (blk + 1) * R)
                    g = _gla_block_stages(
                        q_s, k_s, gb_s if d else gf_s, v_s, in_rows,
                        slice(p * 2 * DK_B, (p + 1) * 2 * DK_B), slice(p * 2 * DV_B, (p + 1) * 2 * DV_B),
                        vt_cache, blk, bool(d), oa.at[e, ch], qe.at[e, ch], ut.at[e, ch], dec.at[e, ch])
                    deps[id(g)] = [projs[e, blk]]
                    blocks.append(g)
                    s = scan_stage(e, p, d, blk)
                    prev = scans.get((e, p, d, blk + 1 if d else blk - 1))
                    deps[id(s)] = [g] + ([prev] if prev is not None else [])
                    scans[e, p, d, blk] = s
    aux = []
    for e in range(n_elems):
        for t in range(n_blk):
            for p in range(n_pairs):
                aux.append(scans[e, p, 0, t])
                aux.append(scans[e, p, 1, n_blk - 1 - t])
        for blk in range(n_blk):
            for p in range(n_pairs):
                f = fin_stage(e, p, blk)
                deps[id(f)] = [scans[e, p, 0, blk], scans[e, p, 1, blk]]
                aux.append(f)
    _run_skewed(list(projs.values()) + blocks, aux, deps)


def _gla(x2d, mods, g_pre, w_in_b, wg, bg, gla_norm, state_f, state_b, batch, seq_len):
    has_state = state_f is not None
    n_chunks = seq_len // GLA_CHUNK
    L = seq_len
    rows = MIX_TILE
    n_elems = rows // L
    n_ch = H_B
    wk, wv = H_B * DK_B, H_B * DV_B
    const = lambda i: (0, 0)
    in_specs = [
        pl.BlockSpec((rows, D_MODEL), lambda i: (i, 0)),
        pl.BlockSpec((None, N_MOD, D_MODEL), (lambda i: (i, 0, 0)) if has_state else (lambda i: (0, 0, 0))),
        pl.BlockSpec((1, D_MODEL), const),
        pl.BlockSpec((D_MODEL, IN_COLS), const, pipeline_mode=pl.Buffered(1)),
        pl.BlockSpec((2 * GATE_RANK, 2 * wk), const),
        pl.BlockSpec((1, 2 * wk), const),
        pl.BlockSpec((1, DV_B), const),
    ]
    args = [x2d, mods, g_pre, w_in_b, wg, bg, gla_norm]
    out_shape = [jax.ShapeDtypeStruct((batch * L, wv), BF16)]
    out_specs = [pl.BlockSpec((rows, wv), lambda i: (i, 0))]
    if has_state:
        assert n_elems == 1 and mods.shape[0] == batch
        st_spec = pl.BlockSpec((None, None, H_B, DK_B, DV_B), lambda i: (i, 0, 0, 0, 0))
        in_specs += [st_spec, st_spec]
        args += [state_f, state_b]
    else:
        assert mods.shape[0] == 1
        for _ in range(2):
            out_shape.append(jax.ShapeDtypeStruct((batch, 1, H_B, DK_B, DV_B), F32))
            out_specs.append(pl.BlockSpec((n_elems, None, H_B, DK_B, DV_B), lambda i: (i, 0, 0, 0, 0)))
    return pl.pallas_call(
        functools.partial(_gla_kernel, n_chunks=n_chunks, n_elems=n_elems, has_state=has_state),
        out_shape=out_shape,
        grid=(batch // n_elems,),
        in_specs=in_specs,
        out_specs=out_specs,
        scratch_shapes=[
            pltpu.VMEM((rows, wk), F32),
            pltpu.VMEM((rows, wk), F32),
            pltpu.VMEM((rows, wk), F32),
            pltpu.VMEM((rows, wk), F32),
            pltpu.VMEM((rows, wv), F32),
            pltpu.VMEM((rows, wv), BF16),
            pltpu.VMEM((n_elems, n_ch, L, 2 * DV_B), F32),
            pltpu.VMEM((n_elems, n_ch, L, 2 * DV_B), F32),
            pltpu.VMEM((n_elems, n_ch, 2 * DV_B, 2 * DK_B), F32),
            pltpu.VMEM((n_elems, n_ch, L, 2 * DK_B), BF16),
            pltpu.VMEM((n_elems, n_ch, n_chunks, 2 * DV_B, 2 * DK_B), F32),
            pltpu.VMEM((n_elems, n_ch, max(n_chunks, 8), 2 * DK_B), F32),
        ],
        compiler_params=pltpu.CompilerParams(
            dimension_semantics=("arbitrary",), vmem_limit_bytes=VMEM_LIMIT),
        name="gla_state" if has_state else "gla_ctx",
    )(*args)


def _weight_copy(src_hbm, r0, c0, stage, sem, slot):
    return pltpu.make_async_copy(src_hbm.at[pl.ds(r0, W_STAGE_ROWS), pl.ds(c0, D_MODEL)], stage.at[slot], sem.at[slot])


def _out_mlp_kernel(xc_ref, oac_ref, ogc_ref, xl_ref, oal_ref, ogl_ref, mod_ref, wo_hbm, gpost_ref, gmpre_ref,
                    gmpost_ref, w1_hbm, w2_hbm, yc_ref, yl_ref, wo_ref, w1_ref, w2_ref, stage, sem, *, n_ctx_tiles):
    rc = ROW_CHUNK
    tf = 1024
    n_slots = stage.shape[0]
    per_win = tf // W_STAGE_ROWS

    def weight_stream():
        windows = [(wo_hbm, wo_ref, r0, 0) for r0 in range(0, D_MODEL, W_STAGE_ROWS)]
        for j in range(D_FF // tf):
            windows += [(w1_hbm, w1_ref, r0, j * tf) for r0 in range(0, D_MODEL, W_STAGE_ROWS)]
            windows += [(w2_hbm, w2_ref, j * tf + r0, 0) for r0 in range(0, tf, W_STAGE_ROWS)]
        copies = [_weight_copy(src, r0, c0, stage, sem, n % n_slots) for n, (src, _, r0, c0) in enumerate(windows)]
        pos = {"started": 0, "taken": 0}

        def start_below(limit):
            while pos["started"] < min(limit, len(copies)):
                copies[pos["started"]].start()
                pos["started"] += 1

        def take(k):
            for _ in range(k):
                n = pos["taken"]
                _, dst, r0, c0 = windows[n]
                copies[n].wait()
                dst[r0:r0 + W_STAGE_ROWS, c0:c0 + D_MODEL] = stage[n % n_slots].astype(BF16)
                pos["taken"] += 1
                start_below(n + n_slots + 1)

        start_below(n_slots)
        return take, lambda: pos["taken"] == len(copies)

    def run(x_ref, oa_ref, og_ref, y_ref, take=None):
        def chunk(r0):
            need = take if (take is not None and r0 == 0) else (lambda k: None)
            rows = slice(r0, r0 + rc)
            need(D_MODEL // W_STAGE_ROWS)
            mix = _dot(oa_ref[rows, :], wo_ref[0:512, :]) + _dot(og_ref[rows, :], wo_ref[512:1024, :])
            yield
            x1 = x_ref[rows, :] + mod_ref[2:3, :] * _rms(mix, gpost_ref[...])
            h2 = (_rms(x1, gmpre_ref[...]) * (1.0 + mod_ref[4:5, :]) + mod_ref[3:4, :]).astype(BF16)
            yield
            acc = None
            for j in range(D_FF // tf):
                need(D_MODEL // W_STAGE_ROWS + per_win)
                u = jnp.maximum(_dot(h2, w1_ref[:, j * tf:(j + 1) * tf]), 0.0)
                part = _dot((u * u).astype(BF16), w2_ref[j * tf:(j + 1) * tf, :])
                acc = part if acc is None else acc + part
                yield
            y_ref[rows, :] = x1 + mod_ref[5:6, :] * _rms(acc, gmpost_ref[...])

        _run_skewed([chunk(r0) for r0 in range(0, x_ref.shape[0], rc)])

    step = pl.program_id(0)
    is_ctx = step < n_ctx_tiles

    @pl.when(step == 0)
    def _():
        take, all_taken = weight_stream()
        run(xc_ref, oac_ref, ogc_ref, yc_ref, take)
        assert all_taken()

    @pl.when(jnp.logical_and(step > 0, is_ctx))
    def _():
        run(xc_ref, oac_ref, ogc_ref, yc_ref)

    @pl.when(jnp.logical_not(is_ctx))
    def _():
        run(xl_ref, oal_ref, ogl_ref, yl_ref)


def _out_mlp(xc2d, oac, ogc, xl2d, oal, ogl, mods, lat_seq, w_out, g_post, g_mpre, g_mpost, w1, w2):
    tm = 512
    n_c, n_l = xc2d.shape[0] // tm, xl2d.shape[0] // tm
    per_b = lat_seq // tm
    ctx_row = lambda i: (jnp.minimum(i, n_c - 1), 0)
    lat_row = lambda i: (jnp.maximum(i - n_c, 0), 0)
    mod_idx = lambda i: (jnp.where(i < n_c, 0, 1 + jnp.maximum(i - n_c, 0) // per_b), 0, 0)
    const = lambda i: (0, 0)
    return pl.pallas_call(
        functools.partial(_out_mlp_kernel, n_ctx_tiles=n_c),
        out_shape=[jax.ShapeDtypeStruct(xc2d.shape, F32), jax.ShapeDtypeStruct(xl2d.shape, F32)],
        grid=(n_c + n_l,),
        in_specs=[
            pl.BlockSpec((tm, D_MODEL), ctx_row),
            pl.BlockSpec((tm, 512), ctx_row),
            pl.BlockSpec((tm, 512), ctx_row),
            pl.BlockSpec((tm, D_MODEL), lat_row),
            pl.BlockSpec((tm, 512), lat_row),
            pl.BlockSpec((tm, 512), lat_row),
            pl.BlockSpec((None, N_MOD, D_MODEL), mod_idx),
            pl.BlockSpec(memory_space=pl.ANY),
            pl.BlockSpec((1, D_MODEL), const),
            pl.BlockSpec((1, D_MODEL), const),
            pl.BlockSpec((1, D_MODEL), const),
            pl.BlockSpec(memory_space=pl.ANY),
            pl.BlockSpec(memory_space=pl.ANY),
        ],
        out_specs=[pl.BlockSpec((tm, D_MODEL), ctx_row), pl.BlockSpec((tm, D_MODEL), lat_row)],
        scratch_shapes=[
            pltpu.VMEM((D_MODEL, D_MODEL), BF16),
            pltpu.VMEM((D_MODEL, D_FF), BF16),
            pltpu.VMEM((D_FF, D_MODEL), BF16),
            pltpu.VMEM((W_STAGE_SLOTS, W_STAGE_ROWS, D_MODEL), F32),
            pltpu.SemaphoreType.DMA((W_STAGE_SLOTS,)),
        ],
        compiler_params=pltpu.CompilerParams(
            dimension_semantics=("arbitrary",), vmem_limit_bytes=VMEM_LIMIT),
        name="out_mlp",
    )(xc2d, oac, ogc, xl2d, oal, ogl, mods, w_out, g_post, g_mpre, g_mpost, w1, w2)


def _rope_tables(n_lat):
    pos = np.arange(n_lat)
    row_pos = (pos // GRID_W).astype(np.float64)
    col_pos = (pos % GRID_W).astype(np.float64)
    half = DK_A // 4
    inv = ROPE_BASE ** (-np.arange(half, dtype=np.float64) / half)
    lane = np.arange(128)
    in64 = lane % 64
    in32 = in64 % 32
    p = np.where((in64 < 32)[None, :], row_pos[:, None], col_pos[:, None])
    ang = p * inv[in32 % half][None, :]
    cos, sin = np.cos(ang), np.sin(ang)
    first = (in32 < half)[None, :]
    sa = np.where(first, -sin, 0.0)
    sb = np.where(first, 0.0, sin)
    return tuple(jnp.asarray(t, dtype=F32) for t in (cos, sa, sb))


def _mixers(x, mods, w, rope_tabs, cache_k, cache_v, state_f, state_b):
    batch, seq_len, _ = x.shape
    x2d = x.reshape(batch * seq_len, D_MODEL)
    aouts = _attention(x2d, mods, w["g_pre"], w["w_in"], rope_tabs, cache_k, cache_v, w["lam"], w["diff_norm"],
                       batch, seq_len)
    gouts = _gla(x2d, mods, w["g_pre"], w["w_in"], w["wg"], w["bg"], w["gla_norm"], state_f, state_b,
                 batch, seq_len)
    return x2d, aouts[0], gouts[0], tuple(aouts[1:]) + tuple(gouts[1:])


def kernel(x_prompt, x_sample, c, cache_k, cache_v, state_fwd, state_bwd, c_ctx, w_ada, b_ada,
           norm_attn_pre, norm_attn_post, norm_mlp_pre, norm_mlp_post, w_in, w_gate_fwd, b_gate_fwd,
           w_gate_bwd, b_gate_bwd, lam_q1, lam_k1, lam_q2, lam_k2, diff_norm, gla_norm, w_out,
           w_mlp1, w_mlp2):
    dec_batch = c.shape[0]
    rows = 16
    cvec = jnp.zeros((rows, D_MODEL), F32).at[0].set(c_ctx).at[1:1 + dec_batch].set(c)
    m = _adaln(cvec, w_ada[0], b_ada[0][None, :])
    mods_ctx = m[0:1].reshape(1, N_MOD, D_MODEL)
    mods_lat = m[1:1 + dec_batch].reshape(dec_batch, N_MOD, D_MODEL)

    wg = jnp.zeros((2 * GATE_RANK, 512), F32)
    wg = wg.at[0:GATE_RANK, 0:256].set(w_gate_fwd[0]).at[GATE_RANK:2 * GATE_RANK, 256:512].set(w_gate_bwd[0])
    w = {
        "g_pre": norm_attn_pre[0][None, :],
        "g_post": norm_attn_post[0][None, :],
        "g_mpre": norm_mlp_pre[0][None, :],
        "g_mpost": norm_mlp_post[0][None, :],
        "w_in": w_in[0].astype(BF16),
        "wg": wg.astype(BF16),
        "bg": jnp.concatenate([b_gate_fwd[0], b_gate_bwd[0]])[None, :],
        "lam": jnp.stack([lam_q1[0], lam_k1[0], lam_q2[0], lam_k2[0]]),
        "diff_norm": diff_norm[0][None, :],
        "gla_norm": gla_norm[0][None, :],
        "w_out": w_out[0],
        "w1": w_mlp1[0],
        "w2": w_mlp2[0],
    }
    xc2d, oac, ogc, (new_k, new_v, new_sf, new_sb) = _mixers(x_prompt, mods_ctx, w, None, None, None, None, None)
    xl2d, oal, ogl, _ = _mixers(x_sample, mods_lat, w, _rope_tables(x_sample.shape[1]),
                                cache_k, cache_v, state_fwd, state_bwd)
    mods_all = m[0:1 + dec_batch].reshape(1 + dec_batch, N_MOD, D_MODEL)
    y_prompt, y_sample = _out_mlp(xc2d, oac, ogc, xl2d, oal, ogl, mods_all, x_sample.shape[1],
                                  w["w_out"], w["g_post"], w["g_mpre"], w["g_mpost"], w["w1"], w["w2"])
    return (y_prompt.reshape(x_prompt.shape), y_sample.reshape(x_sample.shape), new_k, new_v, new_sf, new_sb)
```

```python
import functools
import math

import numpy as np
import jax
import jax.numpy as jnp
from jax import lax
from jax.experimental import pallas as pl
from jax.experimental.pallas import tpu as pltpu

F32 = jnp.float32
BF16 = jnp.bfloat16

D_MODEL = 1024
GRID_W = 64
H_A = 4
DV_A = 128
DK_A = 64
H_B = 4
DV_B = 128
DK_B = 64
GATE_RANK = 16
GATE_NORM = 16.0
GLA_CHUNK = 64
D_FF = 4 * D_MODEL
N_MOD = 6
ROPE_BASE = 10000.0
EPS = 1e-6
LAM_INIT = 0.8 - 0.6 * math.exp(-0.3 * 0)

C_QA, C_KA, C_VA, C_QB, C_KB, C_VB, C_RB, C_GL = 0, 512, 1024, 1536, 1792, 2048, 2560, 3072
IN_COLS = 3104
B_QB, B_KB, B_VB, B_RB, B_GL = (c - C_QB for c in (C_QB, C_KB, C_VB, C_RB, C_GL))

V7X_VMEM_BYTES = 64 * 1024 * 1024
VMEM_LIMIT = V7X_VMEM_BYTES - 4 * 1024 * 1024
ROW_CHUNK = 256
MIX_TILE = 1024
W_STAGE_ROWS = 512
W_STAGE_SLOTS = 4


def _rms(x, g):
    return x * lax.rsqrt(jnp.mean(x * x, axis=-1, keepdims=True) + EPS) * g


def _dot(a, b):
    return jnp.dot(a, b, preferred_element_type=F32)


def _dot_nt(a, b):
    return lax.dot_general(a, b, (((1,), (1,)), ((), ())), preferred_element_type=F32)


def _run_skewed(gens, secondary=(), deps=None):
    deps = deps or {}
    queues, running, done = [list(gens), list(secondary)], [], set()
    while any(queues) or running:
        for queue in queues:
            for g in queue:
                if all(id(d) in done for d in deps.get(id(g), ())):
                    queue.remove(g)
                    running.append(g)
                    break
        assert running, "dependency cycle"
        for g in list(running):
            try:
                next(g)
            except StopIteration:
                running.remove(g)
                done.add(id(g))


def _mod(mod_ref, mrow, k):
    row = slice(mrow, mrow + 1) if isinstance(mrow, int) else pl.ds(mrow, 1)
    return mod_ref[row, k * D_MODEL:(k + 1) * D_MODEL]


def _prenorm(x_ref, rows, mod_ref, mrow, g_ref):
    x = x_ref[rows, :]
    return (_rms(x, g_ref[...]) * (1.0 + _mod(mod_ref, mrow, 1)) + _mod(mod_ref, mrow, 0)).astype(BF16)


def _weight_stream(windows, stage, sem):
    n_slots = stage.shape[0]

    def copy(n):
        src, _, r0, c0, _, _, n_rows, width = windows[n]
        slot = n % n_slots
        return pltpu.make_async_copy(src.at[pl.ds(r0, n_rows), pl.ds(c0, width)],
                                     stage.at[slot, pl.ds(0, n_rows), pl.ds(0, width)], sem.at[slot])

    pos = {"started": 0, "taken": 0}

    def start_below(limit):
        while pos["started"] < min(limit, len(windows)):
            copy(pos["started"]).start()
            pos["started"] += 1

    def take(k):
        for _ in range(k):
            n = pos["taken"]
            _, dst, _, _, r0, d0, n_rows, width = windows[n]
            copy(n).wait()
            dst[r0:r0 + n_rows, d0:d0 + width] = stage[n % n_slots, 0:n_rows, 0:width].astype(BF16)
            pos["taken"] += 1
            start_below(n + n_slots + 1)

    start_below(n_slots)
    return take, lambda: pos["taken"] == len(windows)


def _step0_weights(windows, stage, sem):
    first = pl.program_id(0) == 0
    stream = {}

    @pl.when(first)
    def _():
        stream["take"], stream["all_taken"] = _weight_stream(windows, stage, sem)

    def need(k):
        @pl.when(first)
        def _():
            stream["take"](k)

    return need, lambda: stream["all_taken"]()


def _adaln_kernel(c_ref, cctx_ref, w_ref, b_ref, o_ref):
    c = jnp.concatenate([c_ref[...], jnp.broadcast_to(cctx_ref[...], c_ref.shape)], axis=0)
    s = c * jax.nn.sigmoid(c)
    o_ref[...] = _dot(s.astype(BF16), w_ref[...].astype(BF16)) + b_ref[...]


def _adaln(c, c_ctx, w_ada, b_ada):
    n_lat = c.shape[0]
    assert n_lat % 8 == 0
    n = w_ada.shape[1]
    tn = 1536
    return pl.pallas_call(
        _adaln_kernel,
        out_shape=jax.ShapeDtypeStruct((2 * n_lat, n), F32),
        grid=(n // tn,),
        in_specs=[
            pl.BlockSpec((n_lat, D_MODEL), lambda j: (0, 0)),
            pl.BlockSpec((1, D_MODEL), lambda j: (0, 0)),
            pl.BlockSpec((D_MODEL, tn), lambda j: (0, j)),
            pl.BlockSpec((1, tn), lambda j: (0, j)),
        ],
        out_specs=pl.BlockSpec((2 * n_lat, tn), lambda j: (0, j)),
        compiler_params=pltpu.CompilerParams(
            dimension_semantics=("arbitrary",), vmem_limit_bytes=VMEM_LIMIT),
        name="adaln",
    )(c, c_ctx, w_ada, b_ada)


def _attn_kernel(*refs, cached, seq, ctx_row):
    if cached:
        (x_ref, mod_ref, g_ref, w_hbm, cos_ref, sa_ref, sb_ref, ck_ref, cv_ref, lq1_ref, lk1_ref, lq2_ref, lk2_ref,
         dn_ref, o_ref, qkv_s, w_ref, stage, sem) = refs
    else:
        (x_ref, mod_ref, g_ref, w_hbm, lq1_ref, lk1_ref, lq2_ref, lk2_ref, dn_ref, o_ref, nk_ref, nv_ref,
         qkv_s, w_ref, stage, sem) = refs
        assert seq == ROW_CHUNK
    rc = ROW_CHUNK
    n_rows = x_ref.shape[0]
    mrow = pl.program_id(0) if cached else ctx_row
    first_rows = list(range(C_KA, C_QB, W_STAGE_ROWS)) if cached else []
    need, all_taken = _step0_weights(
        [(w_hbm, w_ref, r0, 0, r0, 0, W_STAGE_ROWS, D_MODEL)
         for r0 in first_rows + [r for r in range(C_QA, C_QB, W_STAGE_ROWS) if r not in first_rows]], stage, sem)
    lam = (jnp.exp(jnp.sum(lq1_ref[...] * lk1_ref[...], axis=-1, keepdims=True))
           - jnp.exp(jnp.sum(lq2_ref[...] * lk2_ref[...], axis=-1, keepdims=True)) + LAM_INIT)
    lane = lax.broadcasted_iota(jnp.int32, (1, 128), 1)
    scale = DK_A ** -0.5
    m1 = jnp.where(lane < DK_A, scale, 0.0).astype(BF16)
    m2 = jnp.where(lane >= DK_A, scale, 0.0).astype(BF16)
    n_keys = (n_rows + ck_ref.shape[1]) if cached else rc
    ones = jnp.ones((n_keys, 128), BF16)

    def proj(r0):
        rows = slice(r0, r0 + rc)
        h = _prenorm(x_ref, rows, mod_ref, mrow, g_ref)
        yield
        if r0 == 0:
            need((C_VA - C_QA) // W_STAGE_ROWS)
        pa = _dot_nt(h, w_ref[C_QA:C_VA, :])
        if r0 == 0:
            need((C_QB - C_VA) // W_STAGE_ROWS)
        pv = _dot_nt(h, w_ref[C_VA:C_QB, :])
        qkv_s[rows, C_QA:C_VA] = pa.astype(BF16)
        qkv_s[rows, C_VA:C_QB] = pv.astype(BF16)
        for hh in range(H_A):
            nk_ref[r0 // seq, hh] = pa[:, C_KA + hh * 128:C_KA + (hh + 1) * 128]
            nv_ref[r0 // seq, hh] = pv[:, hh * 128:(hh + 1) * 128]

    def rotary(t, rows):
        return t * cos_ref[rows, :] + pltpu.roll(t, 112, 1) * sa_ref[rows, :] + pltpu.roll(t, 16, 1) * sb_ref[rows, :]

    hs = {}

    def proj_kv(r0):
        rows = slice(r0, r0 + rc)
        hs[r0] = _prenorm(x_ref, rows, mod_ref, mrow, g_ref)
        yield
        if r0 == 0:
            need((C_VA - C_KA) // W_STAGE_ROWS)
        pk = _dot_nt(hs[r0], w_ref[C_KA:C_VA, :])
        if r0 == 0:
            need((C_QB - C_VA) // W_STAGE_ROWS)
        pv = _dot_nt(hs[r0], w_ref[C_VA:C_QB, :])
        for hh in range(H_A):
            cols = slice(C_KA + hh * 128, C_KA + (hh + 1) * 128)
            qkv_s[rows, cols] = rotary(pk[:, hh * 128:(hh + 1) * 128], rows).astype(BF16)
        qkv_s[rows, C_VA:C_QB] = pv.astype(BF16)

    def proj_q(r0):
        rows = slice(r0, r0 + rc)
        if r0 == 0:
            need((C_KA - C_QA) // W_STAGE_ROWS)
        pq = _dot_nt(hs[r0], w_ref[C_QA:C_KA, :])
        for hh in range(H_A):
            qkv_s[rows, hh * 128:(hh + 1) * 128] = rotary(pq[:, hh * 128:(hh + 1) * 128], rows).astype(BF16)
        yield

    kv = {}

    def chain(r0, h):
        rows = slice(r0, r0 + rc)
        cols = slice(h * 128, (h + 1) * 128)
        kcols = slice(C_KA + h * 128, C_KA + (h + 1) * 128)
        vcols = slice(C_VA + h * 128, C_VA + (h + 1) * 128)
        key = h if cached else (h, r0)
        if key not in kv:
            if cached:
                k = jnp.concatenate([ck_ref[h].astype(BF16), qkv_s[:, kcols]], axis=0)
                v = jnp.concatenate([cv_ref[h].astype(BF16), qkv_s[:, vcols]], axis=0)
            else:
                k, v = qkv_s[rows, kcols], qkv_s[rows, vcols]
            kv[key] = (k, jnp.concatenate([v, ones], axis=1))
        k, v1 = kv[key]
        q = qkv_s[rows, cols]
        s = _dot_nt(jnp.concatenate([q * m1, q * m2], axis=0), k)
        yield
        e = jnp.exp(s - jnp.max(s, axis=-1, keepdims=True)).astype(BF16)
        yield
        r = _dot(e, v1)
        yield
        r1, r2 = r[0:rc], r[rc:2 * rc]
        o = r1[:, 0:128] / r1[:, 128:256] - lam * (r2[:, 0:128] / r2[:, 128:256])
        o_ref[rows, cols] = (_rms(o, dn_ref[...]) * (1.0 - LAM_INIT)).astype(BF16)

    starts = range(0, n_rows, rc)
    gens, deps = [], {}
    if cached:
        kvs = {r0: proj_kv(r0) for r0 in starts}
        gens += list(kvs.values())
    for r0 in starts:
        if cached:
            p = proj_q(r0)
            deps[id(p)] = [kvs[r0]]
            need_first = list(kvs.values()) + [p]
        else:
            p = proj(r0)
            need_first = [p]
        gens.append(p)
        for h in range(H_A):
            c = chain(r0, h)
            deps[id(c)] = need_first
            gens.append(c)
    _run_skewed(gens, deps=deps)
    assert all_taken()


def _attention(x2d, mods, g_pre, w_in_f32, rope_tabs, cache_k, cache_v, lam_params, diff_norm, batch, seq_len):
    cached = cache_k is not None
    tq = MIX_TILE
    n_tiles = batch * seq_len // tq
    const = lambda i: (0, 0)
    in_specs = [
        pl.BlockSpec((tq, D_MODEL), lambda i: (i, 0)),
        pl.BlockSpec(mods.shape, const),
        pl.BlockSpec((1, D_MODEL), const),
        pl.BlockSpec(memory_space=pl.ANY),
    ]
    args = [x2d, mods, g_pre, w_in_f32]
    out_shape = [jax.ShapeDtypeStruct((batch * seq_len, H_A * DV_A), BF16)]
    out_specs = [pl.BlockSpec((tq, H_A * DV_A), lambda i: (i, 0))]
    if cached:
        assert seq_len == tq
        past = cache_k.shape[3]
        cspec = pl.BlockSpec((None, None, H_A, past, 128), lambda i: (i, 0, 0, 0, 0))
        in_specs += [pl.BlockSpec((tq, 128), const)] * 3 + [cspec, cspec]
        args += list(rope_tabs) + [cache_k, cache_v]
    else:
        nb = tq // seq_len
        for _ in range(2):
            out_shape.append(jax.ShapeDtypeStruct((batch, 1, H_A, seq_len, 128), F32))
            out_specs.append(pl.BlockSpec((nb, None, H_A, seq_len, 128), lambda i: (i, 0, 0, 0, 0)))
    in_specs += [pl.BlockSpec((1, DK_A), const)] * 4 + [pl.BlockSpec((1, DV_A), const)]
    args += list(lam_params) + [diff_norm]
    return pl.pallas_call(
        functools.partial(_attn_kernel, cached=cached, seq=seq_len, ctx_row=mods.shape[0] // 2),
        out_shape=out_shape,
        grid=(n_tiles,),
        in_specs=in_specs,
        out_specs=out_specs,
        scratch_shapes=[
            pltpu.VMEM((tq, C_QB), BF16),
            pltpu.VMEM((C_QB, D_MODEL), BF16),
            pltpu.VMEM((W_STAGE_SLOTS, W_STAGE_ROWS, D_MODEL), F32),
            pltpu.SemaphoreType.DMA((W_STAGE_SLOTS,)),
        ],
        compiler_params=pltpu.CompilerParams(
            dimension_semantics=("arbitrary",), vmem_limit_bytes=VMEM_LIMIT),
        name="attn_cached" if cached else "attn_ctx",
    )(*args)


def _split3(g):
    hi = g.astype(BF16)
    r1 = g - hi.astype(F32)
    mid = r1.astype(BF16)
    lo = (r1 - mid.astype(F32)).astype(BF16)
    return hi, mid, lo


def _gla_block_stages(q_ref, k_ref, g_ref, v_ref, in_rows, qc, vc, vt_cache, blk, reverse, oa_s, qe_s, ut_s, dec_s):
    C = GLA_CHUNK
    R = in_rows.stop - in_rows.start
    cpb = R // C
    rows = slice(blk * R, (blk + 1) * R)
    ri = lax.broadcasted_iota(jnp.int32, (R, R), 0)
    ci = lax.broadcasted_iota(jnp.int32, (R, R), 1)
    same_chunk = jnp.right_shift(ri, 6) == jnp.right_shift(ci, 6)
    if reverse:
        keep = same_chunk & (ci >= ri)
        last, ref = 0, C // 2
    else:
        keep = same_chunk & (ci <= ri)
        last, ref = C - 1, C // 2 - 1
    tri = jnp.where(keep, 1.0, 0.0).astype(BF16)
    lane = lax.broadcasted_iota(jnp.int32, (1, 2 * DK_B), 1)
    h0 = jnp.where(lane < DK_B, 1.0, 0.0).astype(BF16)
    h1 = jnp.where(lane >= DK_B, 1.0, 0.0).astype(BF16)

    hi, mid, lo = _split3(g_ref[in_rows, qc])
    b3 = _dot(tri, jnp.concatenate([hi, mid, lo], axis=1))
    yield
    b = b3[:, 0:128] + b3[:, 128:256] + b3[:, 256:384]
    q = q_ref[in_rows, qc]
    k = k_ref[in_rows, qc]
    qt, kt, qe, kd = [], [], [], []
    zeros = jnp.zeros((C, 2 * DK_B), BF16)
    for c in range(cpb):
        s = slice(c * C, (c + 1) * C)
        bc = b[s]
        bm = bc[ref:ref + 1]
        bl = bc[last:last + 1]
        qt.append((q[s] * jnp.exp(bc - bm)).astype(BF16))
        kt.append((k[s] * jnp.exp(bm - bc)).astype(BF16))
        qe.append((q[s] * jnp.exp(bc)).astype(BF16))
        kdc = (k[s] * jnp.exp(bl - bc)).astype(BF16)
        kd.append(jnp.concatenate([kdc if j == c else zeros for j in range(cpb)], axis=1))
        dec_s[blk * cpb + c:blk * cpb + c + 1, :] = jnp.exp(bl)
    qe_s[rows, :] = jnp.concatenate(qe, axis=0)
    qt = jnp.concatenate(qt, axis=0)
    kt = jnp.concatenate(kt, axis=0)
    yield
    s0 = _dot_nt(qt * h0, kt)
    s1 = _dot_nt(qt * h1, kt)
    v = v_ref[in_rows, vc]
    if (in_rows.start, vc.start) not in vt_cache:
        vt_cache[in_rows.start, vc.start] = v.T
    ut = _dot(vt_cache[in_rows.start, vc.start], jnp.concatenate(kd, axis=0))
    yield
    a0 = jnp.where(keep, s0, 0.0).astype(BF16)
    a1 = jnp.where(keep, s1, 0.0).astype(BF16)
    br = lax.broadcasted_iota(jnp.int32, ut.shape, 0)
    bcol = lax.broadcasted_iota(jnp.int32, ut.shape, 1)
    ut = jnp.where((br < DV_B) == ((bcol & (2 * DK_B - 1)) < DK_B), ut, 0.0)
    for c in range(cpb):
        ut_s[blk * cpb + c] = ut[:, c * 2 * DK_B:(c + 1) * 2 * DK_B]
    yield
    oa_s[rows, :] = jnp.concatenate([_dot(a0, v[:, 0:DV_B]), _dot(a1, v[:, DV_B:2 * DV_B])], axis=1)


def _state_to_t(s_pair):
    z = jnp.zeros((DK_B, DV_B), F32)
    a = jnp.concatenate([s_pair[0], z], axis=0).T
    b = jnp.concatenate([z, s_pair[1]], axis=0).T
    return jnp.concatenate([a, b], axis=0)


def _gla_kernel(*refs, n_chunks, n_elems, has_state, ctx_row):
    if has_state:
        (x_ref, mod_ref, g_ref, w_hbm, wgf_ref, wgb_ref, bgf_ref, bgb_ref, gn_ref, sf_ref, sb_ref, og_ref,
         *scratch) = refs
    else:
        (x_ref, mod_ref, g_ref, w_hbm, wgf_ref, wgb_ref, bgf_ref, bgb_ref, gn_ref, og_ref, nsf_ref, nsb_ref,
         *scratch) = refs
    q_s, k_s, gf_s, gb_s, rb_s, v_s, oa, oi, st, qe, ut, dec, w_ref, wgl_ref, stage, sem = scratch
    C = GLA_CHUNK
    R = ROW_CHUNK
    cpb = R // C
    L = n_chunks * C
    n_blk = L // R
    n_pairs = H_B // 2
    mrow = pl.program_id(0) if has_state else ctx_row
    need, all_taken = _step0_weights(
        [(w_hbm, w_ref, C_QB + b0, 0, b0, 0, W_STAGE_ROWS, D_MODEL) for b0 in range(B_RB, B_GL, W_STAGE_ROWS)]
        + [(w_hbm, wgl_ref, C_GL, 0, 0, 0, IN_COLS - C_GL, D_MODEL)]
        + [(w_hbm, w_ref, C_QB + b0, 0, b0, 0, W_STAGE_ROWS, D_MODEL) for b0 in range(B_QB, B_RB, W_STAGE_ROWS)],
        stage, sem)

    def proj(r0):
        rows = slice(r0, r0 + R)
        h = _prenorm(x_ref, rows, mod_ref, mrow, g_ref)
        yield
        if r0 == 0:
            need((B_GL - B_RB) // W_STAGE_ROWS)
        pr = _dot_nt(h, w_ref[B_RB:B_GL, :])
        if r0 == 0:
            need(1)
        pgl = _dot_nt(h, wgl_ref[...])
        if r0 == 0:
            need((B_RB - B_QB) // W_STAGE_ROWS)
        pb = _dot_nt(h, w_ref[B_QB:B_RB, :])
        zero = jnp.zeros(wgf_ref.shape, F32)
        wg = jnp.concatenate([jnp.concatenate([wgf_ref[...], zero], axis=1),
                              jnp.concatenate([zero, wgb_ref[...]], axis=1)], axis=0).astype(BF16)
        z = _dot(pgl.astype(BF16), wg) + jnp.concatenate([bgf_ref[...], bgb_ref[...]], axis=1)
        ls = (jnp.minimum(z, 0.0) - jnp.log(1.0 + jnp.exp(-jnp.abs(z)))) * (1.0 / GATE_NORM)
        gf_s[rows, :] = ls[:, 0:256]
        gb_s[rows, :] = ls[:, 256:512]
        rb_s[rows, :] = pr
        q_s[rows, :] = pb[:, B_QB:B_KB] * (DK_B ** -0.5)
        k_s[rows, :] = pb[:, B_KB:B_VB]
        v_s[rows, :] = pb[:, B_VB:B_RB].astype(BF16)

    def scan_stage(e, p, d, blk):
        ch = 2 * p + d
        if blk != (n_blk - 1 if d else 0):
            s = st[e, ch]
        elif has_state:
            assert n_elems == 1
            s = _state_to_t((sb_ref if d else sf_ref)[2 * p:2 * p + 2])
        else:
            s = jnp.zeros(st.shape[2:], F32)
        for i in range(cpb):
            c = blk * cpb + (cpb - 1 - i if d else i)
            rows = slice(c * C, (c + 1) * C)
            oi[e, ch, rows, :] = _dot_nt(qe[e, ch, rows, :], s.astype(BF16))
            s = s * dec[e, ch, c:c + 1, :] + ut[e, ch, c]
            yield
        if blk != (0 if d else n_blk - 1):
            st[e, ch] = s
        elif not has_state:
            dst_ref = nsb_ref if d else nsf_ref
            dst_ref[e, 2 * p] = s[0:DV_B, :].T[0:DK_B, :]
            dst_ref[e, 2 * p + 1] = s[DV_B:2 * DV_B, :].T[DK_B:2 * DK_B, :]

    def fin_stage(e, p, blk):
        rows = slice(blk * R, (blk + 1) * R)
        out_rows = slice(e * L + blk * R, e * L + (blk + 1) * R)
        o = ((oa[e, 2 * p, rows, :] + oi[e, 2 * p, rows, :])
             + (oa[e, 2 * p + 1, rows, :] + oi[e, 2 * p + 1, rows, :]))
        for j in range(2):
            cols = slice((2 * p + j) * DV_B, (2 * p + j + 1) * DV_B)
            oj = _rms(o[:, j * DV_B:(j + 1) * DV_B], gn_ref[...])
            rj = rb_s[out_rows, cols]
            og_ref[out_rows, cols] = (oj * (rj * jax.nn.sigmoid(rj))).astype(BF16)
        yield

    vt_cache = {}
    projs, blocks, scans, deps = {}, [], {}, {}
    for e in range(n_elems):
        for t in range(n_blk):
            blk = t // 2 if t % 2 == 0 else n_blk - 1 - t // 2
            projs[e, blk] = proj(e * L + blk * R)
    for e in range(n_elems):
        for t in range(n_blk):
            for p in range(n_pairs):
                for d in range(2):
                    blk = n_blk - 1 - t if d else t
                    ch = 2 * p + d
                    in_rows = slice(e * L + blk * R, e * L + (blk + 1) * R)
                    g = _gla_block_stages(
                        q_s, k_s, gb_s if d else gf_s, v_s, in_rows,
                        slice(p * 2 * DK_B, (p + 1) * 2 * DK_B), slice(p * 2 * DV_B, (p + 1) * 2 * DV_B),
                        vt_cache, blk, bool(d), oa.at[e, ch], qe.at[e, ch], ut.at[e, ch], dec.at[e, ch])
                    deps[id(g)] = [projs[e, blk]]
                    blocks.append(g)
                    s = scan_stage(e, p, d, blk)
                    prev = scans.get((e, p, d, blk + 1 if d else blk - 1))
                    deps[id(s)] = [g] + ([prev] if prev is not None else [])
                    scans[e, p, d, blk] = s
    aux = []
    for e in range(n_elems):
        for t in range(n_blk):
            for p in range(n_pairs):
                aux.append(scans[e, p, 0, t])
                aux.append(scans[e, p, 1, n_blk - 1 - t])
        for blk in range(n_blk):
            for p in range(n_pairs):
                f = fin_stage(e, p, blk)
                deps[id(f)] = [scans[e, p, 0, blk], scans[e, p, 1, blk]]
                aux.append(f)
    _run_skewed(list(projs.values()) + blocks, aux, deps)
    assert all_taken()


def _gla(x2d, mods, g_pre, w_in_f32, wgf, wgb, bgf, bgb, gla_norm, state_f, state_b, batch, seq_len):
    has_state = state_f is not None
    n_chunks = seq_len // GLA_CHUNK
    L = seq_len
    rows = MIX_TILE
    n_elems = rows // L
    n_ch = H_B
    wk, wv = H_B * DK_B, H_B * DV_B
    const = lambda i: (0, 0)
    in_specs = [
        pl.BlockSpec((rows, D_MODEL), lambda i: (i, 0)),
        pl.BlockSpec(mods.shape, const),
        pl.BlockSpec((1, D_MODEL), const),
        pl.BlockSpec(memory_space=pl.ANY),
        pl.BlockSpec((GATE_RANK, wk), const),
        pl.BlockSpec((GATE_RANK, wk), const),
        pl.BlockSpec((1, wk), const),
        pl.BlockSpec((1, wk), const),
        pl.BlockSpec((1, DV_B), const),
    ]
    args = [x2d, mods, g_pre, w_in_f32, wgf, wgb, bgf, bgb, gla_norm]
    out_shape = [jax.ShapeDtypeStruct((batch * L, wv), BF16)]
    out_specs = [pl.BlockSpec((rows, wv), lambda i: (i, 0))]
    if has_state:
        assert n_elems == 1
        st_spec = pl.BlockSpec((None, None, H_B, DK_B, DV_B), lambda i: (i, 0, 0, 0, 0))
        in_specs += [st_spec, st_spec]
        args += [state_f, state_b]
    else:
        for _ in range(2):
            out_shape.append(jax.ShapeDtypeStruct((batch, 1, H_B, DK_B, DV_B), F32))
            out_specs.append(pl.BlockSpec((n_elems, None, H_B, DK_B, DV_B), lambda i: (i, 0, 0, 0, 0)))
    return pl.pallas_call(
        functools.partial(_gla_kernel, n_chunks=n_chunks, n_elems=n_elems, has_state=has_state,
                          ctx_row=mods.shape[0] // 2),
        out_shape=out_shape,
        grid=(batch // n_elems,),
        in_specs=in_specs,
        out_specs=out_specs,
        scratch_shapes=[
            pltpu.VMEM((rows, wk), F32),
            pltpu.VMEM((rows, wk), F32),
            pltpu.VMEM((rows, wk), F32),
            pltpu.VMEM((rows, wk), F32),
            pltpu.VMEM((rows, wv), F32),
            pltpu.VMEM((rows, wv), BF16),
            pltpu.VMEM((n_elems, n_ch, L, 2 * DV_B), F32),
            pltpu.VMEM((n_elems, n_ch, L, 2 * DV_B), F32),
            pltpu.VMEM((n_elems, n_ch, 2 * DV_B, 2 * DK_B), F32),
            pltpu.VMEM((n_elems, n_ch, L, 2 * DK_B), BF16),
            pltpu.VMEM((n_elems, n_ch, n_chunks, 2 * DV_B, 2 * DK_B), F32),
            pltpu.VMEM((n_elems, n_ch, max(n_chunks, 8), 2 * DK_B), F32),
            pltpu.VMEM((B_GL, D_MODEL), BF16),
            pltpu.VMEM((IN_COLS - C_GL, D_MODEL), BF16),
            pltpu.VMEM((W_STAGE_SLOTS, W_STAGE_ROWS, D_MODEL), F32),
            pltpu.SemaphoreType.DMA((W_STAGE_SLOTS,)),
        ],
        compiler_params=pltpu.CompilerParams(
            dimension_semantics=("arbitrary",), vmem_limit_bytes=VMEM_LIMIT),
        name="gla_state" if has_state else "gla_ctx",
    )(*args)


def _out_mlp_kernel(xc_ref, oac_ref, ogc_ref, xl_ref, oal_ref, ogl_ref, mod_ref, wo_hbm, gpost_ref, gmpre_ref,
                    gmpost_ref, w1_hbm, w2_hbm, yc_ref, yl_ref, wo_ref, w1_ref, w2_ref, stage, sem, *,
                    n_ctx_tiles, lat_tiles_per_sample, ctx_row):
    rc = ROW_CHUNK
    tf = 1024
    per_win = tf // W_STAGE_ROWS

    def weight_windows():
        windows = [(wo_hbm, wo_ref, r0, 0, r0, 0, W_STAGE_ROWS, D_MODEL) for r0 in range(0, D_MODEL, W_STAGE_ROWS)]
        for j in range(D_FF // tf):
            windows += [(w1_hbm, w1_ref, r0, j * tf, r0, j * tf, W_STAGE_ROWS, tf)
                        for r0 in range(0, D_MODEL, W_STAGE_ROWS)]
            windows += [(w2_hbm, w2_ref, j * tf + r0, 0, j * tf + r0, 0, W_STAGE_ROWS, D_MODEL)
                        for r0 in range(0, tf, W_STAGE_ROWS)]
        return windows

    def run(x_ref, oa_ref, og_ref, y_ref, mrow, take=None):
        def chunk(r0):
            need = take if (take is not None and r0 == 0) else (lambda k: None)
            rows = slice(r0, r0 + rc)
            need(D_MODEL // W_STAGE_ROWS)
            mix = _dot(oa_ref[rows, :], wo_ref[0:512, :]) + _dot(og_ref[rows, :], wo_ref[512:1024, :])
            yield
            x1 = x_ref[rows, :] + _mod(mod_ref, mrow, 2) * _rms(mix, gpost_ref[...])
            h2 = (_rms(x1, gmpre_ref[...]) * (1.0 + _mod(mod_ref, mrow, 4)) + _mod(mod_ref, mrow, 3)).astype(BF16)
            yield
            acc = None
            for j in range(D_FF // tf):
                need(D_MODEL // W_STAGE_ROWS + per_win)
                u = jnp.maximum(_dot(h2, w1_ref[:, j * tf:(j + 1) * tf]), 0.0)
                part = _dot((u * u).astype(BF16), w2_ref[j * tf:(j + 1) * tf, :])
                acc = part if acc is None else acc + part
                yield
            y_ref[rows, :] = x1 + _mod(mod_ref, mrow, 5) * _rms(acc, gmpost_ref[...])

        _run_skewed([chunk(r0) for r0 in range(0, x_ref.shape[0], rc)])

    step = pl.program_id(0)
    is_ctx = step < n_ctx_tiles

    @pl.when(step == 0)
    def _():
        take, all_taken = _weight_stream(weight_windows(), stage, sem)
        run(xc_ref, oac_ref, ogc_ref, yc_ref, ctx_row, take)
        assert all_taken()

    @pl.when(jnp.logical_and(step > 0, is_ctx))
    def _():
        run(xc_ref, oac_ref, ogc_ref, yc_ref, ctx_row)

    @pl.when(jnp.logical_not(is_ctx))
    def _():
        run(xl_ref, oal_ref, ogl_ref, yl_ref, (step - n_ctx_tiles) // lat_tiles_per_sample)


def _out_mlp(xc2d, oac, ogc, xl2d, oal, ogl, mods, lat_seq, w_out, g_post, g_mpre, g_mpost, w1, w2):
    tm = 512
    n_c, n_l = xc2d.shape[0] // tm, xl2d.shape[0] // tm
    per_b = lat_seq // tm
    ctx_row = lambda i: (jnp.minimum(i, n_c - 1), 0)
    lat_row = lambda i: (jnp.maximum(i - n_c, 0), 0)
    const = lambda i: (0, 0)
    return pl.pallas_call(
        functools.partial(_out_mlp_kernel, n_ctx_tiles=n_c, lat_tiles_per_sample=per_b, ctx_row=mods.shape[0] // 2),
        out_shape=[jax.ShapeDtypeStruct(xc2d.shape, F32), jax.ShapeDtypeStruct(xl2d.shape, F32)],
        grid=(n_c + n_l,),
        in_specs=[
            pl.BlockSpec((tm, D_MODEL), ctx_row),
            pl.BlockSpec((tm, 512), ctx_row),
            pl.BlockSpec((tm, 512), ctx_row),
            pl.BlockSpec((tm, D_MODEL), lat_row),
            pl.BlockSpec((tm, 512), lat_row),
            pl.BlockSpec((tm, 512), lat_row),
            pl.BlockSpec(mods.shape, const),
            pl.BlockSpec(memory_space=pl.ANY),
            pl.BlockSpec((1, D_MODEL), const),
            pl.BlockSpec((1, D_MODEL), const),
            pl.BlockSpec((1, D_MODEL), const),
            pl.BlockSpec(memory_space=pl.ANY),
            pl.BlockSpec(memory_space=pl.ANY),
        ],
        out_specs=[pl.BlockSpec((tm, D_MODEL), ctx_row), pl.BlockSpec((tm, D_MODEL), lat_row)],
        scratch_shapes=[
            pltpu.VMEM((D_MODEL, D_MODEL), BF16),
            pltpu.VMEM((D_MODEL, D_FF), BF16),
            pltpu.VMEM((D_FF, D_MODEL), BF16),
            pltpu.VMEM((W_STAGE_SLOTS, W_STAGE_ROWS, D_MODEL), F32),
            pltpu.SemaphoreType.DMA((W_STAGE_SLOTS,)),
        ],
        compiler_params=pltpu.CompilerParams(
            dimension_semantics=("arbitrary",), vmem_limit_bytes=VMEM_LIMIT),
        name="out_mlp",
    )(xc2d, oac, ogc, xl2d, oal, ogl, mods, w_out, g_post, g_mpre, g_mpost, w1, w2)


def _rope_tables(n_lat):
    pos = np.arange(n_lat)
    row_pos = (pos // GRID_W).astype(np.float64)
    col_pos = (pos % GRID_W).astype(np.float64)
    half = DK_A // 4
    inv = ROPE_BASE ** (-np.arange(half, dtype=np.float64) / half)
    lane = np.arange(128)
    in64 = lane % 64
    in32 = in64 % 32
    p = np.where((in64 < 32)[None, :], row_pos[:, None], col_pos[:, None])
    ang = p * inv[in32 % half][None, :]
    cos, sin = np.cos(ang), np.sin(ang)
    first = (in32 < half)[None, :]
    sa = np.where(first, -sin, 0.0)
    sb = np.where(first, 0.0, sin)
    return tuple(jnp.asarray(t, dtype=F32) for t in (cos, sa, sb))


def _mixers(x, mods, w, rope_tabs, cache_k, cache_v, state_f, state_b):
    batch, seq_len, _ = x.shape
    x2d = x.reshape(batch * seq_len, D_MODEL)
    aouts = _attention(x2d, mods, w["g_pre"], w["w_in"], rope_tabs, cache_k, cache_v, w["lam"], w["diff_norm"],
                       batch, seq_len)
    gouts = _gla(x2d, mods, w["g_pre"], w["w_in"], w["wgf"], w["wgb"], w["bgf"], w["bgb"], w["gla_norm"],
                 state_f, state_b, batch, seq_len)
    return x2d, aouts[0], gouts[0], tuple(aouts[1:]) + tuple(gouts[1:])


def kernel(x_prompt, x_sample, c, cache_k, cache_v, state_fwd, state_bwd, c_ctx, w_ada, b_ada,
           norm_attn_pre, norm_attn_post, norm_mlp_pre, norm_mlp_post, w_in, w_gate_fwd, b_gate_fwd,
           w_gate_bwd, b_gate_bwd, lam_q1, lam_k1, lam_q2, lam_k2, diff_norm, gla_norm, w_out,
           w_mlp1, w_mlp2):
    m = _adaln(c, c_ctx[None, :], w_ada[0], b_ada)

    w = {
        "g_pre": norm_attn_pre[0][None, :],
        "g_post": norm_attn_post[0][None, :],
        "g_mpre": norm_mlp_pre[0][None, :],
        "g_mpost": norm_mlp_post[0][None, :],
        "w_in": jnp.transpose(w_in[0]),
        "wgf": w_gate_fwd[0],
        "wgb": w_gate_bwd[0],
        "bgf": b_gate_fwd,
        "bgb": b_gate_bwd,
        "lam": (lam_q1, lam_k1, lam_q2, lam_k2),
        "diff_norm": diff_norm[0][None, :],
        "gla_norm": gla_norm[0][None, :],
        "w_out": w_out[0],
        "w1": w_mlp1[0],
        "w2": w_mlp2[0],
    }
    xc2d, oac, ogc, (new_k, new_v, new_sf, new_sb) = _mixers(x_prompt, m, w, None, None, None, None, None)
    xl2d, oal, ogl, _ = _mixers(x_sample, m, w, _rope_tables(x_sample.shape[1]),
                                cache_k, cache_v, state_fwd, state_bwd)
    y_prompt, y_sample = _out_mlp(xc2d, oac, ogc, xl2d, oal, ogl, m, x_sample.shape[1],
                                  w["w_out"], w["g_post"], w["g_mpre"], w["g_mpost"], w["w1"], w["w2"])
    return (y_prompt.reshape(x_prompt.shape), y_sample.reshape(x_sample.shape), new_k, new_v, new_sf, new_sb)
```

```python
import functools
import math

import numpy as np
import jax
import jax.numpy as jnp
from jax import lax
from jax.experimental import pallas as pl
from jax.experimental.pallas import tpu as pltpu

F32 = jnp.float32
BF16 = jnp.bfloat16

D_MODEL = 1024
GRID_W = 64
H_A = 4
DV_A = 128
DK_A = 64
H_B = 4
DV_B = 128
DK_B = 64
GATE_RANK = 16
GATE_NORM = 16.0
GLA_CHUNK = 64
D_FF = 4 * D_MODEL
N_MOD = 6
ROPE_BASE = 10000.0
EPS = 1e-6
LAM_INIT = 0.8 - 0.6 * math.exp(-0.3 * 0)

C_QA, C_KA, C_VA, C_QB, C_KB, C_VB, C_RB, C_GL = 0, 512, 1024, 1536, 1792, 2048, 2560, 3072
IN_COLS = 3104
B_QB, B_KB, B_VB, B_RB, B_GL, B_COLS = (c - C_QB for c in (C_QB, C_KB, C_VB, C_RB, C_GL, IN_COLS))

V7X_VMEM_BYTES = 64 * 1024 * 1024
VMEM_LIMIT = V7X_VMEM_BYTES - 4 * 1024 * 1024
ROW_CHUNK = 256
MIX_TILE = 1024
W_STAGE_ROWS = 512
W_STAGE_SLOTS = 4


def _rms(x, g):
    return x * lax.rsqrt(jnp.mean(x * x, axis=-1, keepdims=True) + EPS) * g


def _dot(a, b):
    return jnp.dot(a, b, preferred_element_type=F32)


def _dot_nt(a, b):
    return lax.dot_general(a, b, (((1,), (1,)), ((), ())), preferred_element_type=F32)


def _run_skewed(gens, secondary=(), deps=None):
    deps = deps or {}
    queues, running, done = [list(gens), list(secondary)], [], set()
    while any(queues) or running:
        for queue in queues:
            for g in queue:
                if all(id(d) in done for d in deps.get(id(g), ())):
                    queue.remove(g)
                    running.append(g)
                    break
        assert running, "dependency cycle"
        for g in list(running):
            try:
                next(g)
            except StopIteration:
                running.remove(g)
                done.add(id(g))


def _prenorm(x_ref, rows, mod_ref, g_ref):
    x = x_ref[rows, :]
    return (_rms(x, g_ref[...]) * (1.0 + mod_ref[1:2, :]) + mod_ref[0:1, :]).astype(BF16)


def _adaln_kernel(c_ref, w_ref, b_ref, o_ref):
    c = c_ref[...]
    s = c * jax.nn.sigmoid(c)
    o_ref[...] = _dot(s.astype(BF16), w_ref[...].astype(BF16)) + b_ref[...]


def _adaln(cvec, w_ada, b_ada):
    rows = cvec.shape[0]
    n = w_ada.shape[1]
    tn = 1536
    return pl.pallas_call(
        _adaln_kernel,
        out_shape=jax.ShapeDtypeStruct((rows, n), F32),
        grid=(n // tn,),
        in_specs=[
            pl.BlockSpec((rows, D_MODEL), lambda j: (0, 0)),
            pl.BlockSpec((D_MODEL, tn), lambda j: (0, j)),
            pl.BlockSpec((1, tn), lambda j: (0, j)),
        ],
        out_specs=pl.BlockSpec((rows, tn), lambda j: (0, j)),
        compiler_params=pltpu.CompilerParams(
            dimension_semantics=("arbitrary",), vmem_limit_bytes=VMEM_LIMIT),
        name="adaln",
    )(cvec, w_ada, b_ada)


def _attn_kernel(*refs, cached, seq):
    if cached:
        (x_ref, mod_ref, g_ref, w_ref, cos_ref, sa_ref, sb_ref, ck_ref, cv_ref, lam_ref, dn_ref,
         o_ref, qkv_s) = refs
    else:
        x_ref, mod_ref, g_ref, w_ref, lam_ref, dn_ref, o_ref, nk_ref, nv_ref, qkv_s = refs
        assert seq == ROW_CHUNK
    rc = ROW_CHUNK
    n_rows = x_ref.shape[0]
    lp = lam_ref[...]
    lam = (jnp.exp(jnp.sum(lp[0:1] * lp[1:2], axis=-1, keepdims=True))
           - jnp.exp(jnp.sum(lp[2:3] * lp[3:4], axis=-1, keepdims=True)) + LAM_INIT)
    lane = lax.broadcasted_iota(jnp.int32, (1, 128), 1)
    scale = DK_A ** -0.5
    m1 = jnp.where(lane < DK_A, scale, 0.0).astype(BF16)
    m2 = jnp.where(lane >= DK_A, scale, 0.0).astype(BF16)
    n_keys = (n_rows + ck_ref.shape[1]) if cached else rc
    ones = jnp.ones((n_keys, 128), BF16)

    def proj(r0):
        rows = slice(r0, r0 + rc)
        h = _prenorm(x_ref, rows, mod_ref, g_ref)
        yield
        pa = _dot(h, w_ref[:, C_QA:C_VA])
        pv = _dot(h, w_ref[:, C_VA:C_QB])
        for blk in range(8):
            t = pa[:, blk * 128:(blk + 1) * 128]
            if cached:
                t = (t * cos_ref[rows, :] + pltpu.roll(t, 112, 1) * sa_ref[rows, :]
                     + pltpu.roll(t, 16, 1) * sb_ref[rows, :])
            elif blk >= 4:
                nk_ref[r0 // seq, blk - 4] = t
            qkv_s[rows, blk * 128:(blk + 1) * 128] = t.astype(BF16)
        qkv_s[rows, C_VA:C_QB] = pv.astype(BF16)
        if not cached:
            for hh in range(H_A):
                nv_ref[r0 // seq, hh] = pv[:, hh * 128:(hh + 1) * 128]

    kv = {}
    res = {}

    def chain(r0, h, branch):
        rows = slice(r0, r0 + rc)
        cols = slice(h * 128, (h + 1) * 128)
        kcols = slice(C_KA + h * 128, C_KA + (h + 1) * 128)
        vcols = slice(C_VA + h * 128, C_VA + (h + 1) * 128)
        key = h if cached else (h, r0)
        if key not in kv:
            if cached:
                k = jnp.concatenate([ck_ref[h].astype(BF16), qkv_s[:, kcols]], axis=0)
                v = jnp.concatenate([cv_ref[h].astype(BF16), qkv_s[:, vcols]], axis=0)
            else:
                k, v = qkv_s[rows, kcols], qkv_s[rows, vcols]
            kv[key] = (k, jnp.concatenate([v, ones], axis=1))
        k, v1 = kv[key]
        s = _dot_nt(qkv_s[rows, cols] * (m2 if branch else m1), k)
        yield
        e = jnp.exp(s - jnp.max(s, axis=-1, keepdims=True)).astype(BF16)
        yield
        res[r0, h, branch] = _dot(e, v1)
        yield
        if branch:
            r1, r2 = res[r0, h, 0], res[r0, h, 1]
            o = r1[:, 0:128] / r1[:, 128:256] - lam * (r2[:, 0:128] / r2[:, 128:256])
            o_ref[rows, cols] = (_rms(o, dn_ref[...]) * (1.0 - LAM_INIT)).astype(BF16)

    projs = {r0: proj(r0) for r0 in range(0, n_rows, rc)}
    gens, deps = list(projs.values()), {}
    for r0 in projs:
        for h in range(H_A):
            for br in range(2):
                c = chain(r0, h, br)
                deps[id(c)] = list(projs.values()) if cached else [projs[r0]]
                gens.append(c)
    _run_skewed(gens, deps=deps)


def _attention(x2d, mods, g_pre, w_in_b, rope_tabs, cache_k, cache_v, lam_p, diff_norm, batch, seq_len):
    cached = cache_k is not None
    tq = MIX_TILE
    n_tiles = batch * seq_len // tq
    const = lambda i: (0, 0)
    in_specs = [
        pl.BlockSpec((tq, D_MODEL), lambda i: (i, 0)),
        pl.BlockSpec((None, N_MOD, D_MODEL), (lambda i: (i, 0, 0)) if cached else (lambda i: (0, 0, 0))),
        pl.BlockSpec((1, D_MODEL), const),
        pl.BlockSpec((D_MODEL, C_QB), const, pipeline_mode=pl.Buffered(1)),
    ]
    args = [x2d, mods, g_pre, w_in_b]
    out_shape = [jax.ShapeDtypeStruct((batch * seq_len, H_A * DV_A), BF16)]
    out_specs = [pl.BlockSpec((tq, H_A * DV_A), lambda i: (i, 0))]
    if cached:
        assert seq_len == tq and mods.shape[0] == batch
        past = cache_k.shape[3]
        cspec = pl.BlockSpec((None, None, H_A, past, 128), lambda i: (i, 0, 0, 0, 0))
        in_specs += [pl.BlockSpec((tq, 128), const)] * 3 + [cspec, cspec]
        args += list(rope_tabs) + [cache_k, cache_v]
    else:
        assert mods.shape[0] == 1
        nb = tq // seq_len
        for _ in range(2):
            out_shape.append(jax.ShapeDtypeStruct((batch, 1, H_A, seq_len, 128), F32))
            out_specs.append(pl.BlockSpec((nb, None, H_A, seq_len, 128), lambda i: (i, 0, 0, 0, 0)))
    in_specs += [pl.BlockSpec((4, DK_A), const), pl.BlockSpec((1, DV_A), const)]
    args += [lam_p, diff_norm]
    return pl.pallas_call(
        functools.partial(_attn_kernel, cached=cached, seq=seq_len),
        out_shape=out_shape,
        grid=(n_tiles,),
        in_specs=in_specs,
        out_specs=out_specs,
        scratch_shapes=[pltpu.VMEM((tq, C_QB), BF16)],
        compiler_params=pltpu.CompilerParams(
            dimension_semantics=("arbitrary",), vmem_limit_bytes=VMEM_LIMIT),
        name="attn_cached" if cached else "attn_ctx",
    )(*args)


def _split3(g):
    hi = g.astype(BF16)
    r1 = g - hi.astype(F32)
    mid = r1.astype(BF16)
    lo = (r1 - mid.astype(F32)).astype(BF16)
    return hi, mid, lo


def _gla_block_stages(q_ref, k_ref, g_ref, v_ref, in_rows, qc, vc, vt_cache, blk, reverse, oa_s, qe_s, ut_s, dec_s):
    C = GLA_CHUNK
    R = in_rows.stop - in_rows.start
    cpb = R // C
    rows = slice(blk * R, (blk + 1) * R)
    ri = lax.broadcasted_iota(jnp.int32, (R, R), 0)
    ci = lax.broadcasted_iota(jnp.int32, (R, R), 1)
    same_chunk = jnp.right_shift(ri, 6) == jnp.right_shift(ci, 6)
    if reverse:
        keep = same_chunk & (ci >= ri)
        last, ref = 0, C // 2
    else:
        keep = same_chunk & (ci <= ri)
        last, ref = C - 1, C // 2 - 1
    tri = jnp.where(keep, 1.0, 0.0).astype(BF16)
    lane = lax.broadcasted_iota(jnp.int32, (1, 2 * DK_B), 1)
    h0 = jnp.where(lane < DK_B, 1.0, 0.0).astype(BF16)
    h1 = jnp.where(lane >= DK_B, 1.0, 0.0).astype(BF16)

    hi, mid, lo = _split3(g_ref[in_rows, qc])
    b3 = _dot(tri, jnp.concatenate([hi, mid, lo], axis=1))
    yield
    b = b3[:, 0:128] + b3[:, 128:256] + b3[:, 256:384]
    q = q_ref[in_rows, qc]
    k = k_ref[in_rows, qc]
    qt, kt, qe, kd = [], [], [], []
    zeros = jnp.zeros((C, 2 * DK_B), BF16)
    for c in range(cpb):
        s = slice(c * C, (c + 1) * C)
        bc = b[s]
        bm = bc[ref:ref + 1]
        bl = bc[last:last + 1]
        qt.append((q[s] * jnp.exp(bc - bm)).astype(BF16))
        kt.append((k[s] * jnp.exp(bm - bc)).astype(BF16))
        qe.append((q[s] * jnp.exp(bc)).astype(BF16))
        kdc = (k[s] * jnp.exp(bl - bc)).astype(BF16)
        kd.append(jnp.concatenate([kdc if j == c else zeros for j in range(cpb)], axis=1))
        dec_s[blk * cpb + c:blk * cpb + c + 1, :] = jnp.exp(bl)
    qe_s[rows, :] = jnp.concatenate(qe, axis=0)
    qt = jnp.concatenate(qt, axis=0)
    kt = jnp.concatenate(kt, axis=0)
    yield
    s0 = _dot_nt(qt * h0, kt)
    s1 = _dot_nt(qt * h1, kt)
    v = v_ref[in_rows, vc]
    if (in_rows.start, vc.start) not in vt_cache:
        vt_cache[in_rows.start, vc.start] = v.T
    ut = _dot(vt_cache[in_rows.start, vc.start], jnp.concatenate(kd, axis=0))
    yield
    a0 = jnp.where(keep, s0, 0.0).astype(BF16)
    a1 = jnp.where(keep, s1, 0.0).astype(BF16)
    br = lax.broadcasted_iota(jnp.int32, ut.shape, 0)
    bcol = lax.broadcasted_iota(jnp.int32, ut.shape, 1)
    ut = jnp.where((br < DV_B) == ((bcol & (2 * DK_B - 1)) < DK_B), ut, 0.0)
    for c in range(cpb):
        ut_s[blk * cpb + c] = ut[:, c * 2 * DK_B:(c + 1) * 2 * DK_B]
    yield
    oa_s[rows, :] = jnp.concatenate([_dot(a0, v[:, 0:DV_B]), _dot(a1, v[:, DV_B:2 * DV_B])], axis=1)


def _state_to_t(s_pair):
    z = jnp.zeros((DK_B, DV_B), F32)
    a = jnp.concatenate([s_pair[0], z], axis=0).T
    b = jnp.concatenate([z, s_pair[1]], axis=0).T
    return jnp.concatenate([a, b], axis=0)


def _gla_kernel(*refs, n_chunks, n_elems, has_state):
    if has_state:
        (x_ref, mod_ref, g_ref, w_ref, wg_ref, bg_ref, gn_ref, sf_ref, sb_ref, og_ref, *scratch) = refs
    else:
        (x_ref, mod_ref, g_ref, w_ref, wg_ref, bg_ref, gn_ref, og_ref, nsf_ref, nsb_ref, *scratch) = refs
    q_s, k_s, gf_s, gb_s, rb_s, v_s, oa, oi, st, qe, ut, dec = scratch
    C = GLA_CHUNK
    R = ROW_CHUNK
    cpb = R // C
    L = n_chunks * C
    n_blk = L // R
    n_pairs = H_B // 2

    def proj(r0):
        rows = slice(r0, r0 + R)
        h = _prenorm(x_ref, rows, mod_ref, g_ref)
        yield
        pg = _dot(h, w_ref[:, C_RB:IN_COLS])
        pb = _dot(h, w_ref[:, C_QB:C_RB])
        z = _dot(pg[:, B_GL - B_RB:B_COLS - B_RB].astype(BF16), wg_ref[...]) + bg_ref[...]
        ls = (jnp.minimum(z, 0.0) - jnp.log(1.0 + jnp.exp(-jnp.abs(z)))) * (1.0 / GATE_NORM)
        gf_s[rows, :] = ls[:, 0:256]
        gb_s[rows, :] = ls[:, 256:512]
        rb_s[rows, :] = pg[:, 0:512]
        q_s[rows, :] = pb[:, B_QB:B_KB] * (DK_B ** -0.5)
        k_s[rows, :] = pb[:, B_KB:B_VB]
        v_s[rows, :] = pb[:, B_VB:B_RB].astype(BF16)

    def scan_stage(e, p, d, blk):
        ch = 2 * p + d
        if blk != (n_blk - 1 if d else 0):
            s = st[e, ch]
        elif has_state:
            assert n_elems == 1
            s = _state_to_t((sb_ref if d else sf_ref)[2 * p:2 * p + 2])
        else:
            s = jnp.zeros(st.shape[2:], F32)
        for i in range(cpb):
            c = blk * cpb + (cpb - 1 - i if d else i)
            rows = slice(c * C, (c + 1) * C)
            oi[e, ch, rows, :] = _dot_nt(qe[e, ch, rows, :], s.astype(BF16))
            s = s * dec[e, ch, c:c + 1, :] + ut[e, ch, c]
            yield
        if blk != (0 if d else n_blk - 1):
            st[e, ch] = s
        elif not has_state:
            dst_ref = nsb_ref if d else nsf_ref
            dst_ref[e, 2 * p] = s[0:DV_B, :].T[0:DK_B, :]
            dst_ref[e, 2 * p + 1] = s[DV_B:2 * DV_B, :].T[DK_B:2 * DK_B, :]

    def fin_stage(e, p, blk):
        rows = slice(blk * R, (blk + 1) * R)
        out_rows = slice(e * L + blk * R, e * L + (blk + 1) * R)
        o = ((oa[e, 2 * p, rows, :] + oi[e, 2 * p, rows, :])
             + (oa[e, 2 * p + 1, rows, :] + oi[e, 2 * p + 1, rows, :]))
        for j in range(2):
            cols = slice((2 * p + j) * DV_B, (2 * p + j + 1) * DV_B)
            oj = _rms(o[:, j * DV_B:(j + 1) * DV_B], gn_ref[...])
            rj = rb_s[out_rows, cols]
            og_ref[out_rows, cols] = (oj * (rj * jax.nn.sigmoid(rj))).astype(BF16)
        yield

    vt_cache = {}
    projs, blocks, scans, deps = {}, [], {}, {}
    for e in range(n_elems):
        for t in range(n_blk):
            blk = t // 2 if t % 2 == 0 else n_blk - 1 - t // 2
            projs[e, blk] = proj(e * L + blk * R)
    for e in range(n_elems):
        for t in range(n_blk):
            for p in range(n_pairs):
                for d in range(2):
                    blk = n_blk - 1 - t if d else t
                    ch = 2 * p + d
                    in_rows = slice(e * L + blk * R, e * L + (blk + 1) * R)
                    g = _gla_block_stages(
                        q_s, k_s, gb_s if d else gf_s, v_s, in_rows,
                        slice(p * 2 * DK_B, (p + 1) * 2 * DK_B), slice(p * 2 * DV_B, (p + 1) * 2 * DV_B),
                        vt_cache, blk, bool(d), oa.at[e, ch], qe.at[e, ch], ut.at[e, ch], dec.at[e, ch])
                    deps[id(g)] = [projs[e, blk]]
                    blocks.append(g)
                    s = scan_stage(e, p, d, blk)
                    prev = scans.get((e, p, d, blk + 1 if d else blk - 1))
                    deps[id(s)] = [g] + ([prev] if prev is not None else [])
                    scans[e, p, d, blk] = s
    aux = []
    for e in range(n_elems):
        for t in range(n_blk):
            for p in range(n_pairs):
                aux.append(scans[e, p, 0, t])
                aux.append(scans[e, p, 1, n_blk - 1 - t])
        for blk in range(n_blk):
            for p in range(n_pairs):
                f = fin_stage(e, p, blk)
                deps[id(f)] = [scans[e, p, 0, blk], scans[e, p, 1, blk]]
                aux.append(f)
    _run_skewed(list(projs.values()) + blocks, aux, deps)


def _gla(x2d, mods, g_pre, w_in_b, wg, bg, gla_norm, state_f, state_b, batch, seq_len):
    has_state = state_f is not None
    n_chunks = seq_len // GLA_CHUNK
    L = seq_len
    rows = MIX_TILE
    n_elems = rows // L
    n_ch = H_B
    wk, wv = H_B * DK_B, H_B * DV_B
    const = lambda i: (0, 0)
    in_specs = [
        pl.BlockSpec((rows, D_MODEL), lambda i: (i, 0)),
        pl.BlockSpec((None, N_MOD, D_MODEL), (lambda i: (i, 0, 0)) if has_state else (lambda i: (0, 0, 0))),
        pl.BlockSpec((1, D_MODEL), const),
        pl.BlockSpec((D_MODEL, IN_COLS), const, pipeline_mode=pl.Buffered(1)),
        pl.BlockSpec((2 * GATE_RANK, 2 * wk), const),
        pl.BlockSpec((1, 2 * wk), const),
        pl.BlockSpec((1, DV_B), const),
    ]
    args = [x2d, mods, g_pre, w_in_b, wg, bg, gla_norm]
    out_shape = [jax.ShapeDtypeStruct((batch * L, wv), BF16)]
    out_specs = [pl.BlockSpec((rows, wv), lambda i: (i, 0))]
    if has_state:
        assert n_elems == 1 and mods.shape[0] == batch
        st_spec = pl.BlockSpec((None, None, H_B, DK_B, DV_B), lambda i: (i, 0, 0, 0, 0))
        in_specs += [st_spec, st_spec]
        args += [state_f, state_b]
    else:
        assert mods.shape[0] == 1
        for _ in range(2):
            out_shape.append(jax.ShapeDtypeStruct((batch, 1, H_B, DK_B, DV_B), F32))
            out_specs.append(pl.BlockSpec((n_elems, None, H_B, DK_B, DV_B), lambda i: (i, 0, 0, 0, 0)))
    return pl.pallas_call(
        functools.partial(_gla_kernel, n_chunks=n_chunks, n_elems=n_elems, has_state=has_state),
        out_shape=out_shape,
        grid=(batch // n_elems,),
        in_specs=in_specs,
        out_specs=out_specs,
        scratch_shapes=[
            pltpu.VMEM((rows, wk), F32),
            pltpu.VMEM((rows, wk), F32),
            pltpu.VMEM((rows, wk), F32),
            pltpu.VMEM((rows, wk), F32),
            pltpu.VMEM((rows, wv), F32),
            pltpu.VMEM((rows, wv), BF16),
            pltpu.VMEM((n_elems, n_ch, L, 2 * DV_B), F32),
            pltpu.VMEM((n_elems, n_ch, L, 2 * DV_B), F32),
            pltpu.VMEM((n_elems, n_ch, 2 * DV_B, 2 * DK_B), F32),
            pltpu.VMEM((n_elems, n_ch, L, 2 * DK_B), BF16),
            pltpu.VMEM((n_elems, n_ch, n_chunks, 2 * DV_B, 2 * DK_B), F32),
            pltpu.VMEM((n_elems, n_ch, max(n_chunks, 8), 2 * DK_B), F32),
        ],
        compiler_params=pltpu.CompilerParams(
            dimension_semantics=("arbitrary",), vmem_limit_bytes=VMEM_LIMIT),
        name="gla_state" if has_state else "gla_ctx",
    )(*args)


def _weight_copy(src_hbm, r0, c0, stage, sem, slot):
    return pltpu.make_async_copy(src_hbm.at[pl.ds(r0, W_STAGE_ROWS), pl.ds(c0, D_MODEL)], stage.at[slot], sem.at[slot])


def _out_mlp_kernel(xc_ref, oac_ref, ogc_ref, xl_ref, oal_ref, ogl_ref, mod_ref, wo_hbm, gpost_ref, gmpre_ref,
                    gmpost_ref, w1_hbm, w2_hbm, yc_ref, yl_ref, wo_ref, w1_ref, w2_ref, stage, sem, *, n_ctx_tiles):
    rc = ROW_CHUNK
    tf = 1024
    n_slots = stage.shape[0]
    per_win = tf // W_STAGE_ROWS

    def weight_stream():
        windows = [(wo_hbm, wo_ref, r0, 0) for r0 in range(0, D_MODEL, W_STAGE_ROWS)]
        for j in range(D_FF // tf):
            windows += [(w1_hbm, w1_ref, r0, j * tf) for r0 in range(0, D_MODEL, W_STAGE_ROWS)]
            windows += [(w2_hbm, w2_ref, j * tf + r0, 0) for r0 in range(0, tf, W_STAGE_ROWS)]
        copies = [_weight_copy(src, r0, c0, stage, sem, n % n_slots) for n, (src, _, r0, c0) in enumerate(windows)]
        pos = {"started": 0, "taken": 0}

        def start_below(limit):
            while pos["started"] < min(limit, len(copies)):
                copies[pos["started"]].start()
                pos["started"] += 1

        def take(k):
            for _ in range(k):
                n = pos["taken"]
                _, dst, r0, c0 = windows[n]
                copies[n].wait()
                dst[r0:r0 + W_STAGE_ROWS, c0:c0 + D_MODEL] = stage[n % n_slots].astype(BF16)
                pos["taken"] += 1
                start_below(n + n_slots + 1)

        start_below(n_slots)
        return take, lambda: pos["taken"] == len(copies)

    def run(x_ref, oa_ref, og_ref, y_ref, take=None):
        def chunk(r0):
            need = take if (take is not None and r0 == 0) else (lambda k: None)
            rows = slice(r0, r0 + rc)
            need(D_MODEL // W_STAGE_ROWS)
            mix = _dot(oa_ref[rows, :], wo_ref[0:512, :]) + _dot(og_ref[rows, :], wo_ref[512:1024, :])
            yield
            x1 = x_ref[rows, :] + mod_ref[2:3, :] * _rms(mix, gpost_ref[...])
            h2 = (_rms(x1, gmpre_ref[...]) * (1.0 + mod_ref[4:5, :]) + mod_ref[3:4, :]).astype(BF16)
            yield
            acc = None
            for j in range(D_FF // tf):
                need(D_MODEL // W_STAGE_ROWS + per_win)
                u = jnp.maximum(_dot(h2, w1_ref[:, j * tf:(j + 1) * tf]), 0.0)
                part = _dot((u * u).astype(BF16), w2_ref[j * tf:(j + 1) * tf, :])
                acc = part if acc is None else acc + part
                yield
            y_ref[rows, :] = x1 + mod_ref[5:6, :] * _rms(acc, gmpost_ref[...])

        _run_skewed([chunk(r0) for r0 in range(0, x_ref.shape[0], rc)])

    step = pl.program_id(0)
    is_ctx = step < n_ctx_tiles

    @pl.when(step == 0)
    def _():
        take, all_taken = weight_stream()
        run(xc_ref, oac_ref, ogc_ref, yc_ref, take)
        assert all_taken()

    @pl.when(jnp.logical_and(step > 0, is_ctx))
    def _():
        run(xc_ref, oac_ref, ogc_ref, yc_ref)

    @pl.when(jnp.logical_not(is_ctx))
    def _():
        run(xl_ref, oal_ref, ogl_ref, yl_ref)


def _out_mlp(xc2d, oac, ogc, xl2d, oal, ogl, mods, lat_seq, w_out, g_post, g_mpre, g_mpost, w1, w2):
    tm = 512
    n_c, n_l = xc2d.shape[0] // tm, xl2d.shape[0] // tm
    per_b = lat_seq // tm
    ctx_row = lambda i: (jnp.minimum(i, n_c - 1), 0)
    lat_row = lambda i: (jnp.maximum(i - n_c, 0), 0)
    mod_idx = lambda i: (jnp.where(i < n_c, 0, 1 + jnp.maximum(i - n_c, 0) // per_b), 0, 0)
    const = lambda i: (0, 0)
    return pl.pallas_call(
        functools.partial(_out_mlp_kernel, n_ctx_tiles=n_c),
        out_shape=[jax.ShapeDtypeStruct(xc2d.shape, F32), jax.ShapeDtypeStruct(xl2d.shape, F32)],
        grid=(n_c + n_l,),
        in_specs=[
            pl.BlockSpec((tm, D_MODEL), ctx_row),
            pl.BlockSpec((tm, 512), ctx_row),
            pl.BlockSpec((tm, 512), ctx_row),
            pl.BlockSpec((tm, D_MODEL), lat_row),
            pl.BlockSpec((tm, 512), lat_row),
            pl.BlockSpec((tm, 512), lat_row),
            pl.BlockSpec((None, N_MOD, D_MODEL), mod_idx),
            pl.BlockSpec(memory_space=pl.ANY),
            pl.BlockSpec((1, D_MODEL), const),
            pl.BlockSpec((1, D_MODEL), const),
            pl.BlockSpec((1, D_MODEL), const),
            pl.BlockSpec(memory_space=pl.ANY),
            pl.BlockSpec(memory_space=pl.ANY),
        ],
        out_specs=[pl.BlockSpec((tm, D_MODEL), ctx_row), pl.BlockSpec((tm, D_MODEL), lat_row)],
        scratch_shapes=[
            pltpu.VMEM((D_MODEL, D_MODEL), BF16),
            pltpu.VMEM((D_MODEL, D_FF), BF16),
            pltpu.VMEM((D_FF, D_MODEL), BF16),
            pltpu.VMEM((W_STAGE_SLOTS, W_STAGE_ROWS, D_MODEL), F32),
            pltpu.SemaphoreType.DMA((W_STAGE_SLOTS,)),
        ],
        compiler_params=pltpu.CompilerParams(
            dimension_semantics=("arbitrary",), vmem_limit_bytes=VMEM_LIMIT),
        name="out_mlp",
    )(xc2d, oac, ogc, xl2d, oal, ogl, mods, w_out, g_post, g_mpre, g_mpost, w1, w2)


def _rope_tables(n_lat):
    pos = np.arange(n_lat)
    row_pos = (pos // GRID_W).astype(np.float64)
    col_pos = (pos % GRID_W).astype(np.float64)
    half = DK_A // 4
    inv = ROPE_BASE ** (-np.arange(half, dtype=np.float64) / half)
    lane = np.arange(128)
    in64 = lane % 64
    in32 = in64 % 32
    p = np.where((in64 < 32)[None, :], row_pos[:, None], col_pos[:, None])
    ang = p * inv[in32 % half][None, :]
    cos, sin = np.cos(ang), np.sin(ang)
    first = (in32 < half)[None, :]
    sa = np.where(first, -sin, 0.0)
    sb = np.where(first, 0.0, sin)
    return tuple(jnp.asarray(t, dtype=F32) for t in (cos, sa, sb))


def _mixers(x, mods, w, rope_tabs, cache_k, cache_v, state_f, state_b):
    batch, seq_len, _ = x.shape
    x2d = x.reshape(batch * seq_len, D_MODEL)
    aouts = _attention(x2d, mods, w["g_pre"], w["w_in"], rope_tabs, cache_k, cache_v, w["lam"], w["diff_norm"],
                       batch, seq_len)
    gouts = _gla(x2d, mods, w["g_pre"], w["w_in"], w["wg"], w["bg"], w["gla_norm"], state_f, state_b,
                 batch, seq_len)
    return x2d, aouts[0], gouts[0], tuple(aouts[1:]) + tuple(gouts[1:])


def kernel(x_prompt, x_sample, c, cache_k, cache_v, state_fwd, state_bwd, c_ctx, w_ada, b_ada,
           norm_attn_pre, norm_attn_post, norm_mlp_pre, norm_mlp_post, w_in, w_gate_fwd, b_gate_fwd,
           w_gate_bwd, b_gate_bwd, lam_q1, lam_k1, lam_q2, lam_k2, diff_norm, gla_norm, w_out,
           w_mlp1, w_mlp2):
    dec_batch = c.shape[0]
    rows = 16
    cvec = jnp.concatenate([c_ctx[None, :], c, jnp.zeros((rows - 1 - dec_batch, D_MODEL), F32)], axis=0)
    m = _adaln(cvec, w_ada[0], b_ada[0][None, :])
    mods_ctx = m[0:1].reshape(1, N_MOD, D_MODEL)
    mods_lat = m[1:1 + dec_batch].reshape(dec_batch, N_MOD, D_MODEL)

    zg = jnp.zeros((GATE_RANK, 256), F32)
    wg = jnp.concatenate([jnp.concatenate([w_gate_fwd[0], zg], axis=1),
                          jnp.concatenate([zg, w_gate_bwd[0]], axis=1)], axis=0)
    w = {
        "g_pre": norm_attn_pre[0][None, :],
        "g_post": norm_attn_post[0][None, :],
        "g_mpre": norm_mlp_pre[0][None, :],
        "g_mpost": norm_mlp_post[0][None, :],
        "w_in": w_in[0].astype(BF16),
        "wg": wg.astype(BF16),
        "bg": jnp.concatenate([b_gate_fwd[0], b_gate_bwd[0]])[None, :],
        "lam": jnp.stack([lam_q1[0], lam_k1[0], lam_q2[0], lam_k2[0]]),
        "diff_norm": diff_norm[0][None, :],
        "gla_norm": gla_norm[0][None, :],
        "w_out": w_out[0],
        "w1": w_mlp1[0],
        "w2": w_mlp2[0],
    }
    xc2d, oac, ogc, (new_k, new_v, new_sf, new_sb) = _mixers(x_prompt, mods_ctx, w, None, None, None, None, None)
    xl2d, oal, ogl, _ = _mixers(x_sample, mods_lat, w, _rope_tables(x_sample.shape[1]),
                                cache_k, cache_v, state_fwd, state_bwd)
    mods_all = m[0:1 + dec_batch].reshape(1 + dec_batch, N_MOD, D_MODEL)
    y_prompt, y_sample = _out_mlp(xc2d, oac, ogc, xl2d, oal, ogl, mods_all, x_sample.shape[1],
                                  w["w_out"], w["g_post"], w["g_mpre"], w["g_mpost"], w["w1"], w["w2"])
    return (y_prompt.reshape(x_prompt.shape), y_sample.reshape(x_sample.shape), new_k, new_v, new_sf, new_sb)
```
